```python
import jax, jax.numpy as jnp
from jax import lax
import numpy as np

D_MODEL = 2048
BATCH = 1
SEQ = 8192
DEPTH = 1

D_CONV = D_MODEL
CONV_WIDTH = 3
N_HEADS = 16
D_LATENT = 256
D_VHEAD = D_MODEL // N_HEADS
N_IDX_HEADS = 16
D_IDX = 64
TOPK_MAX = 256
Q_BLOCK = 128
ATTN_SCALE = D_LATENT ** -0.5
IDX_SCALE = (D_IDX ** -0.5) * (N_IDX_HEADS ** -0.5)
D_FF = 4 * D_MODEL
D_PLE = 256
N_BRANCH = 2
LN_EPS = 1e-5
RMS_EPS = 1e-6
ALPHA = (2.0 * DEPTH) ** 0.25
BETA = (8.0 * DEPTH) ** -0.25

SPLITS = [D_CONV, D_CONV, D_CONV,
          N_HEADS * D_LATENT, D_LATENT,
          N_IDX_HEADS * D_IDX, D_IDX,
          N_IDX_HEADS,
          N_BRANCH * D_MODEL]
D_IN_TOTAL = int(sum(SPLITS))
SPLIT_POINTS = [int(v) for v in np.cumsum(SPLITS)[:-1]]

kernel_name = "hybrid_shortconv_dsa_gated_deepnorm"


def layer_norm(x, g, b):
    xf = x.astype(jnp.float32)
    mu = jnp.mean(xf, axis=-1, keepdims=True)
    var = jnp.mean(jnp.square(xf - mu), axis=-1, keepdims=True)
    return ((xf - mu) * lax.rsqrt(var + LN_EPS)).astype(x.dtype) * g + b


def rms_norm(x, g):
    xf = x.astype(jnp.float32)
    return (xf * lax.rsqrt(jnp.mean(jnp.square(xf), axis=-1, keepdims=True) + RMS_EPS)).astype(x.dtype) * g


def causal_depthwise_conv(u, w):
    c = u.shape[-1]
    return lax.conv_general_dilated(
        u, w[:, None, :].astype(u.dtype), window_strides=(1,),
        padding=[(CONV_WIDTH - 1, 0)], dimension_numbers=('NWC', 'WIO', 'NWC'),
        feature_group_count=c)


def dsa_attention(q_lat, c_kv, q_idx, k_idx, w_idx, w_uv):
    b, s = c_kv.shape[0], c_kv.shape[1]
    topk = min(TOPK_MAX, s // 4)
    n_blocks = s // Q_BLOCK
    key_pos = jnp.arange(s)

    def one_block(blk):
        start = blk * Q_BLOCK
        q = lax.dynamic_slice_in_dim(q_lat, start, Q_BLOCK, axis=1)
        qi = lax.dynamic_slice_in_dim(q_idx, start, Q_BLOCK, axis=1)
        wi = lax.dynamic_slice_in_dim(w_idx, start, Q_BLOCK, axis=1)
        t = start + jnp.arange(Q_BLOCK)
        causal = key_pos[None, :] <= t[:, None]
        rel = jax.nn.relu(jnp.einsum('bqhd,bsd->bqhs', qi, k_idx))
        score = jnp.einsum('bqh,bqhs->bqs', wi, rel).astype(jnp.float32)
        score = jnp.where(causal[None], score, -jnp.inf)
        _, sel = lax.top_k(score, topk)
        kv = jax.vmap(lambda c, i: c[i])(c_kv, sel)
        logits = jnp.einsum('bqhc,bqkc->bqhk', q, kv).astype(jnp.float32) * ATTN_SCALE
        valid = (sel <= t[None, :, None])[:, :, None, :]
        logits = jnp.where(valid, logits, -jnp.inf)
        probs = jax.nn.softmax(logits, axis=-1).astype(kv.dtype)
        o = jnp.einsum('bqhk,bqkc->bqhc', probs, kv)
        o = jnp.einsum('bqhc,hcv->bqhv', o, w_uv)
        return o.reshape(b, Q_BLOCK, N_HEADS * D_VHEAD)

    out = lax.map(one_block, jnp.arange(n_blocks))
    return jnp.transpose(out, (1, 0, 2, 3)).reshape(b, s, N_HEADS * D_VHEAD)


def setup_inputs(seed: int = 0) -> dict:
    key = jax.random.key(seed)
    ks = jax.random.split(key, 24)
    f32 = jnp.float32

    def nrm(k, shape, scale):
        return jax.random.normal(k, shape, f32) * scale

    def gain(k, shape):
        return 1.0 + 0.01 * jax.random.normal(k, shape, f32)

    L = DEPTH
    return {
        "x": nrm(ks[0], (BATCH, SEQ, D_MODEL), 1.0),
        "p": nrm(ks[1], (DEPTH, BATCH, SEQ, D_PLE), 1.0),
        "w_in": nrm(ks[2], (L, D_MODEL, D_IN_TOTAL), D_MODEL ** -0.5),
        "conv_w": nrm(ks[3], (L, CONV_WIDTH, D_CONV), CONV_WIDTH ** -0.5),
        "w_conv_out": nrm(ks[4], (L, D_CONV, D_MODEL), BETA * D_CONV ** -0.5),
        "g_kv": gain(ks[5], (L, D_LATENT)),
        "w_uv": nrm(ks[6], (L, N_HEADS, D_LATENT, D_VHEAD), BETA * D_LATENT ** -0.5),
        "w_o": nrm(ks[7], (L, D_MODEL, D_MODEL), BETA * D_MODEL ** -0.5),
        "ln1_g": gain(ks[8], (L, D_MODEL)),
        "ln1_b": nrm(ks[9], (L, D_MODEL), 0.01),
        "w_up": nrm(ks[10], (L, D_MODEL, D_FF), D_MODEL ** -0.5),
        "w_down": nrm(ks[11], (L, D_FF, D_MODEL), BETA * D_FF ** -0.5),
        "ln2_g": gain(ks[12], (L, D_MODEL)),
        "ln2_b": nrm(ks[13], (L, D_MODEL), 0.01),
        "w_ple_gate": nrm(ks[14], (L, D_MODEL, D_MODEL), D_MODEL ** -0.5),
        "w_ple": nrm(ks[15], (L, D_PLE, D_MODEL), BETA * D_PLE ** -0.5),
        "ln3_g": gain(ks[16], (L, D_MODEL)),
        "ln3_b": nrm(ks[17], (L, D_MODEL), 0.01),
    }


def reference(x, p, w_in, conv_w, w_conv_out, g_kv, w_uv, w_o, ln1_g, ln1_b,
              w_up, w_down, ln2_g, ln2_b, w_ple_gate, w_ple, ln3_g, ln3_b):
    b, s, _ = x.shape
    h = x
    for i in range(DEPTH):
        proj = jnp.einsum('bsd,dn->bsn', h, w_in[i])
        (b_gate, c_gate, u, q_lat, c_kv, q_idx, k_idx, w_idx,
         g_logits) = jnp.split(proj, SPLIT_POINTS, axis=-1)
        v = causal_depthwise_conv(c_gate * u, conv_w[i])
        y_conv = jnp.einsum('bsc,cd->bsd', b_gate * v, w_conv_out[i])
        y_attn = dsa_attention(
            q_lat.reshape(b, s, N_HEADS, D_LATENT),
            rms_norm(c_kv, g_kv[i]),
            q_idx.reshape(b, s, N_IDX_HEADS, D_IDX),
            k_idx,
            w_idx * IDX_SCALE,
            w_uv[i])
        g_conv, g_attn = jnp.split(jax.nn.sigmoid(g_logits), N_BRANCH, axis=-1)
        mixed = g_conv * y_conv + g_attn * y_attn
        h = layer_norm(ALPHA * h + jnp.einsum('bsd,de->bse', mixed, w_o[i]), ln1_g[i], ln1_b[i])
        ff = jnp.einsum('bsf,fd->bsd', jnp.square(jax.nn.relu(jnp.einsum('bsd,df->bsf', h, w_up[i]))), w_down[i])
        h = layer_norm(ALPHA * h + ff, ln2_g[i], ln2_b[i])
        pe = jax.nn.sigmoid(jnp.einsum('bsd,de->bse', h, w_ple_gate[i])) * jnp.einsum('bsp,pd->bsd', p[i], w_ple[i])
        h = layer_norm(ALPHA * h + pe, ln3_g[i], ln3_b[i])
    return h
```

```python
import functools

import jax
import jax.numpy as jnp
from jax import lax
from jax.experimental import pallas as pl
from jax.experimental.pallas import tpu as pltpu

F32 = jnp.float32
BF16 = jnp.bfloat16
I32 = jnp.int32

N_HEADS = 16
D_LATENT = 256
D_VHEAD = 128
N_IDX_HEADS = 16
D_IDX = 64
TOPK_MAX = 256
CONV_WIDTH = 3
LN_EPS = 1e-5
RMS_EPS = 1e-6
ATTN_SCALE = D_LATENT ** -0.5
IDX_SCALE = (D_IDX ** -0.5) * (N_IDX_HEADS ** -0.5)

VMEM_LIMIT_BYTES = 56 * 1024 * 1024
SUBLANES = 8
LANES = 128

INT_MIN = -(2 ** 31)
KEY_NEG_INF = INT_MIN + 0x7FFFFF
NEG_BIG = -1e30
LOG2E = 1.4426950408889634


def _params(*sem):
    return pltpu.CompilerParams(dimension_semantics=sem, vmem_limit_bytes=VMEM_LIMIT_BYTES)


def _dot(a, b):
    return jnp.dot(a, b, preferred_element_type=F32)


def _sigmoid(v):
    return 1.0 / (1.0 + jnp.exp(-v))


def _layer_norm(v, g, b):
    mu = jnp.mean(v, axis=-1, keepdims=True)
    c = v - mu
    var = jnp.mean(c * c, axis=-1, keepdims=True)
    return c * lax.rsqrt(var + LN_EPS) * g + b


def _split_hi_lo(v):
    hi = v.astype(BF16)
    lo = (v - hi.astype(F32)).astype(BF16)
    return hi, lo


def _conv_kernel(x_ref, wb_ref, wc_ref, wu_ref, cw_ref, z_ref, cu_ref):
    i = pl.program_id(1)
    tm = x_ref.shape[0]
    halo = SUBLANES

    @pl.when(i == 0)
    def _():
        cu_ref[0:halo, :] = jnp.zeros((halo, cu_ref.shape[1]), F32)

    @pl.when(i > 0)
    def _():
        cu_ref[0:halo, :] = cu_ref[tm:tm + halo, :]

    x = x_ref[...]
    cu_ref[halo:halo + tm, :] = _dot(x, wc_ref[...]) * _dot(x, wu_ref[...])
    cw = cw_ref[...]
    v = (cw[0:1, :] * cu_ref[halo - 2:halo - 2 + tm, :]
         + cw[1:2, :] * cu_ref[halo - 1:halo - 1 + tm, :]
         + cw[2:3, :] * cu_ref[halo:halo + tm, :])
    z_ref[...] = (_dot(x, wb_ref[...]) * v).astype(z_ref.dtype)


def _conv_branch(x_bf, w_in_bf, conv_w, d_conv, tm=512, tn=512):
    s, d = x_bf.shape
    nj = d_conv // tn
    return pl.pallas_call(
        _conv_kernel,
        grid=(nj, s // tm),
        in_specs=[
            pl.BlockSpec((tm, d), lambda j, i: (i, 0)),
            pl.BlockSpec((d, tn), lambda j, i: (0, j)),
            pl.BlockSpec((d, tn), lambda j, i: (0, nj + j)),
            pl.BlockSpec((d, tn), lambda j, i: (0, 2 * nj + j)),
            pl.BlockSpec((CONV_WIDTH, tn), lambda j, i: (0, j)),
        ],
        out_specs=pl.BlockSpec((tm, tn), lambda j, i: (i, j)),
        out_shape=jax.ShapeDtypeStruct((s, d_conv), BF16),
        scratch_shapes=[pltpu.VMEM((tm + SUBLANES, tn), F32)],
        compiler_params=_params("parallel", "arbitrary"),
        name="conv_branch",
    )(x_bf, w_in_bf, w_in_bf, w_in_bf, conv_w)


def _qlat_kernel(wt_ref, xt_ref, q_ref):
    acc = _dot(wt_ref[...], xt_ref[...]) * (ATTN_SCALE * LOG2E)
    q_ref[...] = acc.astype(q_ref.dtype).reshape(q_ref.shape)


def _qlat_proj(xt_bf, wq_t, tm=512, heads_per_step=4):
    d, s = xt_bf.shape
    tn = heads_per_step * D_LATENT
    return pl.pallas_call(
        _qlat_kernel,
        grid=(N_HEADS // heads_per_step, s // tm),
        in_specs=[
            pl.BlockSpec((tn, d), lambda j, i: (j, 0)),
            pl.BlockSpec((d, tm), lambda j, i: (0, i)),
        ],
        out_specs=pl.BlockSpec((heads_per_step, D_LATENT, tm), lambda j, i: (j, 0, i)),
        out_shape=jax.ShapeDtypeStruct((N_HEADS, D_LATENT, s), BF16),
        compiler_params=_params("parallel", "parallel"),
        name="qlat_proj",
    )(wq_t, xt_bf)


def _idx_kernel(x_ref, xt_ref, w_ref, wt_ref, g_ref, ckv_ref, kw_ref, qit_ref, wit_ref):
    acc = _dot(x_ref[...], w_ref[...])
    c = acc[:, :D_LATENT]
    ms = jnp.mean(c * c, axis=-1, keepdims=True)
    ckv_ref[...] = (c * lax.rsqrt(ms + RMS_EPS) * g_ref[...]).astype(ckv_ref.dtype)
    kw_ref[...] = acc[:, D_LATENT:]
    acc_t = _dot(wt_ref[...], xt_ref[...])
    nq = N_IDX_HEADS * D_IDX
    q = acc_t[:nq].reshape(N_IDX_HEADS, D_IDX, acc_t.shape[1])
    hi, lo = _split_hi_lo(q)
    qit_ref[...] = jnp.concatenate([hi, lo, hi], axis=1)
    wit_ref[...] = acc_t[nq:] * IDX_SCALE


def _idx_proj(x_bf, xt_bf, w_small, w_small_t, g_kv, tm=512):
    s, d = x_bf.shape
    n = w_small.shape[1]
    nt = w_small_t.shape[0]
    return pl.pallas_call(
        _idx_kernel,
        grid=(s // tm,),
        in_specs=[
            pl.BlockSpec((tm, d), lambda i: (i, 0)),
            pl.BlockSpec((d, tm), lambda i: (0, i)),
            pl.BlockSpec((d, n), lambda i: (0, 0)),
            pl.BlockSpec((nt, d), lambda i: (0, 0)),
            pl.BlockSpec((1, D_LATENT), lambda i: (0, 0)),
        ],
        out_specs=[
            pl.BlockSpec((tm, D_LATENT), lambda i: (i, 0)),
            pl.BlockSpec((tm, n - D_LATENT), lambda i: (i, 0)),
            pl.BlockSpec((N_IDX_HEADS, 3 * D_IDX, tm), lambda i: (0, 0, i)),
            pl.BlockSpec((N_IDX_HEADS, tm), lambda i: (0, i)),
        ],
        out_shape=[
            jax.ShapeDtypeStruct((s, D_LATENT), BF16),
            jax.ShapeDtypeStruct((s, n - D_LATENT), F32),
            jax.ShapeDtypeStruct((N_IDX_HEADS, 3 * D_IDX, s), BF16),
            jax.ShapeDtypeStruct((N_IDX_HEADS, s), F32),
        ],
        compiler_params=_params("parallel"),
        name="idx_proj",
    )(x_bf, xt_bf, w_small, w_small_t, g_kv)


def _gate_kernel(x_ref, w_ref, o_ref):
    o_ref[...] = _sigmoid(_dot(x_ref[...], w_ref[...]))


def _gate_proj(x_bf, w_g, tm=512, tn=1024):
    s, d = x_bf.shape
    n = w_g.shape[1]
    return pl.pallas_call(
        _gate_kernel,
        grid=(n // tn, s // tm),
        in_specs=[
            pl.BlockSpec((tm, d), lambda j, i: (i, 0)),
            pl.BlockSpec((d, tn), lambda j, i: (0, j)),
        ],
        out_specs=pl.BlockSpec((tm, tn), lambda j, i: (i, j)),
        out_shape=jax.ShapeDtypeStruct((s, n), F32),
        compiler_params=_params("parallel", "parallel"),
        name="gate_proj",
    )(x_bf, w_g)


def _key_to_f32(key):
    return pltpu.bitcast(key ^ ((key >> 31) & 0x7FFFFFFF), F32)


def _attn_kernel(qit_ref, wit_ref, kidx_ref, qt_ref, kv_ref, kvt_ref, wuv_ref, y_ref,
                 score_ref, qi_s, q_s, w_s, m_ref, l_ref, acc_ref, *, topk):
    b = pl.program_id(0)
    n_heads, d_lat, tq = qt_ref.shape
    tk = kidx_ref.shape[1]
    n_groups, _, gw = q_s.shape
    hg = gw // tq
    n_kt = ((b + 1) * tq + tk - 1) // tk
    kpos = lax.broadcasted_iota(I32, (tk, tq), 0)
    qpos = b * tq + lax.broadcasted_iota(I32, (tk, tq), 1)

    for h in range(n_heads):
        g, c = divmod(h, hg)
        qi_s[g, :, c * tq:(c + 1) * tq] = qit_ref[h]
        q_s[g, :, c * tq:(c + 1) * tq] = qt_ref[h]
        w_s[g, :, c * tq:(c + 1) * tq] = wit_ref[h:h + 1, :]

    def score_tile(j, carry):
        kidx = kidx_ref[j]
        score = jnp.zeros((tk, tq), F32)
        for g in range(n_groups):
            rel = jnp.maximum(_dot(kidx, qi_s[g]), 0.0) * w_s[g]
            for c in range(hg):
                score = score + rel[:, c * tq:(c + 1) * tq]
        score_ref[j] = jnp.where(kpos + j * tk <= qpos, score, -jnp.inf)
        return carry

    lax.fori_loop(0, n_kt, score_tile, 0)

    fold = 8 * SUBLANES

    def count_ge(cand_key):
        cand = _key_to_f32(jnp.maximum(cand_key, KEY_NEG_INF))

        def body(j, cnt):
            hit = jnp.where(score_ref[j] >= cand, 1.0, 0.0)
            return cnt + jnp.sum(hit.reshape(tk // fold, fold, tq), axis=0)

        cnt = lax.fori_loop(0, n_kt, body, jnp.zeros((fold, tq), F32))
        return jnp.sum(cnt, axis=0, keepdims=True)

    kf = float(topk)
    thr_key = jnp.where(count_ge(jnp.zeros((1, tq), I32)) >= kf, 0, INT_MIN).astype(I32)

    def bit_step(t, thr_key):
        cand_key = thr_key | jnp.left_shift(jnp.int32(1), 30 - t)
        return jnp.where(count_ge(cand_key) >= kf, cand_key, thr_key)

    thr_key = lax.fori_loop(0, 31, bit_step, thr_key)
    thr = _key_to_f32(jnp.maximum(thr_key, KEY_NEG_INF))

    m_ref[...] = jnp.full(m_ref.shape, NEG_BIG, F32)
    l_ref[...] = jnp.zeros(l_ref.shape, F32)
    acc_ref[...] = jnp.zeros(acc_ref.shape, F32)

    def attn_tile(j, carry):
        keep = (score_ref[j] >= thr) & (kpos + j * tk <= qpos)
        bias = jnp.where(keep, 0.0, NEG_BIG)
        bias = jnp.concatenate([bias] * hg, axis=1)
        kv = kv_ref[j]
        kvt = kvt_ref[j]
        for g in range(n_groups):
            s = _dot(kv, q_s[g]) + bias
            m_old = m_ref[g]
            m_new = jnp.maximum(m_old, jnp.max(s, axis=0, keepdims=True))
            p = jnp.exp2(s - m_new)
            alpha = jnp.exp2(m_old - m_new)
            l_ref[g] = alpha * l_ref[g] + jnp.sum(p, axis=0, keepdims=True)
            acc_ref[g] = alpha * acc_ref[g] + _dot(kvt, p.astype(kvt.dtype))
            m_ref[g] = m_new
        return carry

    lax.fori_loop(0, n_kt, attn_tile, 0)

    for h in range(n_heads):
        g, c = divmod(h, hg)
        lanes = slice(c * tq, (c + 1) * tq)
        o = (acc_ref[g, :, lanes] / l_ref[g, :, lanes]).T
        y_ref[:, h * D_VHEAD:(h + 1) * D_VHEAD] = _dot(o.astype(wuv_ref.dtype), wuv_ref[h])


def _dsa_attention(qi_t, wi_t, kidx, q_t, kv, kvt, w_uv_bf, topk, tq=128, heads_per_group=4):
    n_heads, d_lat, s = q_t.shape
    n_kt, tk, dk = kidx.shape
    n_groups = n_heads // heads_per_group
    gw = heads_per_group * tq
    const3 = lambda b: (0, 0, 0)
    return pl.pallas_call(
        functools.partial(_attn_kernel, topk=topk),
        grid=(s // tq,),
        in_specs=[
            pl.BlockSpec((N_IDX_HEADS, dk, tq), lambda b: (0, 0, b)),
            pl.BlockSpec((N_IDX_HEADS, tq), lambda b: (0, b)),
            pl.BlockSpec((n_kt, tk, dk), const3),
            pl.BlockSpec((n_heads, d_lat, tq), lambda b: (0, 0, b)),
            pl.BlockSpec((n_kt, tk, d_lat), const3),
            pl.BlockSpec((n_kt, d_lat, tk), const3),
            pl.BlockSpec((n_heads, d_lat, D_VHEAD), const3),
        ],
        out_specs=pl.BlockSpec((tq, n_heads * D_VHEAD), lambda b: (b, 0)),
        out_shape=jax.ShapeDtypeStruct((s, n_heads * D_VHEAD), F32),
        scratch_shapes=[
            pltpu.VMEM((n_kt, tk, tq), F32),
            pltpu.VMEM((n_groups, dk, gw), BF16),
            pltpu.VMEM((n_groups, d_lat, gw), BF16),
            pltpu.VMEM((n_groups, 1, gw), F32),
            pltpu.VMEM((n_groups, 1, gw), F32),
            pltpu.VMEM((n_groups, 1, gw), F32),
            pltpu.VMEM((n_groups, d_lat, gw), F32),
        ],
        compiler_params=_params("parallel"),
        name="dsa_attention",
    )(qi_t, wi_t, kidx, q_t, kv, kvt, w_uv_bf)


def _merge_kernel(z_ref, w_ref, gc_ref, ga_ref, ya_ref, o_ref):
    y_conv = _dot(z_ref[...], w_ref[...])
    o_ref[...] = (gc_ref[...] * y_conv + ga_ref[...] * ya_ref[...]).astype(o_ref.dtype)


def _merge(z, w_co, gates, y_attn, tm=512, tn=1024):
    s, d = z.shape
    n = w_co.shape[1]
    nj = n // tn
    return pl.pallas_call(
        _merge_kernel,
        grid=(nj, s // tm),
        in_specs=[
            pl.BlockSpec((tm, d), lambda j, i: (i, 0)),
            pl.BlockSpec((d, tn), lambda j, i: (0, j)),
            pl.BlockSpec((tm, tn), lambda j, i: (i, j)),
            pl.BlockSpec((tm, tn), lambda j, i: (i, nj + j)),
            pl.BlockSpec((tm, tn), lambda j, i: (i, j)),
        ],
        out_specs=pl.BlockSpec((tm, tn), lambda j, i: (i, j)),
        out_shape=jax.ShapeDtypeStruct((s, n), BF16),
        compiler_params=_params("parallel", "parallel"),
        name="merge",
    )(z, w_co, gates, gates, y_attn)


def _oproj_kernel(m_ref, w_ref, h_ref, g_ref, b_ref, o_ref, obf_ref, *, alpha):
    v = alpha * h_ref[...] + _dot(m_ref[...], w_ref[...])
    out = _layer_norm(v, g_ref[...], b_ref[...])
    o_ref[...] = out
    obf_ref[...] = out.astype(obf_ref.dtype)


def _oproj(mixed, w_o, h, g, b, alpha, tm=256):
    s, d = h.shape
    row = pl.BlockSpec((tm, d), lambda i: (i, 0))
    vec = pl.BlockSpec((1, d), lambda i: (0, 0))
    return pl.pallas_call(
        functools.partial(_oproj_kernel, alpha=alpha),
        grid=(s // tm,),
        in_specs=[row, pl.BlockSpec((d, d), lambda i: (0, 0)), row, vec, vec],
        out_specs=[row, row],
        out_shape=[jax.ShapeDtypeStruct((s, d), F32), jax.ShapeDtypeStruct((s, d), BF16)],
        compiler_params=_params("parallel"),
        name="oproj_ln",
    )(mixed, w_o, h, g, b)


def _ffn_kernel(hbf_ref, wu_ref, wd_ref, h_ref, g_ref, b_ref, o_ref, obf_ref, acc_ref, *, alpha):
    f = pl.program_id(1)

    @pl.when(f == 0)
    def _():
        acc_ref[...] = jnp.zeros(acc_ref.shape, F32)

    a = jnp.maximum(_dot(hbf_ref[...], wu_ref[...]), 0.0)
    acc_ref[...] += _dot((a * a).astype(wd_ref.dtype), wd_ref[...])

    @pl.when(f == pl.num_programs(1) - 1)
    def _():
        out = _layer_norm(alpha * h_ref[...] + acc_ref[...], g_ref[...], b_ref[...])
        o_ref[...] = out
        obf_ref[...] = out.astype(obf_ref.dtype)


def _ffn(h_bf, w_up, w_down, h, g, b, alpha, tm=512, tf=512):
    s, d = h.shape
    dff = w_up.shape[1]
    row = pl.BlockSpec((tm, d), lambda i, f: (i, 0))
    vec = pl.BlockSpec((1, d), lambda i, f: (0, 0))
    return pl.pallas_call(
        functools.partial(_ffn_kernel, alpha=alpha),
        grid=(s // tm, dff // tf),
        in_specs=[row, pl.BlockSpec((d, tf), lambda i, f: (0, f)),
                  pl.BlockSpec((tf, d), lambda i, f: (f, 0)), row, vec, vec],
        out_specs=[row, row],
        out_shape=[jax.ShapeDtypeStruct((s, d), F32), jax.ShapeDtypeStruct((s, d), BF16)],
        scratch_shapes=[pltpu.VMEM((tm, d), F32)],
        compiler_params=_params("parallel", "arbitrary"),
        name="ffn_ln",
    )(h_bf, w_up, w_down, h, g, b)


def _ple_kernel(hbf_ref, wg_ref, p_ref, wp_ref, h_ref, g_ref, b_ref, o_ref, *, alpha):
    gate = _sigmoid(_dot(hbf_ref[...], wg_ref[...]))
    pe = gate * _dot(p_ref[...], wp_ref[...])
    o_ref[...] = _layer_norm(alpha * h_ref[...] + pe, g_ref[...], b_ref[...])


def _ple(h_bf, w_pg, p_bf, w_p, h, g, b, alpha, tm=256):
    s, d = h.shape
    dp = p_bf.shape[1]
    row = pl.BlockSpec((tm, d), lambda i: (i, 0))
    vec = pl.BlockSpec((1, d), lambda i: (0, 0))
    return pl.pallas_call(
        functools.partial(_ple_kernel, alpha=alpha),
        grid=(s // tm,),
        in_specs=[row, pl.BlockSpec((d, d), lambda i: (0, 0)),
                  pl.BlockSpec((tm, dp), lambda i: (i, 0)),
                  pl.BlockSpec((dp, d), lambda i: (0, 0)), row, vec, vec],
        out_specs=row,
        out_shape=jax.ShapeDtypeStruct((s, d), F32),
        compiler_params=_params("parallel"),
        name="ple_ln",
    )(h_bf, w_pg, p_bf, w_p, h, g, b)


def _layer(h, p, w_in, conv_w, w_conv_out, g_kv, w_uv, w_o, ln1_g, ln1_b, w_up, w_down,
           ln2_g, ln2_b, w_ple_gate, w_ple, ln3_g, ln3_b, alpha, tk=512):
    s, d = h.shape
    d_conv = conv_w.shape[1]
    nq = N_IDX_HEADS * D_IDX
    c_qlat = 3 * d_conv
    c_ckv = c_qlat + N_HEADS * D_LATENT
    c_qidx = c_ckv + D_LATENT
    c_kidx = c_qidx + nq
    c_widx = c_kidx + D_IDX
    c_gates = c_widx + N_IDX_HEADS
    c_kw_end = -(-c_gates // LANES) * LANES
    topk = min(TOPK_MAX, s // 4)
    n_kt = s // tk

    h_bf = h.astype(BF16)
    w_in_bf = w_in.astype(BF16)

    z = _conv_branch(h_bf, w_in_bf, conv_w, d_conv)
    ht_bf = jnp.transpose(h_bf)
    q_t = _qlat_proj(ht_bf, jnp.transpose(w_in_bf[:, c_qlat:c_ckv]))
    w_small = jnp.concatenate([w_in_bf[:, c_ckv:c_qidx], w_in_bf[:, c_kidx:c_kw_end]], axis=1)
    w_small_t = jnp.transpose(jnp.concatenate(
        [w_in_bf[:, c_qidx:c_kidx], w_in_bf[:, c_widx:c_gates]], axis=1))
    ckv, kw, qi_t, wi_t = _idx_proj(h_bf, ht_bf, w_small, w_small_t, g_kv.reshape(1, D_LATENT))
    gates = _gate_proj(h_bf, w_in_bf[:, c_gates:])

    k_hi, k_lo = _split_hi_lo(kw[:, :D_IDX])
    kidx = jnp.concatenate([k_hi, k_hi, k_lo], axis=-1).reshape(n_kt, tk, 3 * D_IDX)
    kv = ckv.reshape(n_kt, tk, D_LATENT)
    kvt = jnp.transpose(kv, (0, 2, 1))

    y_attn = _dsa_attention(qi_t, wi_t, kidx, q_t, kv, kvt, w_uv.astype(BF16), topk)

    mixed = _merge(z, w_conv_out.astype(BF16), gates, y_attn)
    h1, h1_bf = _oproj(mixed, w_o.astype(BF16), h, ln1_g.reshape(1, d), ln1_b.reshape(1, d), alpha)
    h2, h2_bf = _ffn(h1_bf, w_up.astype(BF16), w_down.astype(BF16), h1,
                     ln2_g.reshape(1, d), ln2_b.reshape(1, d), alpha)
    return _ple(h2_bf, w_ple_gate.astype(BF16), p.astype(BF16), w_ple.astype(BF16), h2,
                ln3_g.reshape(1, d), ln3_b.reshape(1, d), alpha)


def kernel(x, p, w_in, conv_w, w_conv_out, g_kv, w_uv, w_o, ln1_g, ln1_b, w_up, w_down,
           ln2_g, ln2_b, w_ple_gate, w_ple, ln3_g, ln3_b):
    depth = w_in.shape[0]
    alpha = (2.0 * depth) ** 0.25
    batch = x.shape[0]
    outs = []
    for bi in range(batch):
        h = x[bi]
        for i in range(depth):
            h = _layer(h, p[i, bi], w_in[i], conv_w[i], w_conv_out[i], g_kv[i], w_uv[i], w_o[i],
                       ln1_g[i], ln1_b[i], w_up[i], w_down[i], ln2_g[i], ln2_b[i],
                       w_ple_gate[i], w_ple[i], ln3_g[i], ln3_b[i], alpha)
        outs.append(h)
    return jnp.stack(outs, axis=0)
```

```python
import functools

import jax
import jax.numpy as jnp
from jax import lax
from jax.experimental import pallas as pl
from jax.experimental.pallas import tpu as pltpu

F32 = jnp.float32
BF16 = jnp.bfloat16
I32 = jnp.int32

N_HEADS = 16
D_LATENT = 256
D_VHEAD = 128
N_IDX_HEADS = 16
D_IDX = 64
TOPK_MAX = 256
CONV_WIDTH = 3
LN_EPS = 1e-5
RMS_EPS = 1e-6
ATTN_SCALE = D_LATENT ** -0.5
IDX_SCALE = (D_IDX ** -0.5) * (N_IDX_HEADS ** -0.5)

VMEM_LIMIT_BYTES = 56 * 1024 * 1024
SUBLANES = 8
LANES = 128

INT_MIN = -(2 ** 31)
KEY_NEG_INF = INT_MIN + 0x7FFFFF
NEG_BIG = -1e30
LOG2E = 1.4426950408889634


def _params(*sem):
    return pltpu.CompilerParams(dimension_semantics=sem, vmem_limit_bytes=VMEM_LIMIT_BYTES)


def _dot(a, b):
    return jnp.dot(a, b, preferred_element_type=F32)


def _sigmoid(v):
    return 1.0 / (1.0 + jnp.exp(-v))


def _layer_norm(v, g, b):
    mu = jnp.mean(v, axis=-1, keepdims=True)
    c = v - mu
    var = jnp.mean(c * c, axis=-1, keepdims=True)
    return c * lax.rsqrt(var + LN_EPS) * g + b


def _split_hi_lo(v):
    hi = v.astype(BF16)
    lo = (v - hi.astype(F32)).astype(BF16)
    return hi, lo


def _conv_kernel(x_ref, wb_ref, wc_ref, wu_ref, cw_ref, z_ref, cu_ref):
    i = pl.program_id(1)
    tm = x_ref.shape[0]
    halo = SUBLANES

    @pl.when(i == 0)
    def _():
        cu_ref[0:halo, :] = jnp.zeros((halo, cu_ref.shape[1]), F32)

    @pl.when(i > 0)
    def _():
        cu_ref[0:halo, :] = cu_ref[tm:tm + halo, :]

    x = x_ref[...]
    cu_ref[halo:halo + tm, :] = _dot(x, wc_ref[...]) * _dot(x, wu_ref[...])
    cw = cw_ref[...]
    v = (cw[0:1, :] * cu_ref[halo - 2:halo - 2 + tm, :]
         + cw[1:2, :] * cu_ref[halo - 1:halo - 1 + tm, :]
         + cw[2:3, :] * cu_ref[halo:halo + tm, :])
    z_ref[...] = (_dot(x, wb_ref[...]) * v).astype(z_ref.dtype)


def _conv_branch(x_bf, w_in_bf, conv_w, d_conv, tm=512, tn=512):
    s, d = x_bf.shape
    nj = d_conv // tn
    return pl.pallas_call(
        _conv_kernel,
        grid=(nj, s // tm),
        in_specs=[
            pl.BlockSpec((tm, d), lambda j, i: (i, 0)),
            pl.BlockSpec((d, tn), lambda j, i: (0, j)),
            pl.BlockSpec((d, tn), lambda j, i: (0, nj + j)),
            pl.BlockSpec((d, tn), lambda j, i: (0, 2 * nj + j)),
            pl.BlockSpec((CONV_WIDTH, tn), lambda j, i: (0, j)),
        ],
        out_specs=pl.BlockSpec((tm, tn), lambda j, i: (i, j)),
        out_shape=jax.ShapeDtypeStruct((s, d_conv), BF16),
        scratch_shapes=[pltpu.VMEM((tm + SUBLANES, tn), F32)],
        compiler_params=_params("parallel", "arbitrary"),
        name="conv_branch",
    )(x_bf, w_in_bf, w_in_bf, w_in_bf, conv_w)


def _qlat_kernel(wt_ref, xt_ref, q_ref):
    acc = _dot(wt_ref[...], xt_ref[...]) * (ATTN_SCALE * LOG2E)
    q_ref[...] = acc.astype(q_ref.dtype).reshape(q_ref.shape)


def _qlat_proj(xt_bf, wq_t, tm=512, heads_per_step=4):
    d, s = xt_bf.shape
    tn = heads_per_step * D_LATENT
    return pl.pallas_call(
        _qlat_kernel,
        grid=(N_HEADS // heads_per_step, s // tm),
        in_specs=[
            pl.BlockSpec((tn, d), lambda j, i: (j, 0)),
            pl.BlockSpec((d, tm), lambda j, i: (0, i)),
        ],
        out_specs=pl.BlockSpec((heads_per_step, D_LATENT, tm), lambda j, i: (j, 0, i)),
        out_shape=jax.ShapeDtypeStruct((N_HEADS, D_LATENT, s), BF16),
        compiler_params=_params("parallel", "parallel"),
        name="qlat_proj",
    )(wq_t, xt_bf)


def _idx_kernel(x_ref, xt_ref, w_ref, wt_ref, g_ref, ckv_ref, kw_ref, qit_ref, wit_ref):
    acc = _dot(x_ref[...], w_ref[...])
    c = acc[:, :D_LATENT]
    ms = jnp.mean(c * c, axis=-1, keepdims=True)
    ckv_ref[...] = (c * lax.rsqrt(ms + RMS_EPS) * g_ref[...]).astype(ckv_ref.dtype)
    kw_ref[...] = acc[:, D_LATENT:]
    acc_t = _dot(wt_ref[...], xt_ref[...])
    nq = N_IDX_HEADS * D_IDX
    q = acc_t[:nq].reshape(N_IDX_HEADS, D_IDX, acc_t.shape[1])
    hi, lo = _split_hi_lo(q)
    qit_ref[...] = jnp.concatenate([hi, lo, hi], axis=1)
    wit_ref[...] = acc_t[nq:] * IDX_SCALE


def _idx_proj(x_bf, xt_bf, w_small, w_small_t, g_kv, tm=512):
    s, d = x_bf.shape
    n = w_small.shape[1]
    nt = w_small_t.shape[0]
    return pl.pallas_call(
        _idx_kernel,
        grid=(s // tm,),
        in_specs=[
            pl.BlockSpec((tm, d), lambda i: (i, 0)),
            pl.BlockSpec((d, tm), lambda i: (0, i)),
            pl.BlockSpec((d, n), lambda i: (0, 0)),
            pl.BlockSpec((nt, d), lambda i: (0, 0)),
            pl.BlockSpec((1, D_LATENT), lambda i: (0, 0)),
        ],
        out_specs=[
            pl.BlockSpec((tm, D_LATENT), lambda i: (i, 0)),
            pl.BlockSpec((tm, n - D_LATENT), lambda i: (i, 0)),
            pl.BlockSpec((N_IDX_HEADS, 3 * D_IDX, tm), lambda i: (0, 0, i)),
            pl.BlockSpec((N_IDX_HEADS, tm), lambda i: (0, i)),
        ],
        out_shape=[
            jax.ShapeDtypeStruct((s, D_LATENT), BF16),
            jax.ShapeDtypeStruct((s, n - D_LATENT), F32),
            jax.ShapeDtypeStruct((N_IDX_HEADS, 3 * D_IDX, s), BF16),
            jax.ShapeDtypeStruct((N_IDX_HEADS, s), F32),
        ],
        compiler_params=_params("parallel"),
        name="idx_proj",
    )(x_bf, xt_bf, w_small, w_small_t, g_kv)


def _gate_kernel(x_ref, w_ref, o_ref):
    o_ref[...] = _sigmoid(_dot(x_ref[...], w_ref[...]))


def _gate_proj(x_bf, w_g, tm=512, tn=1024):
    s, d = x_bf.shape
    n = w_g.shape[1]
    return pl.pallas_call(
        _gate_kernel,
        grid=(n // tn, s // tm),
        in_specs=[
            pl.BlockSpec((tm, d), lambda j, i: (i, 0)),
            pl.BlockSpec((d, tn), lambda j, i: (0, j)),
        ],
        out_specs=pl.BlockSpec((tm, tn), lambda j, i: (i, j)),
        out_shape=jax.ShapeDtypeStruct((s, n), F32),
        compiler_params=_params("parallel", "parallel"),
        name="gate_proj",
    )(x_bf, w_g)


def _key_to_f32(key):
    return pltpu.bitcast(key ^ ((key >> 31) & 0x7FFFFFFF), F32)


def _attn_kernel(qit_ref, wit_ref, kidx_ref, qt_ref, kv_ref, kvt_ref, wuv_ref, y_ref,
                 score_ref, qi_s, q_s, w_s, m_ref, l_ref, acc_ref, s_buf, p_buf, a_buf, *, topk):
    b = pl.program_id(0)
    n_heads, d_lat, tq = qt_ref.shape
    tk = kidx_ref.shape[1]
    n_groups, _, gw = q_s.shape
    hg = gw // tq
    n_kt = ((b + 1) * tq + tk - 1) // tk
    kpos = lax.broadcasted_iota(I32, (tk, tq), 0)
    qpos = b * tq + lax.broadcasted_iota(I32, (tk, tq), 1)

    for h in range(n_heads):
        g, c = divmod(h, hg)
        qi_s[g, :, c * tq:(c + 1) * tq] = qit_ref[h]
        q_s[g, :, c * tq:(c + 1) * tq] = qt_ref[h]
        w_s[g, :, c * tq:(c + 1) * tq] = wit_ref[h:h + 1, :]

    def score_tile(j, carry):
        kidx = kidx_ref[j]
        score = jnp.zeros((tk, tq), F32)
        for g in range(n_groups):
            rel = jnp.maximum(_dot(kidx, qi_s[g]), 0.0) * w_s[g]
            for c in range(hg):
                score = score + rel[:, c * tq:(c + 1) * tq]
        score_ref[j] = jnp.where(kpos + j * tk <= qpos, score, -jnp.inf)
        return carry

    lax.fori_loop(0, n_kt, score_tile, 0)

    fold = 8 * SUBLANES

    def count_ge(cand_key):
        cand = _key_to_f32(jnp.maximum(cand_key, KEY_NEG_INF))

        def body(j, cnt):
            hit = jnp.where(score_ref[j] >= cand, 1.0, 0.0)
            return cnt + jnp.sum(hit.reshape(tk // fold, fold, tq), axis=0)

        cnt = lax.fori_loop(0, n_kt, body, jnp.zeros((fold, tq), F32))
        return jnp.sum(cnt, axis=0, keepdims=True)

    kf = float(topk)
    thr_key = jnp.where(count_ge(jnp.zeros((1, tq), I32)) >= kf, 0, INT_MIN).astype(I32)

    def bit_step(t, thr_key):
        cand_key = thr_key | jnp.left_shift(jnp.int32(1), 30 - t)
        return jnp.where(count_ge(cand_key) >= kf, cand_key, thr_key)

    thr_key = lax.fori_loop(0, 31, bit_step, thr_key)
    thr = _key_to_f32(jnp.maximum(thr_key, KEY_NEG_INF))

    def mask_tile(j, carry):
        keep = (score_ref[j] >= thr) & (kpos + j * tk <= qpos)
        score_ref[j] = jnp.where(keep, 0.0, NEG_BIG)
        return carry

    lax.fori_loop(0, n_kt, mask_tile, 0)

    m_ref[...] = jnp.full(m_ref.shape, NEG_BIG, F32)
    l_ref[...] = jnp.zeros(l_ref.shape, F32)
    acc_ref[...] = jnp.zeros(acc_ref.shape, F32)
    p_buf[...] = jnp.zeros(p_buf.shape, p_buf.dtype)
    a_buf[...] = jnp.ones(a_buf.shape, F32)

    def qk(j, g):
        bias = jnp.concatenate([score_ref[j]] * hg, axis=1)
        s_buf[g % 2] = _dot(kv_ref[j], q_s[g]) + bias

    def softmax(g):
        s = s_buf[g % 2]
        m_old = m_ref[g]
        m_new = jnp.maximum(m_old, jnp.max(s, axis=0, keepdims=True))
        p = jnp.exp2(s - m_new)
        alpha = jnp.exp2(m_old - m_new)
        l_ref[g] = alpha * l_ref[g] + jnp.sum(p, axis=0, keepdims=True)
        a_buf[g % 2] = alpha
        p_buf[g % 2] = p.astype(p_buf.dtype)
        m_ref[g] = m_new

    def pv(j, g):
        acc_ref[g] = a_buf[g % 2] * acc_ref[g] + _dot(kvt_ref[j], p_buf[g % 2])

    last = n_groups - 1
    qk(0, 0)

    def attn_tile(j, carry):
        for g in range(n_groups):
            if g < last:
                qk(j, g + 1)
            else:
                qk(jnp.minimum(j + 1, n_kt - 1), 0)
            if g > 0:
                pv(j, g - 1)
            else:
                pv(jnp.maximum(j - 1, 0), last)
            softmax(g)
        return carry

    lax.fori_loop(0, n_kt, attn_tile, 0)
    pv(n_kt - 1, last)

    for h in range(n_heads):
        g, c = divmod(h, hg)
        lanes = slice(c * tq, (c + 1) * tq)
        o = (acc_ref[g, :, lanes] / l_ref[g, :, lanes]).T
        y_ref[:, h * D_VHEAD:(h + 1) * D_VHEAD] = _dot(o.astype(wuv_ref.dtype), wuv_ref[h])


def _dsa_attention(qi_t, wi_t, kidx, q_t, kv, kvt, w_uv_bf, topk, tq=128, heads_per_group=2):
    n_heads, d_lat, s = q_t.shape
    n_kt, tk, dk = kidx.shape
    n_groups = n_heads // heads_per_group
    gw = heads_per_group * tq
    const3 = lambda b: (0, 0, 0)
    return pl.pallas_call(
        functools.partial(_attn_kernel, topk=topk),
        grid=(s // tq,),
        in_specs=[
            pl.BlockSpec((N_IDX_HEADS, dk, tq), lambda b: (0, 0, b)),
            pl.BlockSpec((N_IDX_HEADS, tq), lambda b: (0, b)),
            pl.BlockSpec((n_kt, tk, dk), const3),
            pl.BlockSpec((n_heads, d_lat, tq), lambda b: (0, 0, b)),
            pl.BlockSpec((n_kt, tk, d_lat), const3),
            pl.BlockSpec((n_kt, d_lat, tk), const3),
            pl.BlockSpec((n_heads, d_lat, D_VHEAD), const3),
        ],
        out_specs=pl.BlockSpec((tq, n_heads * D_VHEAD), lambda b: (b, 0)),
        out_shape=jax.ShapeDtypeStruct((s, n_heads * D_VHEAD), F32),
        scratch_shapes=[
            pltpu.VMEM((n_kt, tk, tq), F32),
            pltpu.VMEM((n_groups, dk, gw), BF16),
            pltpu.VMEM((n_groups, d_lat, gw), BF16),
            pltpu.VMEM((n_groups, 1, gw), F32),
            pltpu.VMEM((n_groups, 1, gw), F32),
            pltpu.VMEM((n_groups, 1, gw), F32),
            pltpu.VMEM((n_groups, d_lat, gw), F32),
            pltpu.VMEM((2, tk, gw), F32),
            pltpu.VMEM((2, tk, gw), BF16),
            pltpu.VMEM((2, 1, gw), F32),
        ],
        compiler_params=_params("parallel"),
        name="dsa_attention",
    )(qi_t, wi_t, kidx, q_t, kv, kvt, w_uv_bf)


def _merge_kernel(z_ref, w_ref, gc_ref, ga_ref, ya_ref, o_ref):
    y_conv = _dot(z_ref[...], w_ref[...])
    o_ref[...] = (gc_ref[...] * y_conv + ga_ref[...] * ya_ref[...]).astype(o_ref.dtype)


def _merge(z, w_co, gates, y_attn, tm=512, tn=1024):
    s, d = z.shape
    n = w_co.shape[1]
    nj = n // tn
    return pl.pallas_call(
        _merge_kernel,
        grid=(nj, s // tm),
        in_specs=[
            pl.BlockSpec((tm, d), lambda j, i: (i, 0)),
            pl.BlockSpec((d, tn), lambda j, i: (0, j)),
            pl.BlockSpec((tm, tn), lambda j, i: (i, j)),
            pl.BlockSpec((tm, tn), lambda j, i: (i, nj + j)),
            pl.BlockSpec((tm, tn), lambda j, i: (i, j)),
        ],
        out_specs=pl.BlockSpec((tm, tn), lambda j, i: (i, j)),
        out_shape=jax.ShapeDtypeStruct((s, n), BF16),
        compiler_params=_params("parallel", "parallel"),
        name="merge",
    )(z, w_co, gates, gates, y_attn)


def _oproj_kernel(m_ref, w_ref, h_ref, g_ref, b_ref, o_ref, obf_ref, *, alpha):
    v = alpha * h_ref[...] + _dot(m_ref[...], w_ref[...])
    out = _layer_norm(v, g_ref[...], b_ref[...])
    o_ref[...] = out
    obf_ref[...] = out.astype(obf_ref.dtype)


def _oproj(mixed, w_o, h, g, b, alpha, tm=256):
    s, d = h.shape
    row = pl.BlockSpec((tm, d), lambda i: (i, 0))
    vec = pl.BlockSpec((1, d), lambda i: (0, 0))
    return pl.pallas_call(
        functools.partial(_oproj_kernel, alpha=alpha),
        grid=(s // tm,),
        in_specs=[row, pl.BlockSpec((d, d), lambda i: (0, 0)), row, vec, vec],
        out_specs=[row, row],
        out_shape=[jax.ShapeDtypeStruct((s, d), F32), jax.ShapeDtypeStruct((s, d), BF16)],
        compiler_params=_params("parallel"),
        name="oproj_ln",
    )(mixed, w_o, h, g, b)


def _ffn_kernel(hbf_ref, wu_ref, wd_ref, h_ref, g_ref, b_ref, o_ref, obf_ref, acc_ref, *, alpha):
    f = pl.program_id(1)

    @pl.when(f == 0)
    def _():
        acc_ref[...] = jnp.zeros(acc_ref.shape, F32)

    a = jnp.maximum(_dot(hbf_ref[...], wu_ref[...]), 0.0)
    acc_ref[...] += _dot((a * a).astype(wd_ref.dtype), wd_ref[...])

    @pl.when(f == pl.num_programs(1) - 1)
    def _():
        out = _layer_norm(alpha * h_ref[...] + acc_ref[...], g_ref[...], b_ref[...])
        o_ref[...] = out
        obf_ref[...] = out.astype(obf_ref.dtype)


def _ffn(h_bf, w_up, w_down, h, g, b, alpha, tm=512, tf=512):
    s, d = h.shape
    dff = w_up.shape[1]
    row = pl.BlockSpec((tm, d), lambda i, f: (i, 0))
    vec = pl.BlockSpec((1, d), lambda i, f: (0, 0))
    return pl.pallas_call(
        functools.partial(_ffn_kernel, alpha=alpha),
        grid=(s // tm, dff // tf),
        in_specs=[row, pl.BlockSpec((d, tf), lambda i, f: (0, f)),
                  pl.BlockSpec((tf, d), lambda i, f: (f, 0)), row, vec, vec],
        out_specs=[row, row],
        out_shape=[jax.ShapeDtypeStruct((s, d), F32), jax.ShapeDtypeStruct((s, d), BF16)],
        scratch_shapes=[pltpu.VMEM((tm, d), F32)],
        compiler_params=_params("parallel", "arbitrary"),
        name="ffn_ln",
    )(h_bf, w_up, w_down, h, g, b)


def _ple_kernel(hbf_ref, wg_ref, p_ref, wp_ref, h_ref, g_ref, b_ref, o_ref, *, alpha):
    gate = _sigmoid(_dot(hbf_ref[...], wg_ref[...]))
    pe = gate * _dot(p_ref[...], wp_ref[...])
    o_ref[...] = _layer_norm(alpha * h_ref[...] + pe, g_ref[...], b_ref[...])


def _ple(h_bf, w_pg, p_bf, w_p, h, g, b, alpha, tm=256):
    s, d = h.shape
    dp = p_bf.shape[1]
    row = pl.BlockSpec((tm, d), lambda i: (i, 0))
    vec = pl.BlockSpec((1, d), lambda i: (0, 0))
    return pl.pallas_call(
        functools.partial(_ple_kernel, alpha=alpha),
        grid=(s // tm,),
        in_specs=[row, pl.BlockSpec((d, d), lambda i: (0, 0)),
                  pl.BlockSpec((tm, dp), lambda i: (i, 0)),
                  pl.BlockSpec((dp, d), lambda i: (0, 0)), row, vec, vec],
        out_specs=row,
        out_shape=jax.ShapeDtypeStruct((s, d), F32),
        compiler_params=_params("parallel"),
        name="ple_ln",
    )(h_bf, w_pg, p_bf, w_p, h, g, b)


def _layer(h, p, w_in, conv_w, w_conv_out, g_kv, w_uv, w_o, ln1_g, ln1_b, w_up, w_down,
           ln2_g, ln2_b, w_ple_gate, w_ple, ln3_g, ln3_b, alpha, tk=512):
    s, d = h.shape
    d_conv = conv_w.shape[1]
    nq = N_IDX_HEADS * D_IDX
    c_qlat = 3 * d_conv
    c_ckv = c_qlat + N_HEADS * D_LATENT
    c_qidx = c_ckv + D_LATENT
    c_kidx = c_qidx + nq
    c_widx = c_kidx + D_IDX
    c_gates = c_widx + N_IDX_HEADS
    c_kw_end = -(-c_gates // LANES) * LANES
    topk = min(TOPK_MAX, s // 4)
    n_kt = s // tk

    h_bf = h.astype(BF16)
    w_in_bf = w_in.astype(BF16)

    z = _conv_branch(h_bf, w_in_bf, conv_w, d_conv)
    ht_bf = jnp.transpose(h_bf)
    q_t = _qlat_proj(ht_bf, jnp.transpose(w_in_bf[:, c_qlat:c_ckv]))
    w_small = jnp.concatenate([w_in_bf[:, c_ckv:c_qidx], w_in_bf[:, c_kidx:c_kw_end]], axis=1)
    w_small_t = jnp.transpose(jnp.concatenate(
        [w_in_bf[:, c_qidx:c_kidx], w_in_bf[:, c_widx:c_gates]], axis=1))
    ckv, kw, qi_t, wi_t = _idx_proj(h_bf, ht_bf, w_small, w_small_t, g_kv.reshape(1, D_LATENT))
    gates = _gate_proj(h_bf, w_in_bf[:, c_gates:])

    k_hi, k_lo = _split_hi_lo(kw[:, :D_IDX])
    kidx = jnp.concatenate([k_hi, k_hi, k_lo], axis=-1).reshape(n_kt, tk, 3 * D_IDX)
    kv = ckv.reshape(n_kt, tk, D_LATENT)
    kvt = jnp.transpose(kv, (0, 2, 1))

    y_attn = _dsa_attention(qi_t, wi_t, kidx, q_t, kv, kvt, w_uv.astype(BF16), topk)

    mixed = _merge(z, w_conv_out.astype(BF16), gates, y_attn)
    h1, h1_bf = _oproj(mixed, w_o.astype(BF16), h, ln1_g.reshape(1, d), ln1_b.reshape(1, d), alpha)
    h2, h2_bf = _ffn(h1_bf, w_up.astype(BF16), w_down.astype(BF16), h1,
                     ln2_g.reshape(1, d), ln2_b.reshape(1, d), alpha)
    return _ple(h2_bf, w_ple_gate.astype(BF16), p.astype(BF16), w_ple.astype(BF16), h2,
                ln3_g.reshape(1, d), ln3_b.reshape(1, d), alpha)


def kernel(x, p, w_in, conv_w, w_conv_out, g_kv, w_uv, w_o, ln1_g, ln1_b, w_up, w_down,
           ln2_g, ln2_b, w_ple_gate, w_ple, ln3_g, ln3_b):
    depth = w_in.shape[0]
    alpha = (2.0 * depth) ** 0.25
    batch = x.shape[0]
    outs = []
    for bi in range(batch):
        h = x[bi]
        for i in range(depth):
            h = _layer(h, p[i, bi], w_in[i], conv_w[i], w_conv_out[i], g_kv[i], w_uv[i], w_o[i],
                       ln1_g[i], ln1_b[i], w_up[i], w_down[i], ln2_g[i], ln2_b[i],
                       w_ple_gate[i], w_ple[i], ln3_g[i], ln3_b[i], alpha)
        outs.append(h)
    return jnp.stack(outs, axis=0)
```

```python
import functools

import jax
import jax.numpy as jnp
from jax import lax
from jax.experimental import pallas as pl
from jax.experimental.pallas import tpu as pltpu

F32 = jnp.float32
BF16 = jnp.bfloat16
I32 = jnp.int32

N_HEADS = 16
D_LATENT = 256
D_VHEAD = 128
N_IDX_HEADS = 16
D_IDX = 64
TOPK_MAX = 256
CONV_WIDTH = 3
LN_EPS = 1e-5
RMS_EPS = 1e-6
ATTN_SCALE = D_LATENT ** -0.5
IDX_SCALE = (D_IDX ** -0.5) * (N_IDX_HEADS ** -0.5)

VMEM_LIMIT_BYTES = 56 * 1024 * 1024
SUBLANES = 8
LANES = 128

INT_MIN = -(2 ** 31)
KEY_NEG_INF = INT_MIN + 0x7FFFFF
NEG_BIG = -1e30
LOG2E = 1.4426950408889634


def _params(*sem):
    return pltpu.CompilerParams(dimension_semantics=sem, vmem_limit_bytes=VMEM_LIMIT_BYTES)


def _dot(a, b):
    return jnp.dot(a, b, preferred_element_type=F32)


def _sigmoid(v):
    return 1.0 / (1.0 + jnp.exp(-v))


def _layer_norm(v, g, b):
    mu = jnp.mean(v, axis=-1, keepdims=True)
    c = v - mu
    var = jnp.mean(c * c, axis=-1, keepdims=True)
    return c * lax.rsqrt(var + LN_EPS) * g + b


def _split_hi_lo(v):
    hi = v.astype(BF16)
    lo = (v - hi.astype(F32)).astype(BF16)
    return hi, lo


def _conv_kernel(x_ref, wb_ref, wc_ref, wu_ref, cw_ref, z_ref, cu_ref, w_s):
    i = pl.program_id(1)
    tm = x_ref.shape[0]
    halo = SUBLANES

    @pl.when(i == 0)
    def _():
        w_s[0] = wb_ref[...].astype(w_s.dtype)
        w_s[1] = wc_ref[...].astype(w_s.dtype)
        w_s[2] = wu_ref[...].astype(w_s.dtype)
        cu_ref[0:halo, :] = jnp.zeros((halo, cu_ref.shape[1]), F32)

    @pl.when(i > 0)
    def _():
        cu_ref[0:halo, :] = cu_ref[tm:tm + halo, :]

    x = x_ref[...]
    cu_ref[halo:halo + tm, :] = _dot(x, w_s[1]) * _dot(x, w_s[2])
    cw = cw_ref[...]
    v = (cw[0:1, :] * cu_ref[halo - 2:halo - 2 + tm, :]
         + cw[1:2, :] * cu_ref[halo - 1:halo - 1 + tm, :]
         + cw[2:3, :] * cu_ref[halo:halo + tm, :])
    z_ref[...] = (_dot(x, w_s[0]) * v).astype(z_ref.dtype)


def _conv_branch(x_bf, w_in, conv_w, d_conv, tm=512, tn=512):
    s, d = x_bf.shape
    nj = d_conv // tn
    return pl.pallas_call(
        _conv_kernel,
        grid=(nj, s // tm),
        in_specs=[
            pl.BlockSpec((tm, d), lambda j, i: (i, 0)),
            pl.BlockSpec((d, tn), lambda j, i: (0, j)),
            pl.BlockSpec((d, tn), lambda j, i: (0, nj + j)),
            pl.BlockSpec((d, tn), lambda j, i: (0, 2 * nj + j)),
            pl.BlockSpec((CONV_WIDTH, tn), lambda j, i: (0, j)),
        ],
        out_specs=pl.BlockSpec((tm, tn), lambda j, i: (i, j)),
        out_shape=jax.ShapeDtypeStruct((s, d_conv), BF16),
        scratch_shapes=[pltpu.VMEM((tm + SUBLANES, tn), F32),
                        pltpu.VMEM((3, d, tn), BF16)],
        compiler_params=_params("parallel", "arbitrary"),
        name="conv_branch",
    )(x_bf, w_in, w_in, w_in, conv_w)


def _qlat_kernel(x_ref, w_ref, q_ref, w_s):
    @pl.when(pl.program_id(1) == 0)
    def _():
        w_s[...] = w_ref[...].astype(w_s.dtype)

    acc = _dot(x_ref[...], w_s[...]) * (ATTN_SCALE * LOG2E)
    q_ref[...] = acc.T.astype(q_ref.dtype).reshape(q_ref.shape)


def _qlat_proj(x_bf, w_in, col0, tm=512, heads_per_step=4):
    s, d = x_bf.shape
    tn = heads_per_step * D_LATENT
    j0 = col0 // tn
    assert j0 * tn == col0
    return pl.pallas_call(
        _qlat_kernel,
        grid=(N_HEADS // heads_per_step, s // tm),
        in_specs=[
            pl.BlockSpec((tm, d), lambda j, i: (i, 0)),
            pl.BlockSpec((d, tn), lambda j, i: (0, j0 + j)),
        ],
        out_specs=pl.BlockSpec((heads_per_step, D_LATENT, tm), lambda j, i: (j, 0, i)),
        out_shape=jax.ShapeDtypeStruct((N_HEADS, D_LATENT, s), BF16),
        scratch_shapes=[pltpu.VMEM((d, tn), BF16)],
        compiler_params=_params("parallel", "arbitrary"),
        name="qlat_proj",
    )(x_bf, w_in)


def _idx_kernel(x_ref, w_ref, g_ref, ckv_ref, kw_ref, qit_ref):
    acc = _dot(x_ref[...], w_ref[...])
    c = acc[:, :D_LATENT]
    ms = jnp.mean(c * c, axis=-1, keepdims=True)
    ckv_ref[...] = (c * lax.rsqrt(ms + RMS_EPS) * g_ref[...]).astype(ckv_ref.dtype)
    nq = N_IDX_HEADS * D_IDX
    kw_ref[...] = acc[:, D_LATENT + nq:]
    q_t = (acc[:, D_LATENT:D_LATENT + nq] * IDX_SCALE).T
    q_t = q_t.reshape(N_IDX_HEADS, D_IDX, q_t.shape[1])
    hi, lo = _split_hi_lo(q_t)
    qit_ref[...] = jnp.concatenate([hi, lo, hi], axis=1)


def _idx_proj(x_bf, w_small, g_kv, tm=512):
    s, d = x_bf.shape
    n = w_small.shape[1]
    nq = N_IDX_HEADS * D_IDX
    nkw = n - D_LATENT - nq
    return pl.pallas_call(
        _idx_kernel,
        grid=(s // tm,),
        in_specs=[
            pl.BlockSpec((tm, d), lambda i: (i, 0)),
            pl.BlockSpec((d, n), lambda i: (0, 0)),
            pl.BlockSpec((1, D_LATENT), lambda i: (0, 0)),
        ],
        out_specs=[
            pl.BlockSpec((tm, D_LATENT), lambda i: (i, 0)),
            pl.BlockSpec((tm, nkw), lambda i: (i, 0)),
            pl.BlockSpec((N_IDX_HEADS, 3 * D_IDX, tm), lambda i: (0, 0, i)),
        ],
        out_shape=[
            jax.ShapeDtypeStruct((s, D_LATENT), BF16),
            jax.ShapeDtypeStruct((s, nkw), F32),
            jax.ShapeDtypeStruct((N_IDX_HEADS, 3 * D_IDX, s), BF16),
        ],
        compiler_params=_params("parallel"),
        name="idx_proj",
    )(x_bf, w_small, g_kv)


def _gate_kernel(x_ref, w_ref, o_ref):
    o_ref[...] = _sigmoid(_dot(x_ref[...], w_ref[...]))


def _gate_proj(x_bf, w_g, tm=512, tn=1024):
    s, d = x_bf.shape
    n = w_g.shape[1]
    return pl.pallas_call(
        _gate_kernel,
        grid=(n // tn, s // tm),
        in_specs=[
            pl.BlockSpec((tm, d), lambda j, i: (i, 0)),
            pl.BlockSpec((d, tn), lambda j, i: (0, j)),
        ],
        out_specs=pl.BlockSpec((tm, tn), lambda j, i: (i, j)),
        out_shape=jax.ShapeDtypeStruct((s, n), F32),
        compiler_params=_params("parallel", "parallel"),
        name="gate_proj",
    )(x_bf, w_g)


def _key_to_f32(key):
    return pltpu.bitcast(key ^ ((key >> 31) & 0x7FFFFFFF), F32)


def _attn_kernel(qit_ref, wit_ref, kidx_ref, qt_ref, kv_ref, kvt_ref, wuv_ref, y_ref,
                 score_ref, qi_s, q_s, w_s, m_ref, l_ref, acc_ref, s_buf, p_buf, a_buf, *, topk):
    b = pl.program_id(0)
    n_heads, d_lat, tq = qt_ref.shape
    tk = kidx_ref.shape[1]
    n_groups, _, gw = q_s.shape
    hg = gw // tq
    n_kt = ((b + 1) * tq + tk - 1) // tk
    kpos = lax.broadcasted_iota(I32, (tk, tq), 0)
    qpos = b * tq + lax.broadcasted_iota(I32, (tk, tq), 1)

    for h in range(n_heads):
        g, c = divmod(h, hg)
        qi_s[g, :, c * tq:(c + 1) * tq] = qit_ref[h]
        q_s[g, :, c * tq:(c + 1) * tq] = qt_ref[h]
        w_s[g, :, c * tq:(c + 1) * tq] = wit_ref[h:h + 1, :]

    def score_tile(j, carry):
        kidx = kidx_ref[j]
        score = jnp.zeros((tk, tq), F32)
        for g in range(n_groups):
            rel = jnp.maximum(_dot(kidx, qi_s[g]), 0.0) * w_s[g]
            for c in range(hg):
                score = score + rel[:, c * tq:(c + 1) * tq]
        score_ref[j] = jnp.where(kpos + j * tk <= qpos, score, -jnp.inf)
        return carry

    lax.fori_loop(0, n_kt, score_tile, 0)

    fold = 8 * SUBLANES

    def count_ge(cand_key):
        cand = _key_to_f32(jnp.maximum(cand_key, KEY_NEG_INF))

        def body(j, cnt):
            hit = jnp.where(score_ref[j] >= cand, 1.0, 0.0)
            return cnt + jnp.sum(hit.reshape(tk // fold, fold, tq), axis=0)

        cnt = lax.fori_loop(0, n_kt, body, jnp.zeros((fold, tq), F32))
        return jnp.sum(cnt, axis=0, keepdims=True)

    kf = float(topk)
    thr_key = jnp.where(count_ge(jnp.zeros((1, tq), I32)) >= kf, 0, INT_MIN).astype(I32)

    def bit_step(t, thr_key):
        cand_key = thr_key | jnp.left_shift(jnp.int32(1), 30 - t)
        return jnp.where(count_ge(cand_key) >= kf, cand_key, thr_key)

    thr_key = lax.fori_loop(0, 31, bit_step, thr_key)
    thr = _key_to_f32(jnp.maximum(thr_key, KEY_NEG_INF))

    def mask_tile(j, carry):
        keep = (score_ref[j] >= thr) & (kpos + j * tk <= qpos)
        score_ref[j] = jnp.where(keep, 0.0, NEG_BIG)
        return carry

    lax.fori_loop(0, n_kt, mask_tile, 0)

    m_ref[...] = jnp.full(m_ref.shape, NEG_BIG, F32)
    l_ref[...] = jnp.zeros(l_ref.shape, F32)
    acc_ref[...] = jnp.zeros(acc_ref.shape, F32)
    p_buf[...] = jnp.zeros(p_buf.shape, p_buf.dtype)
    a_buf[...] = jnp.ones(a_buf.shape, F32)

    def qk(j, g):
        bias = jnp.concatenate([score_ref[j]] * hg, axis=1)
        s_buf[g % 2] = _dot(kv_ref[j], q_s[g]) + bias

    def softmax(g):
        s = s_buf[g % 2]
        m_old = m_ref[g]
        m_new = jnp.maximum(m_old, jnp.max(s, axis=0, keepdims=True))
        p = jnp.exp2(s - m_new)
        alpha = jnp.exp2(m_old - m_new)
        l_ref[g] = alpha * l_ref[g] + jnp.sum(p, axis=0, keepdims=True)
        a_buf[g % 2] = alpha
        p_buf[g % 2] = p.astype(p_buf.dtype)
        m_ref[g] = m_new

    def pv(j, g):
        acc_ref[g] = a_buf[g % 2] * acc_ref[g] + _dot(kvt_ref[j], p_buf[g % 2])

    last = n_groups - 1
    qk(0, 0)

    def attn_tile(j, carry):
        for g in range(n_groups):
            if g < last:
                qk(j, g + 1)
            else:
                qk(jnp.minimum(j + 1, n_kt - 1), 0)
            if g > 0:
                pv(j, g - 1)
            else:
                pv(jnp.maximum(j - 1, 0), last)
            softmax(g)
        return carry

    lax.fori_loop(0, n_kt, attn_tile, 0)
    pv(n_kt - 1, last)

    for h in range(n_heads):
        g, c = divmod(h, hg)
        lanes = slice(c * tq, (c + 1) * tq)
        o = (acc_ref[g, :, lanes] / l_ref[g, :, lanes]).T
        y_ref[:, h * D_VHEAD:(h + 1) * D_VHEAD] = _dot(o.astype(wuv_ref.dtype), wuv_ref[h])


def _dsa_attention(qi_t, wi_t, kidx, q_t, kv, kvt, w_uv_bf, topk, tq=128, heads_per_group=2):
    n_heads, d_lat, s = q_t.shape
    n_kt, tk, dk = kidx.shape
    n_groups = n_heads // heads_per_group
    gw = heads_per_group * tq
    const3 = lambda b: (0, 0, 0)
    return pl.pallas_call(
        functools.partial(_attn_kernel, topk=topk),
        grid=(s // tq,),
        in_specs=[
            pl.BlockSpec((N_IDX_HEADS, dk, tq), lambda b: (0, 0, b)),
            pl.BlockSpec((N_IDX_HEADS, tq), lambda b: (0, b)),
            pl.BlockSpec((n_kt, tk, dk), const3),
            pl.BlockSpec((n_heads, d_lat, tq), lambda b: (0, 0, b)),
            pl.BlockSpec((n_kt, tk, d_lat), const3),
            pl.BlockSpec((n_kt, d_lat, tk), const3),
            pl.BlockSpec((n_heads, d_lat, D_VHEAD), const3),
        ],
        out_specs=pl.BlockSpec((tq, n_heads * D_VHEAD), lambda b: (b, 0)),
        out_shape=jax.ShapeDtypeStruct((s, n_heads * D_VHEAD), F32),
        scratch_shapes=[
            pltpu.VMEM((n_kt, tk, tq), F32),
            pltpu.VMEM((n_groups, dk, gw), BF16),
            pltpu.VMEM((n_groups, d_lat, gw), BF16),
            pltpu.VMEM((n_groups, 1, gw), F32),
            pltpu.VMEM((n_groups, 1, gw), F32),
            pltpu.VMEM((n_groups, 1, gw), F32),
            pltpu.VMEM((n_groups, d_lat, gw), F32),
            pltpu.VMEM((2, tk, gw), F32),
            pltpu.VMEM((2, tk, gw), BF16),
            pltpu.VMEM((2, 1, gw), F32),
        ],
        compiler_params=_params("parallel"),
        name="dsa_attention",
    )(qi_t, wi_t, kidx, q_t, kv, kvt, w_uv_bf)


def _merge_kernel(z_ref, w_ref, gc_ref, ga_ref, ya_ref, o_ref):
    y_conv = _dot(z_ref[...], w_ref[...])
    o_ref[...] = (gc_ref[...] * y_conv + ga_ref[...] * ya_ref[...]).astype(o_ref.dtype)


def _merge(z, w_co, gates, y_attn, tm=512, tn=1024):
    s, d = z.shape
    n = w_co.shape[1]
    nj = n // tn
    return pl.pallas_call(
        _merge_kernel,
        grid=(nj, s // tm),
        in_specs=[
            pl.BlockSpec((tm, d), lambda j, i: (i, 0)),
            pl.BlockSpec((d, tn), lambda j, i: (0, j)),
            pl.BlockSpec((tm, tn), lambda j, i: (i, j)),
            pl.BlockSpec((tm, tn), lambda j, i: (i, nj + j)),
            pl.BlockSpec((tm, tn), lambda j, i: (i, j)),
        ],
        out_specs=pl.BlockSpec((tm, tn), lambda j, i: (i, j)),
        out_shape=jax.ShapeDtypeStruct((s, n), BF16),
        compiler_params=_params("parallel", "parallel"),
        name="merge",
    )(z, w_co, gates, gates, y_attn)


def _oproj_kernel(m_ref, w_ref, h_ref, g_ref, b_ref, o_ref, obf_ref, *, alpha):
    v = alpha * h_ref[...] + _dot(m_ref[...], w_ref[...])
    out = _layer_norm(v, g_ref[...], b_ref[...])
    o_ref[...] = out
    obf_ref[...] = out.astype(obf_ref.dtype)


def _oproj(mixed, w_o, h, g, b, alpha, tm=256):
    s, d = h.shape
    row = pl.BlockSpec((tm, d), lambda i: (i, 0))
    vec = pl.BlockSpec((1, d), lambda i: (0, 0))
    return pl.pallas_call(
        functools.partial(_oproj_kernel, alpha=alpha),
        grid=(s // tm,),
        in_specs=[row, pl.BlockSpec((d, d), lambda i: (0, 0)), row, vec, vec],
        out_specs=[row, row],
        out_shape=[jax.ShapeDtypeStruct((s, d), F32), jax.ShapeDtypeStruct((s, d), BF16)],
        compiler_params=_params("parallel"),
        name="oproj_ln",
    )(mixed, w_o, h, g, b)


def _ffn_kernel(hbf_ref, wu_ref, wd_ref, h_ref, g_ref, b_ref, o_ref, obf_ref, acc_ref, *, alpha):
    f = pl.program_id(1)

    @pl.when(f == 0)
    def _():
        acc_ref[...] = jnp.zeros(acc_ref.shape, F32)

    a = jnp.maximum(_dot(hbf_ref[...], wu_ref[...]), 0.0)
    acc_ref[...] += _dot((a * a).astype(wd_ref.dtype), wd_ref[...])

    @pl.when(f == pl.num_programs(1) - 1)
    def _():
        out = _layer_norm(alpha * h_ref[...] + acc_ref[...], g_ref[...], b_ref[...])
        o_ref[...] = out
        obf_ref[...] = out.astype(obf_ref.dtype)


def _ffn(h_bf, w_up, w_down, h, g, b, alpha, tm=512, tf=512):
    s, d = h.shape
    dff = w_up.shape[1]
    row = pl.BlockSpec((tm, d), lambda i, f: (i, 0))
    vec = pl.BlockSpec((1, d), lambda i, f: (0, 0))
    return pl.pallas_call(
        functools.partial(_ffn_kernel, alpha=alpha),
        grid=(s // tm, dff // tf),
        in_specs=[row, pl.BlockSpec((d, tf), lambda i, f: (0, f)),
                  pl.BlockSpec((tf, d), lambda i, f: (f, 0)), row, vec, vec],
        out_specs=[row, row],
        out_shape=[jax.ShapeDtypeStruct((s, d), F32), jax.ShapeDtypeStruct((s, d), BF16)],
        scratch_shapes=[pltpu.VMEM((tm, d), F32)],
        compiler_params=_params("parallel", "arbitrary"),
        name="ffn_ln",
    )(h_bf, w_up, w_down, h, g, b)


def _ple_kernel(hbf_ref, wg_ref, p_ref, wp_ref, h_ref, g_ref, b_ref, o_ref, *, alpha):
    gate = _sigmoid(_dot(hbf_ref[...], wg_ref[...]))
    pe = gate * _dot(p_ref[...], wp_ref[...])
    o_ref[...] = _layer_norm(alpha * h_ref[...] + pe, g_ref[...], b_ref[...])


def _ple(h_bf, w_pg, p_bf, w_p, h, g, b, alpha, tm=256):
    s, d = h.shape
    dp = p_bf.shape[1]
    row = pl.BlockSpec((tm, d), lambda i: (i, 0))
    vec = pl.BlockSpec((1, d), lambda i: (0, 0))
    return pl.pallas_call(
        functools.partial(_ple_kernel, alpha=alpha),
        grid=(s // tm,),
        in_specs=[row, pl.BlockSpec((d, d), lambda i: (0, 0)),
                  pl.BlockSpec((tm, dp), lambda i: (i, 0)),
                  pl.BlockSpec((dp, d), lambda i: (0, 0)), row, vec, vec],
        out_specs=row,
        out_shape=jax.ShapeDtypeStruct((s, d), F32),
        compiler_params=_params("parallel"),
        name="ple_ln",
    )(h_bf, w_pg, p_bf, w_p, h, g, b)


def _layer(h, p, w_in, conv_w, w_conv_out, g_kv, w_uv, w_o, ln1_g, ln1_b, w_up, w_down,
           ln2_g, ln2_b, w_ple_gate, w_ple, ln3_g, ln3_b, alpha, tk=512):
    s, d = h.shape
    d_conv = conv_w.shape[1]
    nq = N_IDX_HEADS * D_IDX
    c_qlat = 3 * d_conv
    c_ckv = c_qlat + N_HEADS * D_LATENT
    c_qidx = c_ckv + D_LATENT
    c_kidx = c_qidx + nq
    c_widx = c_kidx + D_IDX
    c_gates = c_widx + N_IDX_HEADS
    c_kw_end = -(-c_gates // LANES) * LANES
    topk = min(TOPK_MAX, s // 4)
    n_kt = s // tk

    h_bf = h.astype(BF16)

    z = _conv_branch(h_bf, w_in, conv_w, d_conv)
    q_t = _qlat_proj(h_bf, w_in, c_qlat)
    ckv, kw, qi_t = _idx_proj(h_bf, w_in[:, c_ckv:c_kw_end].astype(BF16), g_kv.reshape(1, D_LATENT))
    gates = _gate_proj(h_bf, w_in[:, c_gates:].astype(BF16))

    k_hi, k_lo = _split_hi_lo(kw[:, :D_IDX])
    kidx = jnp.concatenate([k_hi, k_hi, k_lo], axis=-1).reshape(n_kt, tk, 3 * D_IDX)
    wi_t = jnp.transpose(kw[:, D_IDX:D_IDX + N_IDX_HEADS])
    kv = ckv.reshape(n_kt, tk, D_LATENT)
    kvt = jnp.transpose(kv, (0, 2, 1))

    y_attn = _dsa_attention(qi_t, wi_t, kidx, q_t, kv, kvt, w_uv.astype(BF16), topk)

    mixed = _merge(z, w_conv_out.astype(BF16), gates, y_attn)
    h1, h1_bf = _oproj(mixed, w_o.astype(BF16), h, ln1_g.reshape(1, d), ln1_b.reshape(1, d), alpha)
    h2, h2_bf = _ffn(h1_bf, w_up.astype(BF16), w_down.astype(BF16), h1,
                     ln2_g.reshape(1, d), ln2_b.reshape(1, d), alpha)
    return _ple(h2_bf, w_ple_gate.astype(BF16), p.astype(BF16), w_ple.astype(BF16), h2,
                ln3_g.reshape(1, d), ln3_b.reshape(1, d), alpha)


def kernel(x, p, w_in, conv_w, w_conv_out, g_kv, w_uv, w_o, ln1_g, ln1_b, w_up, w_down,
           ln2_g, ln2_b, w_ple_gate, w_ple, ln3_g, ln3_b):
    depth = w_in.shape[0]
    alpha = (2.0 * depth) ** 0.25
    batch = x.shape[0]
    outs = []
    for bi in range(batch):
        h = x[bi]
        for i in range(depth):
            h = _layer(h, p[i, bi], w_in[i], conv_w[i], w_conv_out[i], g_kv[i], w_uv[i], w_o[i],
                       ln1_g[i], ln1_b[i], w_up[i], w_down[i], ln2_g[i], ln2_b[i],
                       w_ple_gate[i], w_ple[i], ln3_g[i], ln3_b[i], alpha)
        outs.append(h)
    return jnp.stack(outs, axis=0)
```

```python
import functools

import jax
import jax.numpy as jnp
from jax import lax
from jax.experimental import pallas as pl
from jax.experimental.pallas import tpu as pltpu

F32 = jnp.float32
BF16 = jnp.bfloat16
I32 = jnp.int32

N_HEADS = 16
D_LATENT = 256
D_VHEAD = 128
N_IDX_HEADS = 16
D_IDX = 64
TOPK_MAX = 256
CONV_WIDTH = 3
LN_EPS = 1e-5
RMS_EPS = 1e-6
ATTN_SCALE = D_LATENT ** -0.5
IDX_SCALE = (D_IDX ** -0.5) * (N_IDX_HEADS ** -0.5)

VMEM_LIMIT_BYTES = 56 * 1024 * 1024
SUBLANES = 8
LANES = 128

INT_MIN = -(2 ** 31)
KEY_NEG_INF = INT_MIN + 0x7FFFFF
NEG_BIG = -1e30
LOG2E = 1.4426950408889634


def _params(*sem):
    return pltpu.CompilerParams(dimension_semantics=sem, vmem_limit_bytes=VMEM_LIMIT_BYTES)


def _dot(a, b):
    return jnp.dot(a, b, preferred_element_type=F32)


def _dot_nt(a, b):
    return lax.dot_general(a, b, (((1,), (1,)), ((), ())), preferred_element_type=F32)


def _sigmoid(v):
    return 1.0 / (1.0 + jnp.exp(-v))


def _layer_norm(v, g, b):
    mu = jnp.mean(v, axis=-1, keepdims=True)
    c = v - mu
    var = jnp.mean(c * c, axis=-1, keepdims=True)
    return c * lax.rsqrt(var + LN_EPS) * g + b


def _split_hi_lo(v):
    hi = v.astype(BF16)
    lo = (v - hi.astype(F32)).astype(BF16)
    return hi, lo


def _conv_kernel(x_ref, wb_ref, wc_ref, wu_ref, cw_ref, z_ref, cu_ref, w_s):
    i = pl.program_id(1)
    tm = x_ref.shape[0]
    halo = SUBLANES

    @pl.when(i == 0)
    def _():
        w_s[0] = wb_ref[...].astype(w_s.dtype)
        w_s[1] = wc_ref[...].astype(w_s.dtype)
        w_s[2] = wu_ref[...].astype(w_s.dtype)
        cu_ref[0:halo, :] = jnp.zeros((halo, cu_ref.shape[1]), F32)

    @pl.when(i > 0)
    def _():
        cu_ref[0:halo, :] = cu_ref[tm:tm + halo, :]

    x = x_ref[...]
    cu_ref[halo:halo + tm, :] = _dot_nt(x, w_s[1]) * _dot_nt(x, w_s[2])
    cw = cw_ref[...]
    v = (cw[0:1, :] * cu_ref[halo - 2:halo - 2 + tm, :]
         + cw[1:2, :] * cu_ref[halo - 1:halo - 1 + tm, :]
         + cw[2:3, :] * cu_ref[halo:halo + tm, :])
    z_ref[...] = (_dot_nt(x, w_s[0]) * v).astype(z_ref.dtype)


def _conv_branch(x_bf, w_in_t, conv_w, d_conv, tm=512, tn=512):
    s, d = x_bf.shape
    nj = d_conv // tn
    return pl.pallas_call(
        _conv_kernel,
        grid=(nj, s // tm),
        in_specs=[
            pl.BlockSpec((tm, d), lambda j, i: (i, 0)),
            pl.BlockSpec((tn, d), lambda j, i: (j, 0)),
            pl.BlockSpec((tn, d), lambda j, i: (nj + j, 0)),
            pl.BlockSpec((tn, d), lambda j, i: (2 * nj + j, 0)),
            pl.BlockSpec((CONV_WIDTH, tn), lambda j, i: (0, j)),
        ],
        out_specs=pl.BlockSpec((tm, tn), lambda j, i: (i, j)),
        out_shape=jax.ShapeDtypeStruct((s, d_conv), BF16),
        scratch_shapes=[pltpu.VMEM((tm + SUBLANES, tn), F32),
                        pltpu.VMEM((3, tn, d), BF16)],
        compiler_params=_params("parallel", "arbitrary"),
        name="conv_branch",
    )(x_bf, w_in_t, w_in_t, w_in_t, conv_w)


def _qlat_kernel(x_ref, w_ref, q_ref, w_s):
    @pl.when(pl.program_id(1) == 0)
    def _():
        w_s[...] = w_ref[...].astype(w_s.dtype)

    acc = _dot_nt(w_s[...], x_ref[...]) * (ATTN_SCALE * LOG2E)
    q_ref[...] = acc.astype(q_ref.dtype).reshape(q_ref.shape)


def _qlat_proj(x_bf, w_in_t, row0, tm=512, heads_per_step=4):
    s, d = x_bf.shape
    tn = heads_per_step * D_LATENT
    j0 = row0 // tn
    assert j0 * tn == row0
    return pl.pallas_call(
        _qlat_kernel,
        grid=(N_HEADS // heads_per_step, s // tm),
        in_specs=[
            pl.BlockSpec((tm, d), lambda j, i: (i, 0)),
            pl.BlockSpec((tn, d), lambda j, i: (j0 + j, 0)),
        ],
        out_specs=pl.BlockSpec((heads_per_step, D_LATENT, tm), lambda j, i: (j, 0, i)),
        out_shape=jax.ShapeDtypeStruct((N_HEADS, D_LATENT, s), BF16),
        scratch_shapes=[pltpu.VMEM((tn, d), BF16)],
        compiler_params=_params("parallel", "arbitrary"),
        name="qlat_proj",
    )(x_bf, w_in_t)


def _idx_kernel(x_ref, w_ref, g_ref, ckv_ref, kidx_ref, qit_ref, wit_ref, w_s):
    @pl.when(pl.program_id(0) == 0)
    def _():
        w_s[...] = w_ref[...].astype(w_s.dtype)

    x = x_ref[...]
    nq = N_IDX_HEADS * D_IDX
    r_q, r_k, r_w = D_LATENT, D_LATENT + nq, D_LATENT + nq + D_IDX
    c = _dot_nt(x, w_s[0:r_q])
    ms = jnp.mean(c * c, axis=-1, keepdims=True)
    ckv_ref[...] = (c * lax.rsqrt(ms + RMS_EPS) * g_ref[...]).astype(ckv_ref.dtype)
    k_hi, k_lo = _split_hi_lo(_dot_nt(x, w_s[r_k:r_w]))
    kidx_ref[...] = jnp.concatenate([k_hi, k_hi, k_lo], axis=1)
    q_t = _dot_nt(w_s[r_q:r_k], x)
    q_hi, q_lo = _split_hi_lo(q_t.reshape(N_IDX_HEADS, D_IDX, q_t.shape[1]))
    qit_ref[...] = jnp.concatenate([q_hi, q_lo, q_hi], axis=1)
    wit_ref[...] = _dot_nt(w_s[r_w:r_w + N_IDX_HEADS], x) * IDX_SCALE


def _idx_proj(x_bf, w_in_t, row0, g_kv, tm=512):
    s, d = x_bf.shape
    nq = N_IDX_HEADS * D_IDX
    n = D_LATENT + nq + D_IDX + N_IDX_HEADS
    n_pad = -(-n // LANES) * LANES
    return pl.pallas_call(
        _idx_kernel,
        grid=(s // tm,),
        in_specs=[
            pl.BlockSpec((tm, d), lambda i: (i, 0)),
            pl.BlockSpec((pl.Element(n_pad), pl.Element(d)), lambda i: (row0, 0)),
            pl.BlockSpec((1, D_LATENT), lambda i: (0, 0)),
        ],
        out_specs=[
            pl.BlockSpec((tm, D_LATENT), lambda i: (i, 0)),
            pl.BlockSpec((tm, 3 * D_IDX), lambda i: (i, 0)),
            pl.BlockSpec((N_IDX_HEADS, 3 * D_IDX, tm), lambda i: (0, 0, i)),
            pl.BlockSpec((N_IDX_HEADS, tm), lambda i: (0, i)),
        ],
        out_shape=[
            jax.ShapeDtypeStruct((s, D_LATENT), BF16),
            jax.ShapeDtypeStruct((s, 3 * D_IDX), BF16),
            jax.ShapeDtypeStruct((N_IDX_HEADS, 3 * D_IDX, s), BF16),
            jax.ShapeDtypeStruct((N_IDX_HEADS, s), F32),
        ],
        scratch_shapes=[pltpu.VMEM((n_pad, d), BF16)],
        compiler_params=_params("arbitrary"),
        name="idx_proj",
    )(x_bf, w_in_t, g_kv)


def _gate_kernel(x_ref, w_ref, o_ref, w_s):
    @pl.when(pl.program_id(1) == 0)
    def _():
        w_s[...] = w_ref[...].astype(w_s.dtype)

    o_ref[...] = _sigmoid(_dot_nt(x_ref[...], w_s[...]))


def _gate_proj(x_bf, w_in_t, row0, n, tm=512, tn=1024):
    s, d = x_bf.shape
    return pl.pallas_call(
        _gate_kernel,
        grid=(n // tn, s // tm),
        in_specs=[
            pl.BlockSpec((tm, d), lambda j, i: (i, 0)),
            pl.BlockSpec((pl.Element(tn), pl.Element(d)), lambda j, i: (pl.multiple_of(row0 + j * tn, SUBLANES), 0)),
        ],
        out_specs=pl.BlockSpec((tm, tn), lambda j, i: (i, j)),
        out_shape=jax.ShapeDtypeStruct((s, n), F32),
        scratch_shapes=[pltpu.VMEM((tn, d), BF16)],
        compiler_params=_params("parallel", "arbitrary"),
        name="gate_proj",
    )(x_bf, w_in_t)


def _key_to_f32(key):
    return pltpu.bitcast(key ^ ((key >> 31) & 0x7FFFFFFF), F32)


def _attn_kernel(qit_ref, wit_ref, kidx_ref, qt_ref, kv_ref, kvt_ref, wuv_ref, y_ref,
                 score_ref, qi_s, q_s, w_s, m_ref, l_ref, acc_ref, s_buf, p_buf, a_buf, *, topk):
    b = pl.program_id(0)
    n_heads, d_lat, tq = qt_ref.shape
    tk = kidx_ref.shape[1]
    n_groups, _, gw = q_s.shape
    hg = gw // tq
    n_kt = ((b + 1) * tq + tk - 1) // tk
    kpos = lax.broadcasted_iota(I32, (tk, tq), 0)
    qpos = b * tq + lax.broadcasted_iota(I32, (tk, tq), 1)

    for h in range(n_heads):
        g, c = divmod(h, hg)
        qi_s[g, :, c * tq:(c + 1) * tq] = qit_ref[h]
        q_s[g, :, c * tq:(c + 1) * tq] = qt_ref[h]
        w_s[g, :, c * tq:(c + 1) * tq] = wit_ref[h:h + 1, :]

    def score_tile(j, carry):
        kidx = kidx_ref[j]
        score = jnp.zeros((tk, tq), F32)
        for g in range(n_groups):
            rel = jnp.maximum(_dot(kidx, qi_s[g]), 0.0) * w_s[g]
            for c in range(hg):
                score = score + rel[:, c * tq:(c + 1) * tq]
        score_ref[j] = jnp.where(kpos + j * tk <= qpos, score, -jnp.inf)
        return carry

    lax.fori_loop(0, n_kt, score_tile, 0)

    fold = 8 * SUBLANES

    def count_ge(cand_key):
        cand = _key_to_f32(jnp.maximum(cand_key, KEY_NEG_INF))

        def body(j, cnt):
            hit = jnp.where(score_ref[j] >= cand, 1.0, 0.0)
            return cnt + jnp.sum(hit.reshape(tk // fold, fold, tq), axis=0)

        cnt = lax.fori_loop(0, n_kt, body, jnp.zeros((fold, tq), F32))
        return jnp.sum(cnt, axis=0, keepdims=True)

    kf = float(topk)
    thr_key = jnp.where(count_ge(jnp.zeros((1, tq), I32)) >= kf, 0, INT_MIN).astype(I32)

    def bit_step(t, thr_key):
        cand_key = thr_key | jnp.left_shift(jnp.int32(1), 30 - t)
        return jnp.where(count_ge(cand_key) >= kf, cand_key, thr_key)

    thr_key = lax.fori_loop(0, 31, bit_step, thr_key)
    thr = _key_to_f32(jnp.maximum(thr_key, KEY_NEG_INF))

    def mask_tile(j, carry):
        keep = (score_ref[j] >= thr) & (kpos + j * tk <= qpos)
        score_ref[j] = jnp.where(keep, 0.0, NEG_BIG)
        return carry

    lax.fori_loop(0, n_kt, mask_tile, 0)

    m_ref[...] = jnp.full(m_ref.shape, NEG_BIG, F32)
    l_ref[...] = jnp.zeros(l_ref.shape, F32)
    acc_ref[...] = jnp.zeros(acc_ref.shape, F32)
    p_buf[...] = jnp.zeros(p_buf.shape, p_buf.dtype)
    a_buf[...] = jnp.ones(a_buf.shape, F32)

    def qk(j, g):
        bias = jnp.concatenate([score_ref[j]] * hg, axis=1)
        s_buf[g % 2] = _dot(kv_ref[j], q_s[g]) + bias

    def softmax(g):
        s = s_buf[g % 2]
        m_old = m_ref[g]
        m_new = jnp.maximum(m_old, jnp.max(s, axis=0, keepdims=True))
        p = jnp.exp2(s - m_new)
        alpha = jnp.exp2(m_old - m_new)
        l_ref[g] = alpha * l_ref[g] + jnp.sum(p, axis=0, keepdims=True)
        a_buf[g % 2] = alpha
        p_buf[g % 2] = p.astype(p_buf.dtype)
        m_ref[g] = m_new

    def pv(j, g):
        acc_ref[g] = a_buf[g % 2] * acc_ref[g] + _dot(kvt_ref[j], p_buf[g % 2])

    last = n_groups - 1
    qk(0, 0)

    def attn_tile(j, carry):
        for g in range(n_groups):
            if g < last:
                qk(j, g + 1)
            else:
                qk(jnp.minimum(j + 1, n_kt - 1), 0)
            if g > 0:
                pv(j, g - 1)
            else:
                pv(jnp.maximum(j - 1, 0), last)
            softmax(g)
        return carry

    lax.fori_loop(0, n_kt, attn_tile, 0)
    pv(n_kt - 1, last)

    for h in range(n_heads):
        g, c = divmod(h, hg)
        lanes = slice(c * tq, (c + 1) * tq)
        o = (acc_ref[g, :, lanes] / l_ref[g, :, lanes]).T
        y_ref[:, h * D_VHEAD:(h + 1) * D_VHEAD] = _dot(o.astype(wuv_ref.dtype), wuv_ref[h])


def _dsa_attention(qi_t, wi_t, kidx, q_t, kv, kvt, w_uv_bf, topk, tq=128, heads_per_group=2):
    n_heads, d_lat, s = q_t.shape
    n_kt, tk, dk = kidx.shape
    n_groups = n_heads // heads_per_group
    gw = heads_per_group * tq
    const3 = lambda b: (0, 0, 0)
    return pl.pallas_call(
        functools.partial(_attn_kernel, topk=topk),
        grid=(s // tq,),
        in_specs=[
            pl.BlockSpec((N_IDX_HEADS, dk, tq), lambda b: (0, 0, b)),
            pl.BlockSpec((N_IDX_HEADS, tq), lambda b: (0, b)),
            pl.BlockSpec((n_kt, tk, dk), const3),
            pl.BlockSpec((n_heads, d_lat, tq), lambda b: (0, 0, b)),
            pl.BlockSpec((n_kt, tk, d_lat), const3),
            pl.BlockSpec((n_kt, d_lat, tk), const3),
            pl.BlockSpec((n_heads, d_lat, D_VHEAD), const3),
        ],
        out_specs=pl.BlockSpec((tq, n_heads * D_VHEAD), lambda b: (b, 0)),
        out_shape=jax.ShapeDtypeStruct((s, n_heads * D_VHEAD), F32),
        scratch_shapes=[
            pltpu.VMEM((n_kt, tk, tq), F32),
            pltpu.VMEM((n_groups, dk, gw), BF16),
            pltpu.VMEM((n_groups, d_lat, gw), BF16),
            pltpu.VMEM((n_groups, 1, gw), F32),
            pltpu.VMEM((n_groups, 1, gw), F32),
            pltpu.VMEM((n_groups, 1, gw), F32),
            pltpu.VMEM((n_groups, d_lat, gw), F32),
            pltpu.VMEM((2, tk, gw), F32),
            pltpu.VMEM((2, tk, gw), BF16),
            pltpu.VMEM((2, 1, gw), F32),
        ],
        compiler_params=_params("parallel"),
        name="dsa_attention",
    )(qi_t, wi_t, kidx, q_t, kv, kvt, w_uv_bf)


def _merge_kernel(z_ref, w_ref, gc_ref, ga_ref, ya_ref, o_ref):
    y_conv = _dot(z_ref[...], w_ref[...])
    o_ref[...] = (gc_ref[...] * y_conv + ga_ref[...] * ya_ref[...]).astype(o_ref.dtype)


def _merge(z, w_co, gates, y_attn, tm=512, tn=1024):
    s, d = z.shape
    n = w_co.shape[1]
    nj = n // tn
    return pl.pallas_call(
        _merge_kernel,
        grid=(nj, s // tm),
        in_specs=[
            pl.BlockSpec((tm, d), lambda j, i: (i, 0)),
            pl.BlockSpec((d, tn), lambda j, i: (0, j)),
            pl.BlockSpec((tm, tn), lambda j, i: (i, j)),
            pl.BlockSpec((tm, tn), lambda j, i: (i, nj + j)),
            pl.BlockSpec((tm, tn), lambda j, i: (i, j)),
        ],
        out_specs=pl.BlockSpec((tm, tn), lambda j, i: (i, j)),
        out_shape=jax.ShapeDtypeStruct((s, n), BF16),
        compiler_params=_params("parallel", "parallel"),
        name="merge",
    )(z, w_co, gates, gates, y_attn)


def _oproj_kernel(m_ref, w_ref, h_ref, g_ref, b_ref, o_ref, obf_ref, *, alpha):
    v = alpha * h_ref[...] + _dot(m_ref[...], w_ref[...])
    out = _layer_norm(v, g_ref[...], b_ref[...])
    o_ref[...] = out
    obf_ref[...] = out.astype(obf_ref.dtype)


def _oproj(mixed, w_o, h, g, b, alpha, tm=256):
    s, d = h.shape
    row = pl.BlockSpec((tm, d), lambda i: (i, 0))
    vec = pl.BlockSpec((1, d), lambda i: (0, 0))
    return pl.pallas_call(
        functools.partial(_oproj_kernel, alpha=alpha),
        grid=(s // tm,),
        in_specs=[row, pl.BlockSpec((d, d), lambda i: (0, 0)), row, vec, vec],
        out_specs=[row, row],
        out_shape=[jax.ShapeDtypeStruct((s, d), F32), jax.ShapeDtypeStruct((s, d), BF16)],
        compiler_params=_params("parallel"),
        name="oproj_ln",
    )(mixed, w_o, h, g, b)


def _ffn_kernel(hbf_ref, wu_ref, wd_ref, h_ref, g_ref, b_ref, o_ref, obf_ref, acc_ref, *, alpha):
    f = pl.program_id(1)

    @pl.when(f == 0)
    def _():
        acc_ref[...] = jnp.zeros(acc_ref.shape, F32)

    a = jnp.maximum(_dot(hbf_ref[...], wu_ref[...]), 0.0)
    acc_ref[...] += _dot((a * a).astype(wd_ref.dtype), wd_ref[...])

    @pl.when(f == pl.num_programs(1) - 1)
    def _():
        out = _layer_norm(alpha * h_ref[...] + acc_ref[...], g_ref[...], b_ref[...])
        o_ref[...] = out
        obf_ref[...] = out.astype(obf_ref.dtype)


def _ffn(h_bf, w_up, w_down, h, g, b, alpha, tm=512, tf=512):
    s, d = h.shape
    dff = w_up.shape[1]
    row = pl.BlockSpec((tm, d), lambda i, f: (i, 0))
    vec = pl.BlockSpec((1, d), lambda i, f: (0, 0))
    return pl.pallas_call(
        functools.partial(_ffn_kernel, alpha=alpha),
        grid=(s // tm, dff // tf),
        in_specs=[row, pl.BlockSpec((d, tf), lambda i, f: (0, f)),
                  pl.BlockSpec((tf, d), lambda i, f: (f, 0)), row, vec, vec],
        out_specs=[row, row],
        out_shape=[jax.ShapeDtypeStruct((s, d), F32), jax.ShapeDtypeStruct((s, d), BF16)],
        scratch_shapes=[pltpu.VMEM((tm, d), F32)],
        compiler_params=_params("parallel", "arbitrary"),
        name="ffn_ln",
    )(h_bf, w_up, w_down, h, g, b)


def _ple_kernel(hbf_ref, wg_ref, p_ref, wp_ref, h_ref, g_ref, b_ref, o_ref, *, alpha):
    gate = _sigmoid(_dot(hbf_ref[...], wg_ref[...]))
    pe = gate * _dot(p_ref[...], wp_ref[...])
    o_ref[...] = _layer_norm(alpha * h_ref[...] + pe, g_ref[...], b_ref[...])


def _ple(h_bf, w_pg, p_bf, w_p, h, g, b, alpha, tm=256):
    s, d = h.shape
    dp = p_bf.shape[1]
    row = pl.BlockSpec((tm, d), lambda i: (i, 0))
    vec = pl.BlockSpec((1, d), lambda i: (0, 0))
    return pl.pallas_call(
        functools.partial(_ple_kernel, alpha=alpha),
        grid=(s // tm,),
        in_specs=[row, pl.BlockSpec((d, d), lambda i: (0, 0)),
                  pl.BlockSpec((tm, dp), lambda i: (i, 0)),
                  pl.BlockSpec((dp, d), lambda i: (0, 0)), row, vec, vec],
        out_specs=row,
        out_shape=jax.ShapeDtypeStruct((s, d), F32),
        compiler_params=_params("parallel"),
        name="ple_ln",
    )(h_bf, w_pg, p_bf, w_p, h, g, b)


def _layer(h, p, w_in, conv_w, w_conv_out, g_kv, w_uv, w_o, ln1_g, ln1_b, w_up, w_down,
           ln2_g, ln2_b, w_ple_gate, w_ple, ln3_g, ln3_b, alpha, tk=512):
    s, d = h.shape
    d_conv = conv_w.shape[1]
    nq = N_IDX_HEADS * D_IDX
    c_qlat = 3 * d_conv
    c_ckv = c_qlat + N_HEADS * D_LATENT
    c_gates = c_ckv + D_LATENT + nq + D_IDX + N_IDX_HEADS
    topk = min(TOPK_MAX, s // 4)
    n_kt = s // tk

    h_bf = h.astype(BF16)
    w_in_t = jnp.transpose(w_in)

    z = _conv_branch(h_bf, w_in_t, conv_w, d_conv)
    q_t = _qlat_proj(h_bf, w_in_t, c_qlat)
    ckv, kidx, qi_t, wi_t = _idx_proj(h_bf, w_in_t, c_ckv, g_kv.reshape(1, D_LATENT))
    gates = _gate_proj(h_bf, w_in_t, c_gates, w_in.shape[1] - c_gates)

    kidx = kidx.reshape(n_kt, tk, 3 * D_IDX)
    kv = ckv.reshape(n_kt, tk, D_LATENT)
    kvt = jnp.transpose(kv, (0, 2, 1))

    y_attn = _dsa_attention(qi_t, wi_t, kidx, q_t, kv, kvt, w_uv.astype(BF16), topk)

    mixed = _merge(z, w_conv_out.astype(BF16), gates, y_attn)
    h1, h1_bf = _oproj(mixed, w_o.astype(BF16), h, ln1_g.reshape(1, d), ln1_b.reshape(1, d), alpha)
    h2, h2_bf = _ffn(h1_bf, w_up.astype(BF16), w_down.astype(BF16), h1,
                     ln2_g.reshape(1, d), ln2_b.reshape(1, d), alpha)
    return _ple(h2_bf, w_ple_gate.astype(BF16), p.astype(BF16), w_ple.astype(BF16), h2,
                ln3_g.reshape(1, d), ln3_b.reshape(1, d), alpha)


def kernel(x, p, w_in, conv_w, w_conv_out, g_kv, w_uv, w_o, ln1_g, ln1_b, w_up, w_down,
           ln2_g, ln2_b, w_ple_gate, w_ple, ln3_g, ln3_b):
    depth = w_in.shape[0]
    alpha = (2.0 * depth) ** 0.25
    batch = x.shape[0]
    outs = []
    for bi in range(batch):
        h = x[bi]
        for i in range(depth):
            h = _layer(h, p[i, bi], w_in[i], conv_w[i], w_conv_out[i], g_kv[i], w_uv[i], w_o[i],
                       ln1_g[i], ln1_b[i], w_up[i], w_down[i], ln2_g[i], ln2_b[i],
                       w_ple_gate[i], w_ple[i], ln3_g[i], ln3_b[i], alpha)
        outs.append(h)
    return jnp.stack(outs, axis=0)
```

```python
import functools

import jax
import jax.numpy as jnp
from jax import lax
from jax.experimental import pallas as pl
from jax.experimental.pallas import tpu as pltpu

F32 = jnp.float32
BF16 = jnp.bfloat16
I32 = jnp.int32

N_HEADS = 16
D_LATENT = 256
D_VHEAD = 128
N_IDX_HEADS = 16
D_IDX = 64
TOPK_MAX = 256
CONV_WIDTH = 3
LN_EPS = 1e-5
RMS_EPS = 1e-6
ATTN_SCALE = D_LATENT ** -0.5
IDX_SCALE = (D_IDX ** -0.5) * (N_IDX_HEADS ** -0.5)

VMEM_LIMIT_BYTES = 56 * 1024 * 1024
SUBLANES = 8
LANES = 128

INT_MIN = -(2 ** 31)
KEY_NEG_INF = INT_MIN + 0x7FFFFF
KEY16_NEG_INF = -(1 << 15) + 0x7F
NEG_BIG = -1e30
LOG2E = 1.4426950408889634


def _params(*sem):
    return pltpu.CompilerParams(dimension_semantics=sem, vmem_limit_bytes=VMEM_LIMIT_BYTES)


def _dot(a, b):
    return jnp.dot(a, b, preferred_element_type=F32)


def _dot_nt(a, b):
    return lax.dot_general(a, b, (((1,), (1,)), ((), ())), preferred_element_type=F32)


def _sigmoid(v):
    return 1.0 / (1.0 + jnp.exp(-v))


def _layer_norm(v, g, b):
    mu = jnp.mean(v, axis=-1, keepdims=True)
    c = v - mu
    var = jnp.mean(c * c, axis=-1, keepdims=True)
    return c * lax.rsqrt(var + LN_EPS) * g + b


def _split_hi_lo(v):
    hi = v.astype(BF16)
    lo = (v - hi.astype(F32)).astype(BF16)
    return hi, lo


def _conv_kernel(x_ref, wb_ref, wc_ref, wu_ref, cw_ref, z_ref, cu_ref, w_s):
    i = pl.program_id(1)
    tm = x_ref.shape[0]
    halo = SUBLANES

    @pl.when(i == 0)
    def _():
        w_s[0] = wb_ref[...].astype(w_s.dtype)
        w_s[1] = wc_ref[...].astype(w_s.dtype)
        w_s[2] = wu_ref[...].astype(w_s.dtype)
        cu_ref[0:halo, :] = jnp.zeros((halo, cu_ref.shape[1]), F32)

    @pl.when(i > 0)
    def _():
        cu_ref[0:halo, :] = cu_ref[tm:tm + halo, :]

    x = x_ref[...]
    cu_ref[halo:halo + tm, :] = _dot_nt(x, w_s[1]) * _dot_nt(x, w_s[2])
    cw = cw_ref[...]
    v = (cw[0:1, :] * cu_ref[halo - 2:halo - 2 + tm, :]
         + cw[1:2, :] * cu_ref[halo - 1:halo - 1 + tm, :]
         + cw[2:3, :] * cu_ref[halo:halo + tm, :])
    z_ref[...] = (_dot_nt(x, w_s[0]) * v).astype(z_ref.dtype)


def _conv_branch(x_bf, w_in_t, conv_w, d_conv, tm=512, tn=512):
    s, d = x_bf.shape
    nj = d_conv // tn
    return pl.pallas_call(
        _conv_kernel,
        grid=(nj, s // tm),
        in_specs=[
            pl.BlockSpec((tm, d), lambda j, i: (i, 0)),
            pl.BlockSpec((tn, d), lambda j, i: (j, 0)),
            pl.BlockSpec((tn, d), lambda j, i: (nj + j, 0)),
            pl.BlockSpec((tn, d), lambda j, i: (2 * nj + j, 0)),
            pl.BlockSpec((CONV_WIDTH, tn), lambda j, i: (0, j)),
        ],
        out_specs=pl.BlockSpec((tm, tn), lambda j, i: (i, j)),
        out_shape=jax.ShapeDtypeStruct((s, d_conv), BF16),
        scratch_shapes=[pltpu.VMEM((tm + SUBLANES, tn), F32),
                        pltpu.VMEM((3, tn, d), BF16)],
        compiler_params=_params("parallel", "arbitrary"),
        name="conv_branch",
    )(x_bf, w_in_t, w_in_t, w_in_t, conv_w)


def _qlat_kernel(x_ref, w_ref, q_ref, w_s):
    @pl.when(pl.program_id(1) == 0)
    def _():
        w_s[...] = w_ref[...].astype(w_s.dtype)

    acc = _dot_nt(w_s[...], x_ref[...]) * (ATTN_SCALE * LOG2E)
    q_ref[...] = acc.astype(q_ref.dtype).reshape(q_ref.shape)


def _qlat_proj(x_bf, w_in_t, row0, tm=512, heads_per_step=4):
    s, d = x_bf.shape
    tn = heads_per_step * D_LATENT
    j0 = row0 // tn
    assert j0 * tn == row0
    return pl.pallas_call(
        _qlat_kernel,
        grid=(N_HEADS // heads_per_step, s // tm),
        in_specs=[
            pl.BlockSpec((tm, d), lambda j, i: (i, 0)),
            pl.BlockSpec((tn, d), lambda j, i: (j0 + j, 0)),
        ],
        out_specs=pl.BlockSpec((heads_per_step, D_LATENT, tm), lambda j, i: (j, 0, i)),
        out_shape=jax.ShapeDtypeStruct((N_HEADS, D_LATENT, s), BF16),
        scratch_shapes=[pltpu.VMEM((tn, d), BF16)],
        compiler_params=_params("parallel", "arbitrary"),
        name="qlat_proj",
    )(x_bf, w_in_t)


def _idx_kernel(x_ref, w_ref, g_ref, ckv_ref, kidx_ref, qit_ref, wit_ref, w_s):
    @pl.when(pl.program_id(0) == 0)
    def _():
        w_s[...] = w_ref[...].astype(w_s.dtype)

    x = x_ref[...]
    nq = N_IDX_HEADS * D_IDX
    r_q, r_k, r_w = D_LATENT, D_LATENT + nq, D_LATENT + nq + D_IDX
    c = _dot_nt(x, w_s[0:r_q])
    ms = jnp.mean(c * c, axis=-1, keepdims=True)
    ckv_ref[...] = (c * lax.rsqrt(ms + RMS_EPS) * g_ref[...]).astype(ckv_ref.dtype)
    k_hi, k_lo = _split_hi_lo(_dot_nt(x, w_s[r_k:r_w]))
    kidx_ref[...] = jnp.concatenate([k_hi, k_hi, k_lo], axis=1)
    q_t = _dot_nt(w_s[r_q:r_k], x)
    q_hi, q_lo = _split_hi_lo(q_t.reshape(N_IDX_HEADS, D_IDX, q_t.shape[1]))
    qit_ref[...] = jnp.concatenate([q_hi, q_lo, q_hi], axis=1)
    wit_ref[...] = _dot_nt(w_s[r_w:r_w + N_IDX_HEADS], x) * IDX_SCALE


def _idx_proj(x_bf, w_in_t, row0, g_kv, tm=512):
    s, d = x_bf.shape
    nq = N_IDX_HEADS * D_IDX
    n = D_LATENT + nq + D_IDX + N_IDX_HEADS
    n_pad = -(-n // LANES) * LANES
    return pl.pallas_call(
        _idx_kernel,
        grid=(s // tm,),
        in_specs=[
            pl.BlockSpec((tm, d), lambda i: (i, 0)),
            pl.BlockSpec((pl.Element(n_pad), pl.Element(d)), lambda i: (row0, 0)),
            pl.BlockSpec((1, D_LATENT), lambda i: (0, 0)),
        ],
        out_specs=[
            pl.BlockSpec((tm, D_LATENT), lambda i: (i, 0)),
            pl.BlockSpec((tm, 3 * D_IDX), lambda i: (i, 0)),
            pl.BlockSpec((N_IDX_HEADS, 3 * D_IDX, tm), lambda i: (0, 0, i)),
            pl.BlockSpec((N_IDX_HEADS, tm), lambda i: (0, i)),
        ],
        out_shape=[
            jax.ShapeDtypeStruct((s, D_LATENT), BF16),
            jax.ShapeDtypeStruct((s, 3 * D_IDX), BF16),
            jax.ShapeDtypeStruct((N_IDX_HEADS, 3 * D_IDX, s), BF16),
            jax.ShapeDtypeStruct((N_IDX_HEADS, s), F32),
        ],
        scratch_shapes=[pltpu.VMEM((n_pad, d), BF16)],
        compiler_params=_params("arbitrary"),
        name="idx_proj",
    )(x_bf, w_in_t, g_kv)


def _gate_kernel(x_ref, w_ref, o_ref, w_s):
    @pl.when(pl.program_id(1) == 0)
    def _():
        w_s[...] = w_ref[...].astype(w_s.dtype)

    o_ref[...] = _sigmoid(_dot_nt(x_ref[...], w_s[...]))


def _gate_proj(x_bf, w_in_t, row0, n, tm=512, tn=1024):
    s, d = x_bf.shape
    return pl.pallas_call(
        _gate_kernel,
        grid=(n // tn, s // tm),
        in_specs=[
            pl.BlockSpec((tm, d), lambda j, i: (i, 0)),
            pl.BlockSpec((pl.Element(tn), pl.Element(d)), lambda j, i: (pl.multiple_of(row0 + j * tn, SUBLANES), 0)),
        ],
        out_specs=pl.BlockSpec((tm, tn), lambda j, i: (i, j)),
        out_shape=jax.ShapeDtypeStruct((s, n), F32),
        scratch_shapes=[pltpu.VMEM((tn, d), BF16)],
        compiler_params=_params("parallel", "arbitrary"),
        name="gate_proj",
    )(x_bf, w_in_t)


def _flip_key(v):
    return v ^ ((v >> 31) & 0x7FFFFFFF)


def _key_to_f32(key):
    return pltpu.bitcast(_flip_key(key), F32)


def _attn_kernel(qit_ref, wit_ref, kidx_ref, qt_ref, kv_ref, kvt_ref, wuv_ref, y_ref,
                 score_ref, sbf_ref, qi_s, q_s, w_s, m_ref, l_ref, acc_ref, s_buf, p_buf, a_buf,
                 *, topk):
    b = pl.program_id(0)
    n_heads, d_lat, tq = qt_ref.shape
    tk = kidx_ref.shape[1]
    n_groups, _, gw = q_s.shape
    hg = gw // tq
    n_kt = ((b + 1) * tq + tk - 1) // tk
    kpos = lax.broadcasted_iota(I32, (tk, tq), 0)
    qpos = b * tq + lax.broadcasted_iota(I32, (tk, tq), 1)

    for h in range(n_heads):
        g, c = divmod(h, hg)
        qi_s[g, :, c * tq:(c + 1) * tq] = qit_ref[h]
        q_s[g, :, c * tq:(c + 1) * tq] = qt_ref[h]
        w_s[g, :, c * tq:(c + 1) * tq] = wit_ref[h:h + 1, :]

    def score_tile(j, carry):
        kidx = kidx_ref[j]
        score = jnp.zeros((tk, tq), F32)
        for g in range(n_groups):
            rel = jnp.maximum(_dot(kidx, qi_s[g]), 0.0) * w_s[g]
            for c in range(hg):
                score = score + rel[:, c * tq:(c + 1) * tq]
        score = jnp.where(kpos + j * tk <= qpos, score, -jnp.inf)
        score_ref[j] = score
        sbf_ref[j] = score.astype(sbf_ref.dtype)
        return carry

    lax.fori_loop(0, n_kt, score_tile, 0)

    fold = 8 * SUBLANES
    fold16 = 2 * SUBLANES
    kf = float(topk)
    n_pairs = (n_kt + 1) // 2

    @pl.when(n_kt % 2 == 1)
    def _():
        score_ref[n_kt] = jnp.full((tk, tq), -jnp.inf, F32)
        sbf_ref[n_kt] = jnp.full((tk, tq), -jnp.inf, sbf_ref.dtype)

    def bf16_bits(k16):
        k16 = jnp.maximum(k16, KEY16_NEG_INF)
        return jnp.left_shift(k16 ^ ((k16 >> 31) & 0x7FFF), 16)

    def count_ge16(k16):
        cand = pltpu.bitcast(bf16_bits(k16), F32).astype(BF16)

        def body(i, cnt):
            for j in (2 * i, 2 * i + 1):
                hit = jnp.where(sbf_ref[j] >= cand, jnp.ones((), BF16), jnp.zeros((), BF16))
                parts = [hit[r * fold16:(r + 1) * fold16] for r in range(tk // fold16)]
                while len(parts) > 1:
                    parts = [a + b for a, b in zip(parts[0::2], parts[1::2])]
                cnt = cnt + parts[0].astype(F32)
            return cnt

        cnt = lax.fori_loop(0, n_pairs, body, jnp.zeros((fold16, tq), F32))
        return jnp.sum(cnt, axis=0, keepdims=True)

    def count_ge(cand_key):
        cand = _key_to_f32(jnp.maximum(cand_key, KEY_NEG_INF))

        def body(i, cnt):
            for j in (2 * i, 2 * i + 1):
                hit = jnp.where(score_ref[j] >= cand, 1.0, 0.0)
                cnt = cnt + jnp.sum(hit.reshape(tk // fold, fold, tq), axis=0)
            return cnt

        cnt = lax.fori_loop(0, n_pairs, body, jnp.zeros((fold, tq), F32))
        return jnp.sum(cnt, axis=0, keepdims=True)

    key16 = jnp.where(count_ge16(jnp.zeros((1, tq), I32)) >= kf, 0, -(1 << 15)).astype(I32)

    def bit_step(t, key16):
        cand_key = key16 | jnp.left_shift(jnp.int32(1), 14 - t)
        return jnp.where(count_ge16(cand_key) >= kf, cand_key, key16)

    key16 = lax.fori_loop(0, 15, bit_step, key16)
    half = 1 << 15
    lo0 = _flip_key(bf16_bits(key16)) - half
    hi0 = lo0 + 3 * half

    def bisect(t, lohi):
        lo, hi = lohi
        mid = lo + ((hi - lo) >> 1)
        ok = count_ge(mid) >= kf
        return jnp.where(ok, mid, lo), jnp.where(ok, hi, mid)

    thr_key, _ = lax.fori_loop(0, 17, bisect, (lo0, hi0))
    thr = _key_to_f32(jnp.maximum(thr_key, KEY_NEG_INF))

    def mask_tile(j, carry):
        keep = (score_ref[j] >= thr) & (kpos + j * tk <= qpos)
        score_ref[j] = jnp.where(keep, 0.0, NEG_BIG)
        return carry

    lax.fori_loop(0, n_kt, mask_tile, 0)

    m_ref[...] = jnp.full(m_ref.shape, NEG_BIG, F32)
    l_ref[...] = jnp.zeros(l_ref.shape, F32)
    acc_ref[...] = jnp.zeros(acc_ref.shape, F32)
    p_buf[...] = jnp.zeros(p_buf.shape, p_buf.dtype)
    a_buf[...] = jnp.ones(a_buf.shape, F32)

    def qk(j, g):
        bias = jnp.concatenate([score_ref[j]] * hg, axis=1)
        s_buf[g % 2] = _dot(kv_ref[j], q_s[g]) + bias

    def softmax(g):
        s = s_buf[g % 2]
        m_old = m_ref[g]
        m_new = jnp.maximum(m_old, jnp.max(s, axis=0, keepdims=True))
        p = jnp.exp2(s - m_new)
        alpha = jnp.exp2(m_old - m_new)
        l_ref[g] = alpha * l_ref[g] + jnp.sum(p, axis=0, keepdims=True)
        a_buf[g % 2] = alpha
        p_buf[g % 2] = p.astype(p_buf.dtype)
        m_ref[g] = m_new

    def pv(j, g):
        acc_ref[g] = a_buf[g % 2] * acc_ref[g] + _dot(kvt_ref[j], p_buf[g % 2])

    last = n_groups - 1
    qk(0, 0)

    def attn_tile(j, carry):
        for g in range(n_groups):
            if g < last:
                qk(j, g + 1)
            else:
                qk(jnp.minimum(j + 1, n_kt - 1), 0)
            if g > 0:
                pv(j, g - 1)
            else:
                pv(jnp.maximum(j - 1, 0), last)
            softmax(g)
        return carry

    lax.fori_loop(0, n_kt, attn_tile, 0)
    pv(n_kt - 1, last)

    for h in range(n_heads):
        g, c = divmod(h, hg)
        lanes = slice(c * tq, (c + 1) * tq)
        o = (acc_ref[g, :, lanes] / l_ref[g, :, lanes]).T
        y_ref[:, h * D_VHEAD:(h + 1) * D_VHEAD] = _dot(o.astype(wuv_ref.dtype), wuv_ref[h])


def _dsa_attention(qi_t, wi_t, kidx, q_t, kv, kvt, w_uv_bf, topk, tq=128, heads_per_group=2):
    n_heads, d_lat, s = q_t.shape
    n_kt, tk, dk = kidx.shape
    assert n_kt % 2 == 0 and tk % tq == 0
    n_groups = n_heads // heads_per_group
    gw = heads_per_group * tq
    const3 = lambda b: (0, 0, 0)
    return pl.pallas_call(
        functools.partial(_attn_kernel, topk=topk),
        grid=(s // tq,),
        in_specs=[
            pl.BlockSpec((N_IDX_HEADS, dk, tq), lambda b: (0, 0, b)),
            pl.BlockSpec((N_IDX_HEADS, tq), lambda b: (0, b)),
            pl.BlockSpec((n_kt, tk, dk), const3),
            pl.BlockSpec((n_heads, d_lat, tq), lambda b: (0, 0, b)),
            pl.BlockSpec((n_kt, tk, d_lat), const3),
            pl.BlockSpec((n_kt, d_lat, tk), const3),
            pl.BlockSpec((n_heads, d_lat, D_VHEAD), const3),
        ],
        out_specs=pl.BlockSpec((tq, n_heads * D_VHEAD), lambda b: (b, 0)),
        out_shape=jax.ShapeDtypeStruct((s, n_heads * D_VHEAD), F32),
        scratch_shapes=[
            pltpu.VMEM((n_kt, tk, tq), F32),
            pltpu.VMEM((n_kt, tk, tq), BF16),
            pltpu.VMEM((n_groups, dk, gw), BF16),
            pltpu.VMEM((n_groups, d_lat, gw), BF16),
            pltpu.VMEM((n_groups, 1, gw), F32),
            pltpu.VMEM((n_groups, 1, gw), F32),
            pltpu.VMEM((n_groups, 1, gw), F32),
            pltpu.VMEM((n_groups, d_lat, gw), F32),
            pltpu.VMEM((2, tk, gw), F32),
            pltpu.VMEM((2, tk, gw), BF16),
            pltpu.VMEM((2, 1, gw), F32),
        ],
        compiler_params=_params("parallel"),
        name="dsa_attention",
    )(qi_t, wi_t, kidx, q_t, kv, kvt, w_uv_bf)


def _merge_kernel(z_ref, w_ref, gc_ref, ga_ref, ya_ref, o_ref):
    y_conv = _dot(z_ref[...], w_ref[...])
    o_ref[...] = (gc_ref[...] * y_conv + ga_ref[...] * ya_ref[...]).astype(o_ref.dtype)


def _merge(z, w_co, gates, y_attn, tm=512, tn=1024):
    s, d = z.shape
    n = w_co.shape[1]
    nj = n // tn
    return pl.pallas_call(
        _merge_kernel,
        grid=(nj, s // tm),
        in_specs=[
            pl.BlockSpec((tm, d), lambda j, i: (i, 0)),
            pl.BlockSpec((d, tn), lambda j, i: (0, j)),
            pl.BlockSpec((tm, tn), lambda j, i: (i, j)),
            pl.BlockSpec((tm, tn), lambda j, i: (i, nj + j)),
            pl.BlockSpec((tm, tn), lambda j, i: (i, j)),
        ],
        out_specs=pl.BlockSpec((tm, tn), lambda j, i: (i, j)),
        out_shape=jax.ShapeDtypeStruct((s, n), BF16),
        compiler_params=_params("parallel", "parallel"),
        name="merge",
    )(z, w_co, gates, gates, y_attn)


def _oproj_kernel(m_ref, w_ref, h_ref, g_ref, b_ref, o_ref, obf_ref, *, alpha):
    v = alpha * h_ref[...] + _dot(m_ref[...], w_ref[...])
    out = _layer_norm(v, g_ref[...], b_ref[...])
    o_ref[...] = out
    obf_ref[...] = out.astype(obf_ref.dtype)


def _oproj(mixed, w_o, h, g, b, alpha, tm=256):
    s, d = h.shape
    row = pl.BlockSpec((tm, d), lambda i: (i, 0))
    vec = pl.BlockSpec((1, d), lambda i: (0, 0))
    return pl.pallas_call(
        functools.partial(_oproj_kernel, alpha=alpha),
        grid=(s // tm,),
        in_specs=[row, pl.BlockSpec((d, d), lambda i: (0, 0)), row, vec, vec],
        out_specs=[row, row],
        out_shape=[jax.ShapeDtypeStruct((s, d), F32), jax.ShapeDtypeStruct((s, d), BF16)],
        compiler_params=_params("parallel"),
        name="oproj_ln",
    )(mixed, w_o, h, g, b)


def _ffn_kernel(hbf_ref, wu_ref, wd_ref, h_ref, g_ref, b_ref, o_ref, obf_ref, acc_ref, *, alpha):
    f = pl.program_id(1)

    @pl.when(f == 0)
    def _():
        acc_ref[...] = jnp.zeros(acc_ref.shape, F32)

    a = jnp.maximum(_dot(hbf_ref[...], wu_ref[...]), 0.0)
    acc_ref[...] += _dot((a * a).astype(wd_ref.dtype), wd_ref[...])

    @pl.when(f == pl.num_programs(1) - 1)
    def _():
        out = _layer_norm(alpha * h_ref[...] + acc_ref[...], g_ref[...], b_ref[...])
        o_ref[...] = out
        obf_ref[...] = out.astype(obf_ref.dtype)


def _ffn(h_bf, w_up, w_down, h, g, b, alpha, tm=512, tf=512):
    s, d = h.shape
    dff = w_up.shape[1]
    row = pl.BlockSpec((tm, d), lambda i, f: (i, 0))
    vec = pl.BlockSpec((1, d), lambda i, f: (0, 0))
    return pl.pallas_call(
        functools.partial(_ffn_kernel, alpha=alpha),
        grid=(s // tm, dff // tf),
        in_specs=[row, pl.BlockSpec((d, tf), lambda i, f: (0, f)),
                  pl.BlockSpec((tf, d), lambda i, f: (f, 0)), row, vec, vec],
        out_specs=[row, row],
        out_shape=[jax.ShapeDtypeStruct((s, d), F32), jax.ShapeDtypeStruct((s, d), BF16)],
        scratch_shapes=[pltpu.VMEM((tm, d), F32)],
        compiler_params=_params("parallel", "arbitrary"),
        name="ffn_ln",
    )(h_bf, w_up, w_down, h, g, b)


def _ple_kernel(hbf_ref, wg_ref, p_ref, wp_ref, h_ref, g_ref, b_ref, o_ref, *, alpha):
    gate = _sigmoid(_dot(hbf_ref[...], wg_ref[...]))
    pe = gate * _dot(p_ref[...], wp_ref[...])
    o_ref[...] = _layer_norm(alpha * h_ref[...] + pe, g_ref[...], b_ref[...])


def _ple(h_bf, w_pg, p_bf, w_p, h, g, b, alpha, tm=256):
    s, d = h.shape
    dp = p_bf.shape[1]
    row = pl.BlockSpec((tm, d), lambda i: (i, 0))
    vec = pl.BlockSpec((1, d), lambda i: (0, 0))
    return pl.pallas_call(
        functools.partial(_ple_kernel, alpha=alpha),
        grid=(s // tm,),
        in_specs=[row, pl.BlockSpec((d, d), lambda i: (0, 0)),
                  pl.BlockSpec((tm, dp), lambda i: (i, 0)),
                  pl.BlockSpec((dp, d), lambda i: (0, 0)), row, vec, vec],
        out_specs=row,
        out_shape=jax.ShapeDtypeStruct((s, d), F32),
        compiler_params=_params("parallel"),
        name="ple_ln",
    )(h_bf, w_pg, p_bf, w_p, h, g, b)


def _layer(h, p, w_in, conv_w, w_conv_out, g_kv, w_uv, w_o, ln1_g, ln1_b, w_up, w_down,
           ln2_g, ln2_b, w_ple_gate, w_ple, ln3_g, ln3_b, alpha, tk=512):
    s, d = h.shape
    d_conv = conv_w.shape[1]
    nq = N_IDX_HEADS * D_IDX
    c_qlat = 3 * d_conv
    c_ckv = c_qlat + N_HEADS * D_LATENT
    c_gates = c_ckv + D_LATENT + nq + D_IDX + N_IDX_HEADS
    topk = min(TOPK_MAX, s // 4)
    n_kt = s // tk

    h_bf = h.astype(BF16)
    w_in_t = jnp.transpose(w_in)

    z = _conv_branch(h_bf, w_in_t, conv_w, d_conv)
    q_t = _qlat_proj(h_bf, w_in_t, c_qlat)
    ckv, kidx, qi_t, wi_t = _idx_proj(h_bf, w_in_t, c_ckv, g_kv.reshape(1, D_LATENT))
    gates = _gate_proj(h_bf, w_in_t, c_gates, w_in.shape[1] - c_gates)

    kidx = kidx.reshape(n_kt, tk, 3 * D_IDX)
    kv = ckv.reshape(n_kt, tk, D_LATENT)
    kvt = jnp.transpose(kv, (0, 2, 1))

    y_attn = _dsa_attention(qi_t, wi_t, kidx, q_t, kv, kvt, w_uv.astype(BF16), topk)

    mixed = _merge(z, w_conv_out.astype(BF16), gates, y_attn)
    h1, h1_bf = _oproj(mixed, w_o.astype(BF16), h, ln1_g.reshape(1, d), ln1_b.reshape(1, d), alpha)
    h2, h2_bf = _ffn(h1_bf, w_up.astype(BF16), w_down.astype(BF16), h1,
                     ln2_g.reshape(1, d), ln2_b.reshape(1, d), alpha)
    return _ple(h2_bf, w_ple_gate.astype(BF16), p.astype(BF16), w_ple.astype(BF16), h2,
                ln3_g.reshape(1, d), ln3_b.reshape(1, d), alpha)


def kernel(x, p, w_in, conv_w, w_conv_out, g_kv, w_uv, w_o, ln1_g, ln1_b, w_up, w_down,
           ln2_g, ln2_b, w_ple_gate, w_ple, ln3_g, ln3_b):
    depth = w_in.shape[0]
    alpha = (2.0 * depth) ** 0.25
    batch = x.shape[0]
    outs = []
    for bi in range(batch):
        h = x[bi]
        for i in range(depth):
            h = _layer(h, p[i, bi], w_in[i], conv_w[i], w_conv_out[i], g_kv[i], w_uv[i], w_o[i],
                       ln1_g[i], ln1_b[i], w_up[i], w_down[i], ln2_g[i], ln2_b[i],
                       w_ple_gate[i], w_ple[i], ln3_g[i], ln3_b[i], alpha)
        outs.append(h)
    return jnp.stack(outs, axis=0)
```

```python
import functools

import jax
import jax.numpy as jnp
from jax import lax
from jax.experimental import pallas as pl
from jax.experimental.pallas import tpu as pltpu

F32 = jnp.float32
BF16 = jnp.bfloat16
I32 = jnp.int32

N_HEADS = 16
D_LATENT = 256
D_VHEAD = 128
N_IDX_HEADS = 16
D_IDX = 64
TOPK_MAX = 256
CONV_WIDTH = 3
LN_EPS = 1e-5
RMS_EPS = 1e-6
ATTN_SCALE = D_LATENT ** -0.5
IDX_SCALE = (D_IDX ** -0.5) * (N_IDX_HEADS ** -0.5)

VMEM_LIMIT_BYTES = 56 * 1024 * 1024
SUBLANES = 8
LANES = 128

INT_MIN = -(2 ** 31)
KEY_NEG_INF = INT_MIN + 0x7FFFFF
NEG_BIG = -1e30
LOG2E = 1.4426950408889634


def _params(*sem):
    return pltpu.CompilerParams(dimension_semantics=sem, vmem_limit_bytes=VMEM_LIMIT_BYTES)


def _dot(a, b):
    return jnp.dot(a, b, preferred_element_type=F32)


def _dot_nt(a, b):
    return lax.dot_general(a, b, (((1,), (1,)), ((), ())), preferred_element_type=F32)


def _sigmoid(v):
    return 1.0 / (1.0 + jnp.exp(-v))


def _layer_norm(v, g, b):
    mu = jnp.mean(v, axis=-1, keepdims=True)
    c = v - mu
    var = jnp.mean(c * c, axis=-1, keepdims=True)
    return c * lax.rsqrt(var + LN_EPS) * g + b


def _split_hi_lo(v):
    hi = v.astype(BF16)
    lo = (v - hi.astype(F32)).astype(BF16)
    return hi, lo


def _conv_kernel(x_ref, wb_ref, wc_ref, wu_ref, cw_ref, z_ref, cu_ref, w_s):
    i = pl.program_id(1)
    tm = x_ref.shape[0]
    halo = SUBLANES

    @pl.when(i == 0)
    def _():
        w_s[0] = wb_ref[...].astype(w_s.dtype)
        w_s[1] = wc_ref[...].astype(w_s.dtype)
        w_s[2] = wu_ref[...].astype(w_s.dtype)
        cu_ref[0:halo, :] = jnp.zeros((halo, cu_ref.shape[1]), F32)

    @pl.when(i > 0)
    def _():
        cu_ref[0:halo, :] = cu_ref[tm:tm + halo, :]

    x = x_ref[...]
    cu_ref[halo:halo + tm, :] = _dot_nt(x, w_s[1]) * _dot_nt(x, w_s[2])
    cw = cw_ref[...]
    v = (cw[0:1, :] * cu_ref[halo - 2:halo - 2 + tm, :]
         + cw[1:2, :] * cu_ref[halo - 1:halo - 1 + tm, :]
         + cw[2:3, :] * cu_ref[halo:halo + tm, :])
    z_ref[...] = (_dot_nt(x, w_s[0]) * v).astype(z_ref.dtype)


def _conv_branch(x_bf, w_in_t, conv_w, d_conv, tm=512, tn=512):
    s, d = x_bf.shape
    nj = d_conv // tn
    return pl.pallas_call(
        _conv_kernel,
        grid=(nj, s // tm),
        in_specs=[
            pl.BlockSpec((tm, d), lambda j, i: (i, 0)),
            pl.BlockSpec((tn, d), lambda j, i: (j, 0)),
            pl.BlockSpec((tn, d), lambda j, i: (nj + j, 0)),
            pl.BlockSpec((tn, d), lambda j, i: (2 * nj + j, 0)),
            pl.BlockSpec((CONV_WIDTH, tn), lambda j, i: (0, j)),
        ],
        out_specs=pl.BlockSpec((tm, tn), lambda j, i: (i, j)),
        out_shape=jax.ShapeDtypeStruct((s, d_conv), BF16),
        scratch_shapes=[pltpu.VMEM((tm + SUBLANES, tn), F32),
                        pltpu.VMEM((3, tn, d), BF16)],
        compiler_params=_params("parallel", "arbitrary"),
        name="conv_branch",
    )(x_bf, w_in_t, w_in_t, w_in_t, conv_w)


def _qlat_kernel(x_ref, w_ref, q_ref, w_s):
    @pl.when(pl.program_id(1) == 0)
    def _():
        w_s[...] = w_ref[...].astype(w_s.dtype)

    acc = _dot_nt(w_s[...], x_ref[...]) * (ATTN_SCALE * LOG2E)
    q_ref[...] = acc.astype(q_ref.dtype).reshape(q_ref.shape)


def _qlat_proj(x_bf, w_in_t, row0, tm=512, heads_per_step=4):
    s, d = x_bf.shape
    tn = heads_per_step * D_LATENT
    j0 = row0 // tn
    assert j0 * tn == row0
    return pl.pallas_call(
        _qlat_kernel,
        grid=(N_HEADS // heads_per_step, s // tm),
        in_specs=[
            pl.BlockSpec((tm, d), lambda j, i: (i, 0)),
            pl.BlockSpec((tn, d), lambda j, i: (j0 + j, 0)),
        ],
        out_specs=pl.BlockSpec((heads_per_step, D_LATENT, tm), lambda j, i: (j, 0, i)),
        out_shape=jax.ShapeDtypeStruct((N_HEADS, D_LATENT, s), BF16),
        scratch_shapes=[pltpu.VMEM((tn, d), BF16)],
        compiler_params=_params("parallel", "arbitrary"),
        name="qlat_proj",
    )(x_bf, w_in_t)


def _idx_kernel(x_ref, w_ref, g_ref, ckv_ref, kidx_ref, qit_ref, wit_ref, w_s):
    @pl.when(pl.program_id(0) == 0)
    def _():
        w_s[...] = w_ref[...].astype(w_s.dtype)

    x = x_ref[...]
    nq = N_IDX_HEADS * D_IDX
    r_q, r_k, r_w = D_LATENT, D_LATENT + nq, D_LATENT + nq + D_IDX
    c = _dot_nt(x, w_s[0:r_q])
    ms = jnp.mean(c * c, axis=-1, keepdims=True)
    ckv_ref[...] = (c * lax.rsqrt(ms + RMS_EPS) * g_ref[...]).astype(ckv_ref.dtype)
    k_hi, k_lo = _split_hi_lo(_dot_nt(x, w_s[r_k:r_w]))
    kidx_ref[...] = jnp.concatenate([k_hi, k_hi, k_lo], axis=1)
    q_t = _dot_nt(w_s[r_q:r_k], x)
    q_hi, q_lo = _split_hi_lo(q_t.reshape(N_IDX_HEADS, D_IDX, q_t.shape[1]))
    qit_ref[...] = jnp.concatenate([q_hi, q_lo, q_hi], axis=1)
    wit_ref[...] = _dot_nt(w_s[r_w:r_w + N_IDX_HEADS], x) * IDX_SCALE


def _idx_proj(x_bf, w_in_t, row0, g_kv, tm=512):
    s, d = x_bf.shape
    nq = N_IDX_HEADS * D_IDX
    n = D_LATENT + nq + D_IDX + N_IDX_HEADS
    n_pad = -(-n // LANES) * LANES
    return pl.pallas_call(
        _idx_kernel,
        grid=(s // tm,),
        in_specs=[
            pl.BlockSpec((tm, d), lambda i: (i, 0)),
            pl.BlockSpec((pl.Element(n_pad), pl.Element(d)), lambda i: (row0, 0)),
            pl.BlockSpec((1, D_LATENT), lambda i: (0, 0)),
        ],
        out_specs=[
            pl.BlockSpec((tm, D_LATENT), lambda i: (i, 0)),
            pl.BlockSpec((tm, 3 * D_IDX), lambda i: (i, 0)),
            pl.BlockSpec((N_IDX_HEADS, 3 * D_IDX, tm), lambda i: (0, 0, i)),
            pl.BlockSpec((N_IDX_HEADS, tm), lambda i: (0, i)),
        ],
        out_shape=[
            jax.ShapeDtypeStruct((s, D_LATENT), BF16),
            jax.ShapeDtypeStruct((s, 3 * D_IDX), BF16),
            jax.ShapeDtypeStruct((N_IDX_HEADS, 3 * D_IDX, s), BF16),
            jax.ShapeDtypeStruct((N_IDX_HEADS, s), F32),
        ],
        scratch_shapes=[pltpu.VMEM((n_pad, d), BF16)],
        compiler_params=_params("arbitrary"),
        name="idx_proj",
    )(x_bf, w_in_t, g_kv)


def _gate_kernel(x_ref, w_ref, o_ref, w_s):
    @pl.when(pl.program_id(1) == 0)
    def _():
        w_s[...] = w_ref[...].astype(w_s.dtype)

    o_ref[...] = _sigmoid(_dot_nt(x_ref[...], w_s[...]))


def _gate_proj(x_bf, w_in_t, row0, n, tm=512, tn=1024):
    s, d = x_bf.shape
    return pl.pallas_call(
        _gate_kernel,
        grid=(n // tn, s // tm),
        in_specs=[
            pl.BlockSpec((tm, d), lambda j, i: (i, 0)),
            pl.BlockSpec((pl.Element(tn), pl.Element(d)), lambda j, i: (pl.multiple_of(row0 + j * tn, SUBLANES), 0)),
        ],
        out_specs=pl.BlockSpec((tm, tn), lambda j, i: (i, j)),
        out_shape=jax.ShapeDtypeStruct((s, n), F32),
        scratch_shapes=[pltpu.VMEM((tn, d), BF16)],
        compiler_params=_params("parallel", "arbitrary"),
        name="gate_proj",
    )(x_bf, w_in_t)


def _flip_key(v):
    return v ^ ((v >> 31) & 0x7FFFFFFF)


def _key_to_f32(key):
    return pltpu.bitcast(_flip_key(key), F32)


def _attn_kernel(qit_ref, wit_ref, kidx_ref, qt_ref, kv_ref, kvt_ref, wuv_ref, y_ref,
                 score_ref, qi_s, q_s, w_s, m_ref, l_ref, acc_ref, s_buf, p_buf, a_buf, *, topk):
    b = pl.program_id(0)
    n_heads, d_lat, tq = qt_ref.shape
    n_kt_all, tk, _ = kidx_ref.shape
    n_groups, _, gw = q_s.shape
    hg = gw // tq
    n_kt = ((b + 1) * tq + tk - 1) // tk
    kpos = lax.broadcasted_iota(I32, (tk, tq), 0)
    qpos = b * tq + lax.broadcasted_iota(I32, (tk, tq), 1)

    for h in range(n_heads):
        g, c = divmod(h, hg)
        qi_s[g, :, c * tq:(c + 1) * tq] = qit_ref[h]
        q_s[g, :, c * tq:(c + 1) * tq] = qt_ref[h]
        w_s[g, :, c * tq:(c + 1) * tq] = wit_ref[h:h + 1, :]

    def score_tile(j, carry):
        kidx = kidx_ref[j]
        score = jnp.zeros((tk, tq), F32)
        for g in range(n_groups):
            rel = jnp.maximum(_dot(kidx, qi_s[g]), 0.0) * w_s[g]
            for c in range(hg):
                score = score + rel[:, c * tq:(c + 1) * tq]
        score_ref[j] = jnp.where(kpos + j * tk <= qpos, score, -jnp.inf)
        return carry

    lax.fori_loop(0, n_kt, score_tile, 0)

    fold = 8 * SUBLANES
    kf = float(topk)
    n_pairs = (n_kt + 1) // 2

    @pl.when(n_kt % 2 == 1)
    def _():
        score_ref[n_kt] = jnp.full((tk, tq), -jnp.inf, F32)

    def count(pred):
        def body(i, cnt):
            for j in (2 * i, 2 * i + 1):
                hit = jnp.where(pred(kpos + j * tk, score_ref[j]), 1.0, 0.0)
                cnt = cnt + jnp.sum(hit.reshape(tk // fold, fold, tq), axis=0)
            return cnt

        cnt = lax.fori_loop(0, n_pairs, body, jnp.zeros((fold, tq), F32))
        return jnp.sum(cnt, axis=0, keepdims=True)

    def count_ge(cand_key):
        cand = _key_to_f32(jnp.maximum(cand_key, KEY_NEG_INF))
        return count(lambda pos, sc: sc >= cand)

    c0 = count_ge(jnp.zeros((1, tq), I32))
    thr_key = jnp.where(c0 >= kf, 0, INT_MIN).astype(I32)
    n_ge = jnp.where(c0 >= kf, c0, 0.0)

    def bit_step(t, carry):
        thr_key, n_ge = carry
        cand_key = thr_key | jnp.left_shift(jnp.int32(1), 30 - t)
        c = count_ge(cand_key)
        return jnp.where(c >= kf, cand_key, thr_key), jnp.where(c >= kf, c, n_ge)

    thr_key, n_ge = lax.fori_loop(0, 31, bit_step, (thr_key, n_ge))
    thr = _key_to_f32(jnp.maximum(thr_key, KEY_NEG_INF))

    @pl.when(jnp.max(n_ge) > kf)
    def _():
        need = kf - count(lambda pos, sc: sc > thr)
        n_pos_bits = (n_kt_all * tk - 1).bit_length()

        def pos_step(t, cut):
            cand = cut | jnp.left_shift(jnp.int32(1), n_pos_bits - 1 - t)
            before = count(lambda pos, sc: (sc == thr) & (pos < cand))
            return jnp.where(before < need, cand, cut)

        cut = lax.fori_loop(0, n_pos_bits, pos_step, jnp.zeros((1, tq), I32))

        def drop_tile(j, carry):
            sc = score_ref[j]
            score_ref[j] = jnp.where((sc == thr) & (kpos + j * tk > cut), -jnp.inf, sc)
            return carry

        lax.fori_loop(0, n_kt, drop_tile, 0)

    def mask_tile(j, carry):
        keep = (score_ref[j] >= thr) & (kpos + j * tk <= qpos)
        score_ref[j] = jnp.where(keep, 0.0, NEG_BIG)
        return carry

    lax.fori_loop(0, n_kt, mask_tile, 0)

    m_ref[...] = jnp.full(m_ref.shape, NEG_BIG, F32)
    l_ref[...] = jnp.zeros(l_ref.shape, F32)
    acc_ref[...] = jnp.zeros(acc_ref.shape, F32)
    p_buf[...] = jnp.zeros(p_buf.shape, p_buf.dtype)
    a_buf[...] = jnp.ones(a_buf.shape, F32)

    def qk(j, g):
        bias = jnp.concatenate([score_ref[j]] * hg, axis=1)
        s_buf[g % 2] = _dot(kv_ref[j], q_s[g]) + bias

    def softmax(g):
        s = s_buf[g % 2]
        m_old = m_ref[g]
        m_new = jnp.maximum(m_old, jnp.max(s, axis=0, keepdims=True))
        p = jnp.exp2(s - m_new)
        alpha = jnp.exp2(m_old - m_new)
        l_ref[g] = alpha * l_ref[g] + jnp.sum(p, axis=0, keepdims=True)
        a_buf[g % 2] = alpha
        p_buf[g % 2] = p.astype(p_buf.dtype)
        m_ref[g] = m_new

    def pv(j, g):
        acc_ref[g] = a_buf[g % 2] * acc_ref[g] + _dot(kvt_ref[j], p_buf[g % 2])

    last = n_groups - 1
    qk(0, 0)

    def attn_tile(j, carry):
        for g in range(n_groups):
            if g < last:
                qk(j, g + 1)
            else:
                qk(jnp.minimum(j + 1, n_kt - 1), 0)
            if g > 0:
                pv(j, g - 1)
            else:
                pv(jnp.maximum(j - 1, 0), last)
            softmax(g)
        return carry

    lax.fori_loop(0, n_kt, attn_tile, 0)
    pv(n_kt - 1, last)

    for h in range(n_heads):
        g, c = divmod(h, hg)
        lanes = slice(c * tq, (c + 1) * tq)
        o = (acc_ref[g, :, lanes] / l_ref[g, :, lanes]).T
        y_ref[:, h * D_VHEAD:(h + 1) * D_VHEAD] = _dot(o.astype(wuv_ref.dtype), wuv_ref[h])


def _dsa_attention(qi_t, wi_t, kidx, q_t, kv, kvt, w_uv_bf, topk, tq=128, heads_per_group=2):
    n_heads, d_lat, s = q_t.shape
    n_kt, tk, dk = kidx.shape
    assert n_kt % 2 == 0 and tk % tq == 0
    n_groups = n_heads // heads_per_group
    gw = heads_per_group * tq
    const3 = lambda b: (0, 0, 0)
    return pl.pallas_call(
        functools.partial(_attn_kernel, topk=topk),
        grid=(s // tq,),
        in_specs=[
            pl.BlockSpec((N_IDX_HEADS, dk, tq), lambda b: (0, 0, b)),
            pl.BlockSpec((N_IDX_HEADS, tq), lambda b: (0, b)),
            pl.BlockSpec((n_kt, tk, dk), const3),
            pl.BlockSpec((n_heads, d_lat, tq), lambda b: (0, 0, b)),
            pl.BlockSpec((n_kt, tk, d_lat), const3),
            pl.BlockSpec((n_kt, d_lat, tk), const3),
            pl.BlockSpec((n_heads, d_lat, D_VHEAD), const3),
        ],
        out_specs=pl.BlockSpec((tq, n_heads * D_VHEAD), lambda b: (b, 0)),
        out_shape=jax.ShapeDtypeStruct((s, n_heads * D_VHEAD), F32),
        scratch_shapes=[
            pltpu.VMEM((n_kt, tk, tq), F32),
            pltpu.VMEM((n_groups, dk, gw), BF16),
            pltpu.VMEM((n_groups, d_lat, gw), BF16),
            pltpu.VMEM((n_groups, 1, gw), F32),
            pltpu.VMEM((n_groups, 1, gw), F32),
            pltpu.VMEM((n_groups, 1, gw), F32),
            pltpu.VMEM((n_groups, d_lat, gw), F32),
            pltpu.VMEM((2, tk, gw), F32),
            pltpu.VMEM((2, tk, gw), BF16),
            pltpu.VMEM((2, 1, gw), F32),
        ],
        compiler_params=_params("parallel"),
        name="dsa_attention",
    )(qi_t, wi_t, kidx, q_t, kv, kvt, w_uv_bf)


def _merge_kernel(z_ref, w_ref, gc_ref, ga_ref, ya_ref, o_ref):
    y_conv = _dot(z_ref[...], w_ref[...])
    o_ref[...] = (gc_ref[...] * y_conv + ga_ref[...] * ya_ref[...]).astype(o_ref.dtype)


def _merge(z, w_co, gates, y_attn, tm=512, tn=1024):
    s, d = z.shape
    n = w_co.shape[1]
    nj = n // tn
    return pl.pallas_call(
        _merge_kernel,
        grid=(nj, s // tm),
        in_specs=[
            pl.BlockSpec((tm, d), lambda j, i: (i, 0)),
            pl.BlockSpec((d, tn), lambda j, i: (0, j)),
            pl.BlockSpec((tm, tn), lambda j, i: (i, j)),
            pl.BlockSpec((tm, tn), lambda j, i: (i, nj + j)),
            pl.BlockSpec((tm, tn), lambda j, i: (i, j)),
        ],
        out_specs=pl.BlockSpec((tm, tn), lambda j, i: (i, j)),
        out_shape=jax.ShapeDtypeStruct((s, n), BF16),
        compiler_params=_params("parallel", "parallel"),
        name="merge",
    )(z, w_co, gates, gates, y_attn)


def _oproj_kernel(m_ref, w_ref, h_ref, g_ref, b_ref, o_ref, obf_ref, *, alpha):
    v = alpha * h_ref[...] + _dot(m_ref[...], w_ref[...])
    out = _layer_norm(v, g_ref[...], b_ref[...])
    o_ref[...] = out
    obf_ref[...] = out.astype(obf_ref.dtype)


def _oproj(mixed, w_o, h, g, b, alpha, tm=256):
    s, d = h.shape
    row = pl.BlockSpec((tm, d), lambda i: (i, 0))
    vec = pl.BlockSpec((1, d), lambda i: (0, 0))
    return pl.pallas_call(
        functools.partial(_oproj_kernel, alpha=alpha),
        grid=(s // tm,),
        in_specs=[row, pl.BlockSpec((d, d), lambda i: (0, 0)), row, vec, vec],
        out_specs=[row, row],
        out_shape=[jax.ShapeDtypeStruct((s, d), F32), jax.ShapeDtypeStruct((s, d), BF16)],
        compiler_params=_params("parallel"),
        name="oproj_ln",
    )(mixed, w_o, h, g, b)


def _ffn_kernel(hbf_ref, wu_ref, wd_ref, h_ref, g_ref, b_ref, o_ref, obf_ref, acc_ref, *, alpha):
    f = pl.program_id(1)

    @pl.when(f == 0)
    def _():
        acc_ref[...] = jnp.zeros(acc_ref.shape, F32)

    a = jnp.maximum(_dot(hbf_ref[...], wu_ref[...]), 0.0)
    acc_ref[...] += _dot((a * a).astype(wd_ref.dtype), wd_ref[...])

    @pl.when(f == pl.num_programs(1) - 1)
    def _():
        out = _layer_norm(alpha * h_ref[...] + acc_ref[...], g_ref[...], b_ref[...])
        o_ref[...] = out
        obf_ref[...] = out.astype(obf_ref.dtype)


def _ffn(h_bf, w_up, w_down, h, g, b, alpha, tm=512, tf=512):
    s, d = h.shape
    dff = w_up.shape[1]
    row = pl.BlockSpec((tm, d), lambda i, f: (i, 0))
    vec = pl.BlockSpec((1, d), lambda i, f: (0, 0))
    return pl.pallas_call(
        functools.partial(_ffn_kernel, alpha=alpha),
        grid=(s // tm, dff // tf),
        in_specs=[row, pl.BlockSpec((d, tf), lambda i, f: (0, f)),
                  pl.BlockSpec((tf, d), lambda i, f: (f, 0)), row, vec, vec],
        out_specs=[row, row],
        out_shape=[jax.ShapeDtypeStruct((s, d), F32), jax.ShapeDtypeStruct((s, d), BF16)],
        scratch_shapes=[pltpu.VMEM((tm, d), F32)],
        compiler_params=_params("parallel", "arbitrary"),
        name="ffn_ln",
    )(h_bf, w_up, w_down, h, g, b)


def _ple_kernel(hbf_ref, wg_ref, p_ref, wp_ref, h_ref, g_ref, b_ref, o_ref, *, alpha):
    gate = _sigmoid(_dot(hbf_ref[...], wg_ref[...]))
    pe = gate * _dot(p_ref[...], wp_ref[...])
    o_ref[...] = _layer_norm(alpha * h_ref[...] + pe, g_ref[...], b_ref[...])


def _ple(h_bf, w_pg, p_bf, w_p, h, g, b, alpha, tm=256):
    s, d = h.shape
    dp = p_bf.shape[1]
    row = pl.BlockSpec((tm, d), lambda i: (i, 0))
    vec = pl.BlockSpec((1, d), lambda i: (0, 0))
    return pl.pallas_call(
        functools.partial(_ple_kernel, alpha=alpha),
        grid=(s // tm,),
        in_specs=[row, pl.BlockSpec((d, d), lambda i: (0, 0)),
                  pl.BlockSpec((tm, dp), lambda i: (i, 0)),
                  pl.BlockSpec((dp, d), lambda i: (0, 0)), row, vec, vec],
        out_specs=row,
        out_shape=jax.ShapeDtypeStruct((s, d), F32),
        compiler_params=_params("parallel"),
        name="ple_ln",
    )(h_bf, w_pg, p_bf, w_p, h, g, b)


def _layer(h, p, w_in, conv_w, w_conv_out, g_kv, w_uv, w_o, ln1_g, ln1_b, w_up, w_down,
           ln2_g, ln2_b, w_ple_gate, w_ple, ln3_g, ln3_b, alpha, tk=512):
    s, d = h.shape
    d_conv = conv_w.shape[1]
    nq = N_IDX_HEADS * D_IDX
    c_qlat = 3 * d_conv
    c_ckv = c_qlat + N_HEADS * D_LATENT
    c_gates = c_ckv + D_LATENT + nq + D_IDX + N_IDX_HEADS
    topk = min(TOPK_MAX, s // 4)
    n_kt = s // tk

    h_bf = h.astype(BF16)
    w_in_t = jnp.transpose(w_in)

    z = _conv_branch(h_bf, w_in_t, conv_w, d_conv)
    q_t = _qlat_proj(h_bf, w_in_t, c_qlat)
    ckv, kidx, qi_t, wi_t = _idx_proj(h_bf, w_in_t, c_ckv, g_kv.reshape(1, D_LATENT))
    gates = _gate_proj(h_bf, w_in_t, c_gates, w_in.shape[1] - c_gates)

    kidx = kidx.reshape(n_kt, tk, 3 * D_IDX)
    kv = ckv.reshape(n_kt, tk, D_LATENT)
    kvt = jnp.transpose(kv, (0, 2, 1))

    y_attn = _dsa_attention(qi_t, wi_t, kidx, q_t, kv, kvt, w_uv.astype(BF16), topk)

    mixed = _merge(z, w_conv_out.astype(BF16), gates, y_attn)
    h1, h1_bf = _oproj(mixed, w_o.astype(BF16), h, ln1_g.reshape(1, d), ln1_b.reshape(1, d), alpha)
    h2, h2_bf = _ffn(h1_bf, w_up.astype(BF16), w_down.astype(BF16), h1,
                     ln2_g.reshape(1, d), ln2_b.reshape(1, d), alpha)
    return _ple(h2_bf, w_ple_gate.astype(BF16), p.astype(BF16), w_ple.astype(BF16), h2,
                ln3_g.reshape(1, d), ln3_b.reshape(1, d), alpha)


def kernel(x, p, w_in, conv_w, w_conv_out, g_kv, w_uv, w_o, ln1_g, ln1_b, w_up, w_down,
           ln2_g, ln2_b, w_ple_gate, w_ple, ln3_g, ln3_b):
    depth = w_in.shape[0]
    alpha = (2.0 * depth) ** 0.25
    batch = x.shape[0]
    outs = []
    for bi in range(batch):
        h = x[bi]
        for i in range(depth):
            h = _layer(h, p[i, bi], w_in[i], conv_w[i], w_conv_out[i], g_kv[i], w_uv[i], w_o[i],
                       ln1_g[i], ln1_b[i], w_up[i], w_down[i], ln2_g[i], ln2_b[i],
                       w_ple_gate[i], w_ple[i], ln3_g[i], ln3_b[i], alpha)
        outs.append(h)
    return jnp.stack(outs, axis=0)
```

```python
import functools

import jax
import jax.numpy as jnp
from jax import lax
from jax.experimental import pallas as pl
from jax.experimental.pallas import tpu as pltpu

F32 = jnp.float32
BF16 = jnp.bfloat16
I32 = jnp.int32

N_HEADS = 16
D_LATENT = 256
D_VHEAD = 128
N_IDX_HEADS = 16
D_IDX = 64
TOPK_MAX = 256
CONV_WIDTH = 3
LN_EPS = 1e-5
RMS_EPS = 1e-6
ATTN_SCALE = D_LATENT ** -0.5
IDX_SCALE = (D_IDX ** -0.5) * (N_IDX_HEADS ** -0.5)

VMEM_LIMIT_BYTES = 56 * 1024 * 1024
SUBLANES = 8
LANES = 128

INT_MIN = -(2 ** 31)
KEY_NEG_INF = INT_MIN + 0x7FFFFF
NEG_BIG = -1e30
LOG2E = 1.4426950408889634


def _params(*sem):
    return pltpu.CompilerParams(dimension_semantics=sem, vmem_limit_bytes=VMEM_LIMIT_BYTES)


def _dot(a, b):
    return jnp.dot(a, b, preferred_element_type=F32)


def _dot_nt(a, b):
    return lax.dot_general(a, b, (((1,), (1,)), ((), ())), preferred_element_type=F32)


def _sigmoid(v):
    return 1.0 / (1.0 + jnp.exp(-v))


def _layer_norm(v, g, b):
    mu = jnp.mean(v, axis=-1, keepdims=True)
    c = v - mu
    var = jnp.mean(c * c, axis=-1, keepdims=True)
    return c * lax.rsqrt(var + LN_EPS) * g + b


def _split_hi_lo(v):
    hi = v.astype(BF16)
    lo = (v - hi.astype(F32)).astype(BF16)
    return hi, lo


def _conv_kernel(x_ref, wb_ref, wc_ref, wu_ref, cw_ref, z_ref, cu_ref, w_s):
    i = pl.program_id(1)
    tm = x_ref.shape[0]
    halo = SUBLANES

    @pl.when(i == 0)
    def _():
        w_s[0] = wb_ref[...].astype(w_s.dtype)
        w_s[1] = wc_ref[...].astype(w_s.dtype)
        w_s[2] = wu_ref[...].astype(w_s.dtype)
        cu_ref[0:halo, :] = jnp.zeros((halo, cu_ref.shape[1]), F32)

    @pl.when(i > 0)
    def _():
        cu_ref[0:halo, :] = cu_ref[tm:tm + halo, :]

    x = x_ref[...]
    cu_ref[halo:halo + tm, :] = _dot_nt(x, w_s[1]) * _dot_nt(x, w_s[2])
    cw = cw_ref[...]
    v = (cw[0:1, :] * cu_ref[halo - 2:halo - 2 + tm, :]
         + cw[1:2, :] * cu_ref[halo - 1:halo - 1 + tm, :]
         + cw[2:3, :] * cu_ref[halo:halo + tm, :])
    z_ref[...] = (_dot_nt(x, w_s[0]) * v).astype(z_ref.dtype)


def _conv_branch(x_bf, w_in_t, conv_w, d_conv, tm=512, tn=512):
    s, d = x_bf.shape
    nj = d_conv // tn
    return pl.pallas_call(
        _conv_kernel,
        grid=(nj, s // tm),
        in_specs=[
            pl.BlockSpec((tm, d), lambda j, i: (i, 0)),
            pl.BlockSpec((tn, d), lambda j, i: (j, 0)),
            pl.BlockSpec((tn, d), lambda j, i: (nj + j, 0)),
            pl.BlockSpec((tn, d), lambda j, i: (2 * nj + j, 0)),
            pl.BlockSpec((CONV_WIDTH, tn), lambda j, i: (0, j)),
        ],
        out_specs=pl.BlockSpec((tm, tn), lambda j, i: (i, j)),
        out_shape=jax.ShapeDtypeStruct((s, d_conv), BF16),
        scratch_shapes=[pltpu.VMEM((tm + SUBLANES, tn), F32),
                        pltpu.VMEM((3, tn, d), BF16)],
        compiler_params=_params("parallel", "arbitrary"),
        name="conv_branch",
    )(x_bf, w_in_t, w_in_t, w_in_t, conv_w)


def _qlat_kernel(x_ref, w_ref, q_ref, w_s):
    @pl.when(pl.program_id(1) == 0)
    def _():
        w_s[...] = w_ref[...].astype(w_s.dtype)

    acc = _dot_nt(w_s[...], x_ref[...]) * (ATTN_SCALE * LOG2E)
    q_ref[...] = acc.astype(q_ref.dtype).reshape(q_ref.shape)


def _qlat_proj(x_bf, w_in_t, row0, tm=512, heads_per_step=4):
    s, d = x_bf.shape
    tn = heads_per_step * D_LATENT
    j0 = row0 // tn
    assert j0 * tn == row0
    return pl.pallas_call(
        _qlat_kernel,
        grid=(N_HEADS // heads_per_step, s // tm),
        in_specs=[
            pl.BlockSpec((tm, d), lambda j, i: (i, 0)),
            pl.BlockSpec((tn, d), lambda j, i: (j0 + j, 0)),
        ],
        out_specs=pl.BlockSpec((heads_per_step, D_LATENT, tm), lambda j, i: (j, 0, i)),
        out_shape=jax.ShapeDtypeStruct((N_HEADS, D_LATENT, s), BF16),
        scratch_shapes=[pltpu.VMEM((tn, d), BF16)],
        compiler_params=_params("parallel", "arbitrary"),
        name="qlat_proj",
    )(x_bf, w_in_t)


def _idx_kernel(x_ref, w_ref, g_ref, ckv_ref, kidx_ref, qit_ref, wit_ref, w_s):
    @pl.when(pl.program_id(0) == 0)
    def _():
        w_s[...] = w_ref[...].astype(w_s.dtype)

    x = x_ref[...]
    nq = N_IDX_HEADS * D_IDX
    r_q, r_k, r_w = D_LATENT, D_LATENT + nq, D_LATENT + nq + D_IDX
    c = _dot_nt(x, w_s[0:r_q])
    ms = jnp.mean(c * c, axis=-1, keepdims=True)
    ckv_ref[...] = (c * lax.rsqrt(ms + RMS_EPS) * g_ref[...]).astype(ckv_ref.dtype)
    k_hi, k_lo = _split_hi_lo(_dot_nt(x, w_s[r_k:r_w]))
    kidx_ref[...] = jnp.concatenate([k_hi, k_hi, k_lo], axis=1)
    q_t = _dot_nt(w_s[r_q:r_k], x)
    q_hi, q_lo = _split_hi_lo(q_t.reshape(N_IDX_HEADS, D_IDX, q_t.shape[1]))
    qit_ref[...] = jnp.concatenate([q_hi, q_lo, q_hi], axis=1)
    wit_ref[...] = _dot_nt(w_s[r_w:r_w + N_IDX_HEADS], x) * IDX_SCALE


def _idx_proj(x_bf, w_in_t, row0, g_kv, tm=512):
    s, d = x_bf.shape
    nq = N_IDX_HEADS * D_IDX
    n = D_LATENT + nq + D_IDX + N_IDX_HEADS
    n_pad = -(-n // LANES) * LANES
    return pl.pallas_call(
        _idx_kernel,
        grid=(s // tm,),
        in_specs=[
            pl.BlockSpec((tm, d), lambda i: (i, 0)),
            pl.BlockSpec((pl.Element(n_pad), pl.Element(d)), lambda i: (row0, 0)),
            pl.BlockSpec((1, D_LATENT), lambda i: (0, 0)),
        ],
        out_specs=[
            pl.BlockSpec((tm, D_LATENT), lambda i: (i, 0)),
            pl.BlockSpec((tm, 3 * D_IDX), lambda i: (i, 0)),
            pl.BlockSpec((N_IDX_HEADS, 3 * D_IDX, tm), lambda i: (0, 0, i)),
            pl.BlockSpec((N_IDX_HEADS, tm), lambda i: (0, i)),
        ],
        out_shape=[
            jax.ShapeDtypeStruct((s, D_LATENT), BF16),
            jax.ShapeDtypeStruct((s, 3 * D_IDX), BF16),
            jax.ShapeDtypeStruct((N_IDX_HEADS, 3 * D_IDX, s), BF16),
            jax.ShapeDtypeStruct((N_IDX_HEADS, s), F32),
        ],
        scratch_shapes=[pltpu.VMEM((n_pad, d), BF16)],
        compiler_params=_params("arbitrary"),
        name="idx_proj",
    )(x_bf, w_in_t, g_kv)


def _gate_kernel(x_ref, w_ref, o_ref, w_s):
    @pl.when(pl.program_id(1) == 0)
    def _():
        w_s[...] = w_ref[...].astype(w_s.dtype)

    o_ref[...] = _sigmoid(_dot_nt(x_ref[...], w_s[...]))


def _gate_proj(x_bf, w_in_t, row0, n, tm=512, tn=1024):
    s, d = x_bf.shape
    return pl.pallas_call(
        _gate_kernel,
        grid=(n // tn, s // tm),
        in_specs=[
            pl.BlockSpec((tm, d), lambda j, i: (i, 0)),
            pl.BlockSpec((pl.Element(tn), pl.Element(d)), lambda j, i: (pl.multiple_of(row0 + j * tn, SUBLANES), 0)),
        ],
        out_specs=pl.BlockSpec((tm, tn), lambda j, i: (i, j)),
        out_shape=jax.ShapeDtypeStruct((s, n), F32),
        scratch_shapes=[pltpu.VMEM((tn, d), BF16)],
        compiler_params=_params("parallel", "arbitrary"),
        name="gate_proj",
    )(x_bf, w_in_t)


def _flip_key(v):
    return v ^ ((v >> 31) & 0x7FFFFFFF)


def _key_to_f32(key):
    return pltpu.bitcast(_flip_key(key), F32)


def _paired_loop(n, body):
    def two(i, carry):
        body(2 * i)
        body(2 * i + 1)
        return carry

    def one(j, carry):
        body(j)
        return carry

    lax.fori_loop(0, n // 2, two, 0)
    lax.fori_loop(2 * (n // 2), n, one, 0)


def _attn_kernel(qit_ref, wit_ref, kidx_ref, qt_ref, kv_ref, kvt_ref, wuv_ref, y_ref,
                 score_ref, qi_s, q_s, w_s, m_ref, l_ref, acc_ref, s_buf, p_buf, a_buf, *, topk):
    b = pl.program_id(0)
    n_heads, d_lat, tq = qt_ref.shape
    n_kt_all, tk, _ = kidx_ref.shape
    n_groups, _, gw = q_s.shape
    hg = gw // tq
    n_kt = ((b + 1) * tq + tk - 1) // tk
    kpos = lax.broadcasted_iota(I32, (tk, tq), 0)
    qpos = b * tq + lax.broadcasted_iota(I32, (tk, tq), 1)

    for h in range(n_heads):
        g, c = divmod(h, hg)
        qi_s[g, :, c * tq:(c + 1) * tq] = qit_ref[h]
        q_s[g, :, c * tq:(c + 1) * tq] = qt_ref[h]
        w_s[g, :, c * tq:(c + 1) * tq] = wit_ref[h:h + 1, :]

    def score_tile(j):
        kidx = kidx_ref[j]
        score = jnp.zeros((tk, tq), F32)
        for g in range(n_groups):
            rel = jnp.maximum(_dot(kidx, qi_s[g]), 0.0) * w_s[g]
            for c in range(hg):
                score = score + rel[:, c * tq:(c + 1) * tq]
        score_ref[j] = jnp.where(kpos + j * tk <= qpos, score, -jnp.inf)

    _paired_loop(n_kt, score_tile)

    fold = 8 * SUBLANES
    kf = float(topk)
    n_pairs = (n_kt + 1) // 2

    @pl.when(n_kt % 2 == 1)
    def _():
        score_ref[n_kt] = jnp.full((tk, tq), -jnp.inf, F32)

    def count(pred):
        def body(i, cnt):
            for j in (2 * i, 2 * i + 1):
                hit = jnp.where(pred(kpos + j * tk, score_ref[j]), 1.0, 0.0)
                cnt = cnt + jnp.sum(hit.reshape(tk // fold, fold, tq), axis=0)
            return cnt

        cnt = lax.fori_loop(0, n_pairs, body, jnp.zeros((fold, tq), F32))
        return jnp.sum(cnt, axis=0, keepdims=True)

    def count_ge(cand_key):
        cand = _key_to_f32(jnp.maximum(cand_key, KEY_NEG_INF))
        return count(lambda pos, sc: sc >= cand)

    c0 = count_ge(jnp.zeros((1, tq), I32))
    thr_key = jnp.where(c0 >= kf, 0, INT_MIN).astype(I32)
    n_ge = jnp.where(c0 >= kf, c0, 0.0)

    def bit_step(t, carry):
        thr_key, n_ge = carry
        cand_key = thr_key | jnp.left_shift(jnp.int32(1), 30 - t)
        c = count_ge(cand_key)
        return jnp.where(c >= kf, cand_key, thr_key), jnp.where(c >= kf, c, n_ge)

    thr_key, n_ge = lax.fori_loop(0, 31, bit_step, (thr_key, n_ge))
    thr = _key_to_f32(jnp.maximum(thr_key, KEY_NEG_INF))

    @pl.when(jnp.max(n_ge) > kf)
    def _():
        need = kf - count(lambda pos, sc: sc > thr)
        n_pos_bits = (n_kt_all * tk - 1).bit_length()

        def pos_step(t, cut):
            cand = cut | jnp.left_shift(jnp.int32(1), n_pos_bits - 1 - t)
            before = count(lambda pos, sc: (sc == thr) & (pos < cand))
            return jnp.where(before < need, cand, cut)

        cut = lax.fori_loop(0, n_pos_bits, pos_step, jnp.zeros((1, tq), I32))

        def drop_tile(j, carry):
            sc = score_ref[j]
            score_ref[j] = jnp.where((sc == thr) & (kpos + j * tk > cut), -jnp.inf, sc)
            return carry

        lax.fori_loop(0, n_kt, drop_tile, 0)

    def mask_tile(j, carry):
        keep = (score_ref[j] >= thr) & (kpos + j * tk <= qpos)
        score_ref[j] = jnp.where(keep, 0.0, NEG_BIG)
        return carry

    lax.fori_loop(0, n_kt, mask_tile, 0)

    m_ref[...] = jnp.full(m_ref.shape, NEG_BIG, F32)
    l_ref[...] = jnp.zeros(l_ref.shape, F32)
    acc_ref[...] = jnp.zeros(acc_ref.shape, F32)
    p_buf[...] = jnp.zeros(p_buf.shape, p_buf.dtype)
    a_buf[...] = jnp.ones(a_buf.shape, F32)

    def qk(j, g):
        bias = jnp.concatenate([score_ref[j]] * hg, axis=1)
        s_buf[g % 2] = _dot(kv_ref[j], q_s[g]) + bias

    def softmax(g):
        s = s_buf[g % 2]
        m_old = m_ref[g]
        m_new = jnp.maximum(m_old, jnp.max(s, axis=0, keepdims=True))
        p = jnp.exp2(s - m_new)
        alpha = jnp.exp2(m_old - m_new)
        l_ref[g] = alpha * l_ref[g] + jnp.sum(p, axis=0, keepdims=True)
        a_buf[g % 2] = alpha
        p_buf[g % 2] = p.astype(p_buf.dtype)
        m_ref[g] = m_new

    def pv(j, g):
        acc_ref[g] = a_buf[g % 2] * acc_ref[g] + _dot(kvt_ref[j], p_buf[g % 2])

    last = n_groups - 1
    qk(0, 0)

    def attn_tile(j):
        for g in range(n_groups):
            if g < last:
                qk(j, g + 1)
            else:
                qk(jnp.minimum(j + 1, n_kt - 1), 0)
            if g > 0:
                pv(j, g - 1)
            else:
                pv(jnp.maximum(j - 1, 0), last)
            softmax(g)

    _paired_loop(n_kt, attn_tile)
    pv(n_kt - 1, last)

    for h in range(n_heads):
        g, c = divmod(h, hg)
        lanes = slice(c * tq, (c + 1) * tq)
        o = (acc_ref[g, :, lanes] / l_ref[g, :, lanes]).T
        y_ref[:, h * D_VHEAD:(h + 1) * D_VHEAD] = _dot(o.astype(wuv_ref.dtype), wuv_ref[h])


def _dsa_attention(qi_t, wi_t, kidx, q_t, kv, kvt, w_uv_bf, topk, tq=128, heads_per_group=2):
    n_heads, d_lat, s = q_t.shape
    n_kt, tk, dk = kidx.shape
    assert n_kt % 2 == 0 and tk % tq == 0
    n_groups = n_heads // heads_per_group
    gw = heads_per_group * tq
    const3 = lambda b: (0, 0, 0)
    return pl.pallas_call(
        functools.partial(_attn_kernel, topk=topk),
        grid=(s // tq,),
        in_specs=[
            pl.BlockSpec((N_IDX_HEADS, dk, tq), lambda b: (0, 0, b)),
            pl.BlockSpec((N_IDX_HEADS, tq), lambda b: (0, b)),
            pl.BlockSpec((n_kt, tk, dk), const3),
            pl.BlockSpec((n_heads, d_lat, tq), lambda b: (0, 0, b)),
            pl.BlockSpec((n_kt, tk, d_lat), const3),
            pl.BlockSpec((n_kt, d_lat, tk), const3),
            pl.BlockSpec((n_heads, d_lat, D_VHEAD), const3),
        ],
        out_specs=pl.BlockSpec((tq, n_heads * D_VHEAD), lambda b: (b, 0)),
        out_shape=jax.ShapeDtypeStruct((s, n_heads * D_VHEAD), F32),
        scratch_shapes=[
            pltpu.VMEM((n_kt, tk, tq), F32),
            pltpu.VMEM((n_groups, dk, gw), BF16),
            pltpu.VMEM((n_groups, d_lat, gw), BF16),
            pltpu.VMEM((n_groups, 1, gw), F32),
            pltpu.VMEM((n_groups, 1, gw), F32),
            pltpu.VMEM((n_groups, 1, gw), F32),
            pltpu.VMEM((n_groups, d_lat, gw), F32),
            pltpu.VMEM((2, tk, gw), F32),
            pltpu.VMEM((2, tk, gw), BF16),
            pltpu.VMEM((2, 1, gw), F32),
        ],
        compiler_params=_params("parallel"),
        name="dsa_attention",
    )(qi_t, wi_t, kidx, q_t, kv, kvt, w_uv_bf)


def _merge_kernel(z_ref, w_ref, gc_ref, ga_ref, ya_ref, o_ref):
    y_conv = _dot(z_ref[...], w_ref[...])
    o_ref[...] = (gc_ref[...] * y_conv + ga_ref[...] * ya_ref[...]).astype(o_ref.dtype)


def _merge(z, w_co, gates, y_attn, tm=512, tn=1024):
    s, d = z.shape
    n = w_co.shape[1]
    nj = n // tn
    return pl.pallas_call(
        _merge_kernel,
        grid=(nj, s // tm),
        in_specs=[
            pl.BlockSpec((tm, d), lambda j, i: (i, 0)),
            pl.BlockSpec((d, tn), lambda j, i: (0, j)),
            pl.BlockSpec((tm, tn), lambda j, i: (i, j)),
            pl.BlockSpec((tm, tn), lambda j, i: (i, nj + j)),
            pl.BlockSpec((tm, tn), lambda j, i: (i, j)),
        ],
        out_specs=pl.BlockSpec((tm, tn), lambda j, i: (i, j)),
        out_shape=jax.ShapeDtypeStruct((s, n), BF16),
        compiler_params=_params("parallel", "parallel"),
        name="merge",
    )(z, w_co, gates, gates, y_attn)


def _oproj_kernel(m_ref, w_ref, h_ref, g_ref, b_ref, o_ref, obf_ref, *, alpha):
    v = alpha * h_ref[...] + _dot(m_ref[...], w_ref[...])
    out = _layer_norm(v, g_ref[...], b_ref[...])
    o_ref[...] = out
    obf_ref[...] = out.astype(obf_ref.dtype)


def _oproj(mixed, w_o, h, g, b, alpha, tm=256):
    s, d = h.shape
    row = pl.BlockSpec((tm, d), lambda i: (i, 0))
    vec = pl.BlockSpec((1, d), lambda i: (0, 0))
    return pl.pallas_call(
        functools.partial(_oproj_kernel, alpha=alpha),
        grid=(s // tm,),
        in_specs=[row, pl.BlockSpec((d, d), lambda i: (0, 0)), row, vec, vec],
        out_specs=[row, row],
        out_shape=[jax.ShapeDtypeStruct((s, d), F32), jax.ShapeDtypeStruct((s, d), BF16)],
        compiler_params=_params("parallel"),
        name="oproj_ln",
    )(mixed, w_o, h, g, b)


def _ffn_kernel(hbf_ref, wu_ref, wd_ref, h_ref, g_ref, b_ref, o_ref, obf_ref, acc_ref, *, alpha):
    f = pl.program_id(1)

    @pl.when(f == 0)
    def _():
        acc_ref[...] = jnp.zeros(acc_ref.shape, F32)

    a = jnp.maximum(_dot(hbf_ref[...], wu_ref[...]), 0.0)
    acc_ref[...] += _dot((a * a).astype(wd_ref.dtype), wd_ref[...])

    @pl.when(f == pl.num_programs(1) - 1)
    def _():
        out = _layer_norm(alpha * h_ref[...] + acc_ref[...], g_ref[...], b_ref[...])
        o_ref[...] = out
        obf_ref[...] = out.astype(obf_ref.dtype)


def _ffn(h_bf, w_up, w_down, h, g, b, alpha, tm=512, tf=512):
    s, d = h.shape
    dff = w_up.shape[1]
    row = pl.BlockSpec((tm, d), lambda i, f: (i, 0))
    vec = pl.BlockSpec((1, d), lambda i, f: (0, 0))
    return pl.pallas_call(
        functools.partial(_ffn_kernel, alpha=alpha),
        grid=(s // tm, dff // tf),
        in_specs=[row, pl.BlockSpec((d, tf), lambda i, f: (0, f)),
                  pl.BlockSpec((tf, d), lambda i, f: (f, 0)), row, vec, vec],
        out_specs=[row, row],
        out_shape=[jax.ShapeDtypeStruct((s, d), F32), jax.ShapeDtypeStruct((s, d), BF16)],
        scratch_shapes=[pltpu.VMEM((tm, d), F32)],
        compiler_params=_params("parallel", "arbitrary"),
        name="ffn_ln",
    )(h_bf, w_up, w_down, h, g, b)


def _ple_kernel(hbf_ref, wg_ref, p_ref, wp_ref, h_ref, g_ref, b_ref, o_ref, *, alpha):
    gate = _sigmoid(_dot(hbf_ref[...], wg_ref[...]))
    pe = gate * _dot(p_ref[...], wp_ref[...])
    o_ref[...] = _layer_norm(alpha * h_ref[...] + pe, g_ref[...], b_ref[...])


def _ple(h_bf, w_pg, p_bf, w_p, h, g, b, alpha, tm=256):
    s, d = h.shape
    dp = p_bf.shape[1]
    row = pl.BlockSpec((tm, d), lambda i: (i, 0))
    vec = pl.BlockSpec((1, d), lambda i: (0, 0))
    return pl.pallas_call(
        functools.partial(_ple_kernel, alpha=alpha),
        grid=(s // tm,),
        in_specs=[row, pl.BlockSpec((d, d), lambda i: (0, 0)),
                  pl.BlockSpec((tm, dp), lambda i: (i, 0)),
                  pl.BlockSpec((dp, d), lambda i: (0, 0)), row, vec, vec],
        out_specs=row,
        out_shape=jax.ShapeDtypeStruct((s, d), F32),
        compiler_params=_params("parallel"),
        name="ple_ln",
    )(h_bf, w_pg, p_bf, w_p, h, g, b)


def _layer(h, p, w_in, conv_w, w_conv_out, g_kv, w_uv, w_o, ln1_g, ln1_b, w_up, w_down,
           ln2_g, ln2_b, w_ple_gate, w_ple, ln3_g, ln3_b, alpha, tk=512):
    s, d = h.shape
    d_conv = conv_w.shape[1]
    nq = N_IDX_HEADS * D_IDX
    c_qlat = 3 * d_conv
    c_ckv = c_qlat + N_HEADS * D_LATENT
    c_gates = c_ckv + D_LATENT + nq + D_IDX + N_IDX_HEADS
    topk = min(TOPK_MAX, s // 4)
    n_kt = s // tk

    h_bf = h.astype(BF16)
    w_in_t = jnp.transpose(w_in)

    z = _conv_branch(h_bf, w_in_t, conv_w, d_conv)
    q_t = _qlat_proj(h_bf, w_in_t, c_qlat)
    ckv, kidx, qi_t, wi_t = _idx_proj(h_bf, w_in_t, c_ckv, g_kv.reshape(1, D_LATENT))
    gates = _gate_proj(h_bf, w_in_t, c_gates, w_in.shape[1] - c_gates)

    kidx = kidx.reshape(n_kt, tk, 3 * D_IDX)
    kv = ckv.reshape(n_kt, tk, D_LATENT)
    kvt = jnp.transpose(kv, (0, 2, 1))

    y_attn = _dsa_attention(qi_t, wi_t, kidx, q_t, kv, kvt, w_uv.astype(BF16), topk)

    mixed = _merge(z, w_conv_out.astype(BF16), gates, y_attn)
    h1, h1_bf = _oproj(mixed, w_o.astype(BF16), h, ln1_g.reshape(1, d), ln1_b.reshape(1, d), alpha)
    h2, h2_bf = _ffn(h1_bf, w_up.astype(BF16), w_down.astype(BF16), h1,
                     ln2_g.reshape(1, d), ln2_b.reshape(1, d), alpha)
    return _ple(h2_bf, w_ple_gate.astype(BF16), p.astype(BF16), w_ple.astype(BF16), h2,
                ln3_g.reshape(1, d), ln3_b.reshape(1, d), alpha)


def kernel(x, p, w_in, conv_w, w_conv_out, g_kv, w_uv, w_o, ln1_g, ln1_b, w_up, w_down,
           ln2_g, ln2_b, w_ple_gate, w_ple, ln3_g, ln3_b):
    depth = w_in.shape[0]
    alpha = (2.0 * depth) ** 0.25
    batch = x.shape[0]
    outs = []
    for bi in range(batch):
        h = x[bi]
        for i in range(depth):
            h = _layer(h, p[i, bi], w_in[i], conv_w[i], w_conv_out[i], g_kv[i], w_uv[i], w_o[i],
                       ln1_g[i], ln1_b[i], w_up[i], w_down[i], ln2_g[i], ln2_b[i],
                       w_ple_gate[i], w_ple[i], ln3_g[i], ln3_b[i], alpha)
        outs.append(h)
    return jnp.stack(outs, axis=0)
```

```python
import functools

import jax
import jax.numpy as jnp
from jax import lax
from jax.experimental import pallas as pl
from jax.experimental.pallas import tpu as pltpu

F32 = jnp.float32
BF16 = jnp.bfloat16
I32 = jnp.int32

N_HEADS = 16
D_LATENT = 256
D_VHEAD = 128
N_IDX_HEADS = 16
D_IDX = 64
TOPK_MAX = 256
CONV_WIDTH = 3
LN_EPS = 1e-5
RMS_EPS = 1e-6
ATTN_SCALE = D_LATENT ** -0.5
IDX_SCALE = (D_IDX ** -0.5) * (N_IDX_HEADS ** -0.5)

VMEM_LIMIT_BYTES = 56 * 1024 * 1024
SUBLANES = 8
LANES = 128

INT_MIN = -(2 ** 31)
KEY_NEG_INF = INT_MIN + 0x7FFFFF
NEG_BIG = -1e30
TILE_UNROLL = 4
LOG2E = 1.4426950408889634


def _params(*sem):
    return pltpu.CompilerParams(dimension_semantics=sem, vmem_limit_bytes=VMEM_LIMIT_BYTES)


def _dot(a, b):
    return jnp.dot(a, b, preferred_element_type=F32)


def _dot_nt(a, b):
    return lax.dot_general(a, b, (((1,), (1,)), ((), ())), preferred_element_type=F32)


def _sigmoid(v):
    return 1.0 / (1.0 + jnp.exp(-v))


def _layer_norm(v, g, b):
    mu = jnp.mean(v, axis=-1, keepdims=True)
    c = v - mu
    var = jnp.mean(c * c, axis=-1, keepdims=True)
    return c * lax.rsqrt(var + LN_EPS) * g + b


def _split_hi_lo(v):
    hi = v.astype(BF16)
    lo = (v - hi.astype(F32)).astype(BF16)
    return hi, lo


def _conv_kernel(x_ref, wb_ref, wc_ref, wu_ref, cw_ref, z_ref, cu_ref, w_s):
    i = pl.program_id(1)
    tm = x_ref.shape[0]
    halo = SUBLANES

    @pl.when(i == 0)
    def _():
        w_s[0] = wb_ref[...].astype(w_s.dtype)
        w_s[1] = wc_ref[...].astype(w_s.dtype)
        w_s[2] = wu_ref[...].astype(w_s.dtype)
        cu_ref[0:halo, :] = jnp.zeros((halo, cu_ref.shape[1]), F32)

    @pl.when(i > 0)
    def _():
        cu_ref[0:halo, :] = cu_ref[tm:tm + halo, :]

    x = x_ref[...]
    cu_ref[halo:halo + tm, :] = _dot_nt(x, w_s[1]) * _dot_nt(x, w_s[2])
    cw = cw_ref[...]
    v = (cw[0:1, :] * cu_ref[halo - 2:halo - 2 + tm, :]
         + cw[1:2, :] * cu_ref[halo - 1:halo - 1 + tm, :]
         + cw[2:3, :] * cu_ref[halo:halo + tm, :])
    z_ref[...] = (_dot_nt(x, w_s[0]) * v).astype(z_ref.dtype)


def _conv_branch(x_bf, w_in_t, conv_w, d_conv, tm=512, tn=512):
    s, d = x_bf.shape
    nj = d_conv // tn
    return pl.pallas_call(
        _conv_kernel,
        grid=(nj, s // tm),
        in_specs=[
            pl.BlockSpec((tm, d), lambda j, i: (i, 0)),
            pl.BlockSpec((tn, d), lambda j, i: (j, 0)),
            pl.BlockSpec((tn, d), lambda j, i: (nj + j, 0)),
            pl.BlockSpec((tn, d), lambda j, i: (2 * nj + j, 0)),
            pl.BlockSpec((CONV_WIDTH, tn), lambda j, i: (0, j)),
        ],
        out_specs=pl.BlockSpec((tm, tn), lambda j, i: (i, j)),
        out_shape=jax.ShapeDtypeStruct((s, d_conv), BF16),
        scratch_shapes=[pltpu.VMEM((tm + SUBLANES, tn), F32),
                        pltpu.VMEM((3, tn, d), BF16)],
        compiler_params=_params("parallel", "arbitrary"),
        name="conv_branch",
    )(x_bf, w_in_t, w_in_t, w_in_t, conv_w)


def _qlat_kernel(x_ref, w_ref, q_ref, w_s):
    @pl.when(pl.program_id(1) == 0)
    def _():
        w_s[...] = w_ref[...].astype(w_s.dtype)

    acc = _dot_nt(w_s[...], x_ref[...]) * (ATTN_SCALE * LOG2E)
    q_ref[...] = acc.astype(q_ref.dtype).reshape(q_ref.shape)


def _qlat_proj(x_bf, w_in_t, row0, tm=512, heads_per_step=4):
    s, d = x_bf.shape
    tn = heads_per_step * D_LATENT
    j0 = row0 // tn
    assert j0 * tn == row0
    return pl.pallas_call(
        _qlat_kernel,
        grid=(N_HEADS // heads_per_step, s // tm),
        in_specs=[
            pl.BlockSpec((tm, d), lambda j, i: (i, 0)),
            pl.BlockSpec((tn, d), lambda j, i: (j0 + j, 0)),
        ],
        out_specs=pl.BlockSpec((heads_per_step, D_LATENT, tm), lambda j, i: (j, 0, i)),
        out_shape=jax.ShapeDtypeStruct((N_HEADS, D_LATENT, s), BF16),
        scratch_shapes=[pltpu.VMEM((tn, d), BF16)],
        compiler_params=_params("parallel", "arbitrary"),
        name="qlat_proj",
    )(x_bf, w_in_t)


def _idx_kernel(x_ref, w_ref, g_ref, ckv_ref, kidx_ref, qit_ref, wit_ref, w_s):
    @pl.when(pl.program_id(0) == 0)
    def _():
        w_s[...] = w_ref[...].astype(w_s.dtype)

    x = x_ref[...]
    nq = N_IDX_HEADS * D_IDX
    r_q, r_k, r_w = D_LATENT, D_LATENT + nq, D_LATENT + nq + D_IDX
    c = _dot_nt(x, w_s[0:r_q])
    ms = jnp.mean(c * c, axis=-1, keepdims=True)
    ckv_ref[...] = (c * lax.rsqrt(ms + RMS_EPS) * g_ref[...]).astype(ckv_ref.dtype)
    k_hi, k_lo = _split_hi_lo(_dot_nt(x, w_s[r_k:r_w]))
    kidx_ref[...] = jnp.concatenate([k_hi, k_hi, k_lo], axis=1)
    q_t = _dot_nt(w_s[r_q:r_k], x)
    q_hi, q_lo = _split_hi_lo(q_t.reshape(N_IDX_HEADS, D_IDX, q_t.shape[1]))
    qit_ref[...] = jnp.concatenate([q_hi, q_lo, q_hi], axis=1)
    wit_ref[...] = _dot_nt(w_s[r_w:r_w + N_IDX_HEADS], x) * IDX_SCALE


def _idx_proj(x_bf, w_in_t, row0, g_kv, tm=512):
    s, d = x_bf.shape
    nq = N_IDX_HEADS * D_IDX
    n = D_LATENT + nq + D_IDX + N_IDX_HEADS
    n_pad = -(-n // LANES) * LANES
    return pl.pallas_call(
        _idx_kernel,
        grid=(s // tm,),
        in_specs=[
            pl.BlockSpec((tm, d), lambda i: (i, 0)),
            pl.BlockSpec((pl.Element(n_pad), pl.Element(d)), lambda i: (row0, 0)),
            pl.BlockSpec((1, D_LATENT), lambda i: (0, 0)),
        ],
        out_specs=[
            pl.BlockSpec((tm, D_LATENT), lambda i: (i, 0)),
            pl.BlockSpec((tm, 3 * D_IDX), lambda i: (i, 0)),
            pl.BlockSpec((N_IDX_HEADS, 3 * D_IDX, tm), lambda i: (0, 0, i)),
            pl.BlockSpec((N_IDX_HEADS, tm), lambda i: (0, i)),
        ],
        out_shape=[
            jax.ShapeDtypeStruct((s, D_LATENT), BF16),
            jax.ShapeDtypeStruct((s, 3 * D_IDX), BF16),
            jax.ShapeDtypeStruct((N_IDX_HEADS, 3 * D_IDX, s), BF16),
            jax.ShapeDtypeStruct((N_IDX_HEADS, s), F32),
        ],
        scratch_shapes=[pltpu.VMEM((n_pad, d), BF16)],
        compiler_params=_params("arbitrary"),
        name="idx_proj",
    )(x_bf, w_in_t, g_kv)


def _gate_kernel(x_ref, w_ref, o_ref, w_s):
    @pl.when(pl.program_id(1) == 0)
    def _():
        w_s[...] = w_ref[...].astype(w_s.dtype)

    o_ref[...] = _sigmoid(_dot_nt(x_ref[...], w_s[...]))


def _gate_proj(x_bf, w_in_t, row0, n, tm=512, tn=1024):
    s, d = x_bf.shape
    return pl.pallas_call(
        _gate_kernel,
        grid=(n // tn, s // tm),
        in_specs=[
            pl.BlockSpec((tm, d), lambda j, i: (i, 0)),
            pl.BlockSpec((pl.Element(tn), pl.Element(d)), lambda j, i: (pl.multiple_of(row0 + j * tn, SUBLANES), 0)),
        ],
        out_specs=pl.BlockSpec((tm, tn), lambda j, i: (i, j)),
        out_shape=jax.ShapeDtypeStruct((s, n), F32),
        scratch_shapes=[pltpu.VMEM((tn, d), BF16)],
        compiler_params=_params("parallel", "arbitrary"),
        name="gate_proj",
    )(x_bf, w_in_t)


def _flip_key(v):
    return v ^ ((v >> 31) & 0x7FFFFFFF)


def _key_to_f32(key):
    return pltpu.bitcast(_flip_key(key), F32)


def _unrolled_loop(n, body, unroll):
    start = 0
    while unroll >= 1:
        def trip(i, carry, start=start, unroll=unroll):
            for r in range(unroll):
                body(start + unroll * i + r)
            return carry

        trips = (n - start) // unroll
        lax.fori_loop(0, trips, trip, 0)
        start = start + unroll * trips
        unroll //= 2


def _attn_kernel(qit_ref, wit_ref, kidx_ref, qt_ref, kv_ref, kvt_ref, wuv_ref, y_ref,
                 score_ref, qi_s, q_s, w_s, m_ref, l_ref, acc_ref, s_buf, p_buf, a_buf, *, topk):
    b = pl.program_id(0)
    n_heads, d_lat, tq = qt_ref.shape
    n_kt_all, tk, _ = kidx_ref.shape
    n_groups, _, gw = q_s.shape
    hg = gw // tq
    n_kt = ((b + 1) * tq + tk - 1) // tk
    kpos = lax.broadcasted_iota(I32, (tk, tq), 0)
    qpos = b * tq + lax.broadcasted_iota(I32, (tk, tq), 1)

    for h in range(n_heads):
        g, c = divmod(h, hg)
        qi_s[g, :, c * tq:(c + 1) * tq] = qit_ref[h]
        q_s[g, :, c * tq:(c + 1) * tq] = qt_ref[h]
        w_s[g, :, c * tq:(c + 1) * tq] = wit_ref[h:h + 1, :]

    def score_tile(j):
        kidx = kidx_ref[j]
        score = jnp.zeros((tk, tq), F32)
        for g in range(n_groups):
            rel = jnp.maximum(_dot(kidx, qi_s[g]), 0.0) * w_s[g]
            for c in range(hg):
                score = score + rel[:, c * tq:(c + 1) * tq]
        score_ref[j] = jnp.where(kpos + j * tk <= qpos, score, -jnp.inf)

    _unrolled_loop(n_kt, score_tile, TILE_UNROLL)

    fold = 8 * SUBLANES
    kf = float(topk)
    n_pairs = (n_kt + 1) // 2

    @pl.when(n_kt % 2 == 1)
    def _():
        score_ref[n_kt] = jnp.full((tk, tq), -jnp.inf, F32)

    def count(pred):
        def body(i, cnt):
            for j in (2 * i, 2 * i + 1):
                hit = jnp.where(pred(kpos + j * tk, score_ref[j]), 1.0, 0.0)
                cnt = cnt + jnp.sum(hit.reshape(tk // fold, fold, tq), axis=0)
            return cnt

        cnt = lax.fori_loop(0, n_pairs, body, jnp.zeros((fold, tq), F32))
        return jnp.sum(cnt, axis=0, keepdims=True)

    def count_ge(cand_key):
        cand = _key_to_f32(jnp.maximum(cand_key, KEY_NEG_INF))
        return count(lambda pos, sc: sc >= cand)

    c0 = count_ge(jnp.zeros((1, tq), I32))
    thr_key = jnp.where(c0 >= kf, 0, INT_MIN).astype(I32)
    n_ge = jnp.where(c0 >= kf, c0, 0.0)

    def bit_step(t, carry):
        thr_key, n_ge = carry
        cand_key = thr_key | jnp.left_shift(jnp.int32(1), 30 - t)
        c = count_ge(cand_key)
        return jnp.where(c >= kf, cand_key, thr_key), jnp.where(c >= kf, c, n_ge)

    thr_key, n_ge = lax.fori_loop(0, 31, bit_step, (thr_key, n_ge))
    thr = _key_to_f32(jnp.maximum(thr_key, KEY_NEG_INF))

    @pl.when(jnp.max(n_ge) > kf)
    def _():
        need = kf - count(lambda pos, sc: sc > thr)
        n_pos_bits = (n_kt_all * tk - 1).bit_length()

        def pos_step(t, cut):
            cand = cut | jnp.left_shift(jnp.int32(1), n_pos_bits - 1 - t)
            before = count(lambda pos, sc: (sc == thr) & (pos < cand))
            return jnp.where(before < need, cand, cut)

        cut = lax.fori_loop(0, n_pos_bits, pos_step, jnp.zeros((1, tq), I32))

        def drop_tile(j, carry):
            sc = score_ref[j]
            score_ref[j] = jnp.where((sc == thr) & (kpos + j * tk > cut), -jnp.inf, sc)
            return carry

        lax.fori_loop(0, n_kt, drop_tile, 0)

    def mask_tile(j, carry):
        keep = (score_ref[j] >= thr) & (kpos + j * tk <= qpos)
        score_ref[j] = jnp.where(keep, 0.0, NEG_BIG)
        return carry

    lax.fori_loop(0, n_kt, mask_tile, 0)

    m_ref[...] = jnp.full(m_ref.shape, NEG_BIG, F32)
    l_ref[...] = jnp.zeros(l_ref.shape, F32)
    acc_ref[...] = jnp.zeros(acc_ref.shape, F32)
    p_buf[...] = jnp.zeros(p_buf.shape, p_buf.dtype)
    a_buf[...] = jnp.ones(a_buf.shape, F32)

    def qk(j, g):
        bias = jnp.concatenate([score_ref[j]] * hg, axis=1)
        s_buf[g % 2] = _dot(kv_ref[j], q_s[g]) + bias

    def softmax(g):
        s = s_buf[g % 2]
        m_old = m_ref[g]
        m_new = jnp.maximum(m_old, jnp.max(s, axis=0, keepdims=True))
        p = jnp.exp2(s - m_new)
        alpha = jnp.exp2(m_old - m_new)
        l_ref[g] = alpha * l_ref[g] + jnp.sum(p, axis=0, keepdims=True)
        a_buf[g % 2] = alpha
        p_buf[g % 2] = p.astype(p_buf.dtype)
        m_ref[g] = m_new

    def pv(j, g):
        acc_ref[g] = a_buf[g % 2] * acc_ref[g] + _dot(kvt_ref[j], p_buf[g % 2])

    last = n_groups - 1
    qk(0, 0)

    def attn_tile(j):
        for g in range(n_groups):
            if g < last:
                qk(j, g + 1)
            else:
                qk(jnp.minimum(j + 1, n_kt - 1), 0)
            if g > 0:
                pv(j, g - 1)
            else:
                pv(jnp.maximum(j - 1, 0), last)
            softmax(g)

    _unrolled_loop(n_kt, attn_tile, TILE_UNROLL)
    pv(n_kt - 1, last)

    for h in range(n_heads):
        g, c = divmod(h, hg)
        lanes = slice(c * tq, (c + 1) * tq)
        o = (acc_ref[g, :, lanes] / l_ref[g, :, lanes]).T
        y_ref[:, h * D_VHEAD:(h + 1) * D_VHEAD] = _dot(o.astype(wuv_ref.dtype), wuv_ref[h])


def _dsa_attention(qi_t, wi_t, kidx, q_t, kv, kvt, w_uv_bf, topk, tq=128, heads_per_group=2):
    n_heads, d_lat, s = q_t.shape
    n_kt, tk, dk = kidx.shape
    assert n_kt % 2 == 0 and tk % tq == 0
    n_groups = n_heads // heads_per_group
    gw = heads_per_group * tq
    const3 = lambda b: (0, 0, 0)
    return pl.pallas_call(
        functools.partial(_attn_kernel, topk=topk),
        grid=(s // tq,),
        in_specs=[
            pl.BlockSpec((N_IDX_HEADS, dk, tq), lambda b: (0, 0, b)),
            pl.BlockSpec((N_IDX_HEADS, tq), lambda b: (0, b)),
            pl.BlockSpec((n_kt, tk, dk), const3),
            pl.BlockSpec((n_heads, d_lat, tq), lambda b: (0, 0, b)),
            pl.BlockSpec((n_kt, tk, d_lat), const3),
            pl.BlockSpec((n_kt, d_lat, tk), const3),
            pl.BlockSpec((n_heads, d_lat, D_VHEAD), const3),
        ],
        out_specs=pl.BlockSpec((tq, n_heads * D_VHEAD), lambda b: (b, 0)),
        out_shape=jax.ShapeDtypeStruct((s, n_heads * D_VHEAD), F32),
        scratch_shapes=[
            pltpu.VMEM((n_kt, tk, tq), F32),
            pltpu.VMEM((n_groups, dk, gw), BF16),
            pltpu.VMEM((n_groups, d_lat, gw), BF16),
            pltpu.VMEM((n_groups, 1, gw), F32),
            pltpu.VMEM((n_groups, 1, gw), F32),
            pltpu.VMEM((n_groups, 1, gw), F32),
            pltpu.VMEM((n_groups, d_lat, gw), F32),
            pltpu.VMEM((2, tk, gw), F32),
            pltpu.VMEM((2, tk, gw), BF16),
            pltpu.VMEM((2, 1, gw), F32),
        ],
        compiler_params=_params("parallel"),
        name="dsa_attention",
    )(qi_t, wi_t, kidx, q_t, kv, kvt, w_uv_bf)


def _merge_kernel(z_ref, w_ref, gc_ref, ga_ref, ya_ref, o_ref):
    y_conv = _dot(z_ref[...], w_ref[...])
    o_ref[...] = (gc_ref[...] * y_conv + ga_ref[...] * ya_ref[...]).astype(o_ref.dtype)


def _merge(z, w_co, gates, y_attn, tm=512, tn=1024):
    s, d = z.shape
    n = w_co.shape[1]
    nj = n // tn
    return pl.pallas_call(
        _merge_kernel,
        grid=(nj, s // tm),
        in_specs=[
            pl.BlockSpec((tm, d), lambda j, i: (i, 0)),
            pl.BlockSpec((d, tn), lambda j, i: (0, j)),
            pl.BlockSpec((tm, tn), lambda j, i: (i, j)),
            pl.BlockSpec((tm, tn), lambda j, i: (i, nj + j)),
            pl.BlockSpec((tm, tn), lambda j, i: (i, j)),
        ],
        out_specs=pl.BlockSpec((tm, tn), lambda j, i: (i, j)),
        out_shape=jax.ShapeDtypeStruct((s, n), BF16),
        compiler_params=_params("parallel", "parallel"),
        name="merge",
    )(z, w_co, gates, gates, y_attn)


def _oproj_kernel(m_ref, w_ref, h_ref, g_ref, b_ref, o_ref, obf_ref, *, alpha):
    v = alpha * h_ref[...] + _dot(m_ref[...], w_ref[...])
    out = _layer_norm(v, g_ref[...], b_ref[...])
    o_ref[...] = out
    obf_ref[...] = out.astype(obf_ref.dtype)


def _oproj(mixed, w_o, h, g, b, alpha, tm=256):
    s, d = h.shape
    row = pl.BlockSpec((tm, d), lambda i: (i, 0))
    vec = pl.BlockSpec((1, d), lambda i: (0, 0))
    return pl.pallas_call(
        functools.partial(_oproj_kernel, alpha=alpha),
        grid=(s // tm,),
        in_specs=[row, pl.BlockSpec((d, d), lambda i: (0, 0)), row, vec, vec],
        out_specs=[row, row],
        out_shape=[jax.ShapeDtypeStruct((s, d), F32), jax.ShapeDtypeStruct((s, d), BF16)],
        compiler_params=_params("parallel"),
        name="oproj_ln",
    )(mixed, w_o, h, g, b)


def _ffn_kernel(hbf_ref, wu_ref, wd_ref, h_ref, g_ref, b_ref, o_ref, obf_ref, acc_ref, *, alpha):
    f = pl.program_id(1)

    @pl.when(f == 0)
    def _():
        acc_ref[...] = jnp.zeros(acc_ref.shape, F32)

    a = jnp.maximum(_dot(hbf_ref[...], wu_ref[...]), 0.0)
    acc_ref[...] += _dot((a * a).astype(wd_ref.dtype), wd_ref[...])

    @pl.when(f == pl.num_programs(1) - 1)
    def _():
        out = _layer_norm(alpha * h_ref[...] + acc_ref[...], g_ref[...], b_ref[...])
        o_ref[...] = out
        obf_ref[...] = out.astype(obf_ref.dtype)


def _ffn(h_bf, w_up, w_down, h, g, b, alpha, tm=512, tf=512):
    s, d = h.shape
    dff = w_up.shape[1]
    row = pl.BlockSpec((tm, d), lambda i, f: (i, 0))
    vec = pl.BlockSpec((1, d), lambda i, f: (0, 0))
    return pl.pallas_call(
        functools.partial(_ffn_kernel, alpha=alpha),
        grid=(s // tm, dff // tf),
        in_specs=[row, pl.BlockSpec((d, tf), lambda i, f: (0, f)),
                  pl.BlockSpec((tf, d), lambda i, f: (f, 0)), row, vec, vec],
        out_specs=[row, row],
        out_shape=[jax.ShapeDtypeStruct((s, d), F32), jax.ShapeDtypeStruct((s, d), BF16)],
        scratch_shapes=[pltpu.VMEM((tm, d), F32)],
        compiler_params=_params("parallel", "arbitrary"),
        name="ffn_ln",
    )(h_bf, w_up, w_down, h, g, b)


def _ple_kernel(hbf_ref, wg_ref, p_ref, wp_ref, h_ref, g_ref, b_ref, o_ref, *, alpha):
    gate = _sigmoid(_dot(hbf_ref[...], wg_ref[...]))
    pe = gate * _dot(p_ref[...], wp_ref[...])
    o_ref[...] = _layer_norm(alpha * h_ref[...] + pe, g_ref[...], b_ref[...])


def _ple(h_bf, w_pg, p_bf, w_p, h, g, b, alpha, tm=256):
    s, d = h.shape
    dp = p_bf.shape[1]
    row = pl.BlockSpec((tm, d), lambda i: (i, 0))
    vec = pl.BlockSpec((1, d), lambda i: (0, 0))
    return pl.pallas_call(
        functools.partial(_ple_kernel, alpha=alpha),
        grid=(s // tm,),
        in_specs=[row, pl.BlockSpec((d, d), lambda i: (0, 0)),
                  pl.BlockSpec((tm, dp), lambda i: (i, 0)),
                  pl.BlockSpec((dp, d), lambda i: (0, 0)), row, vec, vec],
        out_specs=row,
        out_shape=jax.ShapeDtypeStruct((s, d), F32),
        compiler_params=_params("parallel"),
        name="ple_ln",
    )(h_bf, w_pg, p_bf, w_p, h, g, b)


def _layer(h, p, w_in, conv_w, w_conv_out, g_kv, w_uv, w_o, ln1_g, ln1_b, w_up, w_down,
           ln2_g, ln2_b, w_ple_gate, w_ple, ln3_g, ln3_b, alpha, tk=512):
    s, d = h.shape
    d_conv = conv_w.shape[1]
    nq = N_IDX_HEADS * D_IDX
    c_qlat = 3 * d_conv
    c_ckv = c_qlat + N_HEADS * D_LATENT
    c_gates = c_ckv + D_LATENT + nq + D_IDX + N_IDX_HEADS
    topk = min(TOPK_MAX, s // 4)
    n_kt = s // tk

    h_bf = h.astype(BF16)
    w_in_t = jnp.transpose(w_in)

    z = _conv_branch(h_bf, w_in_t, conv_w, d_conv)
    q_t = _qlat_proj(h_bf, w_in_t, c_qlat)
    ckv, kidx, qi_t, wi_t = _idx_proj(h_bf, w_in_t, c_ckv, g_kv.reshape(1, D_LATENT))
    gates = _gate_proj(h_bf, w_in_t, c_gates, w_in.shape[1] - c_gates)

    kidx = kidx.reshape(n_kt, tk, 3 * D_IDX)
    kv = ckv.reshape(n_kt, tk, D_LATENT)
    kvt = jnp.transpose(kv, (0, 2, 1))

    y_attn = _dsa_attention(qi_t, wi_t, kidx, q_t, kv, kvt, w_uv.astype(BF16), topk)

    mixed = _merge(z, w_conv_out.astype(BF16), gates, y_attn)
    h1, h1_bf = _oproj(mixed, w_o.astype(BF16), h, ln1_g.reshape(1, d), ln1_b.reshape(1, d), alpha)
    h2, h2_bf = _ffn(h1_bf, w_up.astype(BF16), w_down.astype(BF16), h1,
                     ln2_g.reshape(1, d), ln2_b.reshape(1, d), alpha)
    return _ple(h2_bf, w_ple_gate.astype(BF16), p.astype(BF16), w_ple.astype(BF16), h2,
                ln3_g.reshape(1, d), ln3_b.reshape(1, d), alpha)


def kernel(x, p, w_in, conv_w, w_conv_out, g_kv, w_uv, w_o, ln1_g, ln1_b, w_up, w_down,
           ln2_g, ln2_b, w_ple_gate, w_ple, ln3_g, ln3_b):
    depth = w_in.shape[0]
    alpha = (2.0 * depth) ** 0.25
    batch = x.shape[0]
    outs = []
    for bi in range(batch):
        h = x[bi]
        for i in range(depth):
            h = _layer(h, p[i, bi], w_in[i], conv_w[i], w_conv_out[i], g_kv[i], w_uv[i], w_o[i],
                       ln1_g[i], ln1_b[i], w_up[i], w_down[i], ln2_g[i], ln2_b[i],
                       w_ple_gate[i], w_ple[i], ln3_g[i], ln3_b[i], alpha)
        outs.append(h)
    return jnp.stack(outs, axis=0)
```

```python
import functools

import jax
import jax.numpy as jnp
from jax import lax
from jax.experimental import pallas as pl
from jax.experimental.pallas import tpu as pltpu

F32 = jnp.float32
BF16 = jnp.bfloat16
I32 = jnp.int32

N_HEADS = 16
D_LATENT = 256
D_VHEAD = 128
N_IDX_HEADS = 16
D_IDX = 64
TOPK_MAX = 256
CONV_WIDTH = 3
LN_EPS = 1e-5
RMS_EPS = 1e-6
ATTN_SCALE = D_LATENT ** -0.5
IDX_SCALE = (D_IDX ** -0.5) * (N_IDX_HEADS ** -0.5)

VMEM_LIMIT_BYTES = 56 * 1024 * 1024
SUBLANES = 8
LANES = 128

INT_MIN = -(2 ** 31)
KEY_NEG_INF = INT_MIN + 0x7FFFFF
NEG_BIG = -1e30
TILE_UNROLL = 4
LOG2E = 1.4426950408889634


def _params(*sem):
    return pltpu.CompilerParams(dimension_semantics=sem, vmem_limit_bytes=VMEM_LIMIT_BYTES)


def _dot(a, b):
    return jnp.dot(a, b, preferred_element_type=F32)


def _dot_nt(a, b):
    return lax.dot_general(a, b, (((1,), (1,)), ((), ())), preferred_element_type=F32)


def _sigmoid(v):
    return 1.0 / (1.0 + jnp.exp(-v))


def _layer_norm(v, g, b):
    mu = jnp.mean(v, axis=-1, keepdims=True)
    c = v - mu
    var = jnp.mean(c * c, axis=-1, keepdims=True)
    return c * lax.rsqrt(var + LN_EPS) * g + b


def _split_hi_lo(v):
    hi = v.astype(BF16)
    lo = (v - hi.astype(F32)).astype(BF16)
    return hi, lo


def _conv_kernel(x_ref, wb_ref, wc_ref, wu_ref, cw_ref, z_ref, cu_ref, w_s):
    i = pl.program_id(1)
    tm = x_ref.shape[0]
    halo = SUBLANES

    @pl.when(i == 0)
    def _():
        w_s[0] = wb_ref[...].astype(w_s.dtype)
        w_s[1] = wc_ref[...].astype(w_s.dtype)
        w_s[2] = wu_ref[...].astype(w_s.dtype)
        cu_ref[0:halo, :] = jnp.zeros((halo, cu_ref.shape[1]), F32)

    @pl.when(i > 0)
    def _():
        cu_ref[0:halo, :] = cu_ref[tm:tm + halo, :]

    x = x_ref[...]
    cu_ref[halo:halo + tm, :] = _dot_nt(x, w_s[1]) * _dot_nt(x, w_s[2])
    cw = cw_ref[...]
    v = (cw[0:1, :] * cu_ref[halo - 2:halo - 2 + tm, :]
         + cw[1:2, :] * cu_ref[halo - 1:halo - 1 + tm, :]
         + cw[2:3, :] * cu_ref[halo:halo + tm, :])
    z_ref[...] = (_dot_nt(x, w_s[0]) * v).astype(z_ref.dtype)


def _conv_branch(x_bf, w_in_t, conv_w, d_conv, tm=512, tn=512):
    s, d = x_bf.shape
    nj = d_conv // tn
    return pl.pallas_call(
        _conv_kernel,
        grid=(nj, s // tm),
        in_specs=[
            pl.BlockSpec((tm, d), lambda j, i: (i, 0)),
            pl.BlockSpec((tn, d), lambda j, i: (j, 0)),
            pl.BlockSpec((tn, d), lambda j, i: (nj + j, 0)),
            pl.BlockSpec((tn, d), lambda j, i: (2 * nj + j, 0)),
            pl.BlockSpec((CONV_WIDTH, tn), lambda j, i: (0, j)),
        ],
        out_specs=pl.BlockSpec((tm, tn), lambda j, i: (i, j)),
        out_shape=jax.ShapeDtypeStruct((s, d_conv), BF16),
        scratch_shapes=[pltpu.VMEM((tm + SUBLANES, tn), F32),
                        pltpu.VMEM((3, tn, d), BF16)],
        compiler_params=_params("parallel", "arbitrary"),
        name="conv_branch",
    )(x_bf, w_in_t, w_in_t, w_in_t, conv_w)


def _qlat_kernel(x_ref, w_ref, q_ref, w_s):
    @pl.when(pl.program_id(1) == 0)
    def _():
        w_s[...] = w_ref[...].astype(w_s.dtype)

    acc = _dot_nt(w_s[...], x_ref[...]) * (ATTN_SCALE * LOG2E)
    q_ref[...] = acc.astype(q_ref.dtype).reshape(q_ref.shape)


def _qlat_proj(x_bf, w_in_t, row0, tm=512, heads_per_step=4):
    s, d = x_bf.shape
    tn = heads_per_step * D_LATENT
    j0 = row0 // tn
    assert j0 * tn == row0
    return pl.pallas_call(
        _qlat_kernel,
        grid=(N_HEADS // heads_per_step, s // tm),
        in_specs=[
            pl.BlockSpec((tm, d), lambda j, i: (i, 0)),
            pl.BlockSpec((tn, d), lambda j, i: (j0 + j, 0)),
        ],
        out_specs=pl.BlockSpec((heads_per_step, D_LATENT, tm), lambda j, i: (j, 0, i)),
        out_shape=jax.ShapeDtypeStruct((N_HEADS, D_LATENT, s), BF16),
        scratch_shapes=[pltpu.VMEM((tn, d), BF16)],
        compiler_params=_params("parallel", "arbitrary"),
        name="qlat_proj",
    )(x_bf, w_in_t)


def _idx_kernel(x_ref, w_ref, g_ref, ckv_ref, kidx_ref, qit_ref, wit_ref, w_s):
    @pl.when(pl.program_id(0) == 0)
    def _():
        w_s[...] = w_ref[...].astype(w_s.dtype)

    x = x_ref[...]
    nq = N_IDX_HEADS * D_IDX
    r_q, r_k, r_w = D_LATENT, D_LATENT + nq, D_LATENT + nq + D_IDX
    c = _dot_nt(x, w_s[0:r_q])
    ms = jnp.mean(c * c, axis=-1, keepdims=True)
    ckv_ref[...] = (c * lax.rsqrt(ms + RMS_EPS) * g_ref[...]).astype(ckv_ref.dtype)
    k_hi, k_lo = _split_hi_lo(_dot_nt(x, w_s[r_k:r_w]))
    kidx_ref[...] = jnp.concatenate([k_hi, k_hi, k_lo], axis=1)
    q_t = _dot_nt(w_s[r_q:r_k], x)
    q_hi, q_lo = _split_hi_lo(q_t.reshape(N_IDX_HEADS, D_IDX, q_t.shape[1]))
    qit_ref[...] = jnp.concatenate([q_hi, q_lo, q_hi], axis=1)
    wit_ref[...] = _dot_nt(w_s[r_w:r_w + N_IDX_HEADS], x) * IDX_SCALE


def _idx_proj(x_bf, w_in_t, row0, g_kv, tm=512):
    s, d = x_bf.shape
    nq = N_IDX_HEADS * D_IDX
    n = D_LATENT + nq + D_IDX + N_IDX_HEADS
    n_pad = -(-n // LANES) * LANES
    return pl.pallas_call(
        _idx_kernel,
        grid=(s // tm,),
        in_specs=[
            pl.BlockSpec((tm, d), lambda i: (i, 0)),
            pl.BlockSpec((pl.Element(n_pad), pl.Element(d)), lambda i: (row0, 0)),
            pl.BlockSpec((1, D_LATENT), lambda i: (0, 0)),
        ],
        out_specs=[
            pl.BlockSpec((tm, D_LATENT), lambda i: (i, 0)),
            pl.BlockSpec((tm, 3 * D_IDX), lambda i: (i, 0)),
            pl.BlockSpec((N_IDX_HEADS, 3 * D_IDX, tm), lambda i: (0, 0, i)),
            pl.BlockSpec((N_IDX_HEADS, tm), lambda i: (0, i)),
        ],
        out_shape=[
            jax.ShapeDtypeStruct((s, D_LATENT), BF16),
            jax.ShapeDtypeStruct((s, 3 * D_IDX), BF16),
            jax.ShapeDtypeStruct((N_IDX_HEADS, 3 * D_IDX, s), BF16),
            jax.ShapeDtypeStruct((N_IDX_HEADS, s), F32),
        ],
        scratch_shapes=[pltpu.VMEM((n_pad, d), BF16)],
        compiler_params=_params("arbitrary"),
        name="idx_proj",
    )(x_bf, w_in_t, g_kv)


def _gate_kernel(x_ref, w_ref, o_ref, w_s):
    @pl.when(pl.program_id(1) == 0)
    def _():
        w_s[...] = w_ref[...].astype(w_s.dtype)

    o_ref[...] = _sigmoid(_dot_nt(x_ref[...], w_s[...]))


def _gate_proj(x_bf, w_in_t, row0, n, tm=512, tn=1024):
    s, d = x_bf.shape
    return pl.pallas_call(
        _gate_kernel,
        grid=(n // tn, s // tm),
        in_specs=[
            pl.BlockSpec((tm, d), lambda j, i: (i, 0)),
            pl.BlockSpec((pl.Element(tn), pl.Element(d)), lambda j, i: (pl.multiple_of(row0 + j * tn, SUBLANES), 0)),
        ],
        out_specs=pl.BlockSpec((tm, tn), lambda j, i: (i, j)),
        out_shape=jax.ShapeDtypeStruct((s, n), F32),
        scratch_shapes=[pltpu.VMEM((tn, d), BF16)],
        compiler_params=_params("parallel", "arbitrary"),
        name="gate_proj",
    )(x_bf, w_in_t)


def _flip_key(v):
    return v ^ ((v >> 31) & 0x7FFFFFFF)


def _key_to_f32(key):
    return pltpu.bitcast(_flip_key(key), F32)


def _unrolled_loop(n, body, unroll):
    start = 0
    while unroll >= 1:
        def trip(i, carry, start=start, unroll=unroll):
            for r in range(unroll):
                body(start + unroll * i + r)
            return carry

        trips = (n - start) // unroll
        lax.fori_loop(0, trips, trip, 0)
        start = start + unroll * trips
        unroll //= 2


def _attn_kernel(qit_ref, wit_ref, kidx_ref, qt_ref, kv_ref, kvt_ref, wuv_ref, y_ref,
                 score_ref, qi_s, q_s, w_s, m_ref, l_ref, acc_ref, s_buf, p_buf, a_buf, *, topk):
    b = pl.program_id(0)
    n_heads, d_lat, tq = qt_ref.shape
    n_kt_all, tk, _ = kidx_ref.shape
    n_groups, _, gw = q_s.shape
    hg = gw // tq
    n_kt = ((b + 1) * tq + tk - 1) // tk
    kpos = lax.broadcasted_iota(I32, (tk, tq), 0)
    qpos = b * tq + lax.broadcasted_iota(I32, (tk, tq), 1)

    for h in range(n_heads):
        g, c = divmod(h, hg)
        qi_s[g, :, c * tq:(c + 1) * tq] = qit_ref[h]
        q_s[g, :, c * tq:(c + 1) * tq] = qt_ref[h]
        w_s[g, :, c * tq:(c + 1) * tq] = wit_ref[h:h + 1, :]

    def score_tile(j):
        kidx = kidx_ref[j]
        score = jnp.zeros((tk, tq), F32)
        for g in range(n_groups):
            rel = jnp.maximum(_dot(kidx, qi_s[g]), 0.0) * w_s[g]
            for c in range(hg):
                score = score + rel[:, c * tq:(c + 1) * tq]
        score_ref[j] = jnp.where(kpos + j * tk <= qpos, score, -jnp.inf)

    _unrolled_loop(n_kt, score_tile, TILE_UNROLL)

    fold = 8 * SUBLANES
    kf = float(topk)
    n_pairs = (n_kt + 1) // 2

    @pl.when(n_kt % 2 == 1)
    def _():
        score_ref[n_kt] = jnp.full((tk, tq), -jnp.inf, F32)

    def count(pred):
        def body(i, cnt):
            for j in (2 * i, 2 * i + 1):
                hit = jnp.where(pred(kpos + j * tk, score_ref[j]), 1.0, 0.0)
                cnt = cnt + jnp.sum(hit.reshape(tk // fold, fold, tq), axis=0)
            return cnt

        cnt = lax.fori_loop(0, n_pairs, body, jnp.zeros((fold, tq), F32))
        return jnp.sum(cnt, axis=0, keepdims=True)

    def search(pairs):
        def run():
            def count_ge(cand_key):
                cand = _key_to_f32(jnp.maximum(cand_key, KEY_NEG_INF))
                cnt = jnp.zeros((fold, tq), F32)
                for j in range(2 * pairs):
                    hit = jnp.where(score_ref[j] >= cand, 1.0, 0.0)
                    cnt = cnt + jnp.sum(hit.reshape(tk // fold, fold, tq), axis=0)
                return jnp.sum(cnt, axis=0, keepdims=True)

            c0 = count_ge(jnp.zeros((1, tq), I32))
            thr_key = jnp.where(c0 >= kf, 0, INT_MIN).astype(I32)
            n_ge = jnp.where(c0 >= kf, c0, 0.0)

            def bit_step(t, carry):
                thr_key, n_ge = carry
                cand_key = thr_key | jnp.left_shift(jnp.int32(1), 30 - t)
                c = count_ge(cand_key)
                return jnp.where(c >= kf, cand_key, thr_key), jnp.where(c >= kf, c, n_ge)

            return lax.fori_loop(0, 31, bit_step, (thr_key, n_ge))
        return run

    thr_key, n_ge = lax.switch(n_pairs - 1, [search(k) for k in range(1, n_kt_all // 2 + 1)])
    thr = _key_to_f32(jnp.maximum(thr_key, KEY_NEG_INF))

    @pl.when(jnp.max(n_ge) > kf)
    def _():
        need = kf - count(lambda pos, sc: sc > thr)
        n_pos_bits = (n_kt_all * tk - 1).bit_length()

        def pos_step(t, cut):
            cand = cut | jnp.left_shift(jnp.int32(1), n_pos_bits - 1 - t)
            before = count(lambda pos, sc: (sc == thr) & (pos < cand))
            return jnp.where(before < need, cand, cut)

        cut = lax.fori_loop(0, n_pos_bits, pos_step, jnp.zeros((1, tq), I32))

        def drop_tile(j, carry):
            sc = score_ref[j]
            score_ref[j] = jnp.where((sc == thr) & (kpos + j * tk > cut), -jnp.inf, sc)
            return carry

        lax.fori_loop(0, n_kt, drop_tile, 0)

    def mask_tile(j, carry):
        keep = (score_ref[j] >= thr) & (kpos + j * tk <= qpos)
        score_ref[j] = jnp.where(keep, 0.0, NEG_BIG)
        return carry

    lax.fori_loop(0, n_kt, mask_tile, 0)

    m_ref[...] = jnp.full(m_ref.shape, NEG_BIG, F32)
    l_ref[...] = jnp.zeros(l_ref.shape, F32)
    acc_ref[...] = jnp.zeros(acc_ref.shape, F32)
    p_buf[...] = jnp.zeros(p_buf.shape, p_buf.dtype)
    a_buf[...] = jnp.ones(a_buf.shape, F32)

    def qk(j, g):
        bias = jnp.concatenate([score_ref[j]] * hg, axis=1)
        s_buf[g % 2] = _dot(kv_ref[j], q_s[g]) + bias

    def softmax(g):
        s = s_buf[g % 2]
        m_old = m_ref[g]
        m_new = jnp.maximum(m_old, jnp.max(s, axis=0, keepdims=True))
        p = jnp.exp2(s - m_new)
        alpha = jnp.exp2(m_old - m_new)
        l_ref[g] = alpha * l_ref[g] + jnp.sum(p, axis=0, keepdims=True)
        a_buf[g % 2] = alpha
        p_buf[g % 2] = p.astype(p_buf.dtype)
        m_ref[g] = m_new

    def pv(j, g):
        acc_ref[g] = a_buf[g % 2] * acc_ref[g] + _dot(kvt_ref[j], p_buf[g % 2])

    last = n_groups - 1
    qk(0, 0)

    def attn_tile(j):
        for g in range(n_groups):
            if g < last:
                qk(j, g + 1)
            else:
                qk(jnp.minimum(j + 1, n_kt - 1), 0)
            if g > 0:
                pv(j, g - 1)
            else:
                pv(jnp.maximum(j - 1, 0), last)
            softmax(g)

    _unrolled_loop(n_kt, attn_tile, TILE_UNROLL)
    pv(n_kt - 1, last)

    for h in range(n_heads):
        g, c = divmod(h, hg)
        lanes = slice(c * tq, (c + 1) * tq)
        o = (acc_ref[g, :, lanes] / l_ref[g, :, lanes]).T
        y_ref[:, h * D_VHEAD:(h + 1) * D_VHEAD] = _dot(o.astype(wuv_ref.dtype), wuv_ref[h])


def _dsa_attention(qi_t, wi_t, kidx, q_t, kv, kvt, w_uv_bf, topk, tq=128, heads_per_group=2):
    n_heads, d_lat, s = q_t.shape
    n_kt, tk, dk = kidx.shape
    assert n_kt % 2 == 0 and tk % tq == 0
    n_groups = n_heads // heads_per_group
    gw = heads_per_group * tq
    const3 = lambda b: (0, 0, 0)
    return pl.pallas_call(
        functools.partial(_attn_kernel, topk=topk),
        grid=(s // tq,),
        in_specs=[
            pl.BlockSpec((N_IDX_HEADS, dk, tq), lambda b: (0, 0, b)),
            pl.BlockSpec((N_IDX_HEADS, tq), lambda b: (0, b)),
            pl.BlockSpec((n_kt, tk, dk), const3),
            pl.BlockSpec((n_heads, d_lat, tq), lambda b: (0, 0, b)),
            pl.BlockSpec((n_kt, tk, d_lat), const3),
            pl.BlockSpec((n_kt, d_lat, tk), const3),
            pl.BlockSpec((n_heads, d_lat, D_VHEAD), const3),
        ],
        out_specs=pl.BlockSpec((tq, n_heads * D_VHEAD), lambda b: (b, 0)),
        out_shape=jax.ShapeDtypeStruct((s, n_heads * D_VHEAD), F32),
        scratch_shapes=[
            pltpu.VMEM((n_kt, tk, tq), F32),
            pltpu.VMEM((n_groups, dk, gw), BF16),
            pltpu.VMEM((n_groups, d_lat, gw), BF16),
            pltpu.VMEM((n_groups, 1, gw), F32),
            pltpu.VMEM((n_groups, 1, gw), F32),
            pltpu.VMEM((n_groups, 1, gw), F32),
            pltpu.VMEM((n_groups, d_lat, gw), F32),
            pltpu.VMEM((2, tk, gw), F32),
            pltpu.VMEM((2, tk, gw), BF16),
            pltpu.VMEM((2, 1, gw), F32),
        ],
        compiler_params=_params("parallel"),
        name="dsa_attention",
    )(qi_t, wi_t, kidx, q_t, kv, kvt, w_uv_bf)


def _merge_kernel(z_ref, w_ref, gc_ref, ga_ref, ya_ref, o_ref):
    y_conv = _dot(z_ref[...], w_ref[...])
    o_ref[...] = (gc_ref[...] * y_conv + ga_ref[...] * ya_ref[...]).astype(o_ref.dtype)


def _merge(z, w_co, gates, y_attn, tm=512, tn=1024):
    s, d = z.shape
    n = w_co.shape[1]
    nj = n // tn
    return pl.pallas_call(
        _merge_kernel,
        grid=(nj, s // tm),
        in_specs=[
            pl.BlockSpec((tm, d), lambda j, i: (i, 0)),
            pl.BlockSpec((d, tn), lambda j, i: (0, j)),
            pl.BlockSpec((tm, tn), lambda j, i: (i, j)),
            pl.BlockSpec((tm, tn), lambda j, i: (i, nj + j)),
            pl.BlockSpec((tm, tn), lambda j, i: (i, j)),
        ],
        out_specs=pl.BlockSpec((tm, tn), lambda j, i: (i, j)),
        out_shape=jax.ShapeDtypeStruct((s, n), BF16),
        compiler_params=_params("parallel", "parallel"),
        name="merge",
    )(z, w_co, gates, gates, y_attn)


def _oproj_kernel(m_ref, w_ref, h_ref, g_ref, b_ref, o_ref, obf_ref, *, alpha):
    v = alpha * h_ref[...] + _dot(m_ref[...], w_ref[...])
    out = _layer_norm(v, g_ref[...], b_ref[...])
    o_ref[...] = out
    obf_ref[...] = out.astype(obf_ref.dtype)


def _oproj(mixed, w_o, h, g, b, alpha, tm=256):
    s, d = h.shape
    row = pl.BlockSpec((tm, d), lambda i: (i, 0))
    vec = pl.BlockSpec((1, d), lambda i: (0, 0))
    return pl.pallas_call(
        functools.partial(_oproj_kernel, alpha=alpha),
        grid=(s // tm,),
        in_specs=[row, pl.BlockSpec((d, d), lambda i: (0, 0)), row, vec, vec],
        out_specs=[row, row],
        out_shape=[jax.ShapeDtypeStruct((s, d), F32), jax.ShapeDtypeStruct((s, d), BF16)],
        compiler_params=_params("parallel"),
        name="oproj_ln",
    )(mixed, w_o, h, g, b)


def _ffn_kernel(hbf_ref, wu_ref, wd_ref, h_ref, g_ref, b_ref, o_ref, obf_ref, acc_ref, *, alpha):
    f = pl.program_id(1)

    @pl.when(f == 0)
    def _():
        acc_ref[...] = jnp.zeros(acc_ref.shape, F32)

    a = jnp.maximum(_dot(hbf_ref[...], wu_ref[...]), 0.0)
    acc_ref[...] += _dot((a * a).astype(wd_ref.dtype), wd_ref[...])

    @pl.when(f == pl.num_programs(1) - 1)
    def _():
        out = _layer_norm(alpha * h_ref[...] + acc_ref[...], g_ref[...], b_ref[...])
        o_ref[...] = out
        obf_ref[...] = out.astype(obf_ref.dtype)


def _ffn(h_bf, w_up, w_down, h, g, b, alpha, tm=512, tf=512):
    s, d = h.shape
    dff = w_up.shape[1]
    row = pl.BlockSpec((tm, d), lambda i, f: (i, 0))
    vec = pl.BlockSpec((1, d), lambda i, f: (0, 0))
    return pl.pallas_call(
        functools.partial(_ffn_kernel, alpha=alpha),
        grid=(s // tm, dff // tf),
        in_specs=[row, pl.BlockSpec((d, tf), lambda i, f: (0, f)),
                  pl.BlockSpec((tf, d), lambda i, f: (f, 0)), row, vec, vec],
        out_specs=[row, row],
        out_shape=[jax.ShapeDtypeStruct((s, d), F32), jax.ShapeDtypeStruct((s, d), BF16)],
        scratch_shapes=[pltpu.VMEM((tm, d), F32)],
        compiler_params=_params("parallel", "arbitrary"),
        name="ffn_ln",
    )(h_bf, w_up, w_down, h, g, b)


def _ple_kernel(hbf_ref, wg_ref, p_ref, wp_ref, h_ref, g_ref, b_ref, o_ref, *, alpha):
    gate = _sigmoid(_dot(hbf_ref[...], wg_ref[...]))
    pe = gate * _dot(p_ref[...], wp_ref[...])
    o_ref[...] = _layer_norm(alpha * h_ref[...] + pe, g_ref[...], b_ref[...])


def _ple(h_bf, w_pg, p_bf, w_p, h, g, b, alpha, tm=256):
    s, d = h.shape
    dp = p_bf.shape[1]
    row = pl.BlockSpec((tm, d), lambda i: (i, 0))
    vec = pl.BlockSpec((1, d), lambda i: (0, 0))
    return pl.pallas_call(
        functools.partial(_ple_kernel, alpha=alpha),
        grid=(s // tm,),
        in_specs=[row, pl.BlockSpec((d, d), lambda i: (0, 0)),
                  pl.BlockSpec((tm, dp), lambda i: (i, 0)),
                  pl.BlockSpec((dp, d), lambda i: (0, 0)), row, vec, vec],
        out_specs=row,
        out_shape=jax.ShapeDtypeStruct((s, d), F32),
        compiler_params=_params("parallel"),
        name="ple_ln",
    )(h_bf, w_pg, p_bf, w_p, h, g, b)


def _layer(h, p, w_in, conv_w, w_conv_out, g_kv, w_uv, w_o, ln1_g, ln1_b, w_up, w_down,
           ln2_g, ln2_b, w_ple_gate, w_ple, ln3_g, ln3_b, alpha, tk=512):
    s, d = h.shape
    d_conv = conv_w.shape[1]
    nq = N_IDX_HEADS * D_IDX
    c_qlat = 3 * d_conv
    c_ckv = c_qlat + N_HEADS * D_LATENT
    c_gates = c_ckv + D_LATENT + nq + D_IDX + N_IDX_HEADS
    topk = min(TOPK_MAX, s // 4)
    n_kt = s // tk

    h_bf = h.astype(BF16)
    w_in_t = jnp.transpose(w_in)

    z = _conv_branch(h_bf, w_in_t, conv_w, d_conv)
    q_t = _qlat_proj(h_bf, w_in_t, c_qlat)
    ckv, kidx, qi_t, wi_t = _idx_proj(h_bf, w_in_t, c_ckv, g_kv.reshape(1, D_LATENT))
    gates = _gate_proj(h_bf, w_in_t, c_gates, w_in.shape[1] - c_gates)

    kidx = kidx.reshape(n_kt, tk, 3 * D_IDX)
    kv = ckv.reshape(n_kt, tk, D_LATENT)
    kvt = jnp.transpose(kv, (0, 2, 1))

    y_attn = _dsa_attention(qi_t, wi_t, kidx, q_t, kv, kvt, w_uv.astype(BF16), topk)

    mixed = _merge(z, w_conv_out.astype(BF16), gates, y_attn)
    h1, h1_bf = _oproj(mixed, w_o.astype(BF16), h, ln1_g.reshape(1, d), ln1_b.reshape(1, d), alpha)
    h2, h2_bf = _ffn(h1_bf, w_up.astype(BF16), w_down.astype(BF16), h1,
                     ln2_g.reshape(1, d), ln2_b.reshape(1, d), alpha)
    return _ple(h2_bf, w_ple_gate.astype(BF16), p.astype(BF16), w_ple.astype(BF16), h2,
                ln3_g.reshape(1, d), ln3_b.reshape(1, d), alpha)


def kernel(x, p, w_in, conv_w, w_conv_out, g_kv, w_uv, w_o, ln1_g, ln1_b, w_up, w_down,
           ln2_g, ln2_b, w_ple_gate, w_ple, ln3_g, ln3_b):
    depth = w_in.shape[0]
    alpha = (2.0 * depth) ** 0.25
    batch = x.shape[0]
    outs = []
    for bi in range(batch):
        h = x[bi]
        for i in range(depth):
            h = _layer(h, p[i, bi], w_in[i], conv_w[i], w_conv_out[i], g_kv[i], w_uv[i], w_o[i],
                       ln1_g[i], ln1_b[i], w_up[i], w_down[i], ln2_g[i], ln2_b[i],
                       w_ple_gate[i], w_ple[i], ln3_g[i], ln3_b[i], alpha)
        outs.append(h)
    return jnp.stack(outs, axis=0)
```

```python
import functools

import jax
import jax.numpy as jnp
from jax import lax
from jax.experimental import pallas as pl
from jax.experimental.pallas import tpu as pltpu

F32 = jnp.float32
BF16 = jnp.bfloat16
I32 = jnp.int32

N_HEADS = 16
D_LATENT = 256
D_VHEAD = 128
N_IDX_HEADS = 16
D_IDX = 64
TOPK_MAX = 256
CONV_WIDTH = 3
LN_EPS = 1e-5
RMS_EPS = 1e-6
ATTN_SCALE = D_LATENT ** -0.5
IDX_SCALE = (D_IDX ** -0.5) * (N_IDX_HEADS ** -0.5)

VMEM_LIMIT_BYTES = 56 * 1024 * 1024
SUBLANES = 8
LANES = 128

INT_MIN = -(2 ** 31)
KEY_NEG_INF = INT_MIN + 0x7FFFFF
NEG_BIG = -1e30
TILE_UNROLL = 4
LOG2E = 1.4426950408889634


def _params(*sem):
    return pltpu.CompilerParams(dimension_semantics=sem, vmem_limit_bytes=VMEM_LIMIT_BYTES)


def _dot(a, b):
    return jnp.dot(a, b, preferred_element_type=F32)


def _dot_nt(a, b):
    return lax.dot_general(a, b, (((1,), (1,)), ((), ())), preferred_element_type=F32)


def _sigmoid(v):
    return 1.0 / (1.0 + jnp.exp(-v))


def _layer_norm(v, g, b):
    mu = jnp.mean(v, axis=-1, keepdims=True)
    c = v - mu
    var = jnp.mean(c * c, axis=-1, keepdims=True)
    return c * lax.rsqrt(var + LN_EPS) * g + b


def _split_hi_lo(v):
    hi = v.astype(BF16)
    lo = (v - hi.astype(F32)).astype(BF16)
    return hi, lo


def _conv_kernel(x_ref, wb_ref, wc_ref, wu_ref, cw_ref, z_ref, cu_ref, w_s):
    i = pl.program_id(1)
    tm = x_ref.shape[0]
    halo = SUBLANES

    @pl.when(i == 0)
    def _():
        w_s[0] = wb_ref[...].astype(w_s.dtype)
        w_s[1] = wc_ref[...].astype(w_s.dtype)
        w_s[2] = wu_ref[...].astype(w_s.dtype)
        cu_ref[0:halo, :] = jnp.zeros((halo, cu_ref.shape[1]), F32)

    @pl.when(i > 0)
    def _():
        cu_ref[0:halo, :] = cu_ref[tm:tm + halo, :]

    x = x_ref[...]
    cu_ref[halo:halo + tm, :] = _dot_nt(x, w_s[1]) * _dot_nt(x, w_s[2])
    cw = cw_ref[...]
    v = (cw[0:1, :] * cu_ref[halo - 2:halo - 2 + tm, :]
         + cw[1:2, :] * cu_ref[halo - 1:halo - 1 + tm, :]
         + cw[2:3, :] * cu_ref[halo:halo + tm, :])
    z_ref[...] = (_dot_nt(x, w_s[0]) * v).astype(z_ref.dtype)


def _conv_branch(x_bf, w_in_t, conv_w, d_conv, tm=512, tn=512):
    s, d = x_bf.shape
    nj = d_conv // tn
    return pl.pallas_call(
        _conv_kernel,
        grid=(nj, s // tm),
        in_specs=[
            pl.BlockSpec((tm, d), lambda j, i: (i, 0)),
            pl.BlockSpec((tn, d), lambda j, i: (j, 0)),
            pl.BlockSpec((tn, d), lambda j, i: (nj + j, 0)),
            pl.BlockSpec((tn, d), lambda j, i: (2 * nj + j, 0)),
            pl.BlockSpec((CONV_WIDTH, tn), lambda j, i: (0, j)),
        ],
        out_specs=pl.BlockSpec((tm, tn), lambda j, i: (i, j)),
        out_shape=jax.ShapeDtypeStruct((s, d_conv), BF16),
        scratch_shapes=[pltpu.VMEM((tm + SUBLANES, tn), F32),
                        pltpu.VMEM((3, tn, d), BF16)],
        compiler_params=_params("parallel", "arbitrary"),
        name="conv_branch",
    )(x_bf, w_in_t, w_in_t, w_in_t, conv_w)


def _qlat_kernel(x_ref, w_ref, q_ref, w_s):
    @pl.when(pl.program_id(1) == 0)
    def _():
        w_s[...] = w_ref[...].astype(w_s.dtype)

    acc = _dot_nt(w_s[...], x_ref[...]) * (ATTN_SCALE * LOG2E)
    q_ref[...] = acc.astype(q_ref.dtype).reshape(q_ref.shape)


def _qlat_proj(x_bf, w_in_t, row0, tm=512, heads_per_step=4):
    s, d = x_bf.shape
    tn = heads_per_step * D_LATENT
    j0 = row0 // tn
    assert j0 * tn == row0
    return pl.pallas_call(
        _qlat_kernel,
        grid=(N_HEADS // heads_per_step, s // tm),
        in_specs=[
            pl.BlockSpec((tm, d), lambda j, i: (i, 0)),
            pl.BlockSpec((tn, d), lambda j, i: (j0 + j, 0)),
        ],
        out_specs=pl.BlockSpec((heads_per_step, D_LATENT, tm), lambda j, i: (j, 0, i)),
        out_shape=jax.ShapeDtypeStruct((N_HEADS, D_LATENT, s), BF16),
        scratch_shapes=[pltpu.VMEM((tn, d), BF16)],
        compiler_params=_params("parallel", "arbitrary"),
        name="qlat_proj",
    )(x_bf, w_in_t)


def _idx_kernel(x_ref, w_ref, g_ref, ckv_ref, kidx_ref, qit_ref, wit_ref, w_s):
    @pl.when(pl.program_id(0) == 0)
    def _():
        w_s[...] = w_ref[...].astype(w_s.dtype)

    x = x_ref[...]
    nq = N_IDX_HEADS * D_IDX
    r_q, r_k, r_w = D_LATENT, D_LATENT + nq, D_LATENT + nq + D_IDX
    c = _dot_nt(x, w_s[0:r_q])
    ms = jnp.mean(c * c, axis=-1, keepdims=True)
    ckv_ref[...] = (c * lax.rsqrt(ms + RMS_EPS) * g_ref[...]).astype(ckv_ref.dtype)
    k_hi, k_lo = _split_hi_lo(_dot_nt(x, w_s[r_k:r_w]))
    kidx_ref[...] = jnp.concatenate([k_hi, k_hi, k_lo], axis=1)
    q_t = _dot_nt(w_s[r_q:r_k], x)
    q_hi, q_lo = _split_hi_lo(q_t.reshape(N_IDX_HEADS, D_IDX, q_t.shape[1]))
    qit_ref[...] = jnp.concatenate([q_hi, q_lo, q_hi], axis=1)
    wit_ref[...] = _dot_nt(w_s[r_w:r_w + N_IDX_HEADS], x) * IDX_SCALE


def _idx_proj(x_bf, w_in_t, row0, g_kv, tm=512):
    s, d = x_bf.shape
    nq = N_IDX_HEADS * D_IDX
    n = D_LATENT + nq + D_IDX + N_IDX_HEADS
    n_pad = -(-n // LANES) * LANES
    return pl.pallas_call(
        _idx_kernel,
        grid=(s // tm,),
        in_specs=[
            pl.BlockSpec((tm, d), lambda i: (i, 0)),
            pl.BlockSpec((pl.Element(n_pad), pl.Element(d)), lambda i: (row0, 0)),
            pl.BlockSpec((1, D_LATENT), lambda i: (0, 0)),
        ],
        out_specs=[
            pl.BlockSpec((tm, D_LATENT), lambda i: (i, 0)),
            pl.BlockSpec((tm, 3 * D_IDX), lambda i: (i, 0)),
            pl.BlockSpec((N_IDX_HEADS, 3 * D_IDX, tm), lambda i: (0, 0, i)),
            pl.BlockSpec((N_IDX_HEADS, tm), lambda i: (0, i)),
        ],
        out_shape=[
            jax.ShapeDtypeStruct((s, D_LATENT), BF16),
            jax.ShapeDtypeStruct((s, 3 * D_IDX), BF16),
            jax.ShapeDtypeStruct((N_IDX_HEADS, 3 * D_IDX, s), BF16),
            jax.ShapeDtypeStruct((N_IDX_HEADS, s), F32),
        ],
        scratch_shapes=[pltpu.VMEM((n_pad, d), BF16)],
        compiler_params=_params("arbitrary"),
        name="idx_proj",
    )(x_bf, w_in_t, g_kv)


def _flip_key(v):
    return v ^ ((v >> 31) & 0x7FFFFFFF)


def _key_to_f32(key):
    return pltpu.bitcast(_flip_key(key), F32)


def _unrolled_loop(n, body, unroll):
    start = 0
    while unroll >= 1:
        def trip(i, carry, start=start, unroll=unroll):
            for r in range(unroll):
                body(start + unroll * i + r)
            return carry

        trips = (n - start) // unroll
        lax.fori_loop(0, trips, trip, 0)
        start = start + unroll * trips
        unroll //= 2


def _attn_kernel(qit_ref, wit_ref, kidx_ref, qt_ref, kv_ref, kvt_ref, wuv_ref, y_ref,
                 score_ref, qi_s, q_s, w_s, m_ref, l_ref, acc_ref, s_buf, p_buf, a_buf, *, topk):
    b = pl.program_id(0)
    n_heads, d_lat, tq = qt_ref.shape
    n_kt_all, tk, _ = kidx_ref.shape
    n_groups, _, gw = q_s.shape
    hg = gw // tq
    n_kt = ((b + 1) * tq + tk - 1) // tk
    kpos = lax.broadcasted_iota(I32, (tk, tq), 0)
    qpos = b * tq + lax.broadcasted_iota(I32, (tk, tq), 1)

    for h in range(n_heads):
        g, c = divmod(h, hg)
        qi_s[g, :, c * tq:(c + 1) * tq] = qit_ref[h]
        q_s[g, :, c * tq:(c + 1) * tq] = qt_ref[h]
        w_s[g, :, c * tq:(c + 1) * tq] = wit_ref[h:h + 1, :]

    def score_tile(j):
        kidx = kidx_ref[j]
        score = jnp.zeros((tk, tq), F32)
        for g in range(n_groups):
            rel = jnp.maximum(_dot(kidx, qi_s[g]), 0.0) * w_s[g]
            for c in range(hg):
                score = score + rel[:, c * tq:(c + 1) * tq]
        score_ref[j] = jnp.where(kpos + j * tk <= qpos, score, -jnp.inf)

    _unrolled_loop(n_kt, score_tile, TILE_UNROLL)

    fold = 8 * SUBLANES
    kf = float(topk)

    def search(tiles):
        def run():
            def count_ge(cand_key):
                cand = _key_to_f32(jnp.maximum(cand_key, KEY_NEG_INF))
                cnt = jnp.zeros((fold, tq), F32)
                for j in range(tiles):
                    hit = jnp.where(score_ref[j] >= cand, 1.0, 0.0)
                    cnt = cnt + jnp.sum(hit.reshape(tk // fold, fold, tq), axis=0)
                return jnp.sum(cnt, axis=0, keepdims=True)

            c0 = count_ge(jnp.zeros((1, tq), I32))
            thr_key = jnp.where(c0 >= kf, 0, INT_MIN).astype(I32)
            n_ge = jnp.where(c0 >= kf, c0, 0.0)

            def bit_step(t, carry):
                thr_key, n_ge = carry
                cand_key = thr_key | jnp.left_shift(jnp.int32(1), 30 - t)
                c = count_ge(cand_key)
                return jnp.where(c >= kf, cand_key, thr_key), jnp.where(c >= kf, c, n_ge)

            return lax.fori_loop(0, 31, bit_step, (thr_key, n_ge))
        return run

    thr_key, n_ge = lax.switch(n_kt - 1, [search(k) for k in range(1, n_kt_all + 1)])
    thr = _key_to_f32(jnp.maximum(thr_key, KEY_NEG_INF))

    @pl.when(jnp.max(n_ge) > kf)
    def _():
        def count(pred):
            def body(j, cnt):
                hit = jnp.where(pred(kpos + j * tk, score_ref[j]), 1.0, 0.0)
                return cnt + jnp.sum(hit.reshape(tk // fold, fold, tq), axis=0)

            cnt = lax.fori_loop(0, n_kt, body, jnp.zeros((fold, tq), F32))
            return jnp.sum(cnt, axis=0, keepdims=True)

        need = kf - count(lambda pos, sc: sc > thr)
        n_pos_bits = (n_kt_all * tk - 1).bit_length()

        def pos_step(t, cut):
            cand = cut | jnp.left_shift(jnp.int32(1), n_pos_bits - 1 - t)
            before = count(lambda pos, sc: (sc == thr) & (pos < cand))
            return jnp.where(before < need, cand, cut)

        cut = lax.fori_loop(0, n_pos_bits, pos_step, jnp.zeros((1, tq), I32))

        def drop_tile(j, carry):
            sc = score_ref[j]
            score_ref[j] = jnp.where((sc == thr) & (kpos + j * tk > cut), -jnp.inf, sc)
            return carry

        lax.fori_loop(0, n_kt, drop_tile, 0)

    def mask_tile(j, carry):
        keep = (score_ref[j] >= thr) & (kpos + j * tk <= qpos)
        score_ref[j] = jnp.where(keep, 0.0, NEG_BIG)
        return carry

    lax.fori_loop(0, n_kt, mask_tile, 0)

    m_ref[...] = jnp.full(m_ref.shape, NEG_BIG, F32)
    l_ref[...] = jnp.zeros(l_ref.shape, F32)
    acc_ref[...] = jnp.zeros(acc_ref.shape, F32)
    p_buf[...] = jnp.zeros(p_buf.shape, p_buf.dtype)
    a_buf[...] = jnp.ones(a_buf.shape, F32)

    def qk(j, g):
        bias = jnp.concatenate([score_ref[j]] * hg, axis=1)
        s_buf[g % 2] = _dot(kv_ref[j], q_s[g]) + bias

    def softmax(g):
        s = s_buf[g % 2]
        m_old = m_ref[g]
        m_new = jnp.maximum(m_old, jnp.max(s, axis=0, keepdims=True))
        p = jnp.exp2(s - m_new)
        alpha = jnp.exp2(m_old - m_new)
        l_ref[g] = alpha * l_ref[g] + jnp.sum(p, axis=0, keepdims=True)
        a_buf[g % 2] = alpha
        p_buf[g % 2] = p.astype(p_buf.dtype)
        m_ref[g] = m_new

    def pv(j, g):
        acc_ref[g] = a_buf[g % 2] * acc_ref[g] + _dot(kvt_ref[j], p_buf[g % 2])

    last = n_groups - 1
    qk(0, 0)

    def attn_tile(j):
        for g in range(n_groups):
            if g < last:
                qk(j, g + 1)
            else:
                qk(jnp.minimum(j + 1, n_kt - 1), 0)
            if g > 0:
                pv(j, g - 1)
            else:
                pv(jnp.maximum(j - 1, 0), last)
            softmax(g)

    _unrolled_loop(n_kt, attn_tile, TILE_UNROLL)
    pv(n_kt - 1, last)

    for h in range(n_heads):
        g, c = divmod(h, hg)
        lanes = slice(c * tq, (c + 1) * tq)
        o = (acc_ref[g, :, lanes] / l_ref[g, :, lanes]).T
        y_ref[:, h * D_VHEAD:(h + 1) * D_VHEAD] = _dot(o.astype(wuv_ref.dtype), wuv_ref[h])


def _dsa_attention(qi_t, wi_t, kidx, q_t, kv, kvt, w_uv_bf, topk, tq=128, heads_per_group=2):
    n_heads, d_lat, s = q_t.shape
    n_kt, tk, dk = kidx.shape
    assert tk % tq == 0
    n_groups = n_heads // heads_per_group
    gw = heads_per_group * tq
    const3 = lambda b: (0, 0, 0)
    return pl.pallas_call(
        functools.partial(_attn_kernel, topk=topk),
        grid=(s // tq,),
        in_specs=[
            pl.BlockSpec((N_IDX_HEADS, dk, tq), lambda b: (0, 0, b)),
            pl.BlockSpec((N_IDX_HEADS, tq), lambda b: (0, b)),
            pl.BlockSpec((n_kt, tk, dk), const3),
            pl.BlockSpec((n_heads, d_lat, tq), lambda b: (0, 0, b)),
            pl.BlockSpec((n_kt, tk, d_lat), const3),
            pl.BlockSpec((n_kt, d_lat, tk), const3),
            pl.BlockSpec((n_heads, d_lat, D_VHEAD), const3),
        ],
        out_specs=pl.BlockSpec((tq, n_heads * D_VHEAD), lambda b: (b, 0)),
        out_shape=jax.ShapeDtypeStruct((s, n_heads * D_VHEAD), F32),
        scratch_shapes=[
            pltpu.VMEM((n_kt, tk, tq), F32),
            pltpu.VMEM((n_groups, dk, gw), BF16),
            pltpu.VMEM((n_groups, d_lat, gw), BF16),
            pltpu.VMEM((n_groups, 1, gw), F32),
            pltpu.VMEM((n_groups, 1, gw), F32),
            pltpu.VMEM((n_groups, 1, gw), F32),
            pltpu.VMEM((n_groups, d_lat, gw), F32),
            pltpu.VMEM((2, tk, gw), F32),
            pltpu.VMEM((2, tk, gw), BF16),
            pltpu.VMEM((2, 1, gw), F32),
        ],
        compiler_params=_params("parallel"),
        name="dsa_attention",
    )(qi_t, wi_t, kidx, q_t, kv, kvt, w_uv_bf)


def _merge_kernel(x_ref, z_ref, wgc_ref, wga_ref, wco_ref, ya_ref, o_ref, w_s):
    @pl.when(pl.program_id(1) == 0)
    def _():
        w_s[0] = wgc_ref[...].astype(w_s.dtype)
        w_s[1] = wga_ref[...].astype(w_s.dtype)

    x = x_ref[...]
    g_conv = _sigmoid(_dot_nt(x, w_s[0]))
    g_attn = _sigmoid(_dot_nt(x, w_s[1]))
    y_conv = _dot(z_ref[...], wco_ref[...])
    o_ref[...] = (g_conv * y_conv + g_attn * ya_ref[...]).astype(o_ref.dtype)


def _merge(x_bf, z, w_in_t, row0, w_co, y_attn, tm=512, tn=512):
    s, d = z.shape
    n = w_co.shape[1]
    gate_rows = lambda off: pl.BlockSpec(
        (pl.Element(tn), pl.Element(d)),
        lambda j, i: (pl.multiple_of(row0 + off + j * tn, SUBLANES), 0))
    return pl.pallas_call(
        _merge_kernel,
        grid=(n // tn, s // tm),
        in_specs=[
            pl.BlockSpec((tm, d), lambda j, i: (i, 0)),
            pl.BlockSpec((tm, d), lambda j, i: (i, 0)),
            gate_rows(0),
            gate_rows(n),
            pl.BlockSpec((d, tn), lambda j, i: (0, j)),
            pl.BlockSpec((tm, tn), lambda j, i: (i, j)),
        ],
        out_specs=pl.BlockSpec((tm, tn), lambda j, i: (i, j)),
        out_shape=jax.ShapeDtypeStruct((s, n), BF16),
        scratch_shapes=[pltpu.VMEM((2, tn, d), BF16)],
        compiler_params=_params("parallel", "arbitrary"),
        name="merge",
    )(x_bf, z, w_in_t, w_in_t, w_co, y_attn)


def _oproj_kernel(m_ref, w_ref, h_ref, g_ref, b_ref, o_ref, obf_ref, *, alpha):
    v = alpha * h_ref[...] + _dot(m_ref[...], w_ref[...])
    out = _layer_norm(v, g_ref[...], b_ref[...])
    o_ref[...] = out
    obf_ref[...] = out.astype(obf_ref.dtype)


def _oproj(mixed, w_o, h, g, b, alpha, tm=512):
    s, d = h.shape
    row = pl.BlockSpec((tm, d), lambda i: (i, 0))
    vec = pl.BlockSpec((1, d), lambda i: (0, 0))
    return pl.pallas_call(
        functools.partial(_oproj_kernel, alpha=alpha),
        grid=(s // tm,),
        in_specs=[row, pl.BlockSpec((d, d), lambda i: (0, 0)), row, vec, vec],
        out_specs=[row, row],
        out_shape=[jax.ShapeDtypeStruct((s, d), F32), jax.ShapeDtypeStruct((s, d), BF16)],
        compiler_params=_params("parallel"),
        name="oproj_ln",
    )(mixed, w_o, h, g, b)


def _ffn_kernel(hbf_ref, wu_ref, wd_ref, h_ref, g_ref, b_ref, o_ref, obf_ref, acc_ref, *, alpha):
    f = pl.program_id(1)

    @pl.when(f == 0)
    def _():
        acc_ref[...] = jnp.zeros(acc_ref.shape, F32)

    a = jnp.maximum(_dot(hbf_ref[...], wu_ref[...]), 0.0)
    acc_ref[...] += _dot((a * a).astype(wd_ref.dtype), wd_ref[...])

    @pl.when(f == pl.num_programs(1) - 1)
    def _():
        out = _layer_norm(alpha * h_ref[...] + acc_ref[...], g_ref[...], b_ref[...])
        o_ref[...] = out
        obf_ref[...] = out.astype(obf_ref.dtype)


def _ffn(h_bf, w_up, w_down, h, g, b, alpha, tm=512, tf=1024):
    s, d = h.shape
    dff = w_up.shape[1]
    row = pl.BlockSpec((tm, d), lambda i, f: (i, 0))
    vec = pl.BlockSpec((1, d), lambda i, f: (0, 0))
    return pl.pallas_call(
        functools.partial(_ffn_kernel, alpha=alpha),
        grid=(s // tm, dff // tf),
        in_specs=[row, pl.BlockSpec((d, tf), lambda i, f: (0, f)),
                  pl.BlockSpec((tf, d), lambda i, f: (f, 0)), row, vec, vec],
        out_specs=[row, row],
        out_shape=[jax.ShapeDtypeStruct((s, d), F32), jax.ShapeDtypeStruct((s, d), BF16)],
        scratch_shapes=[pltpu.VMEM((tm, d), F32)],
        compiler_params=_params("parallel", "arbitrary"),
        name="ffn_ln",
    )(h_bf, w_up, w_down, h, g, b)


def _ple_kernel(hbf_ref, wg_ref, p_ref, wp_ref, h_ref, g_ref, b_ref, o_ref, *, alpha):
    gate = _sigmoid(_dot(hbf_ref[...], wg_ref[...]))
    pe = gate * _dot(p_ref[...], wp_ref[...])
    o_ref[...] = _layer_norm(alpha * h_ref[...] + pe, g_ref[...], b_ref[...])


def _ple(h_bf, w_pg, p_bf, w_p, h, g, b, alpha, tm=512):
    s, d = h.shape
    dp = p_bf.shape[1]
    row = pl.BlockSpec((tm, d), lambda i: (i, 0))
    vec = pl.BlockSpec((1, d), lambda i: (0, 0))
    return pl.pallas_call(
        functools.partial(_ple_kernel, alpha=alpha),
        grid=(s // tm,),
        in_specs=[row, pl.BlockSpec((d, d), lambda i: (0, 0)),
                  pl.BlockSpec((tm, dp), lambda i: (i, 0)),
                  pl.BlockSpec((dp, d), lambda i: (0, 0)), row, vec, vec],
        out_specs=row,
        out_shape=jax.ShapeDtypeStruct((s, d), F32),
        compiler_params=_params("parallel"),
        name="ple_ln",
    )(h_bf, w_pg, p_bf, w_p, h, g, b)


def _layer(h, p, w_in, conv_w, w_conv_out, g_kv, w_uv, w_o, ln1_g, ln1_b, w_up, w_down,
           ln2_g, ln2_b, w_ple_gate, w_ple, ln3_g, ln3_b, alpha, tk=512):
    s, d = h.shape
    d_conv = conv_w.shape[1]
    nq = N_IDX_HEADS * D_IDX
    c_qlat = 3 * d_conv
    c_ckv = c_qlat + N_HEADS * D_LATENT
    c_gates = c_ckv + D_LATENT + nq + D_IDX + N_IDX_HEADS
    topk = min(TOPK_MAX, s // 4)
    n_kt = s // tk

    h_bf = h.astype(BF16)
    w_in_t = jnp.transpose(w_in)

    z = _conv_branch(h_bf, w_in_t, conv_w, d_conv)
    q_t = _qlat_proj(h_bf, w_in_t, c_qlat)
    ckv, kidx, qi_t, wi_t = _idx_proj(h_bf, w_in_t, c_ckv, g_kv.reshape(1, D_LATENT))

    kidx = kidx.reshape(n_kt, tk, 3 * D_IDX)
    kv = ckv.reshape(n_kt, tk, D_LATENT)
    kvt = jnp.transpose(kv, (0, 2, 1))

    y_attn = _dsa_attention(qi_t, wi_t, kidx, q_t, kv, kvt, w_uv.astype(BF16), topk)

    mixed = _merge(h_bf, z, w_in_t, c_gates, w_conv_out.astype(BF16), y_attn)
    h1, h1_bf = _oproj(mixed, w_o.astype(BF16), h, ln1_g.reshape(1, d), ln1_b.reshape(1, d), alpha)
    h2, h2_bf = _ffn(h1_bf, w_up.astype(BF16), w_down.astype(BF16), h1,
                     ln2_g.reshape(1, d), ln2_b.reshape(1, d), alpha)
    return _ple(h2_bf, w_ple_gate.astype(BF16), p.astype(BF16), w_ple.astype(BF16), h2,
                ln3_g.reshape(1, d), ln3_b.reshape(1, d), alpha)


def kernel(x, p, w_in, conv_w, w_conv_out, g_kv, w_uv, w_o, ln1_g, ln1_b, w_up, w_down,
           ln2_g, ln2_b, w_ple_gate, w_ple, ln3_g, ln3_b):
    depth = w_in.shape[0]
    alpha = (2.0 * depth) ** 0.25
    batch = x.shape[0]
    outs = []
    for bi in range(batch):
        h = x[bi]
        for i in range(depth):
            h = _layer(h, p[i, bi], w_in[i], conv_w[i], w_conv_out[i], g_kv[i], w_uv[i], w_o[i],
                       ln1_g[i], ln1_b[i], w_up[i], w_down[i], ln2_g[i], ln2_b[i],
                       w_ple_gate[i], w_ple[i], ln3_g[i], ln3_b[i], alpha)
        outs.append(h)
    return jnp.stack(outs, axis=0)
```

```python
import functools

import jax
import jax.numpy as jnp
from jax import lax
from jax.experimental import pallas as pl
from jax.experimental.pallas import tpu as pltpu

F32 = jnp.float32
BF16 = jnp.bfloat16
I32 = jnp.int32

N_HEADS = 16
D_LATENT = 256
D_VHEAD = 128
N_IDX_HEADS = 16
D_IDX = 64
TOPK_MAX = 256
CONV_WIDTH = 3
LN_EPS = 1e-5
RMS_EPS = 1e-6
ATTN_SCALE = D_LATENT ** -0.5
IDX_SCALE = (D_IDX ** -0.5) * (N_IDX_HEADS ** -0.5)

VMEM_LIMIT_BYTES = 56 * 1024 * 1024
SUBLANES = 8
LANES = 128

INT_MIN = -(2 ** 31)
KEY_NEG_INF = INT_MIN + 0x7FFFFF
NEG_BIG = -1e30
TILE_UNROLL = 4
LOG2E = 1.4426950408889634


def _params(*sem):
    return pltpu.CompilerParams(dimension_semantics=sem, vmem_limit_bytes=VMEM_LIMIT_BYTES)


def _dot(a, b):
    return jnp.dot(a, b, preferred_element_type=F32)


def _dot_nt(a, b):
    return lax.dot_general(a, b, (((1,), (1,)), ((), ())), preferred_element_type=F32)


def _sigmoid(v):
    return 1.0 / (1.0 + jnp.exp(-v))


def _layer_norm(v, g, b):
    mu = jnp.mean(v, axis=-1, keepdims=True)
    c = v - mu
    var = jnp.mean(c * c, axis=-1, keepdims=True)
    return c * lax.rsqrt(var + LN_EPS) * g + b


def _split_hi_lo(v):
    hi = v.astype(BF16)
    lo = (v - hi.astype(F32)).astype(BF16)
    return hi, lo


def _conv_kernel(x_ref, wb_ref, wc_ref, wu_ref, cw_ref, z_ref, cu_ref, w_s):
    i = pl.program_id(1)
    tm = x_ref.shape[0]
    halo = SUBLANES

    @pl.when(i == 0)
    def _():
        w_s[0] = wb_ref[...].astype(w_s.dtype)
        w_s[1] = wc_ref[...].astype(w_s.dtype)
        w_s[2] = wu_ref[...].astype(w_s.dtype)
        cu_ref[0:halo, :] = jnp.zeros((halo, cu_ref.shape[1]), F32)

    @pl.when(i > 0)
    def _():
        cu_ref[0:halo, :] = cu_ref[tm:tm + halo, :]

    x = x_ref[...]
    cu_ref[halo:halo + tm, :] = _dot_nt(x, w_s[1]) * _dot_nt(x, w_s[2])
    cw = cw_ref[...]
    v = (cw[0:1, :] * cu_ref[halo - 2:halo - 2 + tm, :]
         + cw[1:2, :] * cu_ref[halo - 1:halo - 1 + tm, :]
         + cw[2:3, :] * cu_ref[halo:halo + tm, :])
    z_ref[...] = (_dot_nt(x, w_s[0]) * v).astype(z_ref.dtype)


def _conv_branch(x_bf, w_in_t, conv_w, d_conv, tm=512, tn=512):
    s, d = x_bf.shape
    nj = d_conv // tn
    return pl.pallas_call(
        _conv_kernel,
        grid=(nj, s // tm),
        in_specs=[
            pl.BlockSpec((tm, d), lambda j, i: (i, 0)),
            pl.BlockSpec((tn, d), lambda j, i: (j, 0)),
            pl.BlockSpec((tn, d), lambda j, i: (nj + j, 0)),
            pl.BlockSpec((tn, d), lambda j, i: (2 * nj + j, 0)),
            pl.BlockSpec((CONV_WIDTH, tn), lambda j, i: (0, j)),
        ],
        out_specs=pl.BlockSpec((tm, tn), lambda j, i: (i, j)),
        out_shape=jax.ShapeDtypeStruct((s, d_conv), BF16),
        scratch_shapes=[pltpu.VMEM((tm + SUBLANES, tn), F32),
                        pltpu.VMEM((3, tn, d), BF16)],
        compiler_params=_params("parallel", "arbitrary"),
        name="conv_branch",
    )(x_bf, w_in_t, w_in_t, w_in_t, conv_w)


def _qlat_kernel(x_ref, w_ref, q_ref, w_s):
    @pl.when(pl.program_id(1) == 0)
    def _():
        w_s[...] = w_ref[...].astype(w_s.dtype)

    acc = _dot_nt(w_s[...], x_ref[...]) * (ATTN_SCALE * LOG2E)
    q_ref[...] = acc.astype(q_ref.dtype).reshape(q_ref.shape)


def _qlat_proj(x_bf, w_in_t, row0, tm=512, heads_per_step=4):
    s, d = x_bf.shape
    tn = heads_per_step * D_LATENT
    j0 = row0 // tn
    assert j0 * tn == row0
    return pl.pallas_call(
        _qlat_kernel,
        grid=(N_HEADS // heads_per_step, s // tm),
        in_specs=[
            pl.BlockSpec((tm, d), lambda j, i: (i, 0)),
            pl.BlockSpec((tn, d), lambda j, i: (j0 + j, 0)),
        ],
        out_specs=pl.BlockSpec((heads_per_step, D_LATENT, tm), lambda j, i: (j, 0, i)),
        out_shape=jax.ShapeDtypeStruct((N_HEADS, D_LATENT, s), BF16),
        scratch_shapes=[pltpu.VMEM((tn, d), BF16)],
        compiler_params=_params("parallel", "arbitrary"),
        name="qlat_proj",
    )(x_bf, w_in_t)


def _idx_kernel(x_ref, w_ref, g_ref, xbf_ref, ckv_ref, kidx_ref, qit_ref, wit_ref, w_s):
    @pl.when(pl.program_id(0) == 0)
    def _():
        w_s[...] = w_ref[...].astype(w_s.dtype)

    x = x_ref[...].astype(xbf_ref.dtype)
    xbf_ref[...] = x
    nq = N_IDX_HEADS * D_IDX
    r_q, r_k, r_w = D_LATENT, D_LATENT + nq, D_LATENT + nq + D_IDX
    c = _dot_nt(x, w_s[0:r_q])
    ms = jnp.mean(c * c, axis=-1, keepdims=True)
    ckv_ref[...] = (c * lax.rsqrt(ms + RMS_EPS) * g_ref[...]).astype(ckv_ref.dtype)
    k_hi, k_lo = _split_hi_lo(_dot_nt(x, w_s[r_k:r_w]))
    kidx_ref[...] = jnp.concatenate([k_hi, k_hi, k_lo], axis=1)
    q_t = _dot_nt(w_s[r_q:r_k], x)
    q_hi, q_lo = _split_hi_lo(q_t.reshape(N_IDX_HEADS, D_IDX, q_t.shape[1]))
    qit_ref[...] = jnp.concatenate([q_hi, q_lo, q_hi], axis=1)
    wit_ref[...] = _dot_nt(w_s[r_w:r_w + N_IDX_HEADS], x) * IDX_SCALE


def _idx_proj(x, w_in_t, row0, g_kv, tm=512):
    s, d = x.shape
    nq = N_IDX_HEADS * D_IDX
    n = D_LATENT + nq + D_IDX + N_IDX_HEADS
    n_pad = -(-n // LANES) * LANES
    return pl.pallas_call(
        _idx_kernel,
        grid=(s // tm,),
        in_specs=[
            pl.BlockSpec((tm, d), lambda i: (i, 0)),
            pl.BlockSpec((pl.Element(n_pad), pl.Element(d)), lambda i: (row0, 0)),
            pl.BlockSpec((1, D_LATENT), lambda i: (0, 0)),
        ],
        out_specs=[
            pl.BlockSpec((tm, d), lambda i: (i, 0)),
            pl.BlockSpec((tm, D_LATENT), lambda i: (i, 0)),
            pl.BlockSpec((tm, 3 * D_IDX), lambda i: (i, 0)),
            pl.BlockSpec((N_IDX_HEADS, 3 * D_IDX, tm), lambda i: (0, 0, i)),
            pl.BlockSpec((N_IDX_HEADS, tm), lambda i: (0, i)),
        ],
        out_shape=[
            jax.ShapeDtypeStruct((s, d), BF16),
            jax.ShapeDtypeStruct((s, D_LATENT), BF16),
            jax.ShapeDtypeStruct((s, 3 * D_IDX), BF16),
            jax.ShapeDtypeStruct((N_IDX_HEADS, 3 * D_IDX, s), BF16),
            jax.ShapeDtypeStruct((N_IDX_HEADS, s), F32),
        ],
        scratch_shapes=[pltpu.VMEM((n_pad, d), BF16)],
        compiler_params=_params("arbitrary"),
        name="idx_proj",
    )(x, w_in_t, g_kv)


def _flip_key(v):
    return v ^ ((v >> 31) & 0x7FFFFFFF)


def _key_to_f32(key):
    return pltpu.bitcast(_flip_key(key), F32)


def _unrolled_loop(n, body, unroll):
    start = 0
    while unroll >= 1:
        def trip(i, carry, start=start, unroll=unroll):
            for r in range(unroll):
                body(start + unroll * i + r)
            return carry

        trips = (n - start) // unroll
        lax.fori_loop(0, trips, trip, 0)
        start = start + unroll * trips
        unroll //= 2


def _attn_kernel(qit_ref, wit_ref, kidx_ref, qt_ref, kv_ref, kvt_ref, wuvt_ref, yt_ref,
                 score_ref, qi_s, q_s, w_s, m_ref, l_ref, acc_ref, s_buf, p_buf, a_buf, *, topk):
    b = pl.program_id(0)
    n_heads, d_lat, tq = qt_ref.shape
    n_kt_all, tk, _ = kidx_ref.shape
    n_groups, _, gw = q_s.shape
    hg = gw // tq
    n_kt = ((b + 1) * tq + tk - 1) // tk
    kpos = lax.broadcasted_iota(I32, (tk, tq), 0)
    qpos = b * tq + lax.broadcasted_iota(I32, (tk, tq), 1)

    for h in range(n_heads):
        g, c = divmod(h, hg)
        qi_s[g, :, c * tq:(c + 1) * tq] = qit_ref[h]
        q_s[g, :, c * tq:(c + 1) * tq] = qt_ref[h]
        w_s[g, :, c * tq:(c + 1) * tq] = wit_ref[h:h + 1, :]

    def score_tile(j):
        kidx = kidx_ref[j]
        score = jnp.zeros((tk, tq), F32)
        for g in range(n_groups):
            rel = jnp.maximum(_dot(kidx, qi_s[g]), 0.0) * w_s[g]
            for c in range(hg):
                score = score + rel[:, c * tq:(c + 1) * tq]
        score_ref[j] = jnp.where(kpos + j * tk <= qpos, score, -jnp.inf)

    _unrolled_loop(n_kt, score_tile, TILE_UNROLL)

    fold = 8 * SUBLANES
    kf = float(topk)

    def search(tiles):
        def run():
            def count_ge(cand_key):
                cand = _key_to_f32(jnp.maximum(cand_key, KEY_NEG_INF))
                cnt = jnp.zeros((fold, tq), F32)
                for j in range(tiles):
                    hit = jnp.where(score_ref[j] >= cand, 1.0, 0.0)
                    cnt = cnt + jnp.sum(hit.reshape(tk // fold, fold, tq), axis=0)
                return jnp.sum(cnt, axis=0, keepdims=True)

            c0 = count_ge(jnp.zeros((1, tq), I32))
            thr_key = jnp.where(c0 >= kf, 0, INT_MIN).astype(I32)
            n_ge = jnp.where(c0 >= kf, c0, 0.0)

            def bit_step(t, carry):
                thr_key, n_ge = carry
                cand_key = thr_key | jnp.left_shift(jnp.int32(1), 30 - t)
                c = count_ge(cand_key)
                return jnp.where(c >= kf, cand_key, thr_key), jnp.where(c >= kf, c, n_ge)

            return lax.fori_loop(0, 31, bit_step, (thr_key, n_ge))
        return run

    thr_key, n_ge = lax.switch(n_kt - 1, [search(k) for k in range(1, n_kt_all + 1)])
    thr = _key_to_f32(jnp.maximum(thr_key, KEY_NEG_INF))

    @pl.when(jnp.max(n_ge) > kf)
    def _():
        def count(pred):
            def body(j, cnt):
                hit = jnp.where(pred(kpos + j * tk, score_ref[j]), 1.0, 0.0)
                return cnt + jnp.sum(hit.reshape(tk // fold, fold, tq), axis=0)

            cnt = lax.fori_loop(0, n_kt, body, jnp.zeros((fold, tq), F32))
            return jnp.sum(cnt, axis=0, keepdims=True)

        need = kf - count(lambda pos, sc: sc > thr)
        n_pos_bits = (n_kt_all * tk - 1).bit_length()

        def pos_step(t, cut):
            cand = cut | jnp.left_shift(jnp.int32(1), n_pos_bits - 1 - t)
            before = count(lambda pos, sc: (sc == thr) & (pos < cand))
            return jnp.where(before < need, cand, cut)

        cut = lax.fori_loop(0, n_pos_bits, pos_step, jnp.zeros((1, tq), I32))

        def drop_tile(j, carry):
            sc = score_ref[j]
            score_ref[j] = jnp.where((sc == thr) & (kpos + j * tk > cut), -jnp.inf, sc)
            return carry

        lax.fori_loop(0, n_kt, drop_tile, 0)

    def mask_tile(j, carry):
        keep = (score_ref[j] >= thr) & (kpos + j * tk <= qpos)
        score_ref[j] = jnp.where(keep, 0.0, NEG_BIG)
        return carry

    lax.fori_loop(0, n_kt, mask_tile, 0)

    m_ref[...] = jnp.full(m_ref.shape, NEG_BIG, F32)
    l_ref[...] = jnp.zeros(l_ref.shape, F32)
    acc_ref[...] = jnp.zeros(acc_ref.shape, F32)
    p_buf[...] = jnp.zeros(p_buf.shape, p_buf.dtype)
    a_buf[...] = jnp.ones(a_buf.shape, F32)

    def qk(j, g):
        bias = jnp.concatenate([score_ref[j]] * hg, axis=1)
        s_buf[g % 2] = _dot(kv_ref[j], q_s[g]) + bias

    def softmax(g):
        s = s_buf[g % 2]
        m_old = m_ref[g]
        m_new = jnp.maximum(m_old, jnp.max(s, axis=0, keepdims=True))
        p = jnp.exp2(s - m_new)
        alpha = jnp.exp2(m_old - m_new)
        l_ref[g] = alpha * l_ref[g] + jnp.sum(p, axis=0, keepdims=True)
        a_buf[g % 2] = alpha
        p_buf[g % 2] = p.astype(p_buf.dtype)
        m_ref[g] = m_new

    def pv(j, g):
        acc_ref[g] = a_buf[g % 2] * acc_ref[g] + _dot(kvt_ref[j], p_buf[g % 2])

    last = n_groups - 1
    qk(0, 0)

    def attn_tile(j):
        for g in range(n_groups):
            if g < last:
                qk(j, g + 1)
            else:
                qk(jnp.minimum(j + 1, n_kt - 1), 0)
            if g > 0:
                pv(j, g - 1)
            else:
                pv(jnp.maximum(j - 1, 0), last)
            softmax(g)

    _unrolled_loop(n_kt, attn_tile, TILE_UNROLL)
    pv(n_kt - 1, last)

    for h in range(n_heads):
        g, c = divmod(h, hg)
        lanes = slice(c * tq, (c + 1) * tq)
        o_t = (acc_ref[g, :, lanes] / l_ref[g, :, lanes]).astype(wuvt_ref.dtype)
        yt_ref[h * D_VHEAD:(h + 1) * D_VHEAD, :] = _dot(wuvt_ref[h], o_t)


def _dsa_attention(qi_t, wi_t, kidx, q_t, kv, kvt, w_uv_t, topk, tq=128, heads_per_group=2):
    n_heads, d_lat, s = q_t.shape
    n_kt, tk, dk = kidx.shape
    assert tk % tq == 0
    n_groups = n_heads // heads_per_group
    gw = heads_per_group * tq
    const3 = lambda b: (0, 0, 0)
    return pl.pallas_call(
        functools.partial(_attn_kernel, topk=topk),
        grid=(s // tq,),
        in_specs=[
            pl.BlockSpec((N_IDX_HEADS, dk, tq), lambda b: (0, 0, b)),
            pl.BlockSpec((N_IDX_HEADS, tq), lambda b: (0, b)),
            pl.BlockSpec((n_kt, tk, dk), const3),
            pl.BlockSpec((n_heads, d_lat, tq), lambda b: (0, 0, b)),
            pl.BlockSpec((n_kt, tk, d_lat), const3),
            pl.BlockSpec((n_kt, d_lat, tk), const3),
            pl.BlockSpec((n_heads, D_VHEAD, d_lat), const3),
        ],
        out_specs=pl.BlockSpec((n_heads * D_VHEAD, tq), lambda b: (0, b)),
        out_shape=jax.ShapeDtypeStruct((n_heads * D_VHEAD, s), F32),
        scratch_shapes=[
            pltpu.VMEM((n_kt, tk, tq), F32),
            pltpu.VMEM((n_groups, dk, gw), BF16),
            pltpu.VMEM((n_groups, d_lat, gw), BF16),
            pltpu.VMEM((n_groups, 1, gw), F32),
            pltpu.VMEM((n_groups, 1, gw), F32),
            pltpu.VMEM((n_groups, 1, gw), F32),
            pltpu.VMEM((n_groups, d_lat, gw), F32),
            pltpu.VMEM((2, tk, gw), F32),
            pltpu.VMEM((2, tk, gw), BF16),
            pltpu.VMEM((2, 1, gw), F32),
        ],
        compiler_params=_params("parallel"),
        name="dsa_attention",
    )(qi_t, wi_t, kidx, q_t, kv, kvt, w_uv_t)


def _merge_kernel(x_ref, z_ref, wgc_ref, wga_ref, wco_ref, yat_ref, o_ref, w_s, wco_s):
    @pl.when(pl.program_id(1) == 0)
    def _():
        w_s[0] = wgc_ref[...].astype(w_s.dtype)
        w_s[1] = wga_ref[...].astype(w_s.dtype)
        wco_s[...] = wco_ref[...].astype(wco_s.dtype)

    x = x_ref[...]
    g_conv = _sigmoid(_dot_nt(x, w_s[0]))
    g_attn = _sigmoid(_dot_nt(x, w_s[1]))
    y_conv = _dot(z_ref[...], wco_s[...])
    o_ref[...] = (g_conv * y_conv + g_attn * yat_ref[...].T).astype(o_ref.dtype)


def _merge(x_bf, z, w_in_t, row0, w_co, y_attn_t, tm=512, tn=512):
    s, d = z.shape
    n = w_co.shape[1]
    gate_rows = lambda off: pl.BlockSpec(
        (pl.Element(tn), pl.Element(d)),
        lambda j, i: (pl.multiple_of(row0 + off + j * tn, SUBLANES), 0))
    return pl.pallas_call(
        _merge_kernel,
        grid=(n // tn, s // tm),
        in_specs=[
            pl.BlockSpec((tm, d), lambda j, i: (i, 0)),
            pl.BlockSpec((tm, d), lambda j, i: (i, 0)),
            gate_rows(0),
            gate_rows(n),
            pl.BlockSpec((d, tn), lambda j, i: (0, j)),
            pl.BlockSpec((tn, tm), lambda j, i: (j, i)),
        ],
        out_specs=pl.BlockSpec((tm, tn), lambda j, i: (i, j)),
        out_shape=jax.ShapeDtypeStruct((s, n), BF16),
        scratch_shapes=[pltpu.VMEM((2, tn, d), BF16), pltpu.VMEM((d, tn), BF16)],
        compiler_params=_params("parallel", "arbitrary"),
        name="merge",
    )(x_bf, z, w_in_t, w_in_t, w_co, y_attn_t)


def _cast_resident(w_ref, w_s):
    @pl.when(pl.program_id(0) == 0)
    def _():
        w_s[...] = w_ref[...].astype(w_s.dtype)


def _resident(shape):
    return pl.BlockSpec(shape, lambda i: (0,) * len(shape), pipeline_mode=pl.Buffered(1))


def _row_halves(tm):
    return (slice(0, tm // 2), slice(tm // 2, tm))


def _oproj_kernel(m_ref, w_ref, h_ref, g_ref, b_ref, o_ref, obf_ref, w_s, *, alpha):
    _cast_resident(w_ref, w_s)
    for rows in _row_halves(m_ref.shape[0]):
        v = alpha * h_ref[rows, :] + _dot(m_ref[rows, :], w_s[...])
        out = _layer_norm(v, g_ref[...], b_ref[...])
        o_ref[rows, :] = out
        obf_ref[rows, :] = out.astype(obf_ref.dtype)


def _oproj(mixed, w_o, h, g, b, alpha, tm=512):
    s, d = h.shape
    row = pl.BlockSpec((tm, d), lambda i: (i, 0))
    vec = pl.BlockSpec((1, d), lambda i: (0, 0))
    return pl.pallas_call(
        functools.partial(_oproj_kernel, alpha=alpha),
        grid=(s // tm,),
        in_specs=[row, _resident((d, d)), row, vec, vec],
        out_specs=[row, row],
        out_shape=[jax.ShapeDtypeStruct((s, d), F32), jax.ShapeDtypeStruct((s, d), BF16)],
        scratch_shapes=[pltpu.VMEM((d, d), BF16)],
        compiler_params=_params("arbitrary"),
        name="oproj_ln",
    )(mixed, w_o, h, g, b)


def _ffn_kernel(hbf_ref, wu_ref, wd_ref, h_ref, g_ref, b_ref, o_ref, obf_ref, acc_ref, *, alpha):
    f = pl.program_id(1)

    @pl.when(f == 0)
    def _():
        acc_ref[...] = jnp.zeros(acc_ref.shape, F32)

    a = jnp.maximum(_dot(hbf_ref[...], wu_ref[...]), 0.0)
    acc_ref[...] += _dot((a * a).astype(wd_ref.dtype), wd_ref[...])

    @pl.when(f == pl.num_programs(1) - 1)
    def _():
        out = _layer_norm(alpha * h_ref[...] + acc_ref[...], g_ref[...], b_ref[...])
        o_ref[...] = out
        obf_ref[...] = out.astype(obf_ref.dtype)


def _ffn(h_bf, w_up, w_down, h, g, b, alpha, tm=512, tf=1024):
    s, d = h.shape
    dff = w_up.shape[1]
    row = pl.BlockSpec((tm, d), lambda i, f: (i, 0))
    vec = pl.BlockSpec((1, d), lambda i, f: (0, 0))
    return pl.pallas_call(
        functools.partial(_ffn_kernel, alpha=alpha),
        grid=(s // tm, dff // tf),
        in_specs=[row, pl.BlockSpec((d, tf), lambda i, f: (0, f)),
                  pl.BlockSpec((tf, d), lambda i, f: (f, 0)), row, vec, vec],
        out_specs=[row, row],
        out_shape=[jax.ShapeDtypeStruct((s, d), F32), jax.ShapeDtypeStruct((s, d), BF16)],
        scratch_shapes=[pltpu.VMEM((tm, d), F32)],
        compiler_params=_params("parallel", "arbitrary"),
        name="ffn_ln",
    )(h_bf, w_up, w_down, h, g, b)


def _ple_kernel(hbf_ref, wg_ref, p_ref, wp_ref, h_ref, g_ref, b_ref, o_ref, wg_s, *, alpha):
    _cast_resident(wg_ref, wg_s)
    for rows in _row_halves(hbf_ref.shape[0]):
        gate = _sigmoid(_dot(hbf_ref[rows, :], wg_s[...]))
        pe = gate * _dot(p_ref[rows, :], wp_ref[...])
        o_ref[rows, :] = _layer_norm(alpha * h_ref[rows, :] + pe, g_ref[...], b_ref[...])


def _ple(h_bf, w_pg, p_bf, w_p, h, g, b, alpha, tm=512):
    s, d = h.shape
    dp = p_bf.shape[1]
    row = pl.BlockSpec((tm, d), lambda i: (i, 0))
    vec = pl.BlockSpec((1, d), lambda i: (0, 0))
    return pl.pallas_call(
        functools.partial(_ple_kernel, alpha=alpha),
        grid=(s // tm,),
        in_specs=[row, _resident((d, d)),
                  pl.BlockSpec((tm, dp), lambda i: (i, 0)),
                  pl.BlockSpec((dp, d), lambda i: (0, 0)), row, vec, vec],
        out_specs=row,
        out_shape=jax.ShapeDtypeStruct((s, d), F32),
        scratch_shapes=[pltpu.VMEM((d, d), BF16)],
        compiler_params=_params("arbitrary"),
        name="ple_ln",
    )(h_bf, w_pg, p_bf, w_p, h, g, b)


def _layer(h, p, w_in, conv_w, w_conv_out, g_kv, w_uv, w_o, ln1_g, ln1_b, w_up, w_down,
           ln2_g, ln2_b, w_ple_gate, w_ple, ln3_g, ln3_b, alpha, tk=512):
    s, d = h.shape
    d_conv = conv_w.shape[1]
    nq = N_IDX_HEADS * D_IDX
    c_qlat = 3 * d_conv
    c_ckv = c_qlat + N_HEADS * D_LATENT
    c_gates = c_ckv + D_LATENT + nq + D_IDX + N_IDX_HEADS
    topk = min(TOPK_MAX, s // 4)
    n_kt = s // tk

    w_in_t = jnp.transpose(w_in)

    h_bf, ckv, kidx, qi_t, wi_t = _idx_proj(h, w_in_t, c_ckv, g_kv.reshape(1, D_LATENT))
    z = _conv_branch(h_bf, w_in_t, conv_w, d_conv)
    q_t = _qlat_proj(h_bf, w_in_t, c_qlat)

    kidx = kidx.reshape(n_kt, tk, 3 * D_IDX)
    kv = ckv.reshape(n_kt, tk, D_LATENT)
    kvt = jnp.transpose(kv, (0, 2, 1))

    w_uv_t = jnp.transpose(w_uv, (0, 2, 1)).astype(BF16)
    y_attn_t = _dsa_attention(qi_t, wi_t, kidx, q_t, kv, kvt, w_uv_t, topk)

    mixed = _merge(h_bf, z, w_in_t, c_gates, w_conv_out, y_attn_t)
    h1, h1_bf = _oproj(mixed, w_o, h, ln1_g.reshape(1, d), ln1_b.reshape(1, d), alpha)
    h2, h2_bf = _ffn(h1_bf, w_up.astype(BF16), w_down.astype(BF16), h1,
                     ln2_g.reshape(1, d), ln2_b.reshape(1, d), alpha)
    return _ple(h2_bf, w_ple_gate, p.astype(BF16), w_ple.astype(BF16), h2,
                ln3_g.reshape(1, d), ln3_b.reshape(1, d), alpha)


def kernel(x, p, w_in, conv_w, w_conv_out, g_kv, w_uv, w_o, ln1_g, ln1_b, w_up, w_down,
           ln2_g, ln2_b, w_ple_gate, w_ple, ln3_g, ln3_b):
    depth = w_in.shape[0]
    alpha = (2.0 * depth) ** 0.25
    batch = x.shape[0]
    outs = []
    for bi in range(batch):
        h = x[bi]
        for i in range(depth):
            h = _layer(h, p[i, bi], w_in[i], conv_w[i], w_conv_out[i], g_kv[i], w_uv[i], w_o[i],
                       ln1_g[i], ln1_b[i], w_up[i], w_down[i], ln2_g[i], ln2_b[i],
                       w_ple_gate[i], w_ple[i], ln3_g[i], ln3_b[i], alpha)
        outs.append(h)
    return jnp.stack(outs, axis=0)
```

```python
import functools

import jax
import jax.numpy as jnp
from jax import lax
from jax.experimental import pallas as pl
from jax.experimental.pallas import tpu as pltpu

F32 = jnp.float32
BF16 = jnp.bfloat16
I32 = jnp.int32

N_HEADS = 16
D_LATENT = 256
D_VHEAD = 128
N_IDX_HEADS = 16
D_IDX = 64
TOPK_MAX = 256
CONV_WIDTH = 3
LN_EPS = 1e-5
RMS_EPS = 1e-6
ATTN_SCALE = D_LATENT ** -0.5
IDX_SCALE = (D_IDX ** -0.5) * (N_IDX_HEADS ** -0.5)

VMEM_LIMIT_BYTES = 56 * 1024 * 1024
SUBLANES = 8
LANES = 128

INT_MIN = -(2 ** 31)
KEY_NEG_INF = INT_MIN + 0x7FFFFF
NEG_BIG = -1e30
TILE_UNROLL = 4
LOG2E = 1.4426950408889634


def _params(*sem):
    return pltpu.CompilerParams(dimension_semantics=sem, vmem_limit_bytes=VMEM_LIMIT_BYTES)


def _dot(a, b):
    return jnp.dot(a, b, preferred_element_type=F32)


def _dot_nt(a, b):
    return lax.dot_general(a, b, (((1,), (1,)), ((), ())), preferred_element_type=F32)


def _sigmoid(v):
    return 1.0 / (1.0 + jnp.exp(-v))


def _layer_norm(v, g, b):
    mu = jnp.mean(v, axis=-1, keepdims=True)
    c = v - mu
    var = jnp.mean(c * c, axis=-1, keepdims=True)
    return c * lax.rsqrt(var + LN_EPS) * g + b


def _split_hi_lo(v):
    hi = v.astype(BF16)
    lo = (v - hi.astype(F32)).astype(BF16)
    return hi, lo


def _conv_kernel(x_ref, wb_ref, wc_ref, wu_ref, cw_ref, z_ref, cu_ref, w_s):
    i = pl.program_id(1)
    tm = x_ref.shape[0]
    halo = SUBLANES

    @pl.when(i == 0)
    def _():
        w_s[0] = wb_ref[...].astype(w_s.dtype)
        w_s[1] = wc_ref[...].astype(w_s.dtype)
        w_s[2] = wu_ref[...].astype(w_s.dtype)
        cu_ref[0:halo, :] = jnp.zeros((halo, cu_ref.shape[1]), F32)

    @pl.when(i > 0)
    def _():
        cu_ref[0:halo, :] = cu_ref[tm:tm + halo, :]

    x = x_ref[...]
    cu_ref[halo:halo + tm, :] = _dot_nt(x, w_s[1]) * _dot_nt(x, w_s[2])
    cw = cw_ref[...]
    v = (cw[0:1, :] * cu_ref[halo - 2:halo - 2 + tm, :]
         + cw[1:2, :] * cu_ref[halo - 1:halo - 1 + tm, :]
         + cw[2:3, :] * cu_ref[halo:halo + tm, :])
    z_ref[...] = (_dot_nt(x, w_s[0]) * v).astype(z_ref.dtype)


def _conv_branch(x_bf, w_in_t, conv_w, d_conv, tm=1024, tn=512):
    s, d = x_bf.shape
    nj = d_conv // tn
    return pl.pallas_call(
        _conv_kernel,
        grid=(nj, s // tm),
        in_specs=[
            pl.BlockSpec((tm, d), lambda j, i: (i, 0)),
            pl.BlockSpec((tn, d), lambda j, i: (j, 0)),
            pl.BlockSpec((tn, d), lambda j, i: (nj + j, 0)),
            pl.BlockSpec((tn, d), lambda j, i: (2 * nj + j, 0)),
            pl.BlockSpec((CONV_WIDTH, tn), lambda j, i: (0, j)),
        ],
        out_specs=pl.BlockSpec((tm, tn), lambda j, i: (i, j)),
        out_shape=jax.ShapeDtypeStruct((s, d_conv), BF16),
        scratch_shapes=[pltpu.VMEM((tm + SUBLANES, tn), F32),
                        pltpu.VMEM((3, tn, d), BF16)],
        compiler_params=_params("parallel", "arbitrary"),
        name="conv_branch",
    )(x_bf, w_in_t, w_in_t, w_in_t, conv_w)


def _qlat_kernel(x_ref, w_ref, q_ref, w_s):
    @pl.when(pl.program_id(1) == 0)
    def _():
        w_s[...] = w_ref[...].astype(w_s.dtype)

    acc = _dot_nt(w_s[...], x_ref[...]) * (ATTN_SCALE * LOG2E)
    q_ref[...] = acc.astype(q_ref.dtype).reshape(q_ref.shape)


def _qlat_proj(x_bf, w_in_t, row0, tm=1024, heads_per_step=4):
    s, d = x_bf.shape
    tn = heads_per_step * D_LATENT
    j0 = row0 // tn
    assert j0 * tn == row0
    return pl.pallas_call(
        _qlat_kernel,
        grid=(N_HEADS // heads_per_step, s // tm),
        in_specs=[
            pl.BlockSpec((tm, d), lambda j, i: (i, 0)),
            pl.BlockSpec((tn, d), lambda j, i: (j0 + j, 0)),
        ],
        out_specs=pl.BlockSpec((heads_per_step, D_LATENT, tm), lambda j, i: (j, 0, i)),
        out_shape=jax.ShapeDtypeStruct((N_HEADS, D_LATENT, s), BF16),
        scratch_shapes=[pltpu.VMEM((tn, d), BF16)],
        compiler_params=_params("parallel", "arbitrary"),
        name="qlat_proj",
    )(x_bf, w_in_t)


def _idx_kernel(x_ref, w_ref, g_ref, xbf_ref, ckv_ref, kidx_ref, qit_ref, wit_ref, w_s):
    @pl.when(pl.program_id(0) == 0)
    def _():
        w_s[...] = w_ref[...].astype(w_s.dtype)

    x = x_ref[...].astype(xbf_ref.dtype)
    xbf_ref[...] = x
    nq = N_IDX_HEADS * D_IDX
    r_q, r_k, r_w = D_LATENT, D_LATENT + nq, D_LATENT + nq + D_IDX
    c = _dot_nt(x, w_s[0:r_q])
    ms = jnp.mean(c * c, axis=-1, keepdims=True)
    ckv_ref[...] = (c * lax.rsqrt(ms + RMS_EPS) * g_ref[...]).astype(ckv_ref.dtype)
    k_hi, k_lo = _split_hi_lo(_dot_nt(x, w_s[r_k:r_w]))
    kidx_ref[...] = jnp.concatenate([k_hi, k_hi, k_lo], axis=1)
    q_t = _dot_nt(w_s[r_q:r_k], x)
    q_hi, q_lo = _split_hi_lo(q_t.reshape(N_IDX_HEADS, D_IDX, q_t.shape[1]))
    qit_ref[...] = jnp.concatenate([q_hi, q_lo, q_hi], axis=1)
    wit_ref[...] = _dot_nt(w_s[r_w:r_w + N_IDX_HEADS], x) * IDX_SCALE


def _idx_proj(x, w_in_t, row0, g_kv, tm=512):
    s, d = x.shape
    nq = N_IDX_HEADS * D_IDX
    n = D_LATENT + nq + D_IDX + N_IDX_HEADS
    n_pad = -(-n // LANES) * LANES
    return pl.pallas_call(
        _idx_kernel,
        grid=(s // tm,),
        in_specs=[
            pl.BlockSpec((tm, d), lambda i: (i, 0)),
            pl.BlockSpec((pl.Element(n_pad), pl.Element(d)), lambda i: (row0, 0)),
            pl.BlockSpec((1, D_LATENT), lambda i: (0, 0)),
        ],
        out_specs=[
            pl.BlockSpec((tm, d), lambda i: (i, 0)),
            pl.BlockSpec((tm, D_LATENT), lambda i: (i, 0)),
            pl.BlockSpec((tm, 3 * D_IDX), lambda i: (i, 0)),
            pl.BlockSpec((N_IDX_HEADS, 3 * D_IDX, tm), lambda i: (0, 0, i)),
            pl.BlockSpec((N_IDX_HEADS, tm), lambda i: (0, i)),
        ],
        out_shape=[
            jax.ShapeDtypeStruct((s, d), BF16),
            jax.ShapeDtypeStruct((s, D_LATENT), BF16),
            jax.ShapeDtypeStruct((s, 3 * D_IDX), BF16),
            jax.ShapeDtypeStruct((N_IDX_HEADS, 3 * D_IDX, s), BF16),
            jax.ShapeDtypeStruct((N_IDX_HEADS, s), F32),
        ],
        scratch_shapes=[pltpu.VMEM((n_pad, d), BF16)],
        compiler_params=_params("arbitrary"),
        name="idx_proj",
    )(x, w_in_t, g_kv)


def _flip_key(v):
    return v ^ ((v >> 31) & 0x7FFFFFFF)


def _key_to_f32(key):
    return pltpu.bitcast(_flip_key(key), F32)


def _unrolled_loop(n, body, unroll):
    start = 0
    while unroll >= 1:
        def trip(i, carry, start=start, unroll=unroll):
            for r in range(unroll):
                body(start + unroll * i + r)
            return carry

        trips = (n - start) // unroll
        lax.fori_loop(0, trips, trip, 0)
        start = start + unroll * trips
        unroll //= 2


def _attn_kernel(qit_ref, wit_ref, kidx_ref, qt_ref, kv_ref, kvt_ref, wuvt_ref, yt_ref,
                 score_ref, qi_s, q_s, w_s, m_ref, l_ref, acc_ref, s_buf, p_buf, a_buf, *, topk):
    b = pl.program_id(0)
    n_heads, d_lat, tq = qt_ref.shape
    n_kt_all, tk, _ = kidx_ref.shape
    n_groups, _, gw = q_s.shape
    hg = gw // tq
    n_kt = ((b + 1) * tq + tk - 1) // tk
    kpos = lax.broadcasted_iota(I32, (tk, tq), 0)
    qpos = b * tq + lax.broadcasted_iota(I32, (tk, tq), 1)

    for h in range(n_heads):
        g, c = divmod(h, hg)
        qi_s[g, :, c * tq:(c + 1) * tq] = qit_ref[h]
        q_s[g, :, c * tq:(c + 1) * tq] = qt_ref[h]
        w_s[g, :, c * tq:(c + 1) * tq] = wit_ref[h:h + 1, :]

    def score_tile(j):
        kidx = kidx_ref[j]
        score = jnp.zeros((tk, tq), F32)
        for g in range(n_groups):
            rel = jnp.maximum(_dot(kidx, qi_s[g]), 0.0) * w_s[g]
            for c in range(hg):
                score = score + rel[:, c * tq:(c + 1) * tq]
        score_ref[j] = jnp.where(kpos + j * tk <= qpos, score, -jnp.inf)

    _unrolled_loop(n_kt, score_tile, TILE_UNROLL)

    fold = 8 * SUBLANES
    kf = float(topk)

    n_causal = (qpos[0:1, :] + 1).astype(F32)

    def extremes(j, carry):
        top, bot = carry
        slabs = score_ref[j].reshape(tk // fold, fold, tq)
        for r in range(tk // fold):
            top = jnp.maximum(top, slabs[r])
            bot = jnp.minimum(bot, jnp.where(slabs[r] > -jnp.inf, slabs[r], jnp.inf))
        return top, bot

    top, bot = lax.fori_loop(0, n_kt, extremes, (jnp.full((fold, tq), -jnp.inf, F32),
                                                 jnp.full((fold, tq), jnp.inf, F32)))
    rep = lambda v: jnp.broadcast_to(v, (SUBLANES, tq))
    short = n_causal < kf
    lo0 = rep(jnp.where(short, KEY_NEG_INF,
                        _flip_key(pltpu.bitcast(jnp.min(bot, axis=0, keepdims=True), I32))))
    hi0 = rep(_flip_key(pltpu.bitcast(jnp.max(top, axis=0, keepdims=True), I32)) + 1)
    c_lo0 = rep(jnp.where(short, 0.0, n_causal))

    def unsettled(lo, hi, c_lo):
        return jnp.max(jnp.where((c_lo > kf) & (hi - 1 > lo), 1.0, 0.0))

    def search(tiles):
        def run():
            def count_ge(cand):
                cnt = jnp.zeros((fold, tq), F32)
                for j in range(tiles):
                    hit = jnp.where(score_ref[j] >= cand, 1.0, 0.0)
                    cnt = cnt + jnp.sum(hit.reshape(tk // fold, fold, tq), axis=0)
                return jnp.sum(cnt, axis=0, keepdims=True)

            def probe(lo, hi, c_lo, by_key):
                live = (c_lo > kf) & (hi - 1 > lo)
                key_mid = (lo >> 1) + (hi >> 1) + (lo & hi & 1)
                val_mid = 0.5 * (_key_to_f32(lo) + _key_to_f32(hi - 1))
                pk = jnp.where(by_key, key_mid, _flip_key(pltpu.bitcast(val_mid, I32)))
                pk = jnp.minimum(jnp.maximum(pk, lo + 1), hi - 1)
                pk = jnp.where(live, pk, lo)
                c = rep(count_ge(_key_to_f32(jnp.maximum(pk, KEY_NEG_INF))[0:1]))
                up = live & (c >= kf)
                down = live & (c < kf)
                return jnp.where(up, pk, lo), jnp.where(down, pk, hi), jnp.where(up, c, c_lo)

            def probes(state):
                lo, hi, c_lo, _, trip = state
                lo, hi, c_lo = probe(lo, hi, c_lo, False)
                lo, hi, c_lo = probe(lo, hi, c_lo, trip % 2 == 1)
                return lo, hi, c_lo, unsettled(lo, hi, c_lo), trip + 1

            def keep_going(state):
                return jnp.logical_and(state[3] > 0.0, state[4] < 64)

            state = (lo0, hi0, c_lo0, unsettled(lo0, hi0, c_lo0), jnp.int32(0))
            lo, _, c_lo, _, _ = lax.while_loop(keep_going, probes, state)
            return lo[0:1], c_lo[0:1]
        return run

    thr_key, n_ge = lax.switch(n_kt - 1, [search(k) for k in range(1, n_kt_all + 1)])
    thr = _key_to_f32(jnp.maximum(thr_key, KEY_NEG_INF))

    @pl.when(jnp.max(jnp.broadcast_to(n_ge, (SUBLANES, tq))) > kf)
    def _():
        def count(pred):
            def body(j, cnt):
                hit = jnp.where(pred(kpos + j * tk, score_ref[j]), 1.0, 0.0)
                return cnt + jnp.sum(hit.reshape(tk // fold, fold, tq), axis=0)

            cnt = lax.fori_loop(0, n_kt, body, jnp.zeros((fold, tq), F32))
            return jnp.sum(cnt, axis=0, keepdims=True)

        need = kf - count(lambda pos, sc: sc > thr)
        n_pos_bits = (n_kt_all * tk - 1).bit_length()

        def pos_step(t, cut):
            cand = cut | jnp.left_shift(jnp.int32(1), n_pos_bits - 1 - t)
            before = count(lambda pos, sc: (sc == thr) & (pos < cand))
            return jnp.where(before < need, cand, cut)

        cut = lax.fori_loop(0, n_pos_bits, pos_step, jnp.zeros((1, tq), I32))

        def drop_tile(j, carry):
            sc = score_ref[j]
            score_ref[j] = jnp.where((sc == thr) & (kpos + j * tk > cut), -jnp.inf, sc)
            return carry

        lax.fori_loop(0, n_kt, drop_tile, 0)

    def mask_tile(j, carry):
        keep = (score_ref[j] >= thr) & (kpos + j * tk <= qpos)
        score_ref[j] = jnp.where(keep, 0.0, NEG_BIG)
        return carry

    lax.fori_loop(0, n_kt, mask_tile, 0)

    m_ref[...] = jnp.full(m_ref.shape, NEG_BIG, F32)
    l_ref[...] = jnp.zeros(l_ref.shape, F32)
    acc_ref[...] = jnp.zeros(acc_ref.shape, F32)
    p_buf[...] = jnp.zeros(p_buf.shape, p_buf.dtype)
    a_buf[...] = jnp.ones(a_buf.shape, F32)

    def qk(j, g):
        bias = jnp.concatenate([score_ref[j]] * hg, axis=1)
        s_buf[g % 2] = _dot(kv_ref[j], q_s[g]) + bias

    def softmax(g):
        s = s_buf[g % 2]
        m_old = m_ref[g]
        m_new = jnp.maximum(m_old, jnp.max(s, axis=0, keepdims=True))
        p = jnp.exp2(s - m_new)
        alpha = jnp.exp2(m_old - m_new)
        l_ref[g] = alpha * l_ref[g] + jnp.sum(p, axis=0, keepdims=True)
        a_buf[g % 2] = alpha
        p_buf[g % 2] = p.astype(p_buf.dtype)
        m_ref[g] = m_new

    def pv(j, g):
        acc_ref[g] = a_buf[g % 2] * acc_ref[g] + _dot(kvt_ref[j], p_buf[g % 2])

    last = n_groups - 1
    qk(0, 0)

    def attn_tile(j):
        for g in range(n_groups):
            if g < last:
                qk(j, g + 1)
            else:
                qk(jnp.minimum(j + 1, n_kt - 1), 0)
            if g > 0:
                pv(j, g - 1)
            else:
                pv(jnp.maximum(j - 1, 0), last)
            softmax(g)

    _unrolled_loop(n_kt, attn_tile, TILE_UNROLL)
    pv(n_kt - 1, last)

    for h in range(n_heads):
        g, c = divmod(h, hg)
        lanes = slice(c * tq, (c + 1) * tq)
        o_t = (acc_ref[g, :, lanes] / l_ref[g, :, lanes]).astype(wuvt_ref.dtype)
        yt_ref[h * D_VHEAD:(h + 1) * D_VHEAD, :] = _dot(wuvt_ref[h], o_t)


def _dsa_attention(qi_t, wi_t, kidx, q_t, kv, kvt, w_uv_t, topk, tq=128, heads_per_group=2):
    n_heads, d_lat, s = q_t.shape
    n_kt, tk, dk = kidx.shape
    assert tk % tq == 0
    n_groups = n_heads // heads_per_group
    gw = heads_per_group * tq
    const3 = lambda b: (0, 0, 0)
    return pl.pallas_call(
        functools.partial(_attn_kernel, topk=topk),
        grid=(s // tq,),
        in_specs=[
            pl.BlockSpec((N_IDX_HEADS, dk, tq), lambda b: (0, 0, b)),
            pl.BlockSpec((N_IDX_HEADS, tq), lambda b: (0, b)),
            pl.BlockSpec((n_kt, tk, dk), const3),
            pl.BlockSpec((n_heads, d_lat, tq), lambda b: (0, 0, b)),
            pl.BlockSpec((n_kt, tk, d_lat), const3),
            pl.BlockSpec((n_kt, d_lat, tk), const3),
            pl.BlockSpec((n_heads, D_VHEAD, d_lat), const3),
        ],
        out_specs=pl.BlockSpec((n_heads * D_VHEAD, tq), lambda b: (0, b)),
        out_shape=jax.ShapeDtypeStruct((n_heads * D_VHEAD, s), F32),
        scratch_shapes=[
            pltpu.VMEM((n_kt, tk, tq), F32),
            pltpu.VMEM((n_groups, dk, gw), BF16),
            pltpu.VMEM((n_groups, d_lat, gw), BF16),
            pltpu.VMEM((n_groups, 1, gw), F32),
            pltpu.VMEM((n_groups, 1, gw), F32),
            pltpu.VMEM((n_groups, 1, gw), F32),
            pltpu.VMEM((n_groups, d_lat, gw), F32),
            pltpu.VMEM((2, tk, gw), F32),
            pltpu.VMEM((2, tk, gw), BF16),
            pltpu.VMEM((2, 1, gw), F32),
        ],
        compiler_params=_params("parallel"),
        name="dsa_attention",
    )(qi_t, wi_t, kidx, q_t, kv, kvt, w_uv_t)


def _merge_kernel(x_ref, z_ref, wgc_ref, wga_ref, wco_ref, yat_ref, o_ref, w_s, wco_s):
    @pl.when(pl.program_id(1) == 0)
    def _():
        w_s[0] = wgc_ref[...].astype(w_s.dtype)
        w_s[1] = wga_ref[...].astype(w_s.dtype)
        wco_s[...] = wco_ref[...].astype(wco_s.dtype)

    x = x_ref[...]
    g_conv = _sigmoid(_dot_nt(x, w_s[0]))
    g_attn = _sigmoid(_dot_nt(x, w_s[1]))
    y_conv = _dot(z_ref[...], wco_s[...])
    o_ref[...] = (g_conv * y_conv + g_attn * yat_ref[...].T).astype(o_ref.dtype)


def _merge(x_bf, z, w_in_t, row0, w_co, y_attn_t, tm=512, tn=512):
    s, d = z.shape
    n = w_co.shape[1]
    gate_rows = lambda off: pl.BlockSpec(
        (pl.Element(tn), pl.Element(d)),
        lambda j, i: (pl.multiple_of(row0 + off + j * tn, SUBLANES), 0))
    return pl.pallas_call(
        _merge_kernel,
        grid=(n // tn, s // tm),
        in_specs=[
            pl.BlockSpec((tm, d), lambda j, i: (i, 0)),
            pl.BlockSpec((tm, d), lambda j, i: (i, 0)),
            gate_rows(0),
            gate_rows(n),
            pl.BlockSpec((d, tn), lambda j, i: (0, j)),
            pl.BlockSpec((tn, tm), lambda j, i: (j, i)),
        ],
        out_specs=pl.BlockSpec((tm, tn), lambda j, i: (i, j)),
        out_shape=jax.ShapeDtypeStruct((s, n), BF16),
        scratch_shapes=[pltpu.VMEM((2, tn, d), BF16), pltpu.VMEM((d, tn), BF16)],
        compiler_params=_params("parallel", "arbitrary"),
        name="merge",
    )(x_bf, z, w_in_t, w_in_t, w_co, y_attn_t)


def _cast_resident(w_ref, w_s):
    @pl.when(pl.program_id(0) == 0)
    def _():
        w_s[...] = w_ref[...].astype(w_s.dtype)


def _resident(shape):
    return pl.BlockSpec(shape, lambda i: (0,) * len(shape), pipeline_mode=pl.Buffered(1))


def _row_halves(tm):
    return (slice(0, tm // 2), slice(tm // 2, tm))


def _oproj_kernel(m_ref, w_ref, h_ref, g_ref, b_ref, o_ref, obf_ref, w_s, *, alpha):
    _cast_resident(w_ref, w_s)
    for rows in _row_halves(m_ref.shape[0]):
        v = alpha * h_ref[rows, :] + _dot(m_ref[rows, :], w_s[...])
        out = _layer_norm(v, g_ref[...], b_ref[...])
        o_ref[rows, :] = out
        obf_ref[rows, :] = out.astype(obf_ref.dtype)


def _oproj(mixed, w_o, h, g, b, alpha, tm=512):
    s, d = h.shape
    row = pl.BlockSpec((tm, d), lambda i: (i, 0))
    vec = pl.BlockSpec((1, d), lambda i: (0, 0))
    return pl.pallas_call(
        functools.partial(_oproj_kernel, alpha=alpha),
        grid=(s // tm,),
        in_specs=[row, _resident((d, d)), row, vec, vec],
        out_specs=[row, row],
        out_shape=[jax.ShapeDtypeStruct((s, d), F32), jax.ShapeDtypeStruct((s, d), BF16)],
        scratch_shapes=[pltpu.VMEM((d, d), BF16)],
        compiler_params=_params("arbitrary"),
        name="oproj_ln",
    )(mixed, w_o, h, g, b)


def _ffn_kernel(hbf_ref, wu_ref, wd_ref, h_ref, g_ref, b_ref, o_ref, obf_ref, acc_ref, *, alpha):
    f = pl.program_id(1)

    @pl.when(f == 0)
    def _():
        acc_ref[...] = jnp.zeros(acc_ref.shape, F32)

    a = jnp.maximum(_dot(hbf_ref[...], wu_ref[...]), 0.0)
    acc_ref[...] += _dot((a * a).astype(wd_ref.dtype), wd_ref[...])

    @pl.when(f == pl.num_programs(1) - 1)
    def _():
        out = _layer_norm(alpha * h_ref[...] + acc_ref[...], g_ref[...], b_ref[...])
        o_ref[...] = out
        obf_ref[...] = out.astype(obf_ref.dtype)


def _ffn(h_bf, w_up, w_down, h, g, b, alpha, tm=512, tf=1024):
    s, d = h.shape
    dff = w_up.shape[1]
    row = pl.BlockSpec((tm, d), lambda i, f: (i, 0))
    vec = pl.BlockSpec((1, d), lambda i, f: (0, 0))
    return pl.pallas_call(
        functools.partial(_ffn_kernel, alpha=alpha),
        grid=(s // tm, dff // tf),
        in_specs=[row, pl.BlockSpec((d, tf), lambda i, f: (0, f)),
                  pl.BlockSpec((tf, d), lambda i, f: (f, 0)), row, vec, vec],
        out_specs=[row, row],
        out_shape=[jax.ShapeDtypeStruct((s, d), F32), jax.ShapeDtypeStruct((s, d), BF16)],
        scratch_shapes=[pltpu.VMEM((tm, d), F32)],
        compiler_params=_params("parallel", "arbitrary"),
        name="ffn_ln",
    )(h_bf, w_up, w_down, h, g, b)


def _ple_kernel(hbf_ref, wg_ref, p_ref, wp_ref, h_ref, g_ref, b_ref, o_ref, wg_s, *, alpha):
    _cast_resident(wg_ref, wg_s)
    for rows in _row_halves(hbf_ref.shape[0]):
        gate = _sigmoid(_dot(hbf_ref[rows, :], wg_s[...]))
        pe = gate * _dot(p_ref[rows, :], wp_ref[...])
        o_ref[rows, :] = _layer_norm(alpha * h_ref[rows, :] + pe, g_ref[...], b_ref[...])


def _ple(h_bf, w_pg, p_bf, w_p, h, g, b, alpha, tm=512):
    s, d = h.shape
    dp = p_bf.shape[1]
    row = pl.BlockSpec((tm, d), lambda i: (i, 0))
    vec = pl.BlockSpec((1, d), lambda i: (0, 0))
    return pl.pallas_call(
        functools.partial(_ple_kernel, alpha=alpha),
        grid=(s // tm,),
        in_specs=[row, _resident((d, d)),
                  pl.BlockSpec((tm, dp), lambda i: (i, 0)),
                  pl.BlockSpec((dp, d), lambda i: (0, 0)), row, vec, vec],
        out_specs=row,
        out_shape=jax.ShapeDtypeStruct((s, d), F32),
        scratch_shapes=[pltpu.VMEM((d, d), BF16)],
        compiler_params=_params("arbitrary"),
        name="ple_ln",
    )(h_bf, w_pg, p_bf, w_p, h, g, b)


def _layer(h, p, w_in, conv_w, w_conv_out, g_kv, w_uv, w_o, ln1_g, ln1_b, w_up, w_down,
           ln2_g, ln2_b, w_ple_gate, w_ple, ln3_g, ln3_b, alpha, tk=512):
    s, d = h.shape
    d_conv = conv_w.shape[1]
    nq = N_IDX_HEADS * D_IDX
    c_qlat = 3 * d_conv
    c_ckv = c_qlat + N_HEADS * D_LATENT
    c_gates = c_ckv + D_LATENT + nq + D_IDX + N_IDX_HEADS
    topk = min(TOPK_MAX, s // 4)
    n_kt = s // tk

    w_in_t = jnp.transpose(w_in)

    h_bf, ckv, kidx, qi_t, wi_t = _idx_proj(h, w_in_t, c_ckv, g_kv.reshape(1, D_LATENT))
    z = _conv_branch(h_bf, w_in_t, conv_w, d_conv)
    q_t = _qlat_proj(h_bf, w_in_t, c_qlat)

    kidx = kidx.reshape(n_kt, tk, 3 * D_IDX)
    kv = ckv.reshape(n_kt, tk, D_LATENT)
    kvt = jnp.transpose(kv, (0, 2, 1))

    w_uv_t = jnp.transpose(w_uv, (0, 2, 1)).astype(BF16)
    y_attn_t = _dsa_attention(qi_t, wi_t, kidx, q_t, kv, kvt, w_uv_t, topk)

    mixed = _merge(h_bf, z, w_in_t, c_gates, w_conv_out, y_attn_t)
    h1, h1_bf = _oproj(mixed, w_o, h, ln1_g.reshape(1, d), ln1_b.reshape(1, d), alpha)
    h2, h2_bf = _ffn(h1_bf, w_up.astype(BF16), w_down.astype(BF16), h1,
                     ln2_g.reshape(1, d), ln2_b.reshape(1, d), alpha)
    return _ple(h2_bf, w_ple_gate, p.astype(BF16), w_ple.astype(BF16), h2,
                ln3_g.reshape(1, d), ln3_b.reshape(1, d), alpha)


def kernel(x, p, w_in, conv_w, w_conv_out, g_kv, w_uv, w_o, ln1_g, ln1_b, w_up, w_down,
           ln2_g, ln2_b, w_ple_gate, w_ple, ln3_g, ln3_b):
    depth = w_in.shape[0]
    alpha = (2.0 * depth) ** 0.25
    batch = x.shape[0]
    outs = []
    for bi in range(batch):
        h = x[bi]
        for i in range(depth):
            h = _layer(h, p[i, bi], w_in[i], conv_w[i], w_conv_out[i], g_kv[i], w_uv[i], w_o[i],
                       ln1_g[i], ln1_b[i], w_up[i], w_down[i], ln2_g[i], ln2_b[i],
                       w_ple_gate[i], w_ple[i], ln3_g[i], ln3_b[i], alpha)
        outs.append(h)
    return jnp.stack(outs, axis=0)
```

```python
import functools

import jax
import jax.numpy as jnp
from jax import lax
from jax.experimental import pallas as pl
from jax.experimental.pallas import tpu as pltpu

F32 = jnp.float32
BF16 = jnp.bfloat16
I32 = jnp.int32

N_HEADS = 16
D_LATENT = 256
D_VHEAD = 128
N_IDX_HEADS = 16
D_IDX = 64
TOPK_MAX = 256
CONV_WIDTH = 3
LN_EPS = 1e-5
RMS_EPS = 1e-6
ATTN_SCALE = D_LATENT ** -0.5
IDX_SCALE = (D_IDX ** -0.5) * (N_IDX_HEADS ** -0.5)

VMEM_LIMIT_BYTES = 56 * 1024 * 1024
SUBLANES = 8
LANES = 128

INT_MIN = -(2 ** 31)
KEY_NEG_INF = INT_MIN + 0x7FFFFF
NEG_BIG = -1e30
TILE_UNROLL = 4
LOG2E = 1.4426950408889634


def _params(*sem):
    return pltpu.CompilerParams(dimension_semantics=sem, vmem_limit_bytes=VMEM_LIMIT_BYTES)


def _dot(a, b):
    return jnp.dot(a, b, preferred_element_type=F32)


def _dot_nt(a, b):
    return lax.dot_general(a, b, (((1,), (1,)), ((), ())), preferred_element_type=F32)


def _sigmoid(v):
    return 1.0 / (1.0 + jnp.exp(-v))


def _layer_norm(v, g, b):
    mu = jnp.mean(v, axis=-1, keepdims=True)
    c = v - mu
    var = jnp.mean(c * c, axis=-1, keepdims=True)
    return c * lax.rsqrt(var + LN_EPS) * g + b


def _split_hi_lo(v):
    hi = v.astype(BF16)
    lo = (v - hi.astype(F32)).astype(BF16)
    return hi, lo


def _side_cast_specs(w, grid):
    n_steps = grid[0] * grid[1]
    rows, cols = w.shape
    chunk = rows // n_steps
    assert chunk * n_steps == rows and chunk % (2 * SUBLANES) == 0
    spec = pl.BlockSpec((chunk, cols), lambda j, i: (j * grid[1] + i, 0))
    return spec, spec, jax.ShapeDtypeStruct(w.shape, BF16)


def _conv_kernel(x_ref, wb_ref, wc_ref, wu_ref, cw_ref, side_ref, z_ref, side_bf_ref, cu_ref, w_s):
    side_bf_ref[...] = side_ref[...].astype(side_bf_ref.dtype)
    i = pl.program_id(1)
    tm = x_ref.shape[0]
    halo = SUBLANES

    @pl.when(i == 0)
    def _():
        w_s[0] = wb_ref[...].astype(w_s.dtype)
        w_s[1] = wc_ref[...].astype(w_s.dtype)
        w_s[2] = wu_ref[...].astype(w_s.dtype)
        cu_ref[0:halo, :] = jnp.zeros((halo, cu_ref.shape[1]), F32)

    @pl.when(i > 0)
    def _():
        cu_ref[0:halo, :] = cu_ref[tm:tm + halo, :]

    x = x_ref[...]
    cu_ref[halo:halo + tm, :] = _dot_nt(x, w_s[1]) * _dot_nt(x, w_s[2])
    cw = cw_ref[...]
    v = (cw[0:1, :] * cu_ref[halo - 2:halo - 2 + tm, :]
         + cw[1:2, :] * cu_ref[halo - 1:halo - 1 + tm, :]
         + cw[2:3, :] * cu_ref[halo:halo + tm, :])
    z_ref[...] = (_dot_nt(x, w_s[0]) * v).astype(z_ref.dtype)


def _conv_branch(x_bf, w_in_t, conv_w, d_conv, w_side, tm=1024, tn=512):
    s, d = x_bf.shape
    nj = d_conv // tn
    grid = (nj, s // tm)
    side_in, side_out, side_shape = _side_cast_specs(w_side, grid)
    return pl.pallas_call(
        _conv_kernel,
        grid=grid,
        in_specs=[
            pl.BlockSpec((tm, d), lambda j, i: (i, 0)),
            pl.BlockSpec((tn, d), lambda j, i: (j, 0)),
            pl.BlockSpec((tn, d), lambda j, i: (nj + j, 0)),
            pl.BlockSpec((tn, d), lambda j, i: (2 * nj + j, 0)),
            pl.BlockSpec((CONV_WIDTH, tn), lambda j, i: (0, j)),
            side_in,
        ],
        out_specs=[pl.BlockSpec((tm, tn), lambda j, i: (i, j)), side_out],
        out_shape=[jax.ShapeDtypeStruct((s, d_conv), BF16), side_shape],
        scratch_shapes=[pltpu.VMEM((tm + SUBLANES, tn), F32),
                        pltpu.VMEM((3, tn, d), BF16)],
        compiler_params=_params("parallel", "arbitrary"),
        name="conv_branch",
    )(x_bf, w_in_t, w_in_t, w_in_t, conv_w, w_side)


def _qlat_kernel(x_ref, w_ref, side_ref, q_ref, side_bf_ref, w_s):
    side_bf_ref[...] = side_ref[...].astype(side_bf_ref.dtype)

    @pl.when(pl.program_id(1) == 0)
    def _():
        w_s[...] = w_ref[...].astype(w_s.dtype)

    acc = _dot_nt(w_s[...], x_ref[...]) * (ATTN_SCALE * LOG2E)
    q_ref[...] = acc.astype(q_ref.dtype).reshape(q_ref.shape)


def _qlat_proj(x_bf, w_in_t, row0, w_side, tm=1024, heads_per_step=4):
    s, d = x_bf.shape
    tn = heads_per_step * D_LATENT
    j0 = row0 // tn
    assert j0 * tn == row0
    grid = (N_HEADS // heads_per_step, s // tm)
    side_in, side_out, side_shape = _side_cast_specs(w_side, grid)
    return pl.pallas_call(
        _qlat_kernel,
        grid=grid,
        in_specs=[
            pl.BlockSpec((tm, d), lambda j, i: (i, 0)),
            pl.BlockSpec((tn, d), lambda j, i: (j0 + j, 0)),
            side_in,
        ],
        out_specs=[pl.BlockSpec((heads_per_step, D_LATENT, tm), lambda j, i: (j, 0, i)), side_out],
        out_shape=[jax.ShapeDtypeStruct((N_HEADS, D_LATENT, s), BF16), side_shape],
        scratch_shapes=[pltpu.VMEM((tn, d), BF16)],
        compiler_params=_params("parallel", "arbitrary"),
        name="qlat_proj",
    )(x_bf, w_in_t, w_side)


def _idx_kernel(x_ref, w_ref, g_ref, xbf_ref, ckv_ref, kidx_ref, qit_ref, wit_ref, w_s):
    @pl.when(pl.program_id(0) == 0)
    def _():
        w_s[...] = w_ref[...].astype(w_s.dtype)

    x = x_ref[...].astype(xbf_ref.dtype)
    xbf_ref[...] = x
    nq = N_IDX_HEADS * D_IDX
    r_q, r_k, r_w = D_LATENT, D_LATENT + nq, D_LATENT + nq + D_IDX
    c = _dot_nt(x, w_s[0:r_q])
    ms = jnp.mean(c * c, axis=-1, keepdims=True)
    ckv_ref[...] = (c * lax.rsqrt(ms + RMS_EPS) * g_ref[...]).astype(ckv_ref.dtype)
    k_hi, k_lo = _split_hi_lo(_dot_nt(x, w_s[r_k:r_w]))
    kidx_ref[...] = jnp.concatenate([k_hi, k_hi, k_lo], axis=1)
    q_t = _dot_nt(w_s[r_q:r_k], x)
    q_hi, q_lo = _split_hi_lo(q_t.reshape(N_IDX_HEADS, D_IDX, q_t.shape[1]))
    qit_ref[...] = jnp.concatenate([q_hi, q_lo, q_hi], axis=1)
    wit_ref[...] = _dot_nt(w_s[r_w:r_w + N_IDX_HEADS], x) * IDX_SCALE


def _idx_proj(x, w_in_t, row0, g_kv, tm=512):
    s, d = x.shape
    nq = N_IDX_HEADS * D_IDX
    n = D_LATENT + nq + D_IDX + N_IDX_HEADS
    n_pad = -(-n // LANES) * LANES
    return pl.pallas_call(
        _idx_kernel,
        grid=(s // tm,),
        in_specs=[
            pl.BlockSpec((tm, d), lambda i: (i, 0)),
            pl.BlockSpec((pl.Element(n_pad), pl.Element(d)), lambda i: (row0, 0)),
            pl.BlockSpec((1, D_LATENT), lambda i: (0, 0)),
        ],
        out_specs=[
            pl.BlockSpec((tm, d), lambda i: (i, 0)),
            pl.BlockSpec((tm, D_LATENT), lambda i: (i, 0)),
            pl.BlockSpec((tm, 3 * D_IDX), lambda i: (i, 0)),
            pl.BlockSpec((N_IDX_HEADS, 3 * D_IDX, tm), lambda i: (0, 0, i)),
            pl.BlockSpec((N_IDX_HEADS, tm), lambda i: (0, i)),
        ],
        out_shape=[
            jax.ShapeDtypeStruct((s, d), BF16),
            jax.ShapeDtypeStruct((s, D_LATENT), BF16),
            jax.ShapeDtypeStruct((s, 3 * D_IDX), BF16),
            jax.ShapeDtypeStruct((N_IDX_HEADS, 3 * D_IDX, s), BF16),
            jax.ShapeDtypeStruct((N_IDX_HEADS, s), F32),
        ],
        scratch_shapes=[pltpu.VMEM((n_pad, d), BF16)],
        compiler_params=_params("arbitrary"),
        name="idx_proj",
    )(x, w_in_t, g_kv)


def _flip_key(v):
    return v ^ ((v >> 31) & 0x7FFFFFFF)


def _key_to_f32(key):
    return pltpu.bitcast(_flip_key(key), F32)


def _unrolled_loop(n, body, unroll):
    start = 0
    while unroll >= 1:
        def trip(i, carry, start=start, unroll=unroll):
            for r in range(unroll):
                body(start + unroll * i + r)
            return carry

        trips = (n - start) // unroll
        lax.fori_loop(0, trips, trip, 0)
        start = start + unroll * trips
        unroll //= 2


def _attn_kernel(qit_ref, wit_ref, kidx_ref, qt_ref, kv_ref, kvt_ref, wuvt_ref, yt_ref,
                 score_ref, qi_s, q_s, w_s, m_ref, l_ref, acc_ref, s_buf, p_buf, a_buf, *, topk):
    b = pl.program_id(0)
    n_heads, d_lat, tq = qt_ref.shape
    n_kt_all, tk, _ = kidx_ref.shape
    n_groups, _, gw = q_s.shape
    hg = gw // tq
    n_kt = ((b + 1) * tq + tk - 1) // tk
    kpos = lax.broadcasted_iota(I32, (tk, tq), 0)
    qpos = b * tq + lax.broadcasted_iota(I32, (tk, tq), 1)

    for h in range(n_heads):
        g, c = divmod(h, hg)
        qi_s[g, :, c * tq:(c + 1) * tq] = qit_ref[h]
        q_s[g, :, c * tq:(c + 1) * tq] = qt_ref[h]
        w_s[g, :, c * tq:(c + 1) * tq] = wit_ref[h:h + 1, :]

    def score_tile(j):
        kidx = kidx_ref[j]
        score = jnp.zeros((tk, tq), F32)
        for g in range(n_groups):
            rel = jnp.maximum(_dot(kidx, qi_s[g]), 0.0) * w_s[g]
            for c in range(hg):
                score = score + rel[:, c * tq:(c + 1) * tq]
        score_ref[j] = jnp.where(kpos + j * tk <= qpos, score, -jnp.inf)

    _unrolled_loop(n_kt, score_tile, TILE_UNROLL)

    fold = 8 * SUBLANES
    kf = float(topk)

    n_causal = (qpos[0:1, :] + 1).astype(F32)

    def extremes(j, carry):
        top, bot = carry
        slabs = score_ref[j].reshape(tk // fold, fold, tq)
        for r in range(tk // fold):
            top = jnp.maximum(top, slabs[r])
            bot = jnp.minimum(bot, jnp.where(slabs[r] > -jnp.inf, slabs[r], jnp.inf))
        return top, bot

    top, bot = lax.fori_loop(0, n_kt, extremes, (jnp.full((fold, tq), -jnp.inf, F32),
                                                 jnp.full((fold, tq), jnp.inf, F32)))
    rep = lambda v: jnp.broadcast_to(v, (SUBLANES, tq))
    short = n_causal < kf
    lo0 = rep(jnp.where(short, KEY_NEG_INF,
                        _flip_key(pltpu.bitcast(jnp.min(bot, axis=0, keepdims=True), I32))))
    hi0 = rep(_flip_key(pltpu.bitcast(jnp.max(top, axis=0, keepdims=True), I32)) + 1)
    c_lo0 = rep(jnp.where(short, 0.0, n_causal))

    def unsettled(lo, hi, c_lo):
        return jnp.max(jnp.where((c_lo > kf) & (hi - 1 > lo), 1.0, 0.0))

    def search(tiles):
        def run():
            def count_ge(cand):
                cnt = jnp.zeros((fold, tq), F32)
                for j in range(tiles):
                    hit = jnp.where(score_ref[j] >= cand, 1.0, 0.0)
                    cnt = cnt + jnp.sum(hit.reshape(tk // fold, fold, tq), axis=0)
                return jnp.sum(cnt, axis=0, keepdims=True)

            def probe(lo, hi, c_lo, by_key):
                live = (c_lo > kf) & (hi - 1 > lo)
                key_mid = (lo >> 1) + (hi >> 1) + (lo & hi & 1)
                val_mid = 0.5 * (_key_to_f32(lo) + _key_to_f32(hi - 1))
                pk = jnp.where(by_key, key_mid, _flip_key(pltpu.bitcast(val_mid, I32)))
                pk = jnp.minimum(jnp.maximum(pk, lo + 1), hi - 1)
                pk = jnp.where(live, pk, lo)
                c = rep(count_ge(_key_to_f32(jnp.maximum(pk, KEY_NEG_INF))[0:1]))
                up = live & (c >= kf)
                down = live & (c < kf)
                return jnp.where(up, pk, lo), jnp.where(down, pk, hi), jnp.where(up, c, c_lo)

            def probes(state):
                lo, hi, c_lo, _, trip = state
                lo, hi, c_lo = probe(lo, hi, c_lo, False)
                lo, hi, c_lo = probe(lo, hi, c_lo, trip % 2 == 1)
                return lo, hi, c_lo, unsettled(lo, hi, c_lo), trip + 1

            def keep_going(state):
                return jnp.logical_and(state[3] > 0.0, state[4] < 64)

            state = (lo0, hi0, c_lo0, unsettled(lo0, hi0, c_lo0), jnp.int32(0))
            lo, _, c_lo, _, _ = lax.while_loop(keep_going, probes, state)
            return lo[0:1], c_lo[0:1]
        return run

    thr_key, n_ge = lax.switch(n_kt - 1, [search(k) for k in range(1, n_kt_all + 1)])
    thr = _key_to_f32(jnp.maximum(thr_key, KEY_NEG_INF))

    @pl.when(jnp.max(jnp.broadcast_to(n_ge, (SUBLANES, tq))) > kf)
    def _():
        def count(pred):
            def body(j, cnt):
                hit = jnp.where(pred(kpos + j * tk, score_ref[j]), 1.0, 0.0)
                return cnt + jnp.sum(hit.reshape(tk // fold, fold, tq), axis=0)

            cnt = lax.fori_loop(0, n_kt, body, jnp.zeros((fold, tq), F32))
            return jnp.sum(cnt, axis=0, keepdims=True)

        need = kf - count(lambda pos, sc: sc > thr)
        n_pos_bits = (n_kt_all * tk - 1).bit_length()

        def pos_step(t, cut):
            cand = cut | jnp.left_shift(jnp.int32(1), n_pos_bits - 1 - t)
            before = count(lambda pos, sc: (sc == thr) & (pos < cand))
            return jnp.where(before < need, cand, cut)

        cut = lax.fori_loop(0, n_pos_bits, pos_step, jnp.zeros((1, tq), I32))

        def drop_tile(j, carry):
            sc = score_ref[j]
            score_ref[j] = jnp.where((sc == thr) & (kpos + j * tk > cut), -jnp.inf, sc)
            return carry

        lax.fori_loop(0, n_kt, drop_tile, 0)

    def mask_tile(j, carry):
        keep = (score_ref[j] >= thr) & (kpos + j * tk <= qpos)
        score_ref[j] = jnp.where(keep, 0.0, NEG_BIG)
        return carry

    lax.fori_loop(0, n_kt, mask_tile, 0)

    m_ref[...] = jnp.full(m_ref.shape, NEG_BIG, F32)
    l_ref[...] = jnp.zeros(l_ref.shape, F32)
    acc_ref[...] = jnp.zeros(acc_ref.shape, F32)
    p_buf[...] = jnp.zeros(p_buf.shape, p_buf.dtype)
    a_buf[...] = jnp.ones(a_buf.shape, F32)

    def qk(j, g):
        bias = jnp.concatenate([score_ref[j]] * hg, axis=1)
        s_buf[g % 2] = _dot(kv_ref[j], q_s[g]) + bias

    def softmax(g):
        s = s_buf[g % 2]
        m_old = m_ref[g]
        m_new = jnp.maximum(m_old, jnp.max(s, axis=0, keepdims=True))
        p = jnp.exp2(s - m_new)
        alpha = jnp.exp2(m_old - m_new)
        l_ref[g] = alpha * l_ref[g] + jnp.sum(p, axis=0, keepdims=True)
        a_buf[g % 2] = alpha
        p_buf[g % 2] = p.astype(p_buf.dtype)
        m_ref[g] = m_new

    def pv(j, g):
        acc_ref[g] = a_buf[g % 2] * acc_ref[g] + _dot(kvt_ref[j], p_buf[g % 2])

    last = n_groups - 1
    qk(0, 0)

    def attn_tile(j):
        for g in range(n_groups):
            if g < last:
                qk(j, g + 1)
            else:
                qk(jnp.minimum(j + 1, n_kt - 1), 0)
            if g > 0:
                pv(j, g - 1)
            else:
                pv(jnp.maximum(j - 1, 0), last)
            softmax(g)

    _unrolled_loop(n_kt, attn_tile, TILE_UNROLL)
    pv(n_kt - 1, last)

    for h in range(n_heads):
        g, c = divmod(h, hg)
        lanes = slice(c * tq, (c + 1) * tq)
        o_t = (acc_ref[g, :, lanes] / l_ref[g, :, lanes]).astype(wuvt_ref.dtype)
        yt_ref[h * D_VHEAD:(h + 1) * D_VHEAD, :] = _dot(wuvt_ref[h], o_t)


def _dsa_attention(qi_t, wi_t, kidx, q_t, kv, kvt, w_uv_t, topk, tq=128, heads_per_group=2):
    n_heads, d_lat, s = q_t.shape
    n_kt, tk, dk = kidx.shape
    assert tk % tq == 0
    n_groups = n_heads // heads_per_group
    gw = heads_per_group * tq
    const3 = lambda b: (0, 0, 0)
    return pl.pallas_call(
        functools.partial(_attn_kernel, topk=topk),
        grid=(s // tq,),
        in_specs=[
            pl.BlockSpec((N_IDX_HEADS, dk, tq), lambda b: (0, 0, b)),
            pl.BlockSpec((N_IDX_HEADS, tq), lambda b: (0, b)),
            pl.BlockSpec((n_kt, tk, dk), const3),
            pl.BlockSpec((n_heads, d_lat, tq), lambda b: (0, 0, b)),
            pl.BlockSpec((n_kt, tk, d_lat), const3),
            pl.BlockSpec((n_kt, d_lat, tk), const3),
            pl.BlockSpec((n_heads, D_VHEAD, d_lat), const3),
        ],
        out_specs=pl.BlockSpec((n_heads * D_VHEAD, tq), lambda b: (0, b)),
        out_shape=jax.ShapeDtypeStruct((n_heads * D_VHEAD, s), F32),
        scratch_shapes=[
            pltpu.VMEM((n_kt, tk, tq), F32),
            pltpu.VMEM((n_groups, dk, gw), BF16),
            pltpu.VMEM((n_groups, d_lat, gw), BF16),
            pltpu.VMEM((n_groups, 1, gw), F32),
            pltpu.VMEM((n_groups, 1, gw), F32),
            pltpu.VMEM((n_groups, 1, gw), F32),
            pltpu.VMEM((n_groups, d_lat, gw), F32),
            pltpu.VMEM((2, tk, gw), F32),
            pltpu.VMEM((2, tk, gw), BF16),
            pltpu.VMEM((2, 1, gw), F32),
        ],
        compiler_params=_params("parallel"),
        name="dsa_attention",
    )(qi_t, wi_t, kidx, q_t, kv, kvt, w_uv_t)


def _merge_kernel(x_ref, z_ref, wgc_ref, wga_ref, wco_ref, yat_ref, o_ref, w_s, wco_s):
    @pl.when(pl.program_id(1) == 0)
    def _():
        w_s[0] = wgc_ref[...].astype(w_s.dtype)
        w_s[1] = wga_ref[...].astype(w_s.dtype)
        wco_s[...] = wco_ref[...].astype(wco_s.dtype)

    x = x_ref[...]
    g_conv = _sigmoid(_dot_nt(x, w_s[0]))
    g_attn = _sigmoid(_dot_nt(x, w_s[1]))
    y_conv = _dot(z_ref[...], wco_s[...])
    o_ref[...] = (g_conv * y_conv + g_attn * yat_ref[...].T).astype(o_ref.dtype)


def _merge(x_bf, z, w_in_t, row0, w_co, y_attn_t, tm=512, tn=512):
    s, d = z.shape
    n = w_co.shape[1]
    gate_rows = lambda off: pl.BlockSpec(
        (pl.Element(tn), pl.Element(d)),
        lambda j, i: (pl.multiple_of(row0 + off + j * tn, SUBLANES), 0))
    return pl.pallas_call(
        _merge_kernel,
        grid=(n // tn, s // tm),
        in_specs=[
            pl.BlockSpec((tm, d), lambda j, i: (i, 0)),
            pl.BlockSpec((tm, d), lambda j, i: (i, 0)),
            gate_rows(0),
            gate_rows(n),
            pl.BlockSpec((d, tn), lambda j, i: (0, j)),
            pl.BlockSpec((tn, tm), lambda j, i: (j, i)),
        ],
        out_specs=pl.BlockSpec((tm, tn), lambda j, i: (i, j)),
        out_shape=jax.ShapeDtypeStruct((s, n), BF16),
        scratch_shapes=[pltpu.VMEM((2, tn, d), BF16), pltpu.VMEM((d, tn), BF16)],
        compiler_params=_params("parallel", "arbitrary"),
        name="merge",
    )(x_bf, z, w_in_t, w_in_t, w_co, y_attn_t)


def _cast_resident(w_ref, w_s):
    @pl.when(pl.program_id(0) == 0)
    def _():
        w_s[...] = w_ref[...].astype(w_s.dtype)


def _resident(shape):
    return pl.BlockSpec(shape, lambda i: (0,) * len(shape), pipeline_mode=pl.Buffered(1))


def _row_halves(tm):
    return (slice(0, tm // 2), slice(tm // 2, tm))


def _oproj_kernel(m_ref, w_ref, h_ref, g_ref, b_ref, o_ref, obf_ref, w_s, *, alpha):
    _cast_resident(w_ref, w_s)
    for rows in _row_halves(m_ref.shape[0]):
        v = alpha * h_ref[rows, :] + _dot(m_ref[rows, :], w_s[...])
        out = _layer_norm(v, g_ref[...], b_ref[...])
        o_ref[rows, :] = out
        obf_ref[rows, :] = out.astype(obf_ref.dtype)


def _oproj(mixed, w_o, h, g, b, alpha, tm=512):
    s, d = h.shape
    row = pl.BlockSpec((tm, d), lambda i: (i, 0))
    vec = pl.BlockSpec((1, d), lambda i: (0, 0))
    return pl.pallas_call(
        functools.partial(_oproj_kernel, alpha=alpha),
        grid=(s // tm,),
        in_specs=[row, _resident((d, d)), row, vec, vec],
        out_specs=[row, row],
        out_shape=[jax.ShapeDtypeStruct((s, d), F32), jax.ShapeDtypeStruct((s, d), BF16)],
        scratch_shapes=[pltpu.VMEM((d, d), BF16)],
        compiler_params=_params("arbitrary"),
        name="oproj_ln",
    )(mixed, w_o, h, g, b)


def _ffn_kernel(hbf_ref, wu_ref, wd_ref, h_ref, g_ref, b_ref, o_ref, obf_ref, acc_ref, *, alpha):
    f = pl.program_id(1)

    @pl.when(f == 0)
    def _():
        acc_ref[...] = jnp.zeros(acc_ref.shape, F32)

    a = jnp.maximum(_dot(hbf_ref[...], wu_ref[...]), 0.0)
    acc_ref[...] += _dot((a * a).astype(wd_ref.dtype), wd_ref[...])

    @pl.when(f == pl.num_programs(1) - 1)
    def _():
        out = _layer_norm(alpha * h_ref[...] + acc_ref[...], g_ref[...], b_ref[...])
        o_ref[...] = out
        obf_ref[...] = out.astype(obf_ref.dtype)


def _ffn(h_bf, w_up, w_down, h, g, b, alpha, tm=512, tf=1024):
    s, d = h.shape
    dff = w_up.shape[1]
    row = pl.BlockSpec((tm, d), lambda i, f: (i, 0))
    vec = pl.BlockSpec((1, d), lambda i, f: (0, 0))
    return pl.pallas_call(
        functools.partial(_ffn_kernel, alpha=alpha),
        grid=(s // tm, dff // tf),
        in_specs=[row, pl.BlockSpec((d, tf), lambda i, f: (0, f)),
                  pl.BlockSpec((tf, d), lambda i, f: (f, 0)), row, vec, vec],
        out_specs=[row, row],
        out_shape=[jax.ShapeDtypeStruct((s, d), F32), jax.ShapeDtypeStruct((s, d), BF16)],
        scratch_shapes=[pltpu.VMEM((tm, d), F32)],
        compiler_params=_params("parallel", "arbitrary"),
        name="ffn_ln",
    )(h_bf, w_up, w_down, h, g, b)


def _ple_kernel(hbf_ref, wg_ref, p_ref, wp_ref, h_ref, g_ref, b_ref, o_ref, wg_s, *, alpha):
    _cast_resident(wg_ref, wg_s)
    for rows in _row_halves(hbf_ref.shape[0]):
        gate = _sigmoid(_dot(hbf_ref[rows, :], wg_s[...]))
        pe = gate * _dot(p_ref[rows, :], wp_ref[...])
        o_ref[rows, :] = _layer_norm(alpha * h_ref[rows, :] + pe, g_ref[...], b_ref[...])


def _ple(h_bf, w_pg, p_bf, w_p, h, g, b, alpha, tm=512):
    s, d = h.shape
    dp = p_bf.shape[1]
    row = pl.BlockSpec((tm, d), lambda i: (i, 0))
    vec = pl.BlockSpec((1, d), lambda i: (0, 0))
    return pl.pallas_call(
        functools.partial(_ple_kernel, alpha=alpha),
        grid=(s // tm,),
        in_specs=[row, _resident((d, d)),
                  pl.BlockSpec((tm, dp), lambda i: (i, 0)),
                  pl.BlockSpec((dp, d), lambda i: (0, 0)), row, vec, vec],
        out_specs=row,
        out_shape=jax.ShapeDtypeStruct((s, d), F32),
        scratch_shapes=[pltpu.VMEM((d, d), BF16)],
        compiler_params=_params("arbitrary"),
        name="ple_ln",
    )(h_bf, w_pg, p_bf, w_p, h, g, b)


def _layer(h, p, w_in, conv_w, w_conv_out, g_kv, w_uv, w_o, ln1_g, ln1_b, w_up, w_down,
           ln2_g, ln2_b, w_ple_gate, w_ple, ln3_g, ln3_b, alpha, tk=512):
    s, d = h.shape
    d_conv = conv_w.shape[1]
    nq = N_IDX_HEADS * D_IDX
    c_qlat = 3 * d_conv
    c_ckv = c_qlat + N_HEADS * D_LATENT
    c_gates = c_ckv + D_LATENT + nq + D_IDX + N_IDX_HEADS
    topk = min(TOPK_MAX, s // 4)
    n_kt = s // tk

    w_in_t = jnp.transpose(w_in)

    h_bf, ckv, kidx, qi_t, wi_t = _idx_proj(h, w_in_t, c_ckv, g_kv.reshape(1, D_LATENT))
    z, w_down_bf = _conv_branch(h_bf, w_in_t, conv_w, d_conv, w_down)
    q_t, w_up_bf = _qlat_proj(h_bf, w_in_t, c_qlat, w_up)

    kidx = kidx.reshape(n_kt, tk, 3 * D_IDX)
    kv = ckv.reshape(n_kt, tk, D_LATENT)
    kvt = jnp.transpose(kv, (0, 2, 1))

    w_uv_t = jnp.transpose(w_uv, (0, 2, 1)).astype(BF16)
    y_attn_t = _dsa_attention(qi_t, wi_t, kidx, q_t, kv, kvt, w_uv_t, topk)

    mixed = _merge(h_bf, z, w_in_t, c_gates, w_conv_out, y_attn_t)
    h1, h1_bf = _oproj(mixed, w_o, h, ln1_g.reshape(1, d), ln1_b.reshape(1, d), alpha)
    h2, h2_bf = _ffn(h1_bf, w_up_bf, w_down_bf, h1,
                     ln2_g.reshape(1, d), ln2_b.reshape(1, d), alpha)
    return _ple(h2_bf, w_ple_gate, p.astype(BF16), w_ple.astype(BF16), h2,
                ln3_g.reshape(1, d), ln3_b.reshape(1, d), alpha)


def kernel(x, p, w_in, conv_w, w_conv_out, g_kv, w_uv, w_o, ln1_g, ln1_b, w_up, w_down,
           ln2_g, ln2_b, w_ple_gate, w_ple, ln3_g, ln3_b):
    depth = w_in.shape[0]
    alpha = (2.0 * depth) ** 0.25
    batch = x.shape[0]
    outs = []
    for bi in range(batch):
        h = x[bi]
        for i in range(depth):
            h = _layer(h, p[i, bi], w_in[i], conv_w[i], w_conv_out[i], g_kv[i], w_uv[i], w_o[i],
                       ln1_g[i], ln1_b[i], w_up[i], w_down[i], ln2_g[i], ln2_b[i],
                       w_ple_gate[i], w_ple[i], ln3_g[i], ln3_b[i], alpha)
        outs.append(h)
    return jnp.stack(outs, axis=0)
```

```python
import functools

import jax
import jax.numpy as jnp
from jax import lax
from jax.experimental import pallas as pl
from jax.experimental.pallas import tpu as pltpu

F32 = jnp.float32
BF16 = jnp.bfloat16
I32 = jnp.int32

N_HEADS = 16
D_LATENT = 256
D_VHEAD = 128
N_IDX_HEADS = 16
D_IDX = 64
TOPK_MAX = 256
CONV_WIDTH = 3
LN_EPS = 1e-5
RMS_EPS = 1e-6
ATTN_SCALE = D_LATENT ** -0.5
IDX_SCALE = (D_IDX ** -0.5) * (N_IDX_HEADS ** -0.5)

VMEM_LIMIT_BYTES = 56 * 1024 * 1024
SUBLANES = 8
LANES = 128

INT_MIN = -(2 ** 31)
KEY_NEG_INF = INT_MIN + 0x7FFFFF
NEG_BIG = -1e30
TILE_UNROLL = 4
LOG2E = 1.4426950408889634


def _params(*sem):
    return pltpu.CompilerParams(dimension_semantics=sem, vmem_limit_bytes=VMEM_LIMIT_BYTES)


def _dot(a, b):
    return jnp.dot(a, b, preferred_element_type=F32)


def _dot_nt(a, b):
    return lax.dot_general(a, b, (((1,), (1,)), ((), ())), preferred_element_type=F32)


def _sigmoid(v):
    return 1.0 / (1.0 + jnp.exp(-v))


def _layer_norm(v, g, b):
    mu = jnp.mean(v, axis=-1, keepdims=True)
    c = v - mu
    var = jnp.mean(c * c, axis=-1, keepdims=True)
    return c * lax.rsqrt(var + LN_EPS) * g + b


def _split_hi_lo(v):
    hi = v.astype(BF16)
    lo = (v - hi.astype(F32)).astype(BF16)
    return hi, lo


def _side_cast_specs(w, grid):
    n_steps = grid[0] * grid[1]
    rows, cols = w.shape
    chunk = rows // n_steps
    assert chunk * n_steps == rows and chunk % (2 * SUBLANES) == 0
    spec = pl.BlockSpec((chunk, cols), lambda j, i: (j * grid[1] + i, 0))
    return spec, spec, jax.ShapeDtypeStruct(w.shape, BF16)


def _conv_kernel(x_ref, wb_ref, wc_ref, wu_ref, cw_ref, side_ref, z_ref, side_bf_ref, cu_ref, w_s):
    side_bf_ref[...] = side_ref[...].astype(side_bf_ref.dtype)
    i = pl.program_id(1)
    tm = x_ref.shape[0]
    halo = SUBLANES

    @pl.when(i == 0)
    def _():
        w_s[0] = wb_ref[...].astype(w_s.dtype)
        w_s[1] = wc_ref[...].astype(w_s.dtype)
        w_s[2] = wu_ref[...].astype(w_s.dtype)
        cu_ref[0:halo, :] = jnp.zeros((halo, cu_ref.shape[1]), F32)

    @pl.when(i > 0)
    def _():
        cu_ref[0:halo, :] = cu_ref[tm:tm + halo, :]

    x = x_ref[...]
    cu_ref[halo:halo + tm, :] = _dot_nt(x, w_s[1]) * _dot_nt(x, w_s[2])
    cw = cw_ref[...]
    v = (cw[0:1, :] * cu_ref[halo - 2:halo - 2 + tm, :]
         + cw[1:2, :] * cu_ref[halo - 1:halo - 1 + tm, :]
         + cw[2:3, :] * cu_ref[halo:halo + tm, :])
    z_ref[...] = (_dot_nt(x, w_s[0]) * v).astype(z_ref.dtype)


def _conv_branch(x_bf, w_in_t, conv_w, d_conv, w_side, tm=1024, tn=512):
    s, d = x_bf.shape
    nj = d_conv // tn
    grid = (nj, s // tm)
    side_in, side_out, side_shape = _side_cast_specs(w_side, grid)
    return pl.pallas_call(
        _conv_kernel,
        grid=grid,
        in_specs=[
            pl.BlockSpec((tm, d), lambda j, i: (i, 0)),
            pl.BlockSpec((tn, d), lambda j, i: (j, 0)),
            pl.BlockSpec((tn, d), lambda j, i: (nj + j, 0)),
            pl.BlockSpec((tn, d), lambda j, i: (2 * nj + j, 0)),
            pl.BlockSpec((CONV_WIDTH, tn), lambda j, i: (0, j)),
            side_in,
        ],
        out_specs=[pl.BlockSpec((tm, tn), lambda j, i: (i, j)), side_out],
        out_shape=[jax.ShapeDtypeStruct((s, d_conv), BF16), side_shape],
        scratch_shapes=[pltpu.VMEM((tm + SUBLANES, tn), F32),
                        pltpu.VMEM((3, tn, d), BF16)],
        compiler_params=_params("parallel", "arbitrary"),
        name="conv_branch",
    )(x_bf, w_in_t, w_in_t, w_in_t, conv_w, w_side)


def _qlat_kernel(x_ref, w_ref, side_ref, q_ref, side_bf_ref, w_s):
    side_bf_ref[...] = side_ref[...].astype(side_bf_ref.dtype)

    @pl.when(pl.program_id(1) == 0)
    def _():
        w_s[...] = w_ref[...].astype(w_s.dtype)

    acc = _dot_nt(w_s[...], x_ref[...]) * (ATTN_SCALE * LOG2E)
    q_ref[...] = acc.astype(q_ref.dtype).reshape(q_ref.shape)


def _qlat_proj(x_bf, w_in_t, row0, w_side, tm=1024, heads_per_step=4):
    s, d = x_bf.shape
    tn = heads_per_step * D_LATENT
    j0 = row0 // tn
    assert j0 * tn == row0
    grid = (N_HEADS // heads_per_step, s // tm)
    side_in, side_out, side_shape = _side_cast_specs(w_side, grid)
    return pl.pallas_call(
        _qlat_kernel,
        grid=grid,
        in_specs=[
            pl.BlockSpec((tm, d), lambda j, i: (i, 0)),
            pl.BlockSpec((tn, d), lambda j, i: (j0 + j, 0)),
            side_in,
        ],
        out_specs=[pl.BlockSpec((heads_per_step, D_LATENT, tm), lambda j, i: (j, 0, i)), side_out],
        out_shape=[jax.ShapeDtypeStruct((N_HEADS, D_LATENT, s), BF16), side_shape],
        scratch_shapes=[pltpu.VMEM((tn, d), BF16)],
        compiler_params=_params("parallel", "arbitrary"),
        name="qlat_proj",
    )(x_bf, w_in_t, w_side)


def _idx_kernel(x_ref, w_ref, g_ref, xbf_ref, ckv_ref, kidx_ref, qit_ref, wit_ref, w_s):
    @pl.when(pl.program_id(0) == 0)
    def _():
        w_s[...] = w_ref[...].astype(w_s.dtype)

    x = x_ref[...].astype(xbf_ref.dtype)
    xbf_ref[...] = x
    nq = N_IDX_HEADS * D_IDX
    r_q, r_k, r_w = D_LATENT, D_LATENT + nq, D_LATENT + nq + D_IDX
    c = _dot_nt(x, w_s[0:r_q])
    ms = jnp.mean(c * c, axis=-1, keepdims=True)
    ckv_ref[...] = (c * lax.rsqrt(ms + RMS_EPS) * g_ref[...]).astype(ckv_ref.dtype)
    k_hi, k_lo = _split_hi_lo(_dot_nt(x, w_s[r_k:r_w]))
    kidx_ref[...] = jnp.concatenate([k_hi, k_hi, k_lo], axis=1)
    q_t = _dot_nt(w_s[r_q:r_k], x)
    q_hi, q_lo = _split_hi_lo(q_t.reshape(N_IDX_HEADS, D_IDX, q_t.shape[1]))
    qit_ref[...] = jnp.concatenate([q_hi, q_lo, q_hi], axis=1)
    wit_ref[...] = _dot_nt(w_s[r_w:r_w + N_IDX_HEADS], x) * IDX_SCALE


def _idx_proj(x, w_in_t, row0, g_kv, tm=512):
    s, d = x.shape
    nq = N_IDX_HEADS * D_IDX
    n = D_LATENT + nq + D_IDX + N_IDX_HEADS
    n_pad = -(-n // LANES) * LANES
    return pl.pallas_call(
        _idx_kernel,
        grid=(s // tm,),
        in_specs=[
            pl.BlockSpec((tm, d), lambda i: (i, 0)),
            pl.BlockSpec((pl.Element(n_pad), pl.Element(d)), lambda i: (row0, 0)),
            pl.BlockSpec((1, D_LATENT), lambda i: (0, 0)),
        ],
        out_specs=[
            pl.BlockSpec((tm, d), lambda i: (i, 0)),
            pl.BlockSpec((tm, D_LATENT), lambda i: (i, 0)),
            pl.BlockSpec((tm, 3 * D_IDX), lambda i: (i, 0)),
            pl.BlockSpec((N_IDX_HEADS, 3 * D_IDX, tm), lambda i: (0, 0, i)),
            pl.BlockSpec((N_IDX_HEADS, tm), lambda i: (0, i)),
        ],
        out_shape=[
            jax.ShapeDtypeStruct((s, d), BF16),
            jax.ShapeDtypeStruct((s, D_LATENT), BF16),
            jax.ShapeDtypeStruct((s, 3 * D_IDX), BF16),
            jax.ShapeDtypeStruct((N_IDX_HEADS, 3 * D_IDX, s), BF16),
            jax.ShapeDtypeStruct((N_IDX_HEADS, s), F32),
        ],
        scratch_shapes=[pltpu.VMEM((n_pad, d), BF16)],
        compiler_params=_params("arbitrary"),
        name="idx_proj",
    )(x, w_in_t, g_kv)


def _flip_key(v):
    return v ^ ((v >> 31) & 0x7FFFFFFF)


def _key_to_f32(key):
    return pltpu.bitcast(_flip_key(key), F32)


def _unrolled_loop(n, body, unroll):
    start = 0
    while unroll >= 1:
        def trip(i, carry, start=start, unroll=unroll):
            for r in range(unroll):
                body(start + unroll * i + r)
            return carry

        trips = (n - start) // unroll
        lax.fori_loop(0, trips, trip, 0)
        start = start + unroll * trips
        unroll //= 2


def _attn_kernel(qit_ref, wit_ref, kidx_ref, qt_ref, kv_ref, kvt_ref, wuvt_ref, yt_ref,
                 score_ref, qi_s, q_s, w_s, m_ref, l_ref, acc_ref, s_buf, p_buf, a_buf, *, topk):
    b = pl.program_id(0)
    n_heads, d_lat, tq = qt_ref.shape
    n_kt_all, tk, _ = kidx_ref.shape
    n_groups, _, gw = q_s.shape
    hg = gw // tq
    n_kt = ((b + 1) * tq + tk - 1) // tk
    kpos = lax.broadcasted_iota(I32, (tk, tq), 0)
    qpos = b * tq + lax.broadcasted_iota(I32, (tk, tq), 1)

    for h in range(n_heads):
        g, c = divmod(h, hg)
        qi_s[g, :, c * tq:(c + 1) * tq] = qit_ref[h]
        q_s[g, :, c * tq:(c + 1) * tq] = qt_ref[h]
        w_s[g, :, c * tq:(c + 1) * tq] = wit_ref[h:h + 1, :]

    def score_tile(j):
        kidx = kidx_ref[j]
        score = jnp.zeros((tk, tq), F32)
        for g in range(n_groups):
            rel = jnp.maximum(_dot(kidx, qi_s[g]), 0.0) * w_s[g]
            for c in range(hg):
                score = score + rel[:, c * tq:(c + 1) * tq]
        score_ref[j] = jnp.where(kpos + j * tk <= qpos, score, -jnp.inf)

    _unrolled_loop(n_kt, score_tile, TILE_UNROLL)

    fold = 8 * SUBLANES
    kf = float(topk)

    n_causal = (qpos[0:1, :] + 1).astype(F32)

    def extremes(j, carry):
        top, bot = carry
        slabs = score_ref[j].reshape(tk // fold, fold, tq)
        for r in range(tk // fold):
            top = jnp.maximum(top, slabs[r])
            bot = jnp.minimum(bot, jnp.where(slabs[r] > -jnp.inf, slabs[r], jnp.inf))
        return top, bot

    top, bot = lax.fori_loop(0, n_kt, extremes, (jnp.full((fold, tq), -jnp.inf, F32),
                                                 jnp.full((fold, tq), jnp.inf, F32)))
    rep = lambda v: jnp.broadcast_to(v, (SUBLANES, tq))
    short = n_causal < kf
    lo0 = rep(jnp.where(short, KEY_NEG_INF,
                        _flip_key(pltpu.bitcast(jnp.min(bot, axis=0, keepdims=True), I32))))
    hi0 = rep(_flip_key(pltpu.bitcast(jnp.max(top, axis=0, keepdims=True), I32)) + 1)
    c_lo0 = rep(jnp.where(short, 0.0, n_causal))

    def unsettled(lo, hi, c_lo):
        return jnp.max(jnp.where((c_lo > kf) & (hi - 1 > lo), 1.0, 0.0))

    def search(tiles):
        def run():
            def count_ge(cand):
                cnt = jnp.zeros((fold, tq), F32)
                for j in range(tiles):
                    hit = jnp.where(score_ref[j] >= cand, 1.0, 0.0)
                    cnt = cnt + jnp.sum(hit.reshape(tk // fold, fold, tq), axis=0)
                return jnp.sum(cnt, axis=0, keepdims=True)

            def probe(lo, hi, c_lo, by_key):
                live = (c_lo > kf) & (hi - 1 > lo)
                key_mid = (lo >> 1) + (hi >> 1) + (lo & hi & 1)
                val_mid = 0.5 * (_key_to_f32(lo) + _key_to_f32(hi - 1))
                pk = jnp.where(by_key, key_mid, _flip_key(pltpu.bitcast(val_mid, I32)))
                pk = jnp.minimum(jnp.maximum(pk, lo + 1), hi - 1)
                pk = jnp.where(live, pk, lo)
                c = rep(count_ge(_key_to_f32(jnp.maximum(pk, KEY_NEG_INF))[0:1]))
                up = live & (c >= kf)
                down = live & (c < kf)
                return jnp.where(up, pk, lo), jnp.where(down, pk, hi), jnp.where(up, c, c_lo)

            def probes(state):
                lo, hi, c_lo, _, trip = state
                lo, hi, c_lo = probe(lo, hi, c_lo, False)
                lo, hi, c_lo = probe(lo, hi, c_lo, trip % 2 == 1)
                return lo, hi, c_lo, unsettled(lo, hi, c_lo), trip + 1

            def keep_going(state):
                return jnp.logical_and(state[3] > 0.0, state[4] < 64)

            state = (lo0, hi0, c_lo0, unsettled(lo0, hi0, c_lo0), jnp.int32(0))
            lo, _, c_lo, _, _ = lax.while_loop(keep_going, probes, state)
            return lo[0:1], c_lo[0:1]
        return run

    thr_key, n_ge = lax.switch(n_kt - 1, [search(k) for k in range(1, n_kt_all + 1)])
    thr = _key_to_f32(jnp.maximum(thr_key, KEY_NEG_INF))

    @pl.when(jnp.max(jnp.broadcast_to(n_ge, (SUBLANES, tq))) > kf)
    def _():
        def count(pred):
            def body(j, cnt):
                hit = jnp.where(pred(kpos + j * tk, score_ref[j]), 1.0, 0.0)
                return cnt + jnp.sum(hit.reshape(tk // fold, fold, tq), axis=0)

            cnt = lax.fori_loop(0, n_kt, body, jnp.zeros((fold, tq), F32))
            return jnp.sum(cnt, axis=0, keepdims=True)

        need = kf - count(lambda pos, sc: sc > thr)
        n_pos_bits = (n_kt_all * tk - 1).bit_length()

        def pos_step(t, cut):
            cand = cut | jnp.left_shift(jnp.int32(1), n_pos_bits - 1 - t)
            before = count(lambda pos, sc: (sc == thr) & (pos < cand))
            return jnp.where(before < need, cand, cut)

        cut = lax.fori_loop(0, n_pos_bits, pos_step, jnp.zeros((1, tq), I32))

        def drop_tile(j, carry):
            sc = score_ref[j]
            score_ref[j] = jnp.where((sc == thr) & (kpos + j * tk > cut), -jnp.inf, sc)
            return carry

        lax.fori_loop(0, n_kt, drop_tile, 0)

    def mask_tile(j):
        keep = (score_ref[j] >= thr) & (kpos + j * tk <= qpos)
        score_ref[j] = jnp.where(keep, 0.0, NEG_BIG)

    _unrolled_loop(n_kt, mask_tile, TILE_UNROLL)

    m_ref[...] = jnp.full(m_ref.shape, NEG_BIG, F32)
    l_ref[...] = jnp.zeros(l_ref.shape, F32)
    acc_ref[...] = jnp.zeros(acc_ref.shape, F32)
    p_buf[...] = jnp.zeros(p_buf.shape, p_buf.dtype)
    a_buf[...] = jnp.ones(a_buf.shape, F32)

    def qk(j, g):
        bias = jnp.concatenate([score_ref[j]] * hg, axis=1)
        s_buf[g % 2] = _dot(kv_ref[j], q_s[g]) + bias

    def softmax(g):
        s = s_buf[g % 2]
        m_old = m_ref[g]
        m_new = jnp.maximum(m_old, jnp.max(s, axis=0, keepdims=True))
        p = jnp.exp2(s - m_new)
        alpha = jnp.exp2(m_old - m_new)
        l_ref[g] = alpha * l_ref[g] + jnp.sum(p, axis=0, keepdims=True)
        a_buf[g % 2] = alpha
        p_buf[g % 2] = p.astype(p_buf.dtype)
        m_ref[g] = m_new

    def pv(j, g):
        acc_ref[g] = a_buf[g % 2] * acc_ref[g] + _dot(kvt_ref[j], p_buf[g % 2])

    last = n_groups - 1
    qk(0, 0)

    def attn_tile(j):
        for g in range(n_groups):
            if g < last:
                qk(j, g + 1)
            else:
                qk(jnp.minimum(j + 1, n_kt - 1), 0)
            if g > 0:
                pv(j, g - 1)
            else:
                pv(jnp.maximum(j - 1, 0), last)
            softmax(g)

    _unrolled_loop(n_kt, attn_tile, TILE_UNROLL)
    pv(n_kt - 1, last)

    for h in range(n_heads):
        g, c = divmod(h, hg)
        lanes = slice(c * tq, (c + 1) * tq)
        o_t = (acc_ref[g, :, lanes] / l_ref[g, :, lanes]).astype(wuvt_ref.dtype)
        yt_ref[h * D_VHEAD:(h + 1) * D_VHEAD, :] = _dot(wuvt_ref[h], o_t)


def _dsa_attention(qi_t, wi_t, kidx, q_t, kv, kvt, w_uv_t, topk, tq=128, heads_per_group=2):
    n_heads, d_lat, s = q_t.shape
    n_kt, tk, dk = kidx.shape
    assert tk % tq == 0
    n_groups = n_heads // heads_per_group
    gw = heads_per_group * tq
    const3 = lambda b: (0, 0, 0)
    return pl.pallas_call(
        functools.partial(_attn_kernel, topk=topk),
        grid=(s // tq,),
        in_specs=[
            pl.BlockSpec((N_IDX_HEADS, dk, tq), lambda b: (0, 0, b)),
            pl.BlockSpec((N_IDX_HEADS, tq), lambda b: (0, b)),
            pl.BlockSpec((n_kt, tk, dk), const3),
            pl.BlockSpec((n_heads, d_lat, tq), lambda b: (0, 0, b)),
            pl.BlockSpec((n_kt, tk, d_lat), const3),
            pl.BlockSpec((n_kt, d_lat, tk), const3),
            pl.BlockSpec((n_heads, D_VHEAD, d_lat), const3),
        ],
        out_specs=pl.BlockSpec((n_heads * D_VHEAD, tq), lambda b: (0, b)),
        out_shape=jax.ShapeDtypeStruct((n_heads * D_VHEAD, s), F32),
        scratch_shapes=[
            pltpu.VMEM((n_kt, tk, tq), F32),
            pltpu.VMEM((n_groups, dk, gw), BF16),
            pltpu.VMEM((n_groups, d_lat, gw), BF16),
            pltpu.VMEM((n_groups, 1, gw), F32),
            pltpu.VMEM((n_groups, 1, gw), F32),
            pltpu.VMEM((n_groups, 1, gw), F32),
            pltpu.VMEM((n_groups, d_lat, gw), F32),
            pltpu.VMEM((2, tk, gw), F32),
            pltpu.VMEM((2, tk, gw), BF16),
            pltpu.VMEM((2, 1, gw), F32),
        ],
        compiler_params=_params("parallel"),
        name="dsa_attention",
    )(qi_t, wi_t, kidx, q_t, kv, kvt, w_uv_t)


def _merge_kernel(x_ref, z_ref, wgc_ref, wga_ref, wco_ref, yat_ref, o_ref, w_s, wco_s):
    @pl.when(pl.program_id(1) == 0)
    def _():
        w_s[0] = wgc_ref[...].astype(w_s.dtype)
        w_s[1] = wga_ref[...].astype(w_s.dtype)
        wco_s[...] = wco_ref[...].astype(wco_s.dtype)

    x = x_ref[...]
    g_conv = _sigmoid(_dot_nt(x, w_s[0]))
    g_attn = _sigmoid(_dot_nt(x, w_s[1]))
    y_conv = _dot(z_ref[...], wco_s[...])
    o_ref[...] = (g_conv * y_conv + g_attn * yat_ref[...].T).astype(o_ref.dtype)


def _merge(x_bf, z, w_in_t, row0, w_co, y_attn_t, tm=512, tn=512):
    s, d = z.shape
    n = w_co.shape[1]
    gate_rows = lambda off: pl.BlockSpec(
        (pl.Element(tn), pl.Element(d)),
        lambda j, i: (pl.multiple_of(row0 + off + j * tn, SUBLANES), 0))
    return pl.pallas_call(
        _merge_kernel,
        grid=(n // tn, s // tm),
        in_specs=[
            pl.BlockSpec((tm, d), lambda j, i: (i, 0)),
            pl.BlockSpec((tm, d), lambda j, i: (i, 0)),
            gate_rows(0),
            gate_rows(n),
            pl.BlockSpec((d, tn), lambda j, i: (0, j)),
            pl.BlockSpec((tn, tm), lambda j, i: (j, i)),
        ],
        out_specs=pl.BlockSpec((tm, tn), lambda j, i: (i, j)),
        out_shape=jax.ShapeDtypeStruct((s, n), BF16),
        scratch_shapes=[pltpu.VMEM((2, tn, d), BF16), pltpu.VMEM((d, tn), BF16)],
        compiler_params=_params("parallel", "arbitrary"),
        name="merge",
    )(x_bf, z, w_in_t, w_in_t, w_co, y_attn_t)


def _cast_resident(w_ref, w_s):
    @pl.when(pl.program_id(0) == 0)
    def _():
        w_s[...] = w_ref[...].astype(w_s.dtype)


def _resident(shape):
    return pl.BlockSpec(shape, lambda i: (0,) * len(shape), pipeline_mode=pl.Buffered(1))


def _oproj_kernel(m_ref, w_ref, h_ref, g_ref, b_ref, o_ref, obf_ref, w_s, *, alpha):
    _cast_resident(w_ref, w_s)
    v = alpha * h_ref[...] + _dot(m_ref[...], w_s[...])
    out = _layer_norm(v, g_ref[...], b_ref[...])
    o_ref[...] = out
    obf_ref[...] = out.astype(obf_ref.dtype)


def _oproj(mixed, w_o, h, g, b, alpha, tm=512):
    s, d = h.shape
    row = pl.BlockSpec((tm, d), lambda i: (i, 0))
    vec = pl.BlockSpec((1, d), lambda i: (0, 0))
    return pl.pallas_call(
        functools.partial(_oproj_kernel, alpha=alpha),
        grid=(s // tm,),
        in_specs=[row, _resident((d, d)), row, vec, vec],
        out_specs=[row, row],
        out_shape=[jax.ShapeDtypeStruct((s, d), F32), jax.ShapeDtypeStruct((s, d), BF16)],
        scratch_shapes=[pltpu.VMEM((d, d), BF16)],
        compiler_params=_params("arbitrary"),
        name="oproj_ln",
    )(mixed, w_o, h, g, b)


def _ffn_kernel(hbf_ref, wu_ref, wd_ref, h_ref, g_ref, b_ref, o_ref, obf_ref, acc_ref, *, alpha):
    f = pl.program_id(1)

    @pl.when(f == 0)
    def _():
        acc_ref[...] = jnp.zeros(acc_ref.shape, F32)

    a = jnp.maximum(_dot(hbf_ref[...], wu_ref[...]), 0.0)
    acc_ref[...] += _dot((a * a).astype(wd_ref.dtype), wd_ref[...])

    @pl.when(f == pl.num_programs(1) - 1)
    def _():
        out = _layer_norm(alpha * h_ref[...] + acc_ref[...], g_ref[...], b_ref[...])
        o_ref[...] = out
        obf_ref[...] = out.astype(obf_ref.dtype)


def _ffn(h_bf, w_up, w_down, h, g, b, alpha, tm=512, tf=1024):
    s, d = h.shape
    dff = w_up.shape[1]
    row = pl.BlockSpec((tm, d), lambda i, f: (i, 0))
    vec = pl.BlockSpec((1, d), lambda i, f: (0, 0))
    return pl.pallas_call(
        functools.partial(_ffn_kernel, alpha=alpha),
        grid=(s // tm, dff // tf),
        in_specs=[row, pl.BlockSpec((d, tf), lambda i, f: (0, f)),
                  pl.BlockSpec((tf, d), lambda i, f: (f, 0)), row, vec, vec],
        out_specs=[row, row],
        out_shape=[jax.ShapeDtypeStruct((s, d), F32), jax.ShapeDtypeStruct((s, d), BF16)],
        scratch_shapes=[pltpu.VMEM((tm, d), F32)],
        compiler_params=_params("parallel", "arbitrary"),
        name="ffn_ln",
    )(h_bf, w_up, w_down, h, g, b)


def _ple_kernel(hbf_ref, wg_ref, p_ref, wp_ref, h_ref, g_ref, b_ref, o_ref, wg_s, *, alpha):
    _cast_resident(wg_ref, wg_s)
    gate = _sigmoid(_dot(hbf_ref[...], wg_s[...]))
    pe = gate * _dot(p_ref[...], wp_ref[...])
    o_ref[...] = _layer_norm(alpha * h_ref[...] + pe, g_ref[...], b_ref[...])


def _ple(h_bf, w_pg, p_bf, w_p, h, g, b, alpha, tm=512):
    s, d = h.shape
    dp = p_bf.shape[1]
    row = pl.BlockSpec((tm, d), lambda i: (i, 0))
    vec = pl.BlockSpec((1, d), lambda i: (0, 0))
    return pl.pallas_call(
        functools.partial(_ple_kernel, alpha=alpha),
        grid=(s // tm,),
        in_specs=[row, _resident((d, d)),
                  pl.BlockSpec((tm, dp), lambda i: (i, 0)),
                  pl.BlockSpec((dp, d), lambda i: (0, 0)), row, vec, vec],
        out_specs=row,
        out_shape=jax.ShapeDtypeStruct((s, d), F32),
        scratch_shapes=[pltpu.VMEM((d, d), BF16)],
        compiler_params=_params("arbitrary"),
        name="ple_ln",
    )(h_bf, w_pg, p_bf, w_p, h, g, b)


def _layer(h, p, w_in, conv_w, w_conv_out, g_kv, w_uv, w_o, ln1_g, ln1_b, w_up, w_down,
           ln2_g, ln2_b, w_ple_gate, w_ple, ln3_g, ln3_b, alpha, tk=512):
    s, d = h.shape
    d_conv = conv_w.shape[1]
    nq = N_IDX_HEADS * D_IDX
    c_qlat = 3 * d_conv
    c_ckv = c_qlat + N_HEADS * D_LATENT
    c_gates = c_ckv + D_LATENT + nq + D_IDX + N_IDX_HEADS
    topk = min(TOPK_MAX, s // 4)
    n_kt = s // tk

    w_in_t = jnp.transpose(w_in)

    h_bf, ckv, kidx, qi_t, wi_t = _idx_proj(h, w_in_t, c_ckv, g_kv.reshape(1, D_LATENT))
    z, w_down_bf = _conv_branch(h_bf, w_in_t, conv_w, d_conv, w_down)
    q_t, w_up_bf = _qlat_proj(h_bf, w_in_t, c_qlat, w_up)

    kidx = kidx.reshape(n_kt, tk, 3 * D_IDX)
    kv = ckv.reshape(n_kt, tk, D_LATENT)
    kvt = jnp.transpose(kv, (0, 2, 1))

    w_uv_t = jnp.transpose(w_uv, (0, 2, 1)).astype(BF16)
    y_attn_t = _dsa_attention(qi_t, wi_t, kidx, q_t, kv, kvt, w_uv_t, topk)

    mixed = _merge(h_bf, z, w_in_t, c_gates, w_conv_out, y_attn_t)
    h1, h1_bf = _oproj(mixed, w_o, h, ln1_g.reshape(1, d), ln1_b.reshape(1, d), alpha)
    h2, h2_bf = _ffn(h1_bf, w_up_bf, w_down_bf, h1,
                     ln2_g.reshape(1, d), ln2_b.reshape(1, d), alpha)
    return _ple(h2_bf, w_ple_gate, p.astype(BF16), w_ple.astype(BF16), h2,
                ln3_g.reshape(1, d), ln3_b.reshape(1, d), alpha)


def kernel(x, p, w_in, conv_w, w_conv_out, g_kv, w_uv, w_o, ln1_g, ln1_b, w_up, w_down,
           ln2_g, ln2_b, w_ple_gate, w_ple, ln3_g, ln3_b):
    depth = w_in.shape[0]
    alpha = (2.0 * depth) ** 0.25
    batch = x.shape[0]
    outs = []
    for bi in range(batch):
        h = x[bi]
        for i in range(depth):
            h = _layer(h, p[i, bi], w_in[i], conv_w[i], w_conv_out[i], g_kv[i], w_uv[i], w_o[i],
                       ln1_g[i], ln1_b[i], w_up[i], w_down[i], ln2_g[i], ln2_b[i],
                       w_ple_gate[i], w_ple[i], ln3_g[i], ln3_b[i], alpha)
        outs.append(h)
    return jnp.stack(outs, axis=0)
```

```python
import functools

import jax
import jax.numpy as jnp
from jax import lax
from jax.experimental import pallas as pl
from jax.experimental.pallas import tpu as pltpu

F32 = jnp.float32
BF16 = jnp.bfloat16
I32 = jnp.int32

N_HEADS = 16
D_LATENT = 256
D_VHEAD = 128
N_IDX_HEADS = 16
D_IDX = 64
TOPK_MAX = 256
CONV_WIDTH = 3
LN_EPS = 1e-5
RMS_EPS = 1e-6
ATTN_SCALE = D_LATENT ** -0.5
IDX_SCALE = (D_IDX ** -0.5) * (N_IDX_HEADS ** -0.5)

VMEM_LIMIT_BYTES = 56 * 1024 * 1024
SUBLANES = 8
LANES = 128

INT_MIN = -(2 ** 31)
KEY_NEG_INF = INT_MIN + 0x7FFFFF
NEG_BIG = -1e30
TILE_UNROLL = 4
LOG2E = 1.4426950408889634


def _params(*sem):
    return pltpu.CompilerParams(dimension_semantics=sem, vmem_limit_bytes=VMEM_LIMIT_BYTES)


def _dot(a, b):
    return jnp.dot(a, b, preferred_element_type=F32)


def _dot_nt(a, b):
    return lax.dot_general(a, b, (((1,), (1,)), ((), ())), preferred_element_type=F32)


def _sigmoid(v):
    return 1.0 / (1.0 + jnp.exp(-v))


def _layer_norm(v, g, b):
    mu = jnp.mean(v, axis=-1, keepdims=True)
    c = v - mu
    var = jnp.mean(c * c, axis=-1, keepdims=True)
    return c * lax.rsqrt(var + LN_EPS) * g + b


def _split_hi_lo(v):
    hi = v.astype(BF16)
    lo = (v - hi.astype(F32)).astype(BF16)
    return hi, lo


def _side_cast_specs(w, grid):
    n_steps = grid[0] * grid[1]
    rows, cols = w.shape
    chunk = rows // n_steps
    assert chunk * n_steps == rows and chunk % (2 * SUBLANES) == 0
    spec = pl.BlockSpec((chunk, cols), lambda j, i: (j * grid[1] + i, 0))
    return spec, spec, jax.ShapeDtypeStruct(w.shape, BF16)


def _conv_kernel(x_ref, wb_ref, wc_ref, wu_ref, cw_ref, side_ref, z_ref, side_bf_ref, cu_ref, w_s):
    side_bf_ref[...] = side_ref[...].astype(side_bf_ref.dtype)
    i = pl.program_id(1)
    tm = x_ref.shape[0]
    halo = SUBLANES

    @pl.when(i == 0)
    def _():
        w_s[0] = wb_ref[...].astype(w_s.dtype)
        w_s[1] = wc_ref[...].astype(w_s.dtype)
        w_s[2] = wu_ref[...].astype(w_s.dtype)
        cu_ref[0:halo, :] = jnp.zeros((halo, cu_ref.shape[1]), F32)

    @pl.when(i > 0)
    def _():
        cu_ref[0:halo, :] = cu_ref[tm:tm + halo, :]

    x = x_ref[...]
    cu_ref[halo:halo + tm, :] = _dot_nt(x, w_s[1]) * _dot_nt(x, w_s[2])
    cw = cw_ref[...]
    v = (cw[0:1, :] * cu_ref[halo - 2:halo - 2 + tm, :]
         + cw[1:2, :] * cu_ref[halo - 1:halo - 1 + tm, :]
         + cw[2:3, :] * cu_ref[halo:halo + tm, :])
    z_ref[...] = (_dot_nt(x, w_s[0]) * v).astype(z_ref.dtype)


def _conv_branch(x_bf, w_in_t, conv_w, d_conv, w_side, tm=1024, tn=512):
    s, d = x_bf.shape
    nj = d_conv // tn
    grid = (nj, s // tm)
    side_in, side_out, side_shape = _side_cast_specs(w_side, grid)
    return pl.pallas_call(
        _conv_kernel,
        grid=grid,
        in_specs=[
            pl.BlockSpec((tm, d), lambda j, i: (i, 0)),
            pl.BlockSpec((tn, d), lambda j, i: (j, 0)),
            pl.BlockSpec((tn, d), lambda j, i: (nj + j, 0)),
            pl.BlockSpec((tn, d), lambda j, i: (2 * nj + j, 0)),
            pl.BlockSpec((CONV_WIDTH, tn), lambda j, i: (0, j)),
            side_in,
        ],
        out_specs=[pl.BlockSpec((tm, tn), lambda j, i: (i, j)), side_out],
        out_shape=[jax.ShapeDtypeStruct((s, d_conv), BF16), side_shape],
        scratch_shapes=[pltpu.VMEM((tm + SUBLANES, tn), F32),
                        pltpu.VMEM((3, tn, d), BF16)],
        compiler_params=_params("parallel", "arbitrary"),
        name="conv_branch",
    )(x_bf, w_in_t, w_in_t, w_in_t, conv_w, w_side)


def _qlat_kernel(x_ref, w_ref, side_ref, q_ref, side_bf_ref, w_s):
    side_bf_ref[...] = side_ref[...].astype(side_bf_ref.dtype)

    @pl.when(pl.program_id(1) == 0)
    def _():
        w_s[...] = w_ref[...].astype(w_s.dtype)

    acc = _dot_nt(w_s[...], x_ref[...]) * (ATTN_SCALE * LOG2E)
    q_ref[...] = acc.astype(q_ref.dtype).reshape(q_ref.shape)


def _qlat_proj(x_bf, w_in_t, row0, w_side, tm=1024, heads_per_step=4):
    s, d = x_bf.shape
    tn = heads_per_step * D_LATENT
    j0 = row0 // tn
    assert j0 * tn == row0
    grid = (N_HEADS // heads_per_step, s // tm)
    side_in, side_out, side_shape = _side_cast_specs(w_side, grid)
    return pl.pallas_call(
        _qlat_kernel,
        grid=grid,
        in_specs=[
            pl.BlockSpec((tm, d), lambda j, i: (i, 0)),
            pl.BlockSpec((tn, d), lambda j, i: (j0 + j, 0)),
            side_in,
        ],
        out_specs=[pl.BlockSpec((heads_per_step, D_LATENT, tm), lambda j, i: (j, 0, i)), side_out],
        out_shape=[jax.ShapeDtypeStruct((N_HEADS, D_LATENT, s), BF16), side_shape],
        scratch_shapes=[pltpu.VMEM((tn, d), BF16)],
        compiler_params=_params("parallel", "arbitrary"),
        name="qlat_proj",
    )(x_bf, w_in_t, w_side)


def _idx_kernel(x_ref, w_ref, g_ref, xbf_ref, ckv_ref, kidx_ref, qit_ref, wit_ref, w_s):
    @pl.when(pl.program_id(0) == 0)
    def _():
        w_s[...] = w_ref[...].astype(w_s.dtype)

    x = x_ref[...].astype(xbf_ref.dtype)
    xbf_ref[...] = x
    nq = N_IDX_HEADS * D_IDX
    r_q, r_k, r_w = D_LATENT, D_LATENT + nq, D_LATENT + nq + D_IDX
    c = _dot_nt(x, w_s[0:r_q])
    ms = jnp.mean(c * c, axis=-1, keepdims=True)
    ckv_ref[...] = (c * lax.rsqrt(ms + RMS_EPS) * g_ref[...]).astype(ckv_ref.dtype)
    k_hi, k_lo = _split_hi_lo(_dot_nt(x, w_s[r_k:r_w]))
    kidx_ref[...] = jnp.concatenate([k_hi, k_hi, k_lo], axis=1)
    q_t = _dot_nt(w_s[r_q:r_k], x)
    q_hi, q_lo = _split_hi_lo(q_t.reshape(N_IDX_HEADS, D_IDX, q_t.shape[1]))
    qit_ref[...] = jnp.concatenate([q_hi, q_lo, q_hi], axis=1)
    wit_ref[...] = _dot_nt(w_s[r_w:r_w + N_IDX_HEADS], x) * IDX_SCALE


def _idx_proj(x, w_in_t, row0, g_kv, tm=512):
    s, d = x.shape
    nq = N_IDX_HEADS * D_IDX
    n = D_LATENT + nq + D_IDX + N_IDX_HEADS
    n_pad = -(-n // LANES) * LANES
    return pl.pallas_call(
        _idx_kernel,
        grid=(s // tm,),
        in_specs=[
            pl.BlockSpec((tm, d), lambda i: (i, 0)),
            pl.BlockSpec((pl.Element(n_pad), pl.Element(d)), lambda i: (row0, 0)),
            pl.BlockSpec((1, D_LATENT), lambda i: (0, 0)),
        ],
        out_specs=[
            pl.BlockSpec((tm, d), lambda i: (i, 0)),
            pl.BlockSpec((tm, D_LATENT), lambda i: (i, 0)),
            pl.BlockSpec((tm, 3 * D_IDX), lambda i: (i, 0)),
            pl.BlockSpec((N_IDX_HEADS, 3 * D_IDX, tm), lambda i: (0, 0, i)),
            pl.BlockSpec((N_IDX_HEADS, tm), lambda i: (0, i)),
        ],
        out_shape=[
            jax.ShapeDtypeStruct((s, d), BF16),
            jax.ShapeDtypeStruct((s, D_LATENT), BF16),
            jax.ShapeDtypeStruct((s, 3 * D_IDX), BF16),
            jax.ShapeDtypeStruct((N_IDX_HEADS, 3 * D_IDX, s), BF16),
            jax.ShapeDtypeStruct((N_IDX_HEADS, s), F32),
        ],
        scratch_shapes=[pltpu.VMEM((n_pad, d), BF16)],
        compiler_params=_params("arbitrary"),
        name="idx_proj",
    )(x, w_in_t, g_kv)


def _flip_key(v):
    return v ^ ((v >> 31) & 0x7FFFFFFF)


def _key_to_f32(key):
    return pltpu.bitcast(_flip_key(key), F32)


def _unrolled_loop(n, body, unroll):
    start = 0
    while unroll >= 1:
        def trip(i, carry, start=start, unroll=unroll):
            for r in range(unroll):
                body(start + unroll * i + r)
            return carry

        trips = (n - start) // unroll
        lax.fori_loop(0, trips, trip, 0)
        start = start + unroll * trips
        unroll //= 2


def _attn_kernel(qit_ref, wit_ref, kidx_ref, qt_ref, kv_ref, kvt_ref, wuvt_ref, yt_ref,
                 score_ref, qi_s, q_s, w_s, m_ref, l_ref, acc_ref, s_buf, p_buf, a_buf, *, topk):
    b = pl.program_id(0)
    n_heads, d_lat, tq = qt_ref.shape
    n_kt_all, tk, _ = kidx_ref.shape
    n_groups, _, gw = q_s.shape
    hg = gw // tq
    n_kt = ((b + 1) * tq + tk - 1) // tk
    kpos = lax.broadcasted_iota(I32, (tk, tq), 0)
    qpos = b * tq + lax.broadcasted_iota(I32, (tk, tq), 1)

    for h in range(n_heads):
        g, c = divmod(h, hg)
        qi_s[g, :, c * tq:(c + 1) * tq] = qit_ref[h]
        q_s[g, :, c * tq:(c + 1) * tq] = qt_ref[h]
        w_s[g, :, c * tq:(c + 1) * tq] = wit_ref[h:h + 1, :]

    def score_tile(j):
        kidx = kidx_ref[j]
        score = jnp.zeros((tk, tq), F32)
        for g in range(n_groups):
            rel = jnp.maximum(_dot(kidx, qi_s[g]), 0.0) * w_s[g]
            for c in range(hg):
                score = score + rel[:, c * tq:(c + 1) * tq]
        score_ref[j] = jnp.where(kpos + j * tk <= qpos, score, -jnp.inf)

    _unrolled_loop(n_kt, score_tile, TILE_UNROLL)

    fold = 8 * SUBLANES
    kf = float(topk)

    n_causal = (qpos[0:1, :] + 1).astype(F32)

    def extremes(j, carry):
        top, bot = carry
        slabs = score_ref[j].reshape(tk // fold, fold, tq)
        for r in range(tk // fold):
            top = jnp.maximum(top, slabs[r])
            bot = jnp.minimum(bot, jnp.where(slabs[r] > -jnp.inf, slabs[r], jnp.inf))
        return top, bot

    top, bot = lax.fori_loop(0, n_kt, extremes, (jnp.full((fold, tq), -jnp.inf, F32),
                                                 jnp.full((fold, tq), jnp.inf, F32)))
    rep = lambda v: jnp.broadcast_to(v, (SUBLANES, tq))
    short = n_causal < kf
    lo0 = rep(jnp.where(short, KEY_NEG_INF,
                        _flip_key(pltpu.bitcast(jnp.min(bot, axis=0, keepdims=True), I32))))
    hi0 = rep(_flip_key(pltpu.bitcast(jnp.max(top, axis=0, keepdims=True), I32)) + 1)
    c_lo0 = rep(jnp.where(short, 0.0, n_causal))

    def unsettled(lo, hi, c_lo):
        return jnp.max(jnp.where((c_lo > kf) & (hi - 1 > lo), 1.0, 0.0))

    def search(tiles):
        def run():
            def count_ge(cand):
                cnt = jnp.zeros((fold, tq), F32)
                for j in range(tiles):
                    hit = jnp.where(score_ref[j] >= cand, 1.0, 0.0)
                    cnt = cnt + jnp.sum(hit.reshape(tk // fold, fold, tq), axis=0)
                return jnp.sum(cnt, axis=0, keepdims=True)

            def probe(lo, hi, c_lo, by_key):
                live = (c_lo > kf) & (hi - 1 > lo)
                key_mid = (lo >> 1) + (hi >> 1) + (lo & hi & 1)
                val_mid = 0.5 * (_key_to_f32(lo) + _key_to_f32(hi - 1))
                pk = jnp.where(by_key, key_mid, _flip_key(pltpu.bitcast(val_mid, I32)))
                pk = jnp.minimum(jnp.maximum(pk, lo + 1), hi - 1)
                pk = jnp.where(live, pk, lo)
                c = rep(count_ge(_key_to_f32(jnp.maximum(pk, KEY_NEG_INF))[0:1]))
                up = live & (c >= kf)
                down = live & (c < kf)
                return jnp.where(up, pk, lo), jnp.where(down, pk, hi), jnp.where(up, c, c_lo)

            def probes(state):
                lo, hi, c_lo, _, trip = state
                lo, hi, c_lo = probe(lo, hi, c_lo, False)
                lo, hi, c_lo = probe(lo, hi, c_lo, trip % 2 == 1)
                return lo, hi, c_lo, unsettled(lo, hi, c_lo), trip + 1

            def keep_going(state):
                return jnp.logical_and(state[3] > 0.0, state[4] < 64)

            state = (lo0, hi0, c_lo0, unsettled(lo0, hi0, c_lo0), jnp.int32(0))
            lo, _, c_lo, _, _ = lax.while_loop(keep_going, probes, state)
            return lo[0:1], c_lo[0:1]
        return run

    thr_key, n_ge = lax.switch(n_kt - 1, [search(k) for k in range(1, n_kt_all + 1)])
    thr = _key_to_f32(jnp.maximum(thr_key, KEY_NEG_INF))

    @pl.when(jnp.max(jnp.broadcast_to(n_ge, (SUBLANES, tq))) > kf)
    def _():
        def count(pred):
            def body(j, cnt):
                hit = jnp.where(pred(kpos + j * tk, score_ref[j]), 1.0, 0.0)
                return cnt + jnp.sum(hit.reshape(tk // fold, fold, tq), axis=0)

            cnt = lax.fori_loop(0, n_kt, body, jnp.zeros((fold, tq), F32))
            return jnp.sum(cnt, axis=0, keepdims=True)

        need = kf - count(lambda pos, sc: sc > thr)
        n_pos_bits = (n_kt_all * tk - 1).bit_length()

        def pos_step(t, cut):
            cand = cut | jnp.left_shift(jnp.int32(1), n_pos_bits - 1 - t)
            before = count(lambda pos, sc: (sc == thr) & (pos < cand))
            return jnp.where(before < need, cand, cut)

        cut = lax.fori_loop(0, n_pos_bits, pos_step, jnp.zeros((1, tq), I32))

        def drop_tile(j, carry):
            sc = score_ref[j]
            score_ref[j] = jnp.where((sc == thr) & (kpos + j * tk > cut), -jnp.inf, sc)
            return carry

        lax.fori_loop(0, n_kt, drop_tile, 0)

    def mask_tile(j, carry):
        keep = (score_ref[j] >= thr) & (kpos + j * tk <= qpos)
        score_ref[j] = jnp.where(keep, 0.0, NEG_BIG)
        return carry

    lax.fori_loop(0, n_kt, mask_tile, 0)

    m_ref[...] = jnp.full(m_ref.shape, NEG_BIG, F32)
    l_ref[...] = jnp.zeros(l_ref.shape, F32)
    acc_ref[...] = jnp.zeros(acc_ref.shape, F32)
    p_buf[...] = jnp.zeros(p_buf.shape, p_buf.dtype)
    a_buf[...] = jnp.ones(a_buf.shape, F32)

    def qk(j, g):
        bias = jnp.concatenate([score_ref[j]] * hg, axis=1)
        s_buf[g % 2] = _dot(kv_ref[j], q_s[g]) + bias

    def softmax(g):
        s = s_buf[g % 2]
        m_old = m_ref[g]
        m_new = jnp.maximum(m_old, jnp.max(s, axis=0, keepdims=True))
        p = jnp.exp2(s - m_new)
        alpha = jnp.exp2(m_old - m_new)
        l_ref[g] = alpha * l_ref[g] + jnp.sum(p, axis=0, keepdims=True)
        a_buf[g % 2] = alpha
        p_buf[g % 2] = p.astype(p_buf.dtype)
        m_ref[g] = m_new

    def pv(j, g):
        acc_ref[g] = a_buf[g % 2] * acc_ref[g] + _dot(kvt_ref[j], p_buf[g % 2])

    last = n_groups - 1
    qk(0, 0)

    def attn_tile(j):
        for g in range(n_groups):
            if g < last:
                qk(j, g + 1)
            else:
                qk(jnp.minimum(j + 1, n_kt - 1), 0)
            if g > 0:
                pv(j, g - 1)
            else:
                pv(jnp.maximum(j - 1, 0), last)
            softmax(g)

    _unrolled_loop(n_kt, attn_tile, TILE_UNROLL)
    pv(n_kt - 1, last)

    for h in range(n_heads):
        g, c = divmod(h, hg)
        lanes = slice(c * tq, (c + 1) * tq)
        o_t = (acc_ref[g, :, lanes] / l_ref[g, :, lanes]).astype(wuvt_ref.dtype)
        yt_ref[h * D_VHEAD:(h + 1) * D_VHEAD, :] = _dot(wuvt_ref[h], o_t)


def _dsa_attention(qi_t, wi_t, kidx, q_t, kv, kvt, w_uv_t, topk, tq=128, heads_per_group=2):
    n_heads, d_lat, s = q_t.shape
    n_kt, tk, dk = kidx.shape
    assert tk % tq == 0
    n_groups = n_heads // heads_per_group
    gw = heads_per_group * tq
    const3 = lambda b: (0, 0, 0)
    return pl.pallas_call(
        functools.partial(_attn_kernel, topk=topk),
        grid=(s // tq,),
        in_specs=[
            pl.BlockSpec((N_IDX_HEADS, dk, tq), lambda b: (0, 0, b)),
            pl.BlockSpec((N_IDX_HEADS, tq), lambda b: (0, b)),
            pl.BlockSpec((n_kt, tk, dk), const3),
            pl.BlockSpec((n_heads, d_lat, tq), lambda b: (0, 0, b)),
            pl.BlockSpec((n_kt, tk, d_lat), const3),
            pl.BlockSpec((n_kt, d_lat, tk), const3),
            pl.BlockSpec((n_heads, D_VHEAD, d_lat), const3),
        ],
        out_specs=pl.BlockSpec((n_heads * D_VHEAD, tq), lambda b: (0, b)),
        out_shape=jax.ShapeDtypeStruct((n_heads * D_VHEAD, s), F32),
        scratch_shapes=[
            pltpu.VMEM((n_kt, tk, tq), F32),
            pltpu.VMEM((n_groups, dk, gw), BF16),
            pltpu.VMEM((n_groups, d_lat, gw), BF16),
            pltpu.VMEM((n_groups, 1, gw), F32),
            pltpu.VMEM((n_groups, 1, gw), F32),
            pltpu.VMEM((n_groups, 1, gw), F32),
            pltpu.VMEM((n_groups, d_lat, gw), F32),
            pltpu.VMEM((2, tk, gw), F32),
            pltpu.VMEM((2, tk, gw), BF16),
            pltpu.VMEM((2, 1, gw), F32),
        ],
        compiler_params=_params("parallel"),
        name="dsa_attention",
    )(qi_t, wi_t, kidx, q_t, kv, kvt, w_uv_t)


def _merge_kernel(x_ref, z_ref, wgc_ref, wga_ref, wco_ref, yat_ref, o_ref, w_s, wco_s):
    @pl.when(pl.program_id(1) == 0)
    def _():
        w_s[0] = wgc_ref[...].astype(w_s.dtype)
        w_s[1] = wga_ref[...].astype(w_s.dtype)
        wco_s[...] = wco_ref[...].astype(wco_s.dtype)

    x = x_ref[...]
    g_conv = _sigmoid(_dot_nt(x, w_s[0]))
    g_attn = _sigmoid(_dot_nt(x, w_s[1]))
    y_conv = _dot(z_ref[...], wco_s[...])
    o_ref[...] = (g_conv * y_conv + g_attn * yat_ref[...].T).astype(o_ref.dtype)


def _merge(x_bf, z, w_in_t, row0, w_co, y_attn_t, tm=512, tn=512):
    s, d = z.shape
    n = w_co.shape[1]
    gate_rows = lambda off: pl.BlockSpec(
        (pl.Element(tn), pl.Element(d)),
        lambda j, i: (pl.multiple_of(row0 + off + j * tn, SUBLANES), 0))
    return pl.pallas_call(
        _merge_kernel,
        grid=(n // tn, s // tm),
        in_specs=[
            pl.BlockSpec((tm, d), lambda j, i: (i, 0)),
            pl.BlockSpec((tm, d), lambda j, i: (i, 0)),
            gate_rows(0),
            gate_rows(n),
            pl.BlockSpec((d, tn), lambda j, i: (0, j)),
            pl.BlockSpec((tn, tm), lambda j, i: (j, i)),
        ],
        out_specs=pl.BlockSpec((tm, tn), lambda j, i: (i, j)),
        out_shape=jax.ShapeDtypeStruct((s, n), BF16),
        scratch_shapes=[pltpu.VMEM((2, tn, d), BF16), pltpu.VMEM((d, tn), BF16)],
        compiler_params=_params("parallel", "arbitrary"),
        name="merge",
    )(x_bf, z, w_in_t, w_in_t, w_co, y_attn_t)


def _cast_resident(w_ref, w_s):
    @pl.when(pl.program_id(0) == 0)
    def _():
        w_s[...] = w_ref[...].astype(w_s.dtype)


def _resident(shape):
    return pl.BlockSpec(shape, lambda i: (0,) * len(shape), pipeline_mode=pl.Buffered(1))


def _oproj_kernel(m_ref, w_ref, h_ref, g_ref, b_ref, o_ref, obf_ref, w_s, *, alpha):
    _cast_resident(w_ref, w_s)
    v = alpha * h_ref[...] + _dot(m_ref[...], w_s[...])
    out = _layer_norm(v, g_ref[...], b_ref[...])
    o_ref[...] = out
    obf_ref[...] = out.astype(obf_ref.dtype)


def _oproj(mixed, w_o, h, g, b, alpha, tm=512):
    s, d = h.shape
    row = pl.BlockSpec((tm, d), lambda i: (i, 0))
    vec = pl.BlockSpec((1, d), lambda i: (0, 0))
    return pl.pallas_call(
        functools.partial(_oproj_kernel, alpha=alpha),
        grid=(s // tm,),
        in_specs=[row, _resident((d, d)), row, vec, vec],
        out_specs=[row, row],
        out_shape=[jax.ShapeDtypeStruct((s, d), F32), jax.ShapeDtypeStruct((s, d), BF16)],
        scratch_shapes=[pltpu.VMEM((d, d), BF16)],
        compiler_params=_params("arbitrary"),
        name="oproj_ln",
    )(mixed, w_o, h, g, b)


def _ffn_kernel(hbf_ref, wu_ref, wd_ref, h_ref, g_ref, b_ref, o_ref, obf_ref, acc_ref, *, alpha):
    f = pl.program_id(1)

    @pl.when(f == 0)
    def _():
        acc_ref[...] = jnp.zeros(acc_ref.shape, F32)

    a = jnp.maximum(_dot(hbf_ref[...], wu_ref[...]), 0.0)
    acc_ref[...] += _dot((a * a).astype(wd_ref.dtype), wd_ref[...])

    @pl.when(f == pl.num_programs(1) - 1)
    def _():
        out = _layer_norm(alpha * h_ref[...] + acc_ref[...], g_ref[...], b_ref[...])
        o_ref[...] = out
        obf_ref[...] = out.astype(obf_ref.dtype)


def _ffn(h_bf, w_up, w_down, h, g, b, alpha, tm=512, tf=1024):
    s, d = h.shape
    dff = w_up.shape[1]
    row = pl.BlockSpec((tm, d), lambda i, f: (i, 0))
    vec = pl.BlockSpec((1, d), lambda i, f: (0, 0))
    return pl.pallas_call(
        functools.partial(_ffn_kernel, alpha=alpha),
        grid=(s // tm, dff // tf),
        in_specs=[row, pl.BlockSpec((d, tf), lambda i, f: (0, f)),
                  pl.BlockSpec((tf, d), lambda i, f: (f, 0)), row, vec, vec],
        out_specs=[row, row],
        out_shape=[jax.ShapeDtypeStruct((s, d), F32), jax.ShapeDtypeStruct((s, d), BF16)],
        scratch_shapes=[pltpu.VMEM((tm, d), F32)],
        compiler_params=_params("parallel", "arbitrary"),
        name="ffn_ln",
    )(h_bf, w_up, w_down, h, g, b)


def _ple_kernel(hbf_ref, wg_ref, p_ref, wp_ref, h_ref, g_ref, b_ref, o_ref, wg_s, *, alpha):
    _cast_resident(wg_ref, wg_s)
    gate = _sigmoid(_dot(hbf_ref[...], wg_s[...]))
    pe = gate * _dot(p_ref[...], wp_ref[...])
    o_ref[...] = _layer_norm(alpha * h_ref[...] + pe, g_ref[...], b_ref[...])


def _ple(h_bf, w_pg, p_bf, w_p, h, g, b, alpha, tm=512):
    s, d = h.shape
    dp = p_bf.shape[1]
    row = pl.BlockSpec((tm, d), lambda i: (i, 0))
    vec = pl.BlockSpec((1, d), lambda i: (0, 0))
    return pl.pallas_call(
        functools.partial(_ple_kernel, alpha=alpha),
        grid=(s // tm,),
        in_specs=[row, _resident((d, d)),
                  pl.BlockSpec((tm, dp), lambda i: (i, 0)),
                  pl.BlockSpec((dp, d), lambda i: (0, 0)), row, vec, vec],
        out_specs=row,
        out_shape=jax.ShapeDtypeStruct((s, d), F32),
        scratch_shapes=[pltpu.VMEM((d, d), BF16)],
        compiler_params=_params("arbitrary"),
        name="ple_ln",
    )(h_bf, w_pg, p_bf, w_p, h, g, b)


def _layer(h, p, w_in, conv_w, w_conv_out, g_kv, w_uv, w_o, ln1_g, ln1_b, w_up, w_down,
           ln2_g, ln2_b, w_ple_gate, w_ple, ln3_g, ln3_b, alpha, tk=512):
    s, d = h.shape
    d_conv = conv_w.shape[1]
    nq = N_IDX_HEADS * D_IDX
    c_qlat = 3 * d_conv
    c_ckv = c_qlat + N_HEADS * D_LATENT
    c_gates = c_ckv + D_LATENT + nq + D_IDX + N_IDX_HEADS
    topk = min(TOPK_MAX, s // 4)
    n_kt = s // tk

    w_in_t = jnp.transpose(w_in)

    h_bf, ckv, kidx, qi_t, wi_t = _idx_proj(h, w_in_t, c_ckv, g_kv.reshape(1, D_LATENT))
    z, w_down_bf = _conv_branch(h_bf, w_in_t, conv_w, d_conv, w_down)
    q_t, w_up_bf = _qlat_proj(h_bf, w_in_t, c_qlat, w_up)

    kidx = kidx.reshape(n_kt, tk, 3 * D_IDX)
    kv = ckv.reshape(n_kt, tk, D_LATENT)
    kvt = jnp.transpose(kv, (0, 2, 1))

    w_uv_t = jnp.transpose(w_uv, (0, 2, 1)).astype(BF16)
    y_attn_t = _dsa_attention(qi_t, wi_t, kidx, q_t, kv, kvt, w_uv_t, topk)

    mixed = _merge(h_bf, z, w_in_t, c_gates, w_conv_out, y_attn_t)
    h1, h1_bf = _oproj(mixed, w_o, h, ln1_g.reshape(1, d), ln1_b.reshape(1, d), alpha)
    h2, h2_bf = _ffn(h1_bf, w_up_bf, w_down_bf, h1,
                     ln2_g.reshape(1, d), ln2_b.reshape(1, d), alpha)
    return _ple(h2_bf, w_ple_gate, p.astype(BF16), w_ple.astype(BF16), h2,
                ln3_g.reshape(1, d), ln3_b.reshape(1, d), alpha)


def kernel(x, p, w_in, conv_w, w_conv_out, g_kv, w_uv, w_o, ln1_g, ln1_b, w_up, w_down,
           ln2_g, ln2_b, w_ple_gate, w_ple, ln3_g, ln3_b):
    depth = w_in.shape[0]
    alpha = (2.0 * depth) ** 0.25
    batch = x.shape[0]
    outs = []
    for bi in range(batch):
        h = x[bi]
        for i in range(depth):
            h = _layer(h, p[i, bi], w_in[i], conv_w[i], w_conv_out[i], g_kv[i], w_uv[i], w_o[i],
                       ln1_g[i], ln1_b[i], w_up[i], w_down[i], ln2_g[i], ln2_b[i],
                       w_ple_gate[i], w_ple[i], ln3_g[i], ln3_b[i], alpha)
        outs.append(h)
    return jnp.stack(outs, axis=0)
```

```python
import functools

import jax
import jax.numpy as jnp
from jax import lax
from jax.experimental import pallas as pl
from jax.experimental.pallas import tpu as pltpu

F32 = jnp.float32
BF16 = jnp.bfloat16
I32 = jnp.int32

N_HEADS = 16
D_LATENT = 256
D_VHEAD = 128
N_IDX_HEADS = 16
D_IDX = 64
TOPK_MAX = 256
CONV_WIDTH = 3
LN_EPS = 1e-5
RMS_EPS = 1e-6
ATTN_SCALE = D_LATENT ** -0.5
IDX_SCALE = (D_IDX ** -0.5) * (N_IDX_HEADS ** -0.5)

VMEM_LIMIT_BYTES = 56 * 1024 * 1024
SUBLANES = 8
LANES = 128

INT_MIN = -(2 ** 31)
KEY_NEG_INF = INT_MIN + 0x7FFFFF
NEG_BIG = -1e30
TILE_UNROLL = 4
GUESS_MARGIN = 0.15
LOG2E = 1.4426950408889634


def _params(*sem):
    return pltpu.CompilerParams(dimension_semantics=sem, vmem_limit_bytes=VMEM_LIMIT_BYTES)


def _dot(a, b):
    return jnp.dot(a, b, preferred_element_type=F32)


def _dot_nt(a, b):
    return lax.dot_general(a, b, (((1,), (1,)), ((), ())), preferred_element_type=F32)


def _sigmoid(v):
    return 1.0 / (1.0 + jnp.exp(-v))


def _layer_norm(v, g, b):
    mu = jnp.mean(v, axis=-1, keepdims=True)
    c = v - mu
    var = jnp.mean(c * c, axis=-1, keepdims=True)
    return c * lax.rsqrt(var + LN_EPS) * g + b


def _split_hi_lo(v):
    hi = v.astype(BF16)
    lo = (v - hi.astype(F32)).astype(BF16)
    return hi, lo


def _side_cast_specs(w, grid):
    n_steps = grid[0] * grid[1]
    rows, cols = w.shape
    chunk = rows // n_steps
    assert chunk * n_steps == rows and chunk % (2 * SUBLANES) == 0
    spec = pl.BlockSpec((chunk, cols), lambda j, i: (j * grid[1] + i, 0))
    return spec, spec, jax.ShapeDtypeStruct(w.shape, BF16)


def _conv_kernel(x_ref, wb_ref, wc_ref, wu_ref, cw_ref, side_ref, z_ref, side_bf_ref, cu_ref, w_s):
    side_bf_ref[...] = side_ref[...].astype(side_bf_ref.dtype)
    i = pl.program_id(1)
    tm = x_ref.shape[0]
    halo = SUBLANES

    @pl.when(i == 0)
    def _():
        w_s[0] = wb_ref[...].astype(w_s.dtype)
        w_s[1] = wc_ref[...].astype(w_s.dtype)
        w_s[2] = wu_ref[...].astype(w_s.dtype)
        cu_ref[0:halo, :] = jnp.zeros((halo, cu_ref.shape[1]), F32)

    @pl.when(i > 0)
    def _():
        cu_ref[0:halo, :] = cu_ref[tm:tm + halo, :]

    x = x_ref[...]
    cu_ref[halo:halo + tm, :] = _dot_nt(x, w_s[1]) * _dot_nt(x, w_s[2])
    cw = cw_ref[...]
    v = (cw[0:1, :] * cu_ref[halo - 2:halo - 2 + tm, :]
         + cw[1:2, :] * cu_ref[halo - 1:halo - 1 + tm, :]
         + cw[2:3, :] * cu_ref[halo:halo + tm, :])
    z_ref[...] = (_dot_nt(x, w_s[0]) * v).astype(z_ref.dtype)


def _conv_branch(x_bf, w_in_t, conv_w, d_conv, w_side, tm=1024, tn=512):
    s, d = x_bf.shape
    nj = d_conv // tn
    grid = (nj, s // tm)
    side_in, side_out, side_shape = _side_cast_specs(w_side, grid)
    return pl.pallas_call(
        _conv_kernel,
        grid=grid,
        in_specs=[
            pl.BlockSpec((tm, d), lambda j, i: (i, 0)),
            pl.BlockSpec((tn, d), lambda j, i: (j, 0)),
            pl.BlockSpec((tn, d), lambda j, i: (nj + j, 0)),
            pl.BlockSpec((tn, d), lambda j, i: (2 * nj + j, 0)),
            pl.BlockSpec((CONV_WIDTH, tn), lambda j, i: (0, j)),
            side_in,
        ],
        out_specs=[pl.BlockSpec((tm, tn), lambda j, i: (i, j)), side_out],
        out_shape=[jax.ShapeDtypeStruct((s, d_conv), BF16), side_shape],
        scratch_shapes=[pltpu.VMEM((tm + SUBLANES, tn), F32),
                        pltpu.VMEM((3, tn, d), BF16)],
        compiler_params=_params("parallel", "arbitrary"),
        name="conv_branch",
    )(x_bf, w_in_t, w_in_t, w_in_t, conv_w, w_side)


def _qlat_kernel(x_ref, w_ref, side_ref, q_ref, side_bf_ref, w_s):
    side_bf_ref[...] = side_ref[...].astype(side_bf_ref.dtype)

    @pl.when(pl.program_id(1) == 0)
    def _():
        w_s[...] = w_ref[...].astype(w_s.dtype)

    acc = _dot_nt(w_s[...], x_ref[...]) * (ATTN_SCALE * LOG2E)
    q_ref[...] = acc.astype(q_ref.dtype).reshape(q_ref.shape)


def _qlat_proj(x_bf, w_in_t, row0, w_side, tm=1024, heads_per_step=4):
    s, d = x_bf.shape
    tn = heads_per_step * D_LATENT
    j0 = row0 // tn
    assert j0 * tn == row0
    grid = (N_HEADS // heads_per_step, s // tm)
    side_in, side_out, side_shape = _side_cast_specs(w_side, grid)
    return pl.pallas_call(
        _qlat_kernel,
        grid=grid,
        in_specs=[
            pl.BlockSpec((tm, d), lambda j, i: (i, 0)),
            pl.BlockSpec((tn, d), lambda j, i: (j0 + j, 0)),
            side_in,
        ],
        out_specs=[pl.BlockSpec((heads_per_step, D_LATENT, tm), lambda j, i: (j, 0, i)), side_out],
        out_shape=[jax.ShapeDtypeStruct((N_HEADS, D_LATENT, s), BF16), side_shape],
        scratch_shapes=[pltpu.VMEM((tn, d), BF16)],
        compiler_params=_params("parallel", "arbitrary"),
        name="qlat_proj",
    )(x_bf, w_in_t, w_side)


def _idx_kernel(x_ref, w_ref, g_ref, xbf_ref, ckv_ref, kidx_ref, qit_ref, wit_ref, w_s):
    @pl.when(pl.program_id(0) == 0)
    def _():
        w_s[...] = w_ref[...].astype(w_s.dtype)

    x = x_ref[...].astype(xbf_ref.dtype)
    xbf_ref[...] = x
    nq = N_IDX_HEADS * D_IDX
    r_q, r_k, r_w = D_LATENT, D_LATENT + nq, D_LATENT + nq + D_IDX
    c = _dot_nt(x, w_s[0:r_q])
    ms = jnp.mean(c * c, axis=-1, keepdims=True)
    ckv_ref[...] = (c * lax.rsqrt(ms + RMS_EPS) * g_ref[...]).astype(ckv_ref.dtype)
    k_hi, k_lo = _split_hi_lo(_dot_nt(x, w_s[r_k:r_w]))
    kidx_ref[...] = jnp.concatenate([k_hi, k_hi, k_lo], axis=1)
    q_t = _dot_nt(w_s[r_q:r_k], x)
    q_hi, q_lo = _split_hi_lo(q_t.reshape(N_IDX_HEADS, D_IDX, q_t.shape[1]))
    qit_ref[...] = jnp.concatenate([q_hi, q_lo, q_hi], axis=1)
    wit_ref[...] = _dot_nt(w_s[r_w:r_w + N_IDX_HEADS], x) * IDX_SCALE


def _idx_proj(x, w_in_t, row0, g_kv, tm=512):
    s, d = x.shape
    nq = N_IDX_HEADS * D_IDX
    n = D_LATENT + nq + D_IDX + N_IDX_HEADS
    n_pad = -(-n // LANES) * LANES
    return pl.pallas_call(
        _idx_kernel,
        grid=(s // tm,),
        in_specs=[
            pl.BlockSpec((tm, d), lambda i: (i, 0)),
            pl.BlockSpec((pl.Element(n_pad), pl.Element(d)), lambda i: (row0, 0)),
            pl.BlockSpec((1, D_LATENT), lambda i: (0, 0)),
        ],
        out_specs=[
            pl.BlockSpec((tm, d), lambda i: (i, 0)),
            pl.BlockSpec((tm, D_LATENT), lambda i: (i, 0)),
            pl.BlockSpec((tm, 3 * D_IDX), lambda i: (i, 0)),
            pl.BlockSpec((N_IDX_HEADS, 3 * D_IDX, tm), lambda i: (0, 0, i)),
            pl.BlockSpec((N_IDX_HEADS, tm), lambda i: (0, i)),
        ],
        out_shape=[
            jax.ShapeDtypeStruct((s, d), BF16),
            jax.ShapeDtypeStruct((s, D_LATENT), BF16),
            jax.ShapeDtypeStruct((s, 3 * D_IDX), BF16),
            jax.ShapeDtypeStruct((N_IDX_HEADS, 3 * D_IDX, s), BF16),
            jax.ShapeDtypeStruct((N_IDX_HEADS, s), F32),
        ],
        scratch_shapes=[pltpu.VMEM((n_pad, d), BF16)],
        compiler_params=_params("arbitrary"),
        name="idx_proj",
    )(x, w_in_t, g_kv)


def _flip_key(v):
    return v ^ ((v >> 31) & 0x7FFFFFFF)


def _key_to_f32(key):
    return pltpu.bitcast(_flip_key(key), F32)


def _unrolled_loop(n, body, unroll):
    start = 0
    while unroll >= 1:
        def trip(i, carry, start=start, unroll=unroll):
            for r in range(unroll):
                body(start + unroll * i + r)
            return carry

        trips = (n - start) // unroll
        lax.fori_loop(0, trips, trip, 0)
        start = start + unroll * trips
        unroll //= 2


def _attn_kernel(qit_ref, wit_ref, kidx_ref, qt_ref, kv_ref, kvt_ref, wuvt_ref, yt_ref,
                 score_ref, qi_s, q_s, w_s, m_ref, l_ref, acc_ref, s_buf, p_buf, a_buf, *, topk):
    b = pl.program_id(0)
    n_heads, d_lat, tq = qt_ref.shape
    n_kt_all, tk, _ = kidx_ref.shape
    n_groups, _, gw = q_s.shape
    hg = gw // tq
    n_kt = ((b + 1) * tq + tk - 1) // tk
    kpos = lax.broadcasted_iota(I32, (tk, tq), 0)
    qpos = b * tq + lax.broadcasted_iota(I32, (tk, tq), 1)

    for h in range(n_heads):
        g, c = divmod(h, hg)
        qi_s[g, :, c * tq:(c + 1) * tq] = qit_ref[h]
        q_s[g, :, c * tq:(c + 1) * tq] = qt_ref[h]
        w_s[g, :, c * tq:(c + 1) * tq] = wit_ref[h:h + 1, :]

    def score_tile(j):
        kidx = kidx_ref[j]
        score = jnp.zeros((tk, tq), F32)
        for g in range(n_groups):
            rel = jnp.maximum(_dot(kidx, qi_s[g]), 0.0) * w_s[g]
            for c in range(hg):
                score = score + rel[:, c * tq:(c + 1) * tq]
        score_ref[j] = jnp.where(kpos + j * tk <= qpos, score, -jnp.inf)

    _unrolled_loop(n_kt, score_tile, TILE_UNROLL)

    fold = 8 * SUBLANES
    kf = float(topk)

    n_causal = (qpos[0:1, :] + 1).astype(F32)

    def moments(j, carry):
        top, bot, s1, s2 = carry
        slabs = score_ref[j].reshape(tk // fold, fold, tq)
        for r in range(tk // fold):
            causal = slabs[r] > -jnp.inf
            val = jnp.where(causal, slabs[r], 0.0)
            top = jnp.maximum(top, slabs[r])
            bot = jnp.minimum(bot, jnp.where(causal, slabs[r], jnp.inf))
            s1 = s1 + val
            s2 = s2 + val * val
        return top, bot, s1, s2

    zero = jnp.zeros((fold, tq), F32)
    top, bot, s1, s2 = lax.fori_loop(0, n_kt, moments, (zero - jnp.inf, zero + jnp.inf, zero, zero))
    rep = lambda v: jnp.broadcast_to(v, (SUBLANES, tq))
    short = n_causal < kf
    lo0 = rep(jnp.where(short, KEY_NEG_INF,
                        _flip_key(pltpu.bitcast(jnp.min(bot, axis=0, keepdims=True), I32))))
    hi0 = rep(_flip_key(pltpu.bitcast(jnp.max(top, axis=0, keepdims=True), I32)) + 1)
    c_lo0 = rep(jnp.where(short, 0.0, n_causal))
    mean = jnp.sum(s1, axis=0, keepdims=True) / n_causal
    spread = jnp.sqrt(jnp.maximum(jnp.sum(s2, axis=0, keepdims=True) / n_causal - mean * mean, 0.0))
    tail = jnp.clip(kf / n_causal, 1e-6, 1.0 - 1e-6)
    t = jnp.sqrt(-2.0 * jnp.log(jnp.minimum(tail, 1.0 - tail)))
    z = t - ((2.515517 + t * (0.802853 + t * 0.010328))
             / (1.0 + t * (1.432788 + t * (0.189269 + t * 0.001308))))
    z = jnp.where(tail < 0.5, z, -z)
    guess = [rep(_flip_key(pltpu.bitcast(mean + (z + dz) * spread, I32)))
             for dz in (-GUESS_MARGIN, GUESS_MARGIN)]

    def unsettled(lo, hi, c_lo):
        return jnp.max(jnp.where((c_lo > kf) & (hi - 1 > lo), 1.0, 0.0))

    def search(tiles):
        def run():
            def count_ge(cand):
                cnt = jnp.zeros((fold, tq), F32)
                for j in range(tiles):
                    hit = jnp.where(score_ref[j] >= cand, 1.0, 0.0)
                    cnt = cnt + jnp.sum(hit.reshape(tk // fold, fold, tq), axis=0)
                return jnp.sum(cnt, axis=0, keepdims=True)

            def probe(lo, hi, c_lo, pk):
                live = (c_lo > kf) & (hi - 1 > lo)
                pk = jnp.minimum(jnp.maximum(pk, lo + 1), hi - 1)
                pk = jnp.where(live, pk, lo)
                c = rep(count_ge(_key_to_f32(jnp.maximum(pk, KEY_NEG_INF))[0:1]))
                up = live & (c >= kf)
                down = live & (c < kf)
                return jnp.where(up, pk, lo), jnp.where(down, pk, hi), jnp.where(up, c, c_lo)

            def midpoint(lo, hi, by_key):
                key_mid = (lo >> 1) + (hi >> 1) + (lo & hi & 1)
                val_mid = 0.5 * (_key_to_f32(lo) + _key_to_f32(hi - 1))
                return jnp.where(by_key, key_mid, _flip_key(pltpu.bitcast(val_mid, I32)))

            def probes(state):
                lo, hi, c_lo, _, trip = state
                first = trip == 0
                lo, hi, c_lo = probe(lo, hi, c_lo,
                                     jnp.where(first, guess[0], midpoint(lo, hi, False)))
                lo, hi, c_lo = probe(lo, hi, c_lo,
                                     jnp.where(first, guess[1], midpoint(lo, hi, trip % 2 == 1)))
                return lo, hi, c_lo, unsettled(lo, hi, c_lo), trip + 1

            def keep_going(state):
                return jnp.logical_and(state[3] > 0.0, state[4] < 64)

            state = (lo0, hi0, c_lo0, unsettled(lo0, hi0, c_lo0), jnp.int32(0))
            lo, _, c_lo, _, _ = lax.while_loop(keep_going, probes, state)
            return lo[0:1], c_lo[0:1]
        return run

    thr_key, n_ge = lax.switch(n_kt - 1, [search(k) for k in range(1, n_kt_all + 1)])
    thr = _key_to_f32(jnp.maximum(thr_key, KEY_NEG_INF))

    @pl.when(jnp.max(jnp.broadcast_to(n_ge, (SUBLANES, tq))) > kf)
    def _():
        def count(pred):
            def body(j, cnt):
                hit = jnp.where(pred(kpos + j * tk, score_ref[j]), 1.0, 0.0)
                return cnt + jnp.sum(hit.reshape(tk // fold, fold, tq), axis=0)

            cnt = lax.fori_loop(0, n_kt, body, jnp.zeros((fold, tq), F32))
            return jnp.sum(cnt, axis=0, keepdims=True)

        need = kf - count(lambda pos, sc: sc > thr)
        n_pos_bits = (n_kt_all * tk - 1).bit_length()

        def pos_step(t, cut):
            cand = cut | jnp.left_shift(jnp.int32(1), n_pos_bits - 1 - t)
            before = count(lambda pos, sc: (sc == thr) & (pos < cand))
            return jnp.where(before < need, cand, cut)

        cut = lax.fori_loop(0, n_pos_bits, pos_step, jnp.zeros((1, tq), I32))

        def drop_tile(j, carry):
            sc = score_ref[j]
            score_ref[j] = jnp.where((sc == thr) & (kpos + j * tk > cut), -jnp.inf, sc)
            return carry

        lax.fori_loop(0, n_kt, drop_tile, 0)

    def mask_tile(j, carry):
        keep = (score_ref[j] >= thr) & (kpos + j * tk <= qpos)
        score_ref[j] = jnp.where(keep, 0.0, NEG_BIG)
        return carry

    lax.fori_loop(0, n_kt, mask_tile, 0)

    m_ref[...] = jnp.full(m_ref.shape, NEG_BIG, F32)
    l_ref[...] = jnp.zeros(l_ref.shape, F32)
    acc_ref[...] = jnp.zeros(acc_ref.shape, F32)
    p_buf[...] = jnp.zeros(p_buf.shape, p_buf.dtype)
    a_buf[...] = jnp.ones(a_buf.shape, F32)

    def qk(j, g):
        bias = jnp.concatenate([score_ref[j]] * hg, axis=1)
        s_buf[g % 2] = _dot(kv_ref[j], q_s[g]) + bias

    def softmax(g):
        s = s_buf[g % 2]
        m_old = m_ref[g]
        m_new = jnp.maximum(m_old, jnp.max(s, axis=0, keepdims=True))
        p = jnp.exp2(s - m_new)
        alpha = jnp.exp2(m_old - m_new)
        l_ref[g] = alpha * l_ref[g] + jnp.sum(p, axis=0, keepdims=True)
        a_buf[g % 2] = alpha
        p_buf[g % 2] = p.astype(p_buf.dtype)
        m_ref[g] = m_new

    def pv(j, g):
        acc_ref[g] = a_buf[g % 2] * acc_ref[g] + _dot(kvt_ref[j], p_buf[g % 2])

    last = n_groups - 1
    qk(0, 0)

    def attn_tile(j):
        for g in range(n_groups):
            if g < last:
                qk(j, g + 1)
            else:
                qk(jnp.minimum(j + 1, n_kt - 1), 0)
            if g > 0:
                pv(j, g - 1)
            else:
                pv(jnp.maximum(j - 1, 0), last)
            softmax(g)

    _unrolled_loop(n_kt, attn_tile, TILE_UNROLL)
    pv(n_kt - 1, last)

    for h in range(n_heads):
        g, c = divmod(h, hg)
        lanes = slice(c * tq, (c + 1) * tq)
        o_t = (acc_ref[g, :, lanes] / l_ref[g, :, lanes]).astype(wuvt_ref.dtype)
        yt_ref[h * D_VHEAD:(h + 1) * D_VHEAD, :] = _dot(wuvt_ref[h], o_t)


def _dsa_attention(qi_t, wi_t, kidx, q_t, kv, kvt, w_uv_t, topk, tq=128, heads_per_group=2):
    n_heads, d_lat, s = q_t.shape
    n_kt, tk, dk = kidx.shape
    assert tk % tq == 0
    n_groups = n_heads // heads_per_group
    gw = heads_per_group * tq
    const3 = lambda b: (0, 0, 0)
    return pl.pallas_call(
        functools.partial(_attn_kernel, topk=topk),
        grid=(s // tq,),
        in_specs=[
            pl.BlockSpec((N_IDX_HEADS, dk, tq), lambda b: (0, 0, b)),
            pl.BlockSpec((N_IDX_HEADS, tq), lambda b: (0, b)),
            pl.BlockSpec((n_kt, tk, dk), const3),
            pl.BlockSpec((n_heads, d_lat, tq), lambda b: (0, 0, b)),
            pl.BlockSpec((n_kt, tk, d_lat), const3),
            pl.BlockSpec((n_kt, d_lat, tk), const3),
            pl.BlockSpec((n_heads, D_VHEAD, d_lat), const3),
        ],
        out_specs=pl.BlockSpec((n_heads * D_VHEAD, tq), lambda b: (0, b)),
        out_shape=jax.ShapeDtypeStruct((n_heads * D_VHEAD, s), F32),
        scratch_shapes=[
            pltpu.VMEM((n_kt, tk, tq), F32),
            pltpu.VMEM((n_groups, dk, gw), BF16),
            pltpu.VMEM((n_groups, d_lat, gw), BF16),
            pltpu.VMEM((n_groups, 1, gw), F32),
            pltpu.VMEM((n_groups, 1, gw), F32),
            pltpu.VMEM((n_groups, 1, gw), F32),
            pltpu.VMEM((n_groups, d_lat, gw), F32),
            pltpu.VMEM((2, tk, gw), F32),
            pltpu.VMEM((2, tk, gw), BF16),
            pltpu.VMEM((2, 1, gw), F32),
        ],
        compiler_params=_params("parallel"),
        name="dsa_attention",
    )(qi_t, wi_t, kidx, q_t, kv, kvt, w_uv_t)


def _merge_kernel(x_ref, z_ref, wgc_ref, wga_ref, wco_ref, yat_ref, o_ref, w_s, wco_s):
    @pl.when(pl.program_id(1) == 0)
    def _():
        w_s[0] = wgc_ref[...].astype(w_s.dtype)
        w_s[1] = wga_ref[...].astype(w_s.dtype)
        wco_s[...] = wco_ref[...].astype(wco_s.dtype)

    x = x_ref[...]
    g_conv = _sigmoid(_dot_nt(x, w_s[0]))
    g_attn = _sigmoid(_dot_nt(x, w_s[1]))
    y_conv = _dot(z_ref[...], wco_s[...])
    o_ref[...] = (g_conv * y_conv + g_attn * yat_ref[...].T).astype(o_ref.dtype)


def _merge(x_bf, z, w_in_t, row0, w_co, y_attn_t, tm=512, tn=512):
    s, d = z.shape
    n = w_co.shape[1]
    gate_rows = lambda off: pl.BlockSpec(
        (pl.Element(tn), pl.Element(d)),
        lambda j, i: (pl.multiple_of(row0 + off + j * tn, SUBLANES), 0))
    return pl.pallas_call(
        _merge_kernel,
        grid=(n // tn, s // tm),
        in_specs=[
            pl.BlockSpec((tm, d), lambda j, i: (i, 0)),
            pl.BlockSpec((tm, d), lambda j, i: (i, 0)),
            gate_rows(0),
            gate_rows(n),
            pl.BlockSpec((d, tn), lambda j, i: (0, j)),
            pl.BlockSpec((tn, tm), lambda j, i: (j, i)),
        ],
        out_specs=pl.BlockSpec((tm, tn), lambda j, i: (i, j)),
        out_shape=jax.ShapeDtypeStruct((s, n), BF16),
        scratch_shapes=[pltpu.VMEM((2, tn, d), BF16), pltpu.VMEM((d, tn), BF16)],
        compiler_params=_params("parallel", "arbitrary"),
        name="merge",
    )(x_bf, z, w_in_t, w_in_t, w_co, y_attn_t)


def _cast_resident(w_ref, w_s):
    @pl.when(pl.program_id(0) == 0)
    def _():
        w_s[...] = w_ref[...].astype(w_s.dtype)


def _resident(shape):
    return pl.BlockSpec(shape, lambda i: (0,) * len(shape), pipeline_mode=pl.Buffered(1))


def _oproj_kernel(m_ref, w_ref, h_ref, g_ref, b_ref, o_ref, obf_ref, w_s, *, alpha):
    _cast_resident(w_ref, w_s)
    v = alpha * h_ref[...] + _dot(m_ref[...], w_s[...])
    out = _layer_norm(v, g_ref[...], b_ref[...])
    o_ref[...] = out
    obf_ref[...] = out.astype(obf_ref.dtype)


def _oproj(mixed, w_o, h, g, b, alpha, tm=512):
    s, d = h.shape
    row = pl.BlockSpec((tm, d), lambda i: (i, 0))
    vec = pl.BlockSpec((1, d), lambda i: (0, 0))
    return pl.pallas_call(
        functools.partial(_oproj_kernel, alpha=alpha),
        grid=(s // tm,),
        in_specs=[row, _resident((d, d)), row, vec, vec],
        out_specs=[row, row],
        out_shape=[jax.ShapeDtypeStruct((s, d), F32), jax.ShapeDtypeStruct((s, d), BF16)],
        scratch_shapes=[pltpu.VMEM((d, d), BF16)],
        compiler_params=_params("arbitrary"),
        name="oproj_ln",
    )(mixed, w_o, h, g, b)


def _ffn_kernel(hbf_ref, wu_ref, wd_ref, h_ref, g_ref, b_ref, o_ref, obf_ref, acc_ref, *, alpha):
    f = pl.program_id(1)

    @pl.when(f == 0)
    def _():
        acc_ref[...] = jnp.zeros(acc_ref.shape, F32)

    a = jnp.maximum(_dot(hbf_ref[...], wu_ref[...]), 0.0)
    acc_ref[...] += _dot((a * a).astype(wd_ref.dtype), wd_ref[...])

    @pl.when(f == pl.num_programs(1) - 1)
    def _():
        out = _layer_norm(alpha * h_ref[...] + acc_ref[...], g_ref[...], b_ref[...])
        o_ref[...] = out
        obf_ref[...] = out.astype(obf_ref.dtype)


def _ffn(h_bf, w_up, w_down, h, g, b, alpha, tm=512, tf=1024):
    s, d = h.shape
    dff = w_up.shape[1]
    row = pl.BlockSpec((tm, d), lambda i, f: (i, 0))
    vec = pl.BlockSpec((1, d), lambda i, f: (0, 0))
    return pl.pallas_call(
        functools.partial(_ffn_kernel, alpha=alpha),
        grid=(s // tm, dff // tf),
        in_specs=[row, pl.BlockSpec((d, tf), lambda i, f: (0, f)),
                  pl.BlockSpec((tf, d), lambda i, f: (f, 0)), row, vec, vec],
        out_specs=[row, row],
        out_shape=[jax.ShapeDtypeStruct((s, d), F32), jax.ShapeDtypeStruct((s, d), BF16)],
        scratch_shapes=[pltpu.VMEM((tm, d), F32)],
        compiler_params=_params("parallel", "arbitrary"),
        name="ffn_ln",
    )(h_bf, w_up, w_down, h, g, b)


def _ple_kernel(hbf_ref, wg_ref, p_ref, wp_ref, h_ref, g_ref, b_ref, o_ref, wg_s, *, alpha):
    _cast_resident(wg_ref, wg_s)
    gate = _sigmoid(_dot(hbf_ref[...], wg_s[...]))
    pe = gate * _dot(p_ref[...], wp_ref[...])
    o_ref[...] = _layer_norm(alpha * h_ref[...] + pe, g_ref[...], b_ref[...])


def _ple(h_bf, w_pg, p_bf, w_p, h, g, b, alpha, tm=512):
    s, d = h.shape
    dp = p_bf.shape[1]
    row = pl.BlockSpec((tm, d), lambda i: (i, 0))
    vec = pl.BlockSpec((1, d), lambda i: (0, 0))
    return pl.pallas_call(
        functools.partial(_ple_kernel, alpha=alpha),
        grid=(s // tm,),
        in_specs=[row, _resident((d, d)),
                  pl.BlockSpec((tm, dp), lambda i: (i, 0)),
                  pl.BlockSpec((dp, d), lambda i: (0, 0)), row, vec, vec],
        out_specs=row,
        out_shape=jax.ShapeDtypeStruct((s, d), F32),
        scratch_shapes=[pltpu.VMEM((d, d), BF16)],
        compiler_params=_params("arbitrary"),
        name="ple_ln",
    )(h_bf, w_pg, p_bf, w_p, h, g, b)


def _layer(h, p, w_in, conv_w, w_conv_out, g_kv, w_uv, w_o, ln1_g, ln1_b, w_up, w_down,
           ln2_g, ln2_b, w_ple_gate, w_ple, ln3_g, ln3_b, alpha, tk=512):
    s, d = h.shape
    d_conv = conv_w.shape[1]
    nq = N_IDX_HEADS * D_IDX
    c_qlat = 3 * d_conv
    c_ckv = c_qlat + N_HEADS * D_LATENT
    c_gates = c_ckv + D_LATENT + nq + D_IDX + N_IDX_HEADS
    topk = min(TOPK_MAX, s // 4)
    n_kt = s // tk

    w_in_t = jnp.transpose(w_in)

    h_bf, ckv, kidx, qi_t, wi_t = _idx_proj(h, w_in_t, c_ckv, g_kv.reshape(1, D_LATENT))
    z, w_down_bf = _conv_branch(h_bf, w_in_t, conv_w, d_conv, w_down)
    q_t, w_up_bf = _qlat_proj(h_bf, w_in_t, c_qlat, w_up)

    kidx = kidx.reshape(n_kt, tk, 3 * D_IDX)
    kv = ckv.reshape(n_kt, tk, D_LATENT)
    kvt = jnp.transpose(kv, (0, 2, 1))

    w_uv_t = jnp.transpose(w_uv, (0, 2, 1)).astype(BF16)
    y_attn_t = _dsa_attention(qi_t, wi_t, kidx, q_t, kv, kvt, w_uv_t, topk)

    mixed = _merge(h_bf, z, w_in_t, c_gates, w_conv_out, y_attn_t)
    h1, h1_bf = _oproj(mixed, w_o, h, ln1_g.reshape(1, d), ln1_b.reshape(1, d), alpha)
    h2, h2_bf = _ffn(h1_bf, w_up_bf, w_down_bf, h1,
                     ln2_g.reshape(1, d), ln2_b.reshape(1, d), alpha)
    return _ple(h2_bf, w_ple_gate, p.astype(BF16), w_ple.astype(BF16), h2,
                ln3_g.reshape(1, d), ln3_b.reshape(1, d), alpha)


def kernel(x, p, w_in, conv_w, w_conv_out, g_kv, w_uv, w_o, ln1_g, ln1_b, w_up, w_down,
           ln2_g, ln2_b, w_ple_gate, w_ple, ln3_g, ln3_b):
    depth = w_in.shape[0]
    alpha = (2.0 * depth) ** 0.25
    batch = x.shape[0]
    outs = []
    for bi in range(batch):
        h = x[bi]
        for i in range(depth):
            h = _layer(h, p[i, bi], w_in[i], conv_w[i], w_conv_out[i], g_kv[i], w_uv[i], w_o[i],
                       ln1_g[i], ln1_b[i], w_up[i], w_down[i], ln2_g[i], ln2_b[i],
                       w_ple_gate[i], w_ple[i], ln3_g[i], ln3_b[i], alpha)
        outs.append(h)
    return jnp.stack(outs, axis=0)
```

```python
import functools

import jax
import jax.numpy as jnp
from jax import lax
from jax.experimental import pallas as pl
from jax.experimental.pallas import tpu as pltpu

F32 = jnp.float32
BF16 = jnp.bfloat16
I32 = jnp.int32

N_HEADS = 16
D_LATENT = 256
D_VHEAD = 128
N_IDX_HEADS = 16
D_IDX = 64
TOPK_MAX = 256
CONV_WIDTH = 3
LN_EPS = 1e-5
RMS_EPS = 1e-6
ATTN_SCALE = D_LATENT ** -0.5
IDX_SCALE = (D_IDX ** -0.5) * (N_IDX_HEADS ** -0.5)

VMEM_LIMIT_BYTES = 56 * 1024 * 1024
SUBLANES = 8
LANES = 128

INT_MIN = -(2 ** 31)
KEY_NEG_INF = INT_MIN + 0x7FFFFF
NEG_BIG = -1e30
TILE_UNROLL = 4
UNCHECKED_TRIPS = 8
LOG2E = 1.4426950408889634


def _params(*sem):
    return pltpu.CompilerParams(dimension_semantics=sem, vmem_limit_bytes=VMEM_LIMIT_BYTES)


def _dot(a, b):
    return jnp.dot(a, b, preferred_element_type=F32)


def _dot_nt(a, b):
    return lax.dot_general(a, b, (((1,), (1,)), ((), ())), preferred_element_type=F32)


def _sigmoid(v):
    return 1.0 / (1.0 + jnp.exp(-v))


def _layer_norm(v, g, b):
    mu = jnp.mean(v, axis=-1, keepdims=True)
    c = v - mu
    var = jnp.mean(c * c, axis=-1, keepdims=True)
    return c * lax.rsqrt(var + LN_EPS) * g + b


def _split_hi_lo(v):
    hi = v.astype(BF16)
    lo = (v - hi.astype(F32)).astype(BF16)
    return hi, lo


def _side_cast_specs(w, grid):
    n_steps = grid[0] * grid[1]
    rows, cols = w.shape
    chunk = rows // n_steps
    assert chunk * n_steps == rows and chunk % (2 * SUBLANES) == 0
    spec = pl.BlockSpec((chunk, cols), lambda j, i: (j * grid[1] + i, 0))
    return spec, spec, jax.ShapeDtypeStruct(w.shape, BF16)


def _conv_kernel(x_ref, wb_ref, wc_ref, wu_ref, cw_ref, side_ref, z_ref, side_bf_ref, cu_ref, w_s):
    side_bf_ref[...] = side_ref[...].astype(side_bf_ref.dtype)
    i = pl.program_id(1)
    tm = x_ref.shape[0]
    halo = SUBLANES

    @pl.when(i == 0)
    def _():
        w_s[0] = wb_ref[...].astype(w_s.dtype)
        w_s[1] = wc_ref[...].astype(w_s.dtype)
        w_s[2] = wu_ref[...].astype(w_s.dtype)
        cu_ref[0:halo, :] = jnp.zeros((halo, cu_ref.shape[1]), F32)

    @pl.when(i > 0)
    def _():
        cu_ref[0:halo, :] = cu_ref[tm:tm + halo, :]

    x = x_ref[...]
    cu_ref[halo:halo + tm, :] = _dot_nt(x, w_s[1]) * _dot_nt(x, w_s[2])
    cw = cw_ref[...]
    v = (cw[0:1, :] * cu_ref[halo - 2:halo - 2 + tm, :]
         + cw[1:2, :] * cu_ref[halo - 1:halo - 1 + tm, :]
         + cw[2:3, :] * cu_ref[halo:halo + tm, :])
    z_ref[...] = (_dot_nt(x, w_s[0]) * v).astype(z_ref.dtype)


def _conv_branch(x_bf, w_in_t, conv_w, d_conv, w_side, tm=1024, tn=512):
    s, d = x_bf.shape
    nj = d_conv // tn
    grid = (nj, s // tm)
    side_in, side_out, side_shape = _side_cast_specs(w_side, grid)
    return pl.pallas_call(
        _conv_kernel,
        grid=grid,
        in_specs=[
            pl.BlockSpec((tm, d), lambda j, i: (i, 0)),
            pl.BlockSpec((tn, d), lambda j, i: (j, 0)),
            pl.BlockSpec((tn, d), lambda j, i: (nj + j, 0)),
            pl.BlockSpec((tn, d), lambda j, i: (2 * nj + j, 0)),
            pl.BlockSpec((CONV_WIDTH, tn), lambda j, i: (0, j)),
            side_in,
        ],
        out_specs=[pl.BlockSpec((tm, tn), lambda j, i: (i, j)), side_out],
        out_shape=[jax.ShapeDtypeStruct((s, d_conv), BF16), side_shape],
        scratch_shapes=[pltpu.VMEM((tm + SUBLANES, tn), F32),
                        pltpu.VMEM((3, tn, d), BF16)],
        compiler_params=_params("parallel", "arbitrary"),
        name="conv_branch",
    )(x_bf, w_in_t, w_in_t, w_in_t, conv_w, w_side)


def _qlat_kernel(x_ref, w_ref, side_ref, q_ref, side_bf_ref, w_s):
    side_bf_ref[...] = side_ref[...].astype(side_bf_ref.dtype)

    @pl.when(pl.program_id(1) == 0)
    def _():
        w_s[...] = w_ref[...].astype(w_s.dtype)

    acc = _dot_nt(w_s[...], x_ref[...]) * (ATTN_SCALE * LOG2E)
    q_ref[...] = acc.astype(q_ref.dtype).reshape(q_ref.shape)


def _qlat_proj(x_bf, w_in_t, row0, w_side, tm=1024, heads_per_step=4):
    s, d = x_bf.shape
    tn = heads_per_step * D_LATENT
    j0 = row0 // tn
    assert j0 * tn == row0
    grid = (N_HEADS // heads_per_step, s // tm)
    side_in, side_out, side_shape = _side_cast_specs(w_side, grid)
    return pl.pallas_call(
        _qlat_kernel,
        grid=grid,
        in_specs=[
            pl.BlockSpec((tm, d), lambda j, i: (i, 0)),
            pl.BlockSpec((tn, d), lambda j, i: (j0 + j, 0)),
            side_in,
        ],
        out_specs=[pl.BlockSpec((heads_per_step, D_LATENT, tm), lambda j, i: (j, 0, i)), side_out],
        out_shape=[jax.ShapeDtypeStruct((N_HEADS, D_LATENT, s), BF16), side_shape],
        scratch_shapes=[pltpu.VMEM((tn, d), BF16)],
        compiler_params=_params("parallel", "arbitrary"),
        name="qlat_proj",
    )(x_bf, w_in_t, w_side)


def _idx_kernel(x_ref, w_ref, g_ref, xbf_ref, ckv_ref, kidx_ref, qit_ref, wit_ref, w_s):
    @pl.when(pl.program_id(0) == 0)
    def _():
        w_s[...] = w_ref[...].astype(w_s.dtype)

    x = x_ref[...].astype(xbf_ref.dtype)
    xbf_ref[...] = x
    nq = N_IDX_HEADS * D_IDX
    r_q, r_k, r_w = D_LATENT, D_LATENT + nq, D_LATENT + nq + D_IDX
    c = _dot_nt(x, w_s[0:r_q])
    ms = jnp.mean(c * c, axis=-1, keepdims=True)
    ckv_ref[...] = (c * lax.rsqrt(ms + RMS_EPS) * g_ref[...]).astype(ckv_ref.dtype)
    k_hi, k_lo = _split_hi_lo(_dot_nt(x, w_s[r_k:r_w]))
    kidx_ref[...] = jnp.concatenate([k_hi, k_hi, k_lo], axis=1)
    q_t = _dot_nt(w_s[r_q:r_k], x)
    q_hi, q_lo = _split_hi_lo(q_t.reshape(N_IDX_HEADS, D_IDX, q_t.shape[1]))
    qit_ref[...] = jnp.concatenate([q_hi, q_lo, q_hi], axis=1)
    wit_ref[...] = _dot_nt(w_s[r_w:r_w + N_IDX_HEADS], x) * IDX_SCALE


def _idx_proj(x, w_in_t, row0, g_kv, tm=512):
    s, d = x.shape
    nq = N_IDX_HEADS * D_IDX
    n = D_LATENT + nq + D_IDX + N_IDX_HEADS
    n_pad = -(-n // LANES) * LANES
    return pl.pallas_call(
        _idx_kernel,
        grid=(s // tm,),
        in_specs=[
            pl.BlockSpec((tm, d), lambda i: (i, 0)),
            pl.BlockSpec((pl.Element(n_pad), pl.Element(d)), lambda i: (row0, 0)),
            pl.BlockSpec((1, D_LATENT), lambda i: (0, 0)),
        ],
        out_specs=[
            pl.BlockSpec((tm, d), lambda i: (i, 0)),
            pl.BlockSpec((tm, D_LATENT), lambda i: (i, 0)),
            pl.BlockSpec((tm, 3 * D_IDX), lambda i: (i, 0)),
            pl.BlockSpec((N_IDX_HEADS, 3 * D_IDX, tm), lambda i: (0, 0, i)),
            pl.BlockSpec((N_IDX_HEADS, tm), lambda i: (0, i)),
        ],
        out_shape=[
            jax.ShapeDtypeStruct((s, d), BF16),
            jax.ShapeDtypeStruct((s, D_LATENT), BF16),
            jax.ShapeDtypeStruct((s, 3 * D_IDX), BF16),
            jax.ShapeDtypeStruct((N_IDX_HEADS, 3 * D_IDX, s), BF16),
            jax.ShapeDtypeStruct((N_IDX_HEADS, s), F32),
        ],
        scratch_shapes=[pltpu.VMEM((n_pad, d), BF16)],
        compiler_params=_params("arbitrary"),
        name="idx_proj",
    )(x, w_in_t, g_kv)


def _flip_key(v):
    return v ^ ((v >> 31) & 0x7FFFFFFF)


def _key_to_f32(key):
    return pltpu.bitcast(_flip_key(key), F32)


def _unrolled_loop(n, body, unroll):
    start = 0
    while unroll >= 1:
        def trip(i, carry, start=start, unroll=unroll):
            for r in range(unroll):
                body(start + unroll * i + r)
            return carry

        trips = (n - start) // unroll
        lax.fori_loop(0, trips, trip, 0)
        start = start + unroll * trips
        unroll //= 2


def _attn_kernel(qit_ref, wit_ref, kidx_ref, qt_ref, kv_ref, kvt_ref, wuvt_ref, yt_ref,
                 score_ref, qi_s, q_s, w_s, m_ref, l_ref, acc_ref, s_buf, p_buf, a_buf, *, topk):
    b = pl.program_id(0)
    n_heads, d_lat, tq = qt_ref.shape
    n_kt_all, tk, _ = kidx_ref.shape
    n_groups, _, gw = q_s.shape
    hg = gw // tq
    n_kt = ((b + 1) * tq + tk - 1) // tk
    kpos = lax.broadcasted_iota(I32, (tk, tq), 0)
    qpos = b * tq + lax.broadcasted_iota(I32, (tk, tq), 1)

    for h in range(n_heads):
        g, c = divmod(h, hg)
        qi_s[g, :, c * tq:(c + 1) * tq] = qit_ref[h]
        q_s[g, :, c * tq:(c + 1) * tq] = qt_ref[h]
        w_s[g, :, c * tq:(c + 1) * tq] = wit_ref[h:h + 1, :]

    def score_tile(j):
        kidx = kidx_ref[j]
        score = jnp.zeros((tk, tq), F32)
        for g in range(n_groups):
            rel = jnp.maximum(_dot(kidx, qi_s[g]), 0.0) * w_s[g]
            for c in range(hg):
                score = score + rel[:, c * tq:(c + 1) * tq]
        score_ref[j] = jnp.where(kpos + j * tk <= qpos, score, -jnp.inf)

    _unrolled_loop(n_kt, score_tile, TILE_UNROLL)

    fold = 8 * SUBLANES
    kf = float(topk)

    n_causal = (qpos[0:1, :] + 1).astype(F32)

    def extremes(j, carry):
        top, bot = carry
        slabs = score_ref[j].reshape(tk // fold, fold, tq)
        for r in range(tk // fold):
            top = jnp.maximum(top, slabs[r])
            bot = jnp.minimum(bot, jnp.where(slabs[r] > -jnp.inf, slabs[r], jnp.inf))
        return top, bot

    top, bot = lax.fori_loop(0, n_kt, extremes, (jnp.full((fold, tq), -jnp.inf, F32),
                                                 jnp.full((fold, tq), jnp.inf, F32)))
    rep = lambda v: jnp.broadcast_to(v, (SUBLANES, tq))
    short = n_causal < kf
    lo0 = rep(jnp.where(short, KEY_NEG_INF,
                        _flip_key(pltpu.bitcast(jnp.min(bot, axis=0, keepdims=True), I32))))
    hi0 = rep(_flip_key(pltpu.bitcast(jnp.max(top, axis=0, keepdims=True), I32)) + 1)
    c_lo0 = rep(jnp.where(short, 0.0, n_causal))

    def unsettled(lo, hi, c_lo):
        return jnp.max(jnp.where((c_lo > kf) & (hi - 1 > lo), 1.0, 0.0))

    def search(tiles):
        def run():
            def count_ge(cand):
                cnt = jnp.zeros((fold, tq), F32)
                for j in range(tiles):
                    hit = jnp.where(score_ref[j] >= cand, 1.0, 0.0)
                    cnt = cnt + jnp.sum(hit.reshape(tk // fold, fold, tq), axis=0)
                return jnp.sum(cnt, axis=0, keepdims=True)

            def probe(lo, hi, c_lo, by_key):
                live = (c_lo > kf) & (hi - 1 > lo)
                key_mid = (lo >> 1) + (hi >> 1) + (lo & hi & 1)
                val_mid = 0.5 * (_key_to_f32(lo) + _key_to_f32(hi - 1))
                pk = jnp.where(by_key, key_mid, _flip_key(pltpu.bitcast(val_mid, I32)))
                pk = jnp.minimum(jnp.maximum(pk, lo + 1), hi - 1)
                pk = jnp.where(live, pk, lo)
                c = rep(count_ge(_key_to_f32(jnp.maximum(pk, KEY_NEG_INF))[0:1]))
                up = live & (c >= kf)
                down = live & (c < kf)
                return jnp.where(up, pk, lo), jnp.where(down, pk, hi), jnp.where(up, c, c_lo)

            def probes(state):
                lo, hi, c_lo, _, trip = state
                lo, hi, c_lo = probe(lo, hi, c_lo, False)
                lo, hi, c_lo = probe(lo, hi, c_lo, trip % 2 == 1)
                flag = lax.cond(trip >= UNCHECKED_TRIPS, lambda: unsettled(lo, hi, c_lo),
                                lambda: jnp.float32(1.0))
                return lo, hi, c_lo, flag, trip + 1

            def keep_going(state):
                return jnp.logical_and(state[3] > 0.0, state[4] < 64)

            state = (lo0, hi0, c_lo0, unsettled(lo0, hi0, c_lo0), jnp.int32(0))
            lo, _, c_lo, _, _ = lax.while_loop(keep_going, probes, state)
            return lo[0:1], c_lo[0:1]
        return run

    thr_key, n_ge = lax.switch(n_kt - 1, [search(k) for k in range(1, n_kt_all + 1)])
    thr = _key_to_f32(jnp.maximum(thr_key, KEY_NEG_INF))

    @pl.when(jnp.max(jnp.broadcast_to(n_ge, (SUBLANES, tq))) > kf)
    def _():
        def count(pred):
            def body(j, cnt):
                hit = jnp.where(pred(kpos + j * tk, score_ref[j]), 1.0, 0.0)
                return cnt + jnp.sum(hit.reshape(tk // fold, fold, tq), axis=0)

            cnt = lax.fori_loop(0, n_kt, body, jnp.zeros((fold, tq), F32))
            return jnp.sum(cnt, axis=0, keepdims=True)

        need = kf - count(lambda pos, sc: sc > thr)
        n_pos_bits = (n_kt_all * tk - 1).bit_length()

        def pos_step(t, cut):
            cand = cut | jnp.left_shift(jnp.int32(1), n_pos_bits - 1 - t)
            before = count(lambda pos, sc: (sc == thr) & (pos < cand))
            return jnp.where(before < need, cand, cut)

        cut = lax.fori_loop(0, n_pos_bits, pos_step, jnp.zeros((1, tq), I32))

        def drop_tile(j, carry):
            sc = score_ref[j]
            score_ref[j] = jnp.where((sc == thr) & (kpos + j * tk > cut), -jnp.inf, sc)
            return carry

        lax.fori_loop(0, n_kt, drop_tile, 0)

    def mask_tile(j, carry):
        keep = (score_ref[j] >= thr) & (kpos + j * tk <= qpos)
        score_ref[j] = jnp.where(keep, 0.0, NEG_BIG)
        return carry

    lax.fori_loop(0, n_kt, mask_tile, 0)

    m_ref[...] = jnp.full(m_ref.shape, NEG_BIG, F32)
    l_ref[...] = jnp.zeros(l_ref.shape, F32)
    acc_ref[...] = jnp.zeros(acc_ref.shape, F32)
    p_buf[...] = jnp.zeros(p_buf.shape, p_buf.dtype)
    a_buf[...] = jnp.ones(a_buf.shape, F32)

    def qk(j, g):
        bias = jnp.concatenate([score_ref[j]] * hg, axis=1)
        s_buf[g % 2] = _dot(kv_ref[j], q_s[g]) + bias

    def softmax(g):
        s = s_buf[g % 2]
        m_old = m_ref[g]
        m_new = jnp.maximum(m_old, jnp.max(s, axis=0, keepdims=True))
        p = jnp.exp2(s - m_new)
        alpha = jnp.exp2(m_old - m_new)
        l_ref[g] = alpha * l_ref[g] + jnp.sum(p, axis=0, keepdims=True)
        a_buf[g % 2] = alpha
        p_buf[g % 2] = p.astype(p_buf.dtype)
        m_ref[g] = m_new

    def pv(j, g):
        acc_ref[g] = a_buf[g % 2] * acc_ref[g] + _dot(kvt_ref[j], p_buf[g % 2])

    last = n_groups - 1
    qk(0, 0)

    def attn_tile(j):
        for g in range(n_groups):
            if g < last:
                qk(j, g + 1)
            else:
                qk(jnp.minimum(j + 1, n_kt - 1), 0)
            if g > 0:
                pv(j, g - 1)
            else:
                pv(jnp.maximum(j - 1, 0), last)
            softmax(g)

    _unrolled_loop(n_kt, attn_tile, TILE_UNROLL)
    pv(n_kt - 1, last)

    for h in range(n_heads):
        g, c = divmod(h, hg)
        lanes = slice(c * tq, (c + 1) * tq)
        o_t = (acc_ref[g, :, lanes] / l_ref[g, :, lanes]).astype(wuvt_ref.dtype)
        yt_ref[h * D_VHEAD:(h + 1) * D_VHEAD, :] = _dot(wuvt_ref[h], o_t)


def _dsa_attention(qi_t, wi_t, kidx, q_t, kv, kvt, w_uv_t, topk, tq=128, heads_per_group=2):
    n_heads, d_lat, s = q_t.shape
    n_kt, tk, dk = kidx.shape
    assert tk % tq == 0
    n_groups = n_heads // heads_per_group
    gw = heads_per_group * tq
    const3 = lambda b: (0, 0, 0)
    return pl.pallas_call(
        functools.partial(_attn_kernel, topk=topk),
        grid=(s // tq,),
        in_specs=[
            pl.BlockSpec((N_IDX_HEADS, dk, tq), lambda b: (0, 0, b)),
            pl.BlockSpec((N_IDX_HEADS, tq), lambda b: (0, b)),
            pl.BlockSpec((n_kt, tk, dk), const3),
            pl.BlockSpec((n_heads, d_lat, tq), lambda b: (0, 0, b)),
            pl.BlockSpec((n_kt, tk, d_lat), const3),
            pl.BlockSpec((n_kt, d_lat, tk), const3),
            pl.BlockSpec((n_heads, D_VHEAD, d_lat), const3),
        ],
        out_specs=pl.BlockSpec((n_heads * D_VHEAD, tq), lambda b: (0, b)),
        out_shape=jax.ShapeDtypeStruct((n_heads * D_VHEAD, s), F32),
        scratch_shapes=[
            pltpu.VMEM((n_kt, tk, tq), F32),
            pltpu.VMEM((n_groups, dk, gw), BF16),
            pltpu.VMEM((n_groups, d_lat, gw), BF16),
            pltpu.VMEM((n_groups, 1, gw), F32),
            pltpu.VMEM((n_groups, 1, gw), F32),
            pltpu.VMEM((n_groups, 1, gw), F32),
            pltpu.VMEM((n_groups, d_lat, gw), F32),
            pltpu.VMEM((2, tk, gw), F32),
            pltpu.VMEM((2, tk, gw), BF16),
            pltpu.VMEM((2, 1, gw), F32),
        ],
        compiler_params=_params("parallel"),
        name="dsa_attention",
    )(qi_t, wi_t, kidx, q_t, kv, kvt, w_uv_t)


def _merge_kernel(x_ref, z_ref, wgc_ref, wga_ref, wco_ref, yat_ref, o_ref, w_s, wco_s):
    @pl.when(pl.program_id(1) == 0)
    def _():
        w_s[0] = wgc_ref[...].astype(w_s.dtype)
        w_s[1] = wga_ref[...].astype(w_s.dtype)
        wco_s[...] = wco_ref[...].astype(wco_s.dtype)

    x = x_ref[...]
    g_conv = _sigmoid(_dot_nt(x, w_s[0]))
    g_attn = _sigmoid(_dot_nt(x, w_s[1]))
    y_conv = _dot(z_ref[...], wco_s[...])
    o_ref[...] = (g_conv * y_conv + g_attn * yat_ref[...].T).astype(o_ref.dtype)


def _merge(x_bf, z, w_in_t, row0, w_co, y_attn_t, tm=512, tn=512):
    s, d = z.shape
    n = w_co.shape[1]
    gate_rows = lambda off: pl.BlockSpec(
        (pl.Element(tn), pl.Element(d)),
        lambda j, i: (pl.multiple_of(row0 + off + j * tn, SUBLANES), 0))
    return pl.pallas_call(
        _merge_kernel,
        grid=(n // tn, s // tm),
        in_specs=[
            pl.BlockSpec((tm, d), lambda j, i: (i, 0)),
            pl.BlockSpec((tm, d), lambda j, i: (i, 0)),
            gate_rows(0),
            gate_rows(n),
            pl.BlockSpec((d, tn), lambda j, i: (0, j)),
            pl.BlockSpec((tn, tm), lambda j, i: (j, i)),
        ],
        out_specs=pl.BlockSpec((tm, tn), lambda j, i: (i, j)),
        out_shape=jax.ShapeDtypeStruct((s, n), BF16),
        scratch_shapes=[pltpu.VMEM((2, tn, d), BF16), pltpu.VMEM((d, tn), BF16)],
        compiler_params=_params("parallel", "arbitrary"),
        name="merge",
    )(x_bf, z, w_in_t, w_in_t, w_co, y_attn_t)


def _cast_resident(w_ref, w_s):
    @pl.when(pl.program_id(0) == 0)
    def _():
        w_s[...] = w_ref[...].astype(w_s.dtype)


def _resident(shape):
    return pl.BlockSpec(shape, lambda i: (0,) * len(shape), pipeline_mode=pl.Buffered(1))


def _oproj_kernel(m_ref, w_ref, h_ref, g_ref, b_ref, o_ref, obf_ref, w_s, *, alpha):
    _cast_resident(w_ref, w_s)
    v = alpha * h_ref[...] + _dot(m_ref[...], w_s[...])
    out = _layer_norm(v, g_ref[...], b_ref[...])
    o_ref[...] = out
    obf_ref[...] = out.astype(obf_ref.dtype)


def _oproj(mixed, w_o, h, g, b, alpha, tm=512):
    s, d = h.shape
    row = pl.BlockSpec((tm, d), lambda i: (i, 0))
    vec = pl.BlockSpec((1, d), lambda i: (0, 0))
    return pl.pallas_call(
        functools.partial(_oproj_kernel, alpha=alpha),
        grid=(s // tm,),
        in_specs=[row, _resident((d, d)), row, vec, vec],
        out_specs=[row, row],
        out_shape=[jax.ShapeDtypeStruct((s, d), F32), jax.ShapeDtypeStruct((s, d), BF16)],
        scratch_shapes=[pltpu.VMEM((d, d), BF16)],
        compiler_params=_params("arbitrary"),
        name="oproj_ln",
    )(mixed, w_o, h, g, b)


def _ffn_kernel(hbf_ref, wu_ref, wd_ref, h_ref, g_ref, b_ref, o_ref, obf_ref, acc_ref, *, alpha):
    f = pl.program_id(1)

    @pl.when(f == 0)
    def _():
        acc_ref[...] = jnp.zeros(acc_ref.shape, F32)

    a = jnp.maximum(_dot(hbf_ref[...], wu_ref[...]), 0.0)
    acc_ref[...] += _dot((a * a).astype(wd_ref.dtype), wd_ref[...])

    @pl.when(f == pl.num_programs(1) - 1)
    def _():
        out = _layer_norm(alpha * h_ref[...] + acc_ref[...], g_ref[...], b_ref[...])
        o_ref[...] = out
        obf_ref[...] = out.astype(obf_ref.dtype)


def _ffn(h_bf, w_up, w_down, h, g, b, alpha, tm=512, tf=1024):
    s, d = h.shape
    dff = w_up.shape[1]
    row = pl.BlockSpec((tm, d), lambda i, f: (i, 0))
    vec = pl.BlockSpec((1, d), lambda i, f: (0, 0))
    return pl.pallas_call(
        functools.partial(_ffn_kernel, alpha=alpha),
        grid=(s // tm, dff // tf),
        in_specs=[row, pl.BlockSpec((d, tf), lambda i, f: (0, f)),
                  pl.BlockSpec((tf, d), lambda i, f: (f, 0)), row, vec, vec],
        out_specs=[row, row],
        out_shape=[jax.ShapeDtypeStruct((s, d), F32), jax.ShapeDtypeStruct((s, d), BF16)],
        scratch_shapes=[pltpu.VMEM((tm, d), F32)],
        compiler_params=_params("parallel", "arbitrary"),
        name="ffn_ln",
    )(h_bf, w_up, w_down, h, g, b)


def _ple_kernel(hbf_ref, wg_ref, p_ref, wp_ref, h_ref, g_ref, b_ref, o_ref, wg_s, *, alpha):
    _cast_resident(wg_ref, wg_s)
    gate = _sigmoid(_dot(hbf_ref[...], wg_s[...]))
    pe = gate * _dot(p_ref[...], wp_ref[...])
    o_ref[...] = _layer_norm(alpha * h_ref[...] + pe, g_ref[...], b_ref[...])


def _ple(h_bf, w_pg, p_bf, w_p, h, g, b, alpha, tm=512):
    s, d = h.shape
    dp = p_bf.shape[1]
    row = pl.BlockSpec((tm, d), lambda i: (i, 0))
    vec = pl.BlockSpec((1, d), lambda i: (0, 0))
    return pl.pallas_call(
        functools.partial(_ple_kernel, alpha=alpha),
        grid=(s // tm,),
        in_specs=[row, _resident((d, d)),
                  pl.BlockSpec((tm, dp), lambda i: (i, 0)),
                  pl.BlockSpec((dp, d), lambda i: (0, 0)), row, vec, vec],
        out_specs=row,
        out_shape=jax.ShapeDtypeStruct((s, d), F32),
        scratch_shapes=[pltpu.VMEM((d, d), BF16)],
        compiler_params=_params("arbitrary"),
        name="ple_ln",
    )(h_bf, w_pg, p_bf, w_p, h, g, b)


def _layer(h, p, w_in, conv_w, w_conv_out, g_kv, w_uv, w_o, ln1_g, ln1_b, w_up, w_down,
           ln2_g, ln2_b, w_ple_gate, w_ple, ln3_g, ln3_b, alpha, tk=512):
    s, d = h.shape
    d_conv = conv_w.shape[1]
    nq = N_IDX_HEADS * D_IDX
    c_qlat = 3 * d_conv
    c_ckv = c_qlat + N_HEADS * D_LATENT
    c_gates = c_ckv + D_LATENT + nq + D_IDX + N_IDX_HEADS
    topk = min(TOPK_MAX, s // 4)
    n_kt = s // tk

    w_in_t = jnp.transpose(w_in)

    h_bf, ckv, kidx, qi_t, wi_t = _idx_proj(h, w_in_t, c_ckv, g_kv.reshape(1, D_LATENT))
    z, w_down_bf = _conv_branch(h_bf, w_in_t, conv_w, d_conv, w_down)
    q_t, w_up_bf = _qlat_proj(h_bf, w_in_t, c_qlat, w_up)

    kidx = kidx.reshape(n_kt, tk, 3 * D_IDX)
    kv = ckv.reshape(n_kt, tk, D_LATENT)
    kvt = jnp.transpose(kv, (0, 2, 1))

    w_uv_t = jnp.transpose(w_uv, (0, 2, 1)).astype(BF16)
    y_attn_t = _dsa_attention(qi_t, wi_t, kidx, q_t, kv, kvt, w_uv_t, topk)

    mixed = _merge(h_bf, z, w_in_t, c_gates, w_conv_out, y_attn_t)
    h1, h1_bf = _oproj(mixed, w_o, h, ln1_g.reshape(1, d), ln1_b.reshape(1, d), alpha)
    h2, h2_bf = _ffn(h1_bf, w_up_bf, w_down_bf, h1,
                     ln2_g.reshape(1, d), ln2_b.reshape(1, d), alpha)
    return _ple(h2_bf, w_ple_gate, p.astype(BF16), w_ple.astype(BF16), h2,
                ln3_g.reshape(1, d), ln3_b.reshape(1, d), alpha)


def kernel(x, p, w_in, conv_w, w_conv_out, g_kv, w_uv, w_o, ln1_g, ln1_b, w_up, w_down,
           ln2_g, ln2_b, w_ple_gate, w_ple, ln3_g, ln3_b):
    depth = w_in.shape[0]
    alpha = (2.0 * depth) ** 0.25
    batch = x.shape[0]
    outs = []
    for bi in range(batch):
        h = x[bi]
        for i in range(depth):
            h = _layer(h, p[i, bi], w_in[i], conv_w[i], w_conv_out[i], g_kv[i], w_uv[i], w_o[i],
                       ln1_g[i], ln1_b[i], w_up[i], w_down[i], ln2_g[i], ln2_b[i],
                       w_ple_gate[i], w_ple[i], ln3_g[i], ln3_b[i], alpha)
        outs.append(h)
    return jnp.stack(outs, axis=0)
```

```python
import functools

import jax
import jax.numpy as jnp
from jax import lax
from jax.experimental import pallas as pl
from jax.experimental.pallas import tpu as pltpu

F32 = jnp.float32
BF16 = jnp.bfloat16
I32 = jnp.int32

N_HEADS = 16
D_LATENT = 256
D_VHEAD = 128
N_IDX_HEADS = 16
D_IDX = 64
TOPK_MAX = 256
CONV_WIDTH = 3
LN_EPS = 1e-5
RMS_EPS = 1e-6
ATTN_SCALE = D_LATENT ** -0.5
IDX_SCALE = (D_IDX ** -0.5) * (N_IDX_HEADS ** -0.5)

VMEM_LIMIT_BYTES = 56 * 1024 * 1024
SUBLANES = 8
LANES = 128

INT_MIN = -(2 ** 31)
KEY_NEG_INF = INT_MIN + 0x7FFFFF
NEG_BIG = -1e30
TILE_UNROLL = 4
UNCHECKED_TRIPS = 8
LOG2E = 1.4426950408889634


def _params(*sem):
    return pltpu.CompilerParams(dimension_semantics=sem, vmem_limit_bytes=VMEM_LIMIT_BYTES)


def _dot(a, b):
    return jnp.dot(a, b, preferred_element_type=F32)


def _dot_nt(a, b):
    return lax.dot_general(a, b, (((1,), (1,)), ((), ())), preferred_element_type=F32)


def _sigmoid(v):
    return 1.0 / (1.0 + jnp.exp(-v))


def _layer_norm(v, g, b):
    mu = jnp.mean(v, axis=-1, keepdims=True)
    c = v - mu
    var = jnp.mean(c * c, axis=-1, keepdims=True)
    return c * lax.rsqrt(var + LN_EPS) * g + b


def _split_hi_lo(v):
    hi = v.astype(BF16)
    lo = (v - hi.astype(F32)).astype(BF16)
    return hi, lo


def _side_cast_specs(w, grid):
    n_steps = grid[0] * grid[1]
    rows, cols = w.shape
    chunk = rows // n_steps
    assert chunk * n_steps == rows and chunk % (2 * SUBLANES) == 0
    spec = pl.BlockSpec((chunk, cols), lambda j, i: (j * grid[1] + i, 0))
    return spec, spec, jax.ShapeDtypeStruct(w.shape, BF16)


def _conv_kernel(x_ref, wb_ref, wc_ref, wu_ref, cw_ref, side_ref, z_ref, side_bf_ref, cu_ref, w_s):
    side_bf_ref[...] = side_ref[...].astype(side_bf_ref.dtype)
    i = pl.program_id(1)
    tm = x_ref.shape[0]
    halo = SUBLANES

    @pl.when(i == 0)
    def _():
        w_s[0] = wb_ref[...].astype(w_s.dtype)
        w_s[1] = wc_ref[...].astype(w_s.dtype)
        w_s[2] = wu_ref[...].astype(w_s.dtype)
        cu_ref[0:halo, :] = jnp.zeros((halo, cu_ref.shape[1]), F32)

    @pl.when(i > 0)
    def _():
        cu_ref[0:halo, :] = cu_ref[tm:tm + halo, :]

    x = x_ref[...]
    cu_ref[halo:halo + tm, :] = _dot_nt(x, w_s[1]) * _dot_nt(x, w_s[2])
    cw = cw_ref[...]
    v = (cw[0:1, :] * cu_ref[halo - 2:halo - 2 + tm, :]
         + cw[1:2, :] * cu_ref[halo - 1:halo - 1 + tm, :]
         + cw[2:3, :] * cu_ref[halo:halo + tm, :])
    z_ref[...] = (_dot_nt(x, w_s[0]) * v).astype(z_ref.dtype)


def _conv_branch(x_bf, w_in_t, conv_w, d_conv, w_side, tm=1024, tn=512):
    s, d = x_bf.shape
    nj = d_conv // tn
    grid = (nj, s // tm)
    side_in, side_out, side_shape = _side_cast_specs(w_side, grid)
    return pl.pallas_call(
        _conv_kernel,
        grid=grid,
        in_specs=[
            pl.BlockSpec((tm, d), lambda j, i: (i, 0)),
            pl.BlockSpec((tn, d), lambda j, i: (j, 0)),
            pl.BlockSpec((tn, d), lambda j, i: (nj + j, 0)),
            pl.BlockSpec((tn, d), lambda j, i: (2 * nj + j, 0)),
            pl.BlockSpec((CONV_WIDTH, tn), lambda j, i: (0, j)),
            side_in,
        ],
        out_specs=[pl.BlockSpec((tm, tn), lambda j, i: (i, j)), side_out],
        out_shape=[jax.ShapeDtypeStruct((s, d_conv), BF16), side_shape],
        scratch_shapes=[pltpu.VMEM((tm + SUBLANES, tn), F32),
                        pltpu.VMEM((3, tn, d), BF16)],
        compiler_params=_params("parallel", "arbitrary"),
        name="conv_branch",
    )(x_bf, w_in_t, w_in_t, w_in_t, conv_w, w_side)


def _qlat_kernel(x_ref, w_ref, side_ref, q_ref, side_bf_ref, w_s):
    side_bf_ref[...] = side_ref[...].astype(side_bf_ref.dtype)

    @pl.when(pl.program_id(1) == 0)
    def _():
        w_s[...] = w_ref[...].astype(w_s.dtype)

    acc = _dot_nt(w_s[...], x_ref[...]) * (ATTN_SCALE * LOG2E)
    q_ref[...] = acc.astype(q_ref.dtype).reshape(q_ref.shape)


def _qlat_proj(x_bf, w_in_t, row0, w_side, tm=1024, heads_per_step=4):
    s, d = x_bf.shape
    tn = heads_per_step * D_LATENT
    j0 = row0 // tn
    assert j0 * tn == row0
    grid = (N_HEADS // heads_per_step, s // tm)
    side_in, side_out, side_shape = _side_cast_specs(w_side, grid)
    return pl.pallas_call(
        _qlat_kernel,
        grid=grid,
        in_specs=[
            pl.BlockSpec((tm, d), lambda j, i: (i, 0)),
            pl.BlockSpec((tn, d), lambda j, i: (j0 + j, 0)),
            side_in,
        ],
        out_specs=[pl.BlockSpec((heads_per_step, D_LATENT, tm), lambda j, i: (j, 0, i)), side_out],
        out_shape=[jax.ShapeDtypeStruct((N_HEADS, D_LATENT, s), BF16), side_shape],
        scratch_shapes=[pltpu.VMEM((tn, d), BF16)],
        compiler_params=_params("parallel", "arbitrary"),
        name="qlat_proj",
    )(x_bf, w_in_t, w_side)


def _idx_kernel(x_ref, w_ref, g_ref, xbf_ref, ckv_ref, kidx_ref, qit_ref, wit_ref, w_s):
    @pl.when(pl.program_id(0) == 0)
    def _():
        w_s[...] = w_ref[...].astype(w_s.dtype)

    x = x_ref[...].astype(xbf_ref.dtype)
    xbf_ref[...] = x
    nq = N_IDX_HEADS * D_IDX
    r_q, r_k, r_w = D_LATENT, D_LATENT + nq, D_LATENT + nq + D_IDX
    c = _dot_nt(x, w_s[0:r_q])
    ms = jnp.mean(c * c, axis=-1, keepdims=True)
    ckv_ref[...] = (c * lax.rsqrt(ms + RMS_EPS) * g_ref[...]).astype(ckv_ref.dtype)
    k_hi, k_lo = _split_hi_lo(_dot_nt(x, w_s[r_k:r_w]))
    kidx_ref[...] = jnp.concatenate([k_hi, k_hi, k_lo], axis=1)
    q_t = _dot_nt(w_s[r_q:r_k], x)
    q_hi, q_lo = _split_hi_lo(q_t.reshape(N_IDX_HEADS, D_IDX, q_t.shape[1]))
    qit_ref[...] = jnp.concatenate([q_hi, q_lo, q_hi], axis=1)
    wit_ref[...] = _dot_nt(w_s[r_w:r_w + N_IDX_HEADS], x) * IDX_SCALE


def _idx_proj(x, w_in_t, row0, g_kv, tm=512):
    s, d = x.shape
    nq = N_IDX_HEADS * D_IDX
    n = D_LATENT + nq + D_IDX + N_IDX_HEADS
    n_pad = -(-n // LANES) * LANES
    return pl.pallas_call(
        _idx_kernel,
        grid=(s // tm,),
        in_specs=[
            pl.BlockSpec((tm, d), lambda i: (i, 0)),
            pl.BlockSpec((pl.Element(n_pad), pl.Element(d)), lambda i: (row0, 0)),
            pl.BlockSpec((1, D_LATENT), lambda i: (0, 0)),
        ],
        out_specs=[
            pl.BlockSpec((tm, d), lambda i: (i, 0)),
            pl.BlockSpec((tm, D_LATENT), lambda i: (i, 0)),
            pl.BlockSpec((tm, 3 * D_IDX), lambda i: (i, 0)),
            pl.BlockSpec((N_IDX_HEADS, 3 * D_IDX, tm), lambda i: (0, 0, i)),
            pl.BlockSpec((N_IDX_HEADS, tm), lambda i: (0, i)),
        ],
        out_shape=[
            jax.ShapeDtypeStruct((s, d), BF16),
            jax.ShapeDtypeStruct((s, D_LATENT), BF16),
            jax.ShapeDtypeStruct((s, 3 * D_IDX), BF16),
            jax.ShapeDtypeStruct((N_IDX_HEADS, 3 * D_IDX, s), BF16),
            jax.ShapeDtypeStruct((N_IDX_HEADS, s), F32),
        ],
        scratch_shapes=[pltpu.VMEM((n_pad, d), BF16)],
        compiler_params=_params("arbitrary"),
        name="idx_proj",
    )(x, w_in_t, g_kv)


def _flip_key(v):
    return v ^ ((v >> 31) & 0x7FFFFFFF)


def _key_to_f32(key):
    return pltpu.bitcast(_flip_key(key), F32)


def _unrolled_loop(n, body, unroll):
    start = 0
    while unroll >= 1:
        def trip(i, carry, start=start, unroll=unroll):
            for r in range(unroll):
                body(start + unroll * i + r)
            return carry

        trips = (n - start) // unroll
        lax.fori_loop(0, trips, trip, 0)
        start = start + unroll * trips
        unroll //= 2


def _attn_kernel(qit_ref, wit_ref, kidx_ref, qt_ref, kv_ref, kvt_ref, wuvt_ref, yt_ref,
                 score_ref, bias_ref, ext_ref, qi_s, q_s, w_s, m_ref, l_ref, acc_ref,
                 s_buf, p_buf, a_buf, *, topk):
    b = pl.program_id(0)
    n_heads, d_lat, tq = qt_ref.shape
    n_kt_all, tk, _ = kidx_ref.shape
    n_groups, _, gw = q_s.shape
    hg = gw // tq
    n_kt = ((b + 1) * tq + tk - 1) // tk
    kpos = lax.broadcasted_iota(I32, (tk, tq), 0)
    qpos = b * tq + lax.broadcasted_iota(I32, (tk, tq), 1)

    for h in range(n_heads):
        g, c = divmod(h, hg)
        qi_s[g, :, c * tq:(c + 1) * tq] = qit_ref[h]
        q_s[g, :, c * tq:(c + 1) * tq] = qt_ref[h]
        w_s[g, :, c * tq:(c + 1) * tq] = wit_ref[h:h + 1, :]

    def score_tile(j):
        kidx = kidx_ref[j]
        score = jnp.zeros((tk, tq), F32)
        for g in range(n_groups):
            rel = jnp.maximum(_dot(kidx, qi_s[g]), 0.0) * w_s[g]
            for c in range(hg):
                score = score + rel[:, c * tq:(c + 1) * tq]
        score = jnp.where(kpos + j * tk <= qpos, score, -jnp.inf)
        score_ref[j] = score
        slabs = score.reshape(tk // fold, fold, tq)
        top, bot = ext_ref[0], ext_ref[1]
        for r in range(tk // fold):
            top = jnp.maximum(top, slabs[r])
            bot = jnp.minimum(bot, jnp.where(slabs[r] > -jnp.inf, slabs[r], jnp.inf))
        ext_ref[0], ext_ref[1] = top, bot

    fold = 8 * SUBLANES
    ext_ref[0] = jnp.full((fold, tq), -jnp.inf, F32)
    ext_ref[1] = jnp.full((fold, tq), jnp.inf, F32)
    _unrolled_loop(n_kt, score_tile, TILE_UNROLL)

    kf = float(topk)

    n_causal = (qpos[0:1, :] + 1).astype(F32)

    top, bot = ext_ref[0], ext_ref[1]
    rep = lambda v: jnp.broadcast_to(v, (SUBLANES, tq))
    short = n_causal < kf
    lo0 = rep(jnp.where(short, KEY_NEG_INF,
                        _flip_key(pltpu.bitcast(jnp.min(bot, axis=0, keepdims=True), I32))))
    hi0 = rep(_flip_key(pltpu.bitcast(jnp.max(top, axis=0, keepdims=True), I32)) + 1)
    c_lo0 = rep(jnp.where(short, 0.0, n_causal))

    def unsettled(lo, hi, c_lo):
        return jnp.max(jnp.where((c_lo > kf) & (hi - 1 > lo), 1.0, 0.0))

    def search(tiles):
        def run():
            def count_ge(cand):
                cnt = jnp.zeros((fold, tq), F32)
                for j in range(tiles):
                    hit = jnp.where(score_ref[j] >= cand, 1.0, 0.0)
                    cnt = cnt + jnp.sum(hit.reshape(tk // fold, fold, tq), axis=0)
                return jnp.sum(cnt, axis=0, keepdims=True)

            def probe(lo, hi, c_lo, by_key):
                live = (c_lo > kf) & (hi - 1 > lo)
                key_mid = (lo >> 1) + (hi >> 1) + (lo & hi & 1)
                val_mid = 0.5 * (_key_to_f32(lo) + _key_to_f32(hi - 1))
                pk = jnp.where(by_key, key_mid, _flip_key(pltpu.bitcast(val_mid, I32)))
                pk = jnp.minimum(jnp.maximum(pk, lo + 1), hi - 1)
                pk = jnp.where(live, pk, lo)
                c = rep(count_ge(_key_to_f32(jnp.maximum(pk, KEY_NEG_INF))[0:1]))
                up = live & (c >= kf)
                down = live & (c < kf)
                return jnp.where(up, pk, lo), jnp.where(down, pk, hi), jnp.where(up, c, c_lo)

            def probes(state):
                lo, hi, c_lo, _, trip = state
                lo, hi, c_lo = probe(lo, hi, c_lo, False)
                lo, hi, c_lo = probe(lo, hi, c_lo, trip % 2 == 1)
                flag = lax.cond(trip >= UNCHECKED_TRIPS, lambda: unsettled(lo, hi, c_lo),
                                lambda: jnp.float32(1.0))
                return lo, hi, c_lo, flag, trip + 1

            def keep_going(state):
                return jnp.logical_and(state[3] > 0.0, state[4] < 64)

            state = (lo0, hi0, c_lo0, unsettled(lo0, hi0, c_lo0), jnp.int32(0))
            lo, _, c_lo, _, _ = lax.while_loop(keep_going, probes, state)
            return lo[0:1], c_lo[0:1]
        return run

    thr_key, n_ge = lax.switch(n_kt - 1, [search(k) for k in range(1, n_kt_all + 1)])
    thr = _key_to_f32(jnp.maximum(thr_key, KEY_NEG_INF))

    @pl.when(jnp.max(jnp.broadcast_to(n_ge, (SUBLANES, tq))) > kf)
    def _():
        def count(pred):
            def body(j, cnt):
                hit = jnp.where(pred(kpos + j * tk, score_ref[j]), 1.0, 0.0)
                return cnt + jnp.sum(hit.reshape(tk // fold, fold, tq), axis=0)

            cnt = lax.fori_loop(0, n_kt, body, jnp.zeros((fold, tq), F32))
            return jnp.sum(cnt, axis=0, keepdims=True)

        need = kf - count(lambda pos, sc: sc > thr)
        n_pos_bits = (n_kt_all * tk - 1).bit_length()

        def pos_step(t, cut):
            cand = cut | jnp.left_shift(jnp.int32(1), n_pos_bits - 1 - t)
            before = count(lambda pos, sc: (sc == thr) & (pos < cand))
            return jnp.where(before < need, cand, cut)

        cut = lax.fori_loop(0, n_pos_bits, pos_step, jnp.zeros((1, tq), I32))

        def drop_tile(j, carry):
            sc = score_ref[j]
            score_ref[j] = jnp.where((sc == thr) & (kpos + j * tk > cut), -jnp.inf, sc)
            return carry

        lax.fori_loop(0, n_kt, drop_tile, 0)

    def mask_tile(j):
        keep = (score_ref[j] >= thr) & (kpos + j * tk <= qpos)
        bias_ref[j] = jnp.where(keep, 0.0, NEG_BIG)

    m_ref[...] = jnp.full(m_ref.shape, NEG_BIG, F32)
    l_ref[...] = jnp.zeros(l_ref.shape, F32)
    acc_ref[...] = jnp.zeros(acc_ref.shape, F32)
    p_buf[...] = jnp.zeros(p_buf.shape, p_buf.dtype)
    a_buf[...] = jnp.ones(a_buf.shape, F32)

    def qk(j, g):
        bias = jnp.concatenate([bias_ref[j]] * hg, axis=1)
        s_buf[g % 2] = _dot(kv_ref[j], q_s[g]) + bias

    def softmax(g):
        s = s_buf[g % 2]
        m_old = m_ref[g]
        m_new = jnp.maximum(m_old, jnp.max(s, axis=0, keepdims=True))
        p = jnp.exp2(s - m_new)
        alpha = jnp.exp2(m_old - m_new)
        l_ref[g] = alpha * l_ref[g] + jnp.sum(p, axis=0, keepdims=True)
        a_buf[g % 2] = alpha
        p_buf[g % 2] = p.astype(p_buf.dtype)
        m_ref[g] = m_new

    def pv(j, g):
        acc_ref[g] = a_buf[g % 2] * acc_ref[g] + _dot(kvt_ref[j], p_buf[g % 2])

    last = n_groups - 1
    mask_tile(0)
    qk(0, 0)

    def attn_tile(j):
        j_next = jnp.minimum(j + 1, n_kt - 1)
        for g in range(n_groups):
            if g == 0:
                mask_tile(j_next)
            if g < last:
                qk(j, g + 1)
            else:
                qk(j_next, 0)
            if g > 0:
                pv(j, g - 1)
            else:
                pv(jnp.maximum(j - 1, 0), last)
            softmax(g)

    _unrolled_loop(n_kt, attn_tile, TILE_UNROLL)
    pv(n_kt - 1, last)

    for h in range(n_heads):
        g, c = divmod(h, hg)
        lanes = slice(c * tq, (c + 1) * tq)
        o_t = (acc_ref[g, :, lanes] / l_ref[g, :, lanes]).astype(wuvt_ref.dtype)
        yt_ref[h * D_VHEAD:(h + 1) * D_VHEAD, :] = _dot(wuvt_ref[h], o_t)


def _dsa_attention(qi_t, wi_t, kidx, q_t, kv, kvt, w_uv_t, topk, tq=128, heads_per_group=2):
    n_heads, d_lat, s = q_t.shape
    n_kt, tk, dk = kidx.shape
    assert tk % tq == 0
    n_groups = n_heads // heads_per_group
    gw = heads_per_group * tq
    const3 = lambda b: (0, 0, 0)
    return pl.pallas_call(
        functools.partial(_attn_kernel, topk=topk),
        grid=(s // tq,),
        in_specs=[
            pl.BlockSpec((N_IDX_HEADS, dk, tq), lambda b: (0, 0, b)),
            pl.BlockSpec((N_IDX_HEADS, tq), lambda b: (0, b)),
            pl.BlockSpec((n_kt, tk, dk), const3),
            pl.BlockSpec((n_heads, d_lat, tq), lambda b: (0, 0, b)),
            pl.BlockSpec((n_kt, tk, d_lat), const3),
            pl.BlockSpec((n_kt, d_lat, tk), const3),
            pl.BlockSpec((n_heads, D_VHEAD, d_lat), const3),
        ],
        out_specs=pl.BlockSpec((n_heads * D_VHEAD, tq), lambda b: (0, b)),
        out_shape=jax.ShapeDtypeStruct((n_heads * D_VHEAD, s), F32),
        scratch_shapes=[
            pltpu.VMEM((n_kt, tk, tq), F32),
            pltpu.VMEM((n_kt, tk, tq), F32),
            pltpu.VMEM((2, 8 * SUBLANES, tq), F32),
            pltpu.VMEM((n_groups, dk, gw), BF16),
            pltpu.VMEM((n_groups, d_lat, gw), BF16),
            pltpu.VMEM((n_groups, 1, gw), F32),
            pltpu.VMEM((n_groups, 1, gw), F32),
            pltpu.VMEM((n_groups, 1, gw), F32),
            pltpu.VMEM((n_groups, d_lat, gw), F32),
            pltpu.VMEM((2, tk, gw), F32),
            pltpu.VMEM((2, tk, gw), BF16),
            pltpu.VMEM((2, 1, gw), F32),
        ],
        compiler_params=_params("parallel"),
        name="dsa_attention",
    )(qi_t, wi_t, kidx, q_t, kv, kvt, w_uv_t)


def _merge_kernel(x_ref, z_ref, wgc_ref, wga_ref, wco_ref, yat_ref, o_ref, w_s, wco_s):
    @pl.when(pl.program_id(1) == 0)
    def _():
        w_s[0] = wgc_ref[...].astype(w_s.dtype)
        w_s[1] = wga_ref[...].astype(w_s.dtype)
        wco_s[...] = wco_ref[...].astype(wco_s.dtype)

    x = x_ref[...]
    g_conv = _sigmoid(_dot_nt(x, w_s[0]))
    g_attn = _sigmoid(_dot_nt(x, w_s[1]))
    y_conv = _dot(z_ref[...], wco_s[...])
    o_ref[...] = (g_conv * y_conv + g_attn * yat_ref[...].T).astype(o_ref.dtype)


def _merge(x_bf, z, w_in_t, row0, w_co, y_attn_t, tm=512, tn=512):
    s, d = z.shape
    n = w_co.shape[1]
    gate_rows = lambda off: pl.BlockSpec(
        (pl.Element(tn), pl.Element(d)),
        lambda j, i: (pl.multiple_of(row0 + off + j * tn, SUBLANES), 0))
    return pl.pallas_call(
        _merge_kernel,
        grid=(n // tn, s // tm),
        in_specs=[
            pl.BlockSpec((tm, d), lambda j, i: (i, 0)),
            pl.BlockSpec((tm, d), lambda j, i: (i, 0)),
            gate_rows(0),
            gate_rows(n),
            pl.BlockSpec((d, tn), lambda j, i: (0, j)),
            pl.BlockSpec((tn, tm), lambda j, i: (j, i)),
        ],
        out_specs=pl.BlockSpec((tm, tn), lambda j, i: (i, j)),
        out_shape=jax.ShapeDtypeStruct((s, n), BF16),
        scratch_shapes=[pltpu.VMEM((2, tn, d), BF16), pltpu.VMEM((d, tn), BF16)],
        compiler_params=_params("parallel", "arbitrary"),
        name="merge",
    )(x_bf, z, w_in_t, w_in_t, w_co, y_attn_t)


def _cast_resident(w_ref, w_s):
    @pl.when(pl.program_id(0) == 0)
    def _():
        w_s[...] = w_ref[...].astype(w_s.dtype)


def _resident(shape):
    return pl.BlockSpec(shape, lambda i: (0,) * len(shape), pipeline_mode=pl.Buffered(1))


def _oproj_kernel(m_ref, w_ref, h_ref, g_ref, b_ref, o_ref, obf_ref, w_s, *, alpha):
    _cast_resident(w_ref, w_s)
    v = alpha * h_ref[...] + _dot(m_ref[...], w_s[...])
    out = _layer_norm(v, g_ref[...], b_ref[...])
    o_ref[...] = out
    obf_ref[...] = out.astype(obf_ref.dtype)


def _oproj(mixed, w_o, h, g, b, alpha, tm=512):
    s, d = h.shape
    row = pl.BlockSpec((tm, d), lambda i: (i, 0))
    vec = pl.BlockSpec((1, d), lambda i: (0, 0))
    return pl.pallas_call(
        functools.partial(_oproj_kernel, alpha=alpha),
        grid=(s // tm,),
        in_specs=[row, _resident((d, d)), row, vec, vec],
        out_specs=[row, row],
        out_shape=[jax.ShapeDtypeStruct((s, d), F32), jax.ShapeDtypeStruct((s, d), BF16)],
        scratch_shapes=[pltpu.VMEM((d, d), BF16)],
        compiler_params=_params("arbitrary"),
        name="oproj_ln",
    )(mixed, w_o, h, g, b)


def _ffn_kernel(hbf_ref, wu_ref, wd_ref, h_ref, g_ref, b_ref, o_ref, obf_ref, acc_ref, *, alpha):
    f = pl.program_id(1)

    @pl.when(f == 0)
    def _():
        acc_ref[...] = jnp.zeros(acc_ref.shape, F32)

    a = jnp.maximum(_dot(hbf_ref[...], wu_ref[...]), 0.0)
    acc_ref[...] += _dot((a * a).astype(wd_ref.dtype), wd_ref[...])

    @pl.when(f == pl.num_programs(1) - 1)
    def _():
        out = _layer_norm(alpha * h_ref[...] + acc_ref[...], g_ref[...], b_ref[...])
        o_ref[...] = out
        obf_ref[...] = out.astype(obf_ref.dtype)


def _ffn(h_bf, w_up, w_down, h, g, b, alpha, tm=512, tf=1024):
    s, d = h.shape
    dff = w_up.shape[1]
    row = pl.BlockSpec((tm, d), lambda i, f: (i, 0))
    vec = pl.BlockSpec((1, d), lambda i, f: (0, 0))
    return pl.pallas_call(
        functools.partial(_ffn_kernel, alpha=alpha),
        grid=(s // tm, dff // tf),
        in_specs=[row, pl.BlockSpec((d, tf), lambda i, f: (0, f)),
                  pl.BlockSpec((tf, d), lambda i, f: (f, 0)), row, vec, vec],
        out_specs=[row, row],
        out_shape=[jax.ShapeDtypeStruct((s, d), F32), jax.ShapeDtypeStruct((s, d), BF16)],
        scratch_shapes=[pltpu.VMEM((tm, d), F32)],
        compiler_params=_params("parallel", "arbitrary"),
        name="ffn_ln",
    )(h_bf, w_up, w_down, h, g, b)


def _ple_kernel(hbf_ref, wg_ref, p_ref, wp_ref, h_ref, g_ref, b_ref, o_ref, wg_s, *, alpha):
    _cast_resident(wg_ref, wg_s)
    gate = _sigmoid(_dot(hbf_ref[...], wg_s[...]))
    pe = gate * _dot(p_ref[...], wp_ref[...])
    o_ref[...] = _layer_norm(alpha * h_ref[...] + pe, g_ref[...], b_ref[...])


def _ple(h_bf, w_pg, p_bf, w_p, h, g, b, alpha, tm=512):
    s, d = h.shape
    dp = p_bf.shape[1]
    row = pl.BlockSpec((tm, d), lambda i: (i, 0))
    vec = pl.BlockSpec((1, d), lambda i: (0, 0))
    return pl.pallas_call(
        functools.partial(_ple_kernel, alpha=alpha),
        grid=(s // tm,),
        in_specs=[row, _resident((d, d)),
                  pl.BlockSpec((tm, dp), lambda i: (i, 0)),
                  pl.BlockSpec((dp, d), lambda i: (0, 0)), row, vec, vec],
        out_specs=row,
        out_shape=jax.ShapeDtypeStruct((s, d), F32),
        scratch_shapes=[pltpu.VMEM((d, d), BF16)],
        compiler_params=_params("arbitrary"),
        name="ple_ln",
    )(h_bf, w_pg, p_bf, w_p, h, g, b)


def _layer(h, p, w_in, conv_w, w_conv_out, g_kv, w_uv, w_o, ln1_g, ln1_b, w_up, w_down,
           ln2_g, ln2_b, w_ple_gate, w_ple, ln3_g, ln3_b, alpha, tk=512):
    s, d = h.shape
    d_conv = conv_w.shape[1]
    nq = N_IDX_HEADS * D_IDX
    c_qlat = 3 * d_conv
    c_ckv = c_qlat + N_HEADS * D_LATENT
    c_gates = c_ckv + D_LATENT + nq + D_IDX + N_IDX_HEADS
    topk = min(TOPK_MAX, s // 4)
    n_kt = s // tk

    w_in_t = jnp.transpose(w_in)

    h_bf, ckv, kidx, qi_t, wi_t = _idx_proj(h, w_in_t, c_ckv, g_kv.reshape(1, D_LATENT))
    z, w_down_bf = _conv_branch(h_bf, w_in_t, conv_w, d_conv, w_down)
    q_t, w_up_bf = _qlat_proj(h_bf, w_in_t, c_qlat, w_up)

    kidx = kidx.reshape(n_kt, tk, 3 * D_IDX)
    kv = ckv.reshape(n_kt, tk, D_LATENT)
    kvt = jnp.transpose(kv, (0, 2, 1))

    w_uv_t = jnp.transpose(w_uv, (0, 2, 1)).astype(BF16)
    y_attn_t = _dsa_attention(qi_t, wi_t, kidx, q_t, kv, kvt, w_uv_t, topk)

    mixed = _merge(h_bf, z, w_in_t, c_gates, w_conv_out, y_attn_t)
    h1, h1_bf = _oproj(mixed, w_o, h, ln1_g.reshape(1, d), ln1_b.reshape(1, d), alpha)
    h2, h2_bf = _ffn(h1_bf, w_up_bf, w_down_bf, h1,
                     ln2_g.reshape(1, d), ln2_b.reshape(1, d), alpha)
    return _ple(h2_bf, w_ple_gate, p.astype(BF16), w_ple.astype(BF16), h2,
                ln3_g.reshape(1, d), ln3_b.reshape(1, d), alpha)


def kernel(x, p, w_in, conv_w, w_conv_out, g_kv, w_uv, w_o, ln1_g, ln1_b, w_up, w_down,
           ln2_g, ln2_b, w_ple_gate, w_ple, ln3_g, ln3_b):
    depth = w_in.shape[0]
    alpha = (2.0 * depth) ** 0.25
    batch = x.shape[0]
    outs = []
    for bi in range(batch):
        h = x[bi]
        for i in range(depth):
            h = _layer(h, p[i, bi], w_in[i], conv_w[i], w_conv_out[i], g_kv[i], w_uv[i], w_o[i],
                       ln1_g[i], ln1_b[i], w_up[i], w_down[i], ln2_g[i], ln2_b[i],
                       w_ple_gate[i], w_ple[i], ln3_g[i], ln3_b[i], alpha)
        outs.append(h)
    return jnp.stack(outs, axis=0)
```

```python
import functools

import jax
import jax.numpy as jnp
from jax import lax
from jax.experimental import pallas as pl
from jax.experimental.pallas import tpu as pltpu

F32 = jnp.float32
BF16 = jnp.bfloat16
I32 = jnp.int32

N_HEADS = 16
D_LATENT = 256
D_VHEAD = 128
N_IDX_HEADS = 16
D_IDX = 64
TOPK_MAX = 256
CONV_WIDTH = 3
LN_EPS = 1e-5
RMS_EPS = 1e-6
ATTN_SCALE = D_LATENT ** -0.5
IDX_SCALE = (D_IDX ** -0.5) * (N_IDX_HEADS ** -0.5)

VMEM_LIMIT_BYTES = 56 * 1024 * 1024
SUBLANES = 8
LANES = 128

INT_MIN = -(2 ** 31)
KEY_NEG_INF = INT_MIN + 0x7FFFFF
NEG_BIG = -1e30
TILE_UNROLL = 4
UNCHECKED_TRIPS = 8
LOG2E = 1.4426950408889634


def _params(*sem):
    return pltpu.CompilerParams(dimension_semantics=sem, vmem_limit_bytes=VMEM_LIMIT_BYTES)


def _dot(a, b):
    return jnp.dot(a, b, preferred_element_type=F32)


def _dot_nt(a, b):
    return lax.dot_general(a, b, (((1,), (1,)), ((), ())), preferred_element_type=F32)


def _sigmoid(v):
    return 1.0 / (1.0 + jnp.exp(-v))


def _layer_norm(v, g, b):
    mu = jnp.mean(v, axis=-1, keepdims=True)
    c = v - mu
    var = jnp.mean(c * c, axis=-1, keepdims=True)
    return c * lax.rsqrt(var + LN_EPS) * g + b


def _split_hi_lo(v):
    hi = v.astype(BF16)
    lo = (v - hi.astype(F32)).astype(BF16)
    return hi, lo


def _side_cast_specs(w, grid):
    n_steps = grid[0] * grid[1]
    rows, cols = w.shape
    chunk = rows // n_steps
    assert chunk * n_steps == rows and chunk % (2 * SUBLANES) == 0
    spec = pl.BlockSpec((chunk, cols), lambda j, i: (j * grid[1] + i, 0))
    return spec, spec, jax.ShapeDtypeStruct(w.shape, BF16)


def _conv_kernel(x_ref, wb_ref, wc_ref, wu_ref, cw_ref, side_ref, z_ref, side_bf_ref, cu_ref, w_s):
    side_bf_ref[...] = side_ref[...].astype(side_bf_ref.dtype)
    i = pl.program_id(1)
    tm = x_ref.shape[0]
    halo = SUBLANES

    @pl.when(i == 0)
    def _():
        w_s[0] = wb_ref[...].astype(w_s.dtype)
        w_s[1] = wc_ref[...].astype(w_s.dtype)
        w_s[2] = wu_ref[...].astype(w_s.dtype)
        cu_ref[0:halo, :] = jnp.zeros((halo, cu_ref.shape[1]), F32)

    @pl.when(i > 0)
    def _():
        cu_ref[0:halo, :] = cu_ref[tm:tm + halo, :]

    x = x_ref[...]
    cu_ref[halo:halo + tm, :] = _dot_nt(x, w_s[1]) * _dot_nt(x, w_s[2])
    cw = cw_ref[...]
    v = (cw[0:1, :] * cu_ref[halo - 2:halo - 2 + tm, :]
         + cw[1:2, :] * cu_ref[halo - 1:halo - 1 + tm, :]
         + cw[2:3, :] * cu_ref[halo:halo + tm, :])
    z_ref[...] = (_dot_nt(x, w_s[0]) * v).astype(z_ref.dtype)


def _conv_branch(x_bf, w_in_t, conv_w, d_conv, w_side, tm=1024, tn=512):
    s, d = x_bf.shape
    nj = d_conv // tn
    grid = (nj, s // tm)
    side_in, side_out, side_shape = _side_cast_specs(w_side, grid)
    return pl.pallas_call(
        _conv_kernel,
        grid=grid,
        in_specs=[
            pl.BlockSpec((tm, d), lambda j, i: (i, 0)),
            pl.BlockSpec((tn, d), lambda j, i: (j, 0)),
            pl.BlockSpec((tn, d), lambda j, i: (nj + j, 0)),
            pl.BlockSpec((tn, d), lambda j, i: (2 * nj + j, 0)),
            pl.BlockSpec((CONV_WIDTH, tn), lambda j, i: (0, j)),
            side_in,
        ],
        out_specs=[pl.BlockSpec((tm, tn), lambda j, i: (i, j)), side_out],
        out_shape=[jax.ShapeDtypeStruct((s, d_conv), BF16), side_shape],
        scratch_shapes=[pltpu.VMEM((tm + SUBLANES, tn), F32),
                        pltpu.VMEM((3, tn, d), BF16)],
        compiler_params=_params("parallel", "arbitrary"),
        name="conv_branch",
    )(x_bf, w_in_t, w_in_t, w_in_t, conv_w, w_side)


def _qlat_kernel(x_ref, w_ref, side_ref, q_ref, side_bf_ref, w_s):
    side_bf_ref[...] = side_ref[...].astype(side_bf_ref.dtype)

    @pl.when(pl.program_id(1) == 0)
    def _():
        w_s[...] = w_ref[...].astype(w_s.dtype)

    acc = _dot_nt(w_s[...], x_ref[...]) * (ATTN_SCALE * LOG2E)
    q_ref[...] = acc.astype(q_ref.dtype).reshape(q_ref.shape)


def _qlat_proj(x_bf, w_in_t, row0, w_side, tm=1024, heads_per_step=4):
    s, d = x_bf.shape
    tn = heads_per_step * D_LATENT
    j0 = row0 // tn
    assert j0 * tn == row0
    grid = (N_HEADS // heads_per_step, s // tm)
    side_in, side_out, side_shape = _side_cast_specs(w_side, grid)
    return pl.pallas_call(
        _qlat_kernel,
        grid=grid,
        in_specs=[
            pl.BlockSpec((tm, d), lambda j, i: (i, 0)),
            pl.BlockSpec((tn, d), lambda j, i: (j0 + j, 0)),
            side_in,
        ],
        out_specs=[pl.BlockSpec((heads_per_step, D_LATENT, tm), lambda j, i: (j, 0, i)), side_out],
        out_shape=[jax.ShapeDtypeStruct((N_HEADS, D_LATENT, s), BF16), side_shape],
        scratch_shapes=[pltpu.VMEM((tn, d), BF16)],
        compiler_params=_params("parallel", "arbitrary"),
        name="qlat_proj",
    )(x_bf, w_in_t, w_side)


def _idx_kernel(x_ref, w_ref, g_ref, xbf_ref, ckv_ref, kidx_ref, qit_ref, wit_ref, w_s):
    @pl.when(pl.program_id(0) == 0)
    def _():
        w_s[...] = w_ref[...].astype(w_s.dtype)

    x = x_ref[...].astype(xbf_ref.dtype)
    xbf_ref[...] = x
    nq = N_IDX_HEADS * D_IDX
    r_q, r_k, r_w = D_LATENT, D_LATENT + nq, D_LATENT + nq + D_IDX
    c = _dot_nt(x, w_s[0:r_q])
    ms = jnp.mean(c * c, axis=-1, keepdims=True)
    ckv_ref[...] = (c * lax.rsqrt(ms + RMS_EPS) * g_ref[...]).astype(ckv_ref.dtype)
    k_hi, k_lo = _split_hi_lo(_dot_nt(x, w_s[r_k:r_w]))
    kidx_ref[...] = jnp.concatenate([k_hi, k_hi, k_lo], axis=1)
    q_t = _dot_nt(w_s[r_q:r_k], x)
    q_hi, q_lo = _split_hi_lo(q_t.reshape(N_IDX_HEADS, D_IDX, q_t.shape[1]))
    qit_ref[...] = jnp.concatenate([q_hi, q_lo, q_hi], axis=1)
    wit_ref[...] = _dot_nt(w_s[r_w:r_w + N_IDX_HEADS], x) * IDX_SCALE


def _idx_proj(x, w_in_t, row0, g_kv, tm=512):
    s, d = x.shape
    nq = N_IDX_HEADS * D_IDX
    n = D_LATENT + nq + D_IDX + N_IDX_HEADS
    n_pad = -(-n // LANES) * LANES
    return pl.pallas_call(
        _idx_kernel,
        grid=(s // tm,),
        in_specs=[
            pl.BlockSpec((tm, d), lambda i: (i, 0)),
            pl.BlockSpec((pl.Element(n_pad), pl.Element(d)), lambda i: (row0, 0)),
            pl.BlockSpec((1, D_LATENT), lambda i: (0, 0)),
        ],
        out_specs=[
            pl.BlockSpec((tm, d), lambda i: (i, 0)),
            pl.BlockSpec((tm, D_LATENT), lambda i: (i, 0)),
            pl.BlockSpec((tm, 3 * D_IDX), lambda i: (i, 0)),
            pl.BlockSpec((N_IDX_HEADS, 3 * D_IDX, tm), lambda i: (0, 0, i)),
            pl.BlockSpec((N_IDX_HEADS, tm), lambda i: (0, i)),
        ],
        out_shape=[
            jax.ShapeDtypeStruct((s, d), BF16),
            jax.ShapeDtypeStruct((s, D_LATENT), BF16),
            jax.ShapeDtypeStruct((s, 3 * D_IDX), BF16),
            jax.ShapeDtypeStruct((N_IDX_HEADS, 3 * D_IDX, s), BF16),
            jax.ShapeDtypeStruct((N_IDX_HEADS, s), F32),
        ],
        scratch_shapes=[pltpu.VMEM((n_pad, d), BF16)],
        compiler_params=_params("arbitrary"),
        name="idx_proj",
    )(x, w_in_t, g_kv)


def _flip_key(v):
    return v ^ ((v >> 31) & 0x7FFFFFFF)


def _key_to_f32(key):
    return pltpu.bitcast(_flip_key(key), F32)


def _unrolled_loop(n, body, unroll):
    start = 0
    while unroll >= 1:
        def trip(i, carry, start=start, unroll=unroll):
            for r in range(unroll):
                body(start + unroll * i + r)
            return carry

        trips = (n - start) // unroll
        lax.fori_loop(0, trips, trip, 0)
        start = start + unroll * trips
        unroll //= 2


def _attn_kernel(qit_ref, wit_ref, kidx_ref, qt_ref, kv_ref, kvt_ref, wuvt_ref, yt_ref,
                 score_ref, ext_ref, qi_s, q_s, w_s, m_ref, l_ref, acc_ref, s_buf, p_buf, a_buf,
                 *, topk):
    b = pl.program_id(0)
    n_heads, d_lat, tq = qt_ref.shape
    n_kt_all, tk, _ = kidx_ref.shape
    n_groups, _, gw = q_s.shape
    hg = gw // tq
    n_kt = ((b + 1) * tq + tk - 1) // tk
    kpos = lax.broadcasted_iota(I32, (tk, tq), 0)
    qpos = b * tq + lax.broadcasted_iota(I32, (tk, tq), 1)

    for h in range(n_heads):
        g, c = divmod(h, hg)
        qi_s[g, :, c * tq:(c + 1) * tq] = qit_ref[h]
        q_s[g, :, c * tq:(c + 1) * tq] = qt_ref[h]
        w_s[g, :, c * tq:(c + 1) * tq] = wit_ref[h:h + 1, :]

    def score_tile(j):
        kidx = kidx_ref[j]
        score = jnp.zeros((tk, tq), F32)
        for g in range(n_groups):
            rel = jnp.maximum(_dot(kidx, qi_s[g]), 0.0) * w_s[g]
            for c in range(hg):
                score = score + rel[:, c * tq:(c + 1) * tq]
        score = jnp.where(kpos + j * tk <= qpos, score, -jnp.inf)
        score_ref[j] = score
        slabs = score.reshape(tk // fold, fold, tq)
        top, bot = ext_ref[0], ext_ref[1]
        for r in range(tk // fold):
            top = jnp.maximum(top, slabs[r])
            bot = jnp.minimum(bot, jnp.where(slabs[r] > -jnp.inf, slabs[r], jnp.inf))
        ext_ref[0], ext_ref[1] = top, bot

    fold = 8 * SUBLANES
    ext_ref[0] = jnp.full((fold, tq), -jnp.inf, F32)
    ext_ref[1] = jnp.full((fold, tq), jnp.inf, F32)
    _unrolled_loop(n_kt, score_tile, TILE_UNROLL)

    kf = float(topk)

    n_causal = (qpos[0:1, :] + 1).astype(F32)

    top, bot = ext_ref[0], ext_ref[1]
    rep = lambda v: jnp.broadcast_to(v, (SUBLANES, tq))
    short = n_causal < kf
    lo0 = rep(jnp.where(short, KEY_NEG_INF,
                        _flip_key(pltpu.bitcast(jnp.min(bot, axis=0, keepdims=True), I32))))
    hi0 = rep(_flip_key(pltpu.bitcast(jnp.max(top, axis=0, keepdims=True), I32)) + 1)
    c_lo0 = rep(jnp.where(short, 0.0, n_causal))

    def unsettled(lo, hi, c_lo):
        return jnp.max(jnp.where((c_lo > kf) & (hi - 1 > lo), 1.0, 0.0))

    def search(tiles):
        def run():
            def count_ge(cand):
                cnt = jnp.zeros((fold, tq), F32)
                for j in range(tiles):
                    hit = jnp.where(score_ref[j] >= cand, 1.0, 0.0)
                    cnt = cnt + jnp.sum(hit.reshape(tk // fold, fold, tq), axis=0)
                return jnp.sum(cnt, axis=0, keepdims=True)

            def probe(lo, hi, c_lo, by_key):
                live = (c_lo > kf) & (hi - 1 > lo)
                key_mid = (lo >> 1) + (hi >> 1) + (lo & hi & 1)
                val_mid = 0.5 * (_key_to_f32(lo) + _key_to_f32(hi - 1))
                pk = jnp.where(by_key, key_mid, _flip_key(pltpu.bitcast(val_mid, I32)))
                pk = jnp.minimum(jnp.maximum(pk, lo + 1), hi - 1)
                pk = jnp.where(live, pk, lo)
                c = rep(count_ge(_key_to_f32(jnp.maximum(pk, KEY_NEG_INF))[0:1]))
                up = live & (c >= kf)
                down = live & (c < kf)
                return jnp.where(up, pk, lo), jnp.where(down, pk, hi), jnp.where(up, c, c_lo)

            def probes(state):
                lo, hi, c_lo, _, trip = state
                lo, hi, c_lo = probe(lo, hi, c_lo, False)
                lo, hi, c_lo = probe(lo, hi, c_lo, trip % 2 == 1)
                flag = lax.cond(trip >= UNCHECKED_TRIPS, lambda: unsettled(lo, hi, c_lo),
                                lambda: jnp.float32(1.0))
                return lo, hi, c_lo, flag, trip + 1

            def keep_going(state):
                return jnp.logical_and(state[3] > 0.0, state[4] < 64)

            state = (lo0, hi0, c_lo0, unsettled(lo0, hi0, c_lo0), jnp.int32(0))
            lo, _, c_lo, _, _ = lax.while_loop(keep_going, probes, state)
            return lo[0:1], c_lo[0:1]
        return run

    thr_key, n_ge = lax.switch(n_kt - 1, [search(k) for k in range(1, n_kt_all + 1)])
    thr = _key_to_f32(jnp.maximum(thr_key, KEY_NEG_INF))

    @pl.when(jnp.max(jnp.broadcast_to(n_ge, (SUBLANES, tq))) > kf)
    def _():
        def count(pred):
            def body(j, cnt):
                hit = jnp.where(pred(kpos + j * tk, score_ref[j]), 1.0, 0.0)
                return cnt + jnp.sum(hit.reshape(tk // fold, fold, tq), axis=0)

            cnt = lax.fori_loop(0, n_kt, body, jnp.zeros((fold, tq), F32))
            return jnp.sum(cnt, axis=0, keepdims=True)

        need = kf - count(lambda pos, sc: sc > thr)
        n_pos_bits = (n_kt_all * tk - 1).bit_length()

        def pos_step(t, cut):
            cand = cut | jnp.left_shift(jnp.int32(1), n_pos_bits - 1 - t)
            before = count(lambda pos, sc: (sc == thr) & (pos < cand))
            return jnp.where(before < need, cand, cut)

        cut = lax.fori_loop(0, n_pos_bits, pos_step, jnp.zeros((1, tq), I32))

        def drop_tile(j, carry):
            sc = score_ref[j]
            score_ref[j] = jnp.where((sc == thr) & (kpos + j * tk > cut), -jnp.inf, sc)
            return carry

        lax.fori_loop(0, n_kt, drop_tile, 0)

    def mask_tile(j, carry):
        keep = (score_ref[j] >= thr) & (kpos + j * tk <= qpos)
        score_ref[j] = jnp.where(keep, 0.0, NEG_BIG)
        return carry

    lax.fori_loop(0, n_kt, mask_tile, 0)

    m_ref[...] = jnp.full(m_ref.shape, NEG_BIG, F32)
    l_ref[...] = jnp.zeros(l_ref.shape, F32)
    acc_ref[...] = jnp.zeros(acc_ref.shape, F32)
    p_buf[...] = jnp.zeros(p_buf.shape, p_buf.dtype)
    a_buf[...] = jnp.ones(a_buf.shape, F32)

    def qk(j, g):
        bias = jnp.concatenate([score_ref[j]] * hg, axis=1)
        s_buf[g % 2] = _dot(kv_ref[j], q_s[g]) + bias

    def softmax(g):
        s = s_buf[g % 2]
        m_old = m_ref[g]
        m_new = jnp.maximum(m_old, jnp.max(s, axis=0, keepdims=True))
        p = jnp.exp2(s - m_new)
        alpha = jnp.exp2(m_old - m_new)
        l_ref[g] = alpha * l_ref[g] + jnp.sum(p, axis=0, keepdims=True)
        a_buf[g % 2] = alpha
        p_buf[g % 2] = p.astype(p_buf.dtype)
        m_ref[g] = m_new

    def pv(j, g):
        acc_ref[g] = a_buf[g % 2] * acc_ref[g] + _dot(kvt_ref[j], p_buf[g % 2])

    last = n_groups - 1
    qk(0, 0)

    def attn_tile(j):
        for g in range(n_groups):
            if g < last:
                qk(j, g + 1)
            else:
                qk(jnp.minimum(j + 1, n_kt - 1), 0)
            if g > 0:
                pv(j, g - 1)
            else:
                pv(jnp.maximum(j - 1, 0), last)
            softmax(g)

    _unrolled_loop(n_kt, attn_tile, TILE_UNROLL)
    pv(n_kt - 1, last)

    for h in range(n_heads):
        g, c = divmod(h, hg)
        lanes = slice(c * tq, (c + 1) * tq)
        o_t = (acc_ref[g, :, lanes] / l_ref[g, :, lanes]).astype(wuvt_ref.dtype)
        yt_ref[h * D_VHEAD:(h + 1) * D_VHEAD, :] = _dot(wuvt_ref[h], o_t)


def _dsa_attention(qi_t, wi_t, kidx, q_t, kv, kvt, w_uv_t, topk, tq=128, heads_per_group=2):
    n_heads, d_lat, s = q_t.shape
    n_kt, tk, dk = kidx.shape
    assert tk % tq == 0
    n_groups = n_heads // heads_per_group
    gw = heads_per_group * tq
    const3 = lambda b: (0, 0, 0)
    return pl.pallas_call(
        functools.partial(_attn_kernel, topk=topk),
        grid=(s // tq,),
        in_specs=[
            pl.BlockSpec((N_IDX_HEADS, dk, tq), lambda b: (0, 0, b)),
            pl.BlockSpec((N_IDX_HEADS, tq), lambda b: (0, b)),
            pl.BlockSpec((n_kt, tk, dk), const3),
            pl.BlockSpec((n_heads, d_lat, tq), lambda b: (0, 0, b)),
            pl.BlockSpec((n_kt, tk, d_lat), const3),
            pl.BlockSpec((n_kt, d_lat, tk), const3),
            pl.BlockSpec((n_heads, D_VHEAD, d_lat), const3),
        ],
        out_specs=pl.BlockSpec((n_heads * D_VHEAD, tq), lambda b: (0, b)),
        out_shape=jax.ShapeDtypeStruct((n_heads * D_VHEAD, s), F32),
        scratch_shapes=[
            pltpu.VMEM((n_kt, tk, tq), F32),
            pltpu.VMEM((2, 8 * SUBLANES, tq), F32),
            pltpu.VMEM((n_groups, dk, gw), BF16),
            pltpu.VMEM((n_groups, d_lat, gw), BF16),
            pltpu.VMEM((n_groups, 1, gw), F32),
            pltpu.VMEM((n_groups, 1, gw), F32),
            pltpu.VMEM((n_groups, 1, gw), F32),
            pltpu.VMEM((n_groups, d_lat, gw), F32),
            pltpu.VMEM((2, tk, gw), F32),
            pltpu.VMEM((2, tk, gw), BF16),
            pltpu.VMEM((2, 1, gw), F32),
        ],
        compiler_params=_params("parallel"),
        name="dsa_attention",
    )(qi_t, wi_t, kidx, q_t, kv, kvt, w_uv_t)


def _merge_kernel(x_ref, z_ref, wgc_ref, wga_ref, wco_ref, yat_ref, o_ref, w_s, wco_s):
    @pl.when(pl.program_id(1) == 0)
    def _():
        w_s[0] = wgc_ref[...].astype(w_s.dtype)
        w_s[1] = wga_ref[...].astype(w_s.dtype)
        wco_s[...] = wco_ref[...].astype(wco_s.dtype)

    x = x_ref[...]
    g_conv = _sigmoid(_dot_nt(x, w_s[0]))
    g_attn = _sigmoid(_dot_nt(x, w_s[1]))
    y_conv = _dot(z_ref[...], wco_s[...])
    o_ref[...] = (g_conv * y_conv + g_attn * yat_ref[...].T).astype(o_ref.dtype)


def _merge(x_bf, z, w_in_t, row0, w_co, y_attn_t, tm=512, tn=512):
    s, d = z.shape
    n = w_co.shape[1]
    gate_rows = lambda off: pl.BlockSpec(
        (pl.Element(tn), pl.Element(d)),
        lambda j, i: (pl.multiple_of(row0 + off + j * tn, SUBLANES), 0))
    return pl.pallas_call(
        _merge_kernel,
        grid=(n // tn, s // tm),
        in_specs=[
            pl.BlockSpec((tm, d), lambda j, i: (i, 0)),
            pl.BlockSpec((tm, d), lambda j, i: (i, 0)),
            gate_rows(0),
            gate_rows(n),
            pl.BlockSpec((d, tn), lambda j, i: (0, j)),
            pl.BlockSpec((tn, tm), lambda j, i: (j, i)),
        ],
        out_specs=pl.BlockSpec((tm, tn), lambda j, i: (i, j)),
        out_shape=jax.ShapeDtypeStruct((s, n), BF16),
        scratch_shapes=[pltpu.VMEM((2, tn, d), BF16), pltpu.VMEM((d, tn), BF16)],
        compiler_params=_params("parallel", "arbitrary"),
        name="merge",
    )(x_bf, z, w_in_t, w_in_t, w_co, y_attn_t)


def _cast_resident(w_ref, w_s):
    @pl.when(pl.program_id(0) == 0)
    def _():
        w_s[...] = w_ref[...].astype(w_s.dtype)


def _resident(shape):
    return pl.BlockSpec(shape, lambda i: (0,) * len(shape), pipeline_mode=pl.Buffered(1))


def _oproj_kernel(m_ref, w_ref, h_ref, g_ref, b_ref, o_ref, obf_ref, w_s, *, alpha):
    _cast_resident(w_ref, w_s)
    v = alpha * h_ref[...] + _dot(m_ref[...], w_s[...])
    out = _layer_norm(v, g_ref[...], b_ref[...])
    o_ref[...] = out
    obf_ref[...] = out.astype(obf_ref.dtype)


def _oproj(mixed, w_o, h, g, b, alpha, tm=512):
    s, d = h.shape
    row = pl.BlockSpec((tm, d), lambda i: (i, 0))
    vec = pl.BlockSpec((1, d), lambda i: (0, 0))
    return pl.pallas_call(
        functools.partial(_oproj_kernel, alpha=alpha),
        grid=(s // tm,),
        in_specs=[row, _resident((d, d)), row, vec, vec],
        out_specs=[row, row],
        out_shape=[jax.ShapeDtypeStruct((s, d), F32), jax.ShapeDtypeStruct((s, d), BF16)],
        scratch_shapes=[pltpu.VMEM((d, d), BF16)],
        compiler_params=_params("arbitrary"),
        name="oproj_ln",
    )(mixed, w_o, h, g, b)


def _ffn_kernel(hbf_ref, wu_ref, wd_ref, h_ref, g_ref, b_ref, o_ref, obf_ref, acc_ref, *, alpha):
    f = pl.program_id(1)

    @pl.when(f == 0)
    def _():
        acc_ref[...] = jnp.zeros(acc_ref.shape, F32)

    a = jnp.maximum(_dot(hbf_ref[...], wu_ref[...]), 0.0)
    acc_ref[...] += _dot((a * a).astype(wd_ref.dtype), wd_ref[...])

    @pl.when(f == pl.num_programs(1) - 1)
    def _():
        out = _layer_norm(alpha * h_ref[...] + acc_ref[...], g_ref[...], b_ref[...])
        o_ref[...] = out
        obf_ref[...] = out.astype(obf_ref.dtype)


def _ffn(h_bf, w_up, w_down, h, g, b, alpha, tm=512, tf=1024):
    s, d = h.shape
    dff = w_up.shape[1]
    row = pl.BlockSpec((tm, d), lambda i, f: (i, 0))
    vec = pl.BlockSpec((1, d), lambda i, f: (0, 0))
    return pl.pallas_call(
        functools.partial(_ffn_kernel, alpha=alpha),
        grid=(s // tm, dff // tf),
        in_specs=[row, pl.BlockSpec((d, tf), lambda i, f: (0, f)),
                  pl.BlockSpec((tf, d), lambda i, f: (f, 0)), row, vec, vec],
        out_specs=[row, row],
        out_shape=[jax.ShapeDtypeStruct((s, d), F32), jax.ShapeDtypeStruct((s, d), BF16)],
        scratch_shapes=[pltpu.VMEM((tm, d), F32)],
        compiler_params=_params("parallel", "arbitrary"),
        name="ffn_ln",
    )(h_bf, w_up, w_down, h, g, b)


def _ple_kernel(hbf_ref, wg_ref, p_ref, wp_ref, h_ref, g_ref, b_ref, o_ref, wg_s, *, alpha):
    _cast_resident(wg_ref, wg_s)
    gate = _sigmoid(_dot(hbf_ref[...], wg_s[...]))
    pe = gate * _dot(p_ref[...], wp_ref[...])
    o_ref[...] = _layer_norm(alpha * h_ref[...] + pe, g_ref[...], b_ref[...])


def _ple(h_bf, w_pg, p_bf, w_p, h, g, b, alpha, tm=512):
    s, d = h.shape
    dp = p_bf.shape[1]
    row = pl.BlockSpec((tm, d), lambda i: (i, 0))
    vec = pl.BlockSpec((1, d), lambda i: (0, 0))
    return pl.pallas_call(
        functools.partial(_ple_kernel, alpha=alpha),
        grid=(s // tm,),
        in_specs=[row, _resident((d, d)),
                  pl.BlockSpec((tm, dp), lambda i: (i, 0)),
                  pl.BlockSpec((dp, d), lambda i: (0, 0)), row, vec, vec],
        out_specs=row,
        out_shape=jax.ShapeDtypeStruct((s, d), F32),
        scratch_shapes=[pltpu.VMEM((d, d), BF16)],
        compiler_params=_params("arbitrary"),
        name="ple_ln",
    )(h_bf, w_pg, p_bf, w_p, h, g, b)


def _layer(h, p, w_in, conv_w, w_conv_out, g_kv, w_uv, w_o, ln1_g, ln1_b, w_up, w_down,
           ln2_g, ln2_b, w_ple_gate, w_ple, ln3_g, ln3_b, alpha, tk=512):
    s, d = h.shape
    d_conv = conv_w.shape[1]
    nq = N_IDX_HEADS * D_IDX
    c_qlat = 3 * d_conv
    c_ckv = c_qlat + N_HEADS * D_LATENT
    c_gates = c_ckv + D_LATENT + nq + D_IDX + N_IDX_HEADS
    topk = min(TOPK_MAX, s // 4)
    n_kt = s // tk

    w_in_t = jnp.transpose(w_in)

    h_bf, ckv, kidx, qi_t, wi_t = _idx_proj(h, w_in_t, c_ckv, g_kv.reshape(1, D_LATENT))
    z, w_down_bf = _conv_branch(h_bf, w_in_t, conv_w, d_conv, w_down)
    q_t, w_up_bf = _qlat_proj(h_bf, w_in_t, c_qlat, w_up)

    kidx = kidx.reshape(n_kt, tk, 3 * D_IDX)
    kv = ckv.reshape(n_kt, tk, D_LATENT)
    kvt = jnp.transpose(kv, (0, 2, 1))

    w_uv_t = jnp.transpose(w_uv, (0, 2, 1)).astype(BF16)
    y_attn_t = _dsa_attention(qi_t, wi_t, kidx, q_t, kv, kvt, w_uv_t, topk)

    mixed = _merge(h_bf, z, w_in_t, c_gates, w_conv_out, y_attn_t)
    h1, h1_bf = _oproj(mixed, w_o, h, ln1_g.reshape(1, d), ln1_b.reshape(1, d), alpha)
    h2, h2_bf = _ffn(h1_bf, w_up_bf, w_down_bf, h1,
                     ln2_g.reshape(1, d), ln2_b.reshape(1, d), alpha)
    return _ple(h2_bf, w_ple_gate, p.astype(BF16), w_ple.astype(BF16), h2,
                ln3_g.reshape(1, d), ln3_b.reshape(1, d), alpha)


def kernel(x, p, w_in, conv_w, w_conv_out, g_kv, w_uv, w_o, ln1_g, ln1_b, w_up, w_down,
           ln2_g, ln2_b, w_ple_gate, w_ple, ln3_g, ln3_b):
    depth = w_in.shape[0]
    alpha = (2.0 * depth) ** 0.25
    batch = x.shape[0]
    outs = []
    for bi in range(batch):
        h = x[bi]
        for i in range(depth):
            h = _layer(h, p[i, bi], w_in[i], conv_w[i], w_conv_out[i], g_kv[i], w_uv[i], w_o[i],
                       ln1_g[i], ln1_b[i], w_up[i], w_down[i], ln2_g[i], ln2_b[i],
                       w_ple_gate[i], w_ple[i], ln3_g[i], ln3_b[i], alpha)
        outs.append(h)
    return jnp.stack(outs, axis=0)
```

```python
import functools

import jax
import jax.numpy as jnp
from jax import lax
from jax.experimental import pallas as pl
from jax.experimental.pallas import tpu as pltpu

F32 = jnp.float32
BF16 = jnp.bfloat16
I32 = jnp.int32

N_HEADS = 16
D_LATENT = 256
D_VHEAD = 128
N_IDX_HEADS = 16
D_IDX = 64
TOPK_MAX = 256
CONV_WIDTH = 3
LN_EPS = 1e-5
RMS_EPS = 1e-6
ATTN_SCALE = D_LATENT ** -0.5
IDX_SCALE = (D_IDX ** -0.5) * (N_IDX_HEADS ** -0.5)

VMEM_LIMIT_BYTES = 56 * 1024 * 1024
SUBLANES = 8
LANES = 128

INT_MIN = -(2 ** 31)
KEY_NEG_INF = INT_MIN + 0x7FFFFF
NEG_BIG = -1e30
TILE_UNROLL = 4
UNCHECKED_TRIPS = 8
LOG2E = 1.4426950408889634


def _params(*sem):
    return pltpu.CompilerParams(dimension_semantics=sem, vmem_limit_bytes=VMEM_LIMIT_BYTES)


def _dot(a, b):
    return jnp.dot(a, b, preferred_element_type=F32)


def _dot_nt(a, b):
    return lax.dot_general(a, b, (((1,), (1,)), ((), ())), preferred_element_type=F32)


def _sigmoid(v):
    return 1.0 / (1.0 + jnp.exp(-v))


def _layer_norm(v, g, b):
    mu = jnp.mean(v, axis=-1, keepdims=True)
    c = v - mu
    var = jnp.mean(c * c, axis=-1, keepdims=True)
    return c * lax.rsqrt(var + LN_EPS) * g + b


def _split_hi_lo(v):
    hi = v.astype(BF16)
    lo = (v - hi.astype(F32)).astype(BF16)
    return hi, lo


def _side_cast_specs(w, grid):
    n_steps = grid[0] * grid[1]
    rows, cols = w.shape
    chunk = rows // n_steps
    assert chunk * n_steps == rows and chunk % (2 * SUBLANES) == 0
    spec = pl.BlockSpec((chunk, cols), lambda j, i: (j * grid[1] + i, 0))
    return spec, spec, jax.ShapeDtypeStruct(w.shape, BF16)


def _conv_kernel(x_ref, wb_ref, wc_ref, wu_ref, cw_ref, side_ref, z_ref, side_bf_ref, cu_ref, w_s):
    side_bf_ref[...] = side_ref[...].astype(side_bf_ref.dtype)
    i = pl.program_id(1)
    tm = x_ref.shape[0]
    halo = SUBLANES

    @pl.when(i == 0)
    def _():
        w_s[0] = wb_ref[...].astype(w_s.dtype)
        w_s[1] = wc_ref[...].astype(w_s.dtype)
        w_s[2] = wu_ref[...].astype(w_s.dtype)
        cu_ref[0:halo, :] = jnp.zeros((halo, cu_ref.shape[1]), F32)

    @pl.when(i > 0)
    def _():
        cu_ref[0:halo, :] = cu_ref[tm:tm + halo, :]

    x = x_ref[...]
    cu_ref[halo:halo + tm, :] = _dot_nt(x, w_s[1]) * _dot_nt(x, w_s[2])
    cw = cw_ref[...]
    v = (cw[0:1, :] * cu_ref[halo - 2:halo - 2 + tm, :]
         + cw[1:2, :] * cu_ref[halo - 1:halo - 1 + tm, :]
         + cw[2:3, :] * cu_ref[halo:halo + tm, :])
    z_ref[...] = (_dot_nt(x, w_s[0]) * v).astype(z_ref.dtype)


def _conv_branch(x_bf, w_in_t, conv_w, d_conv, w_side, tm=1024, tn=512):
    s, d = x_bf.shape
    nj = d_conv // tn
    grid = (nj, s // tm)
    side_in, side_out, side_shape = _side_cast_specs(w_side, grid)
    return pl.pallas_call(
        _conv_kernel,
        grid=grid,
        in_specs=[
            pl.BlockSpec((tm, d), lambda j, i: (i, 0)),
            pl.BlockSpec((tn, d), lambda j, i: (j, 0)),
            pl.BlockSpec((tn, d), lambda j, i: (nj + j, 0)),
            pl.BlockSpec((tn, d), lambda j, i: (2 * nj + j, 0)),
            pl.BlockSpec((CONV_WIDTH, tn), lambda j, i: (0, j)),
            side_in,
        ],
        out_specs=[pl.BlockSpec((tm, tn), lambda j, i: (i, j)), side_out],
        out_shape=[jax.ShapeDtypeStruct((s, d_conv), BF16), side_shape],
        scratch_shapes=[pltpu.VMEM((tm + SUBLANES, tn), F32),
                        pltpu.VMEM((3, tn, d), BF16)],
        compiler_params=_params("parallel", "arbitrary"),
        name="conv_branch",
    )(x_bf, w_in_t, w_in_t, w_in_t, conv_w, w_side)


def _qlat_kernel(x_ref, w_ref, side_ref, q_ref, side_bf_ref, w_s):
    side_bf_ref[...] = side_ref[...].astype(side_bf_ref.dtype)

    @pl.when(pl.program_id(1) == 0)
    def _():
        w_s[...] = w_ref[...].astype(w_s.dtype)

    acc = _dot_nt(w_s[...], x_ref[...]) * (ATTN_SCALE * LOG2E)
    q_ref[...] = acc.astype(q_ref.dtype).reshape(q_ref.shape)


def _qlat_proj(x_bf, w_in_t, row0, w_side, tm=1024, heads_per_step=4):
    s, d = x_bf.shape
    tn = heads_per_step * D_LATENT
    j0 = row0 // tn
    assert j0 * tn == row0
    grid = (N_HEADS // heads_per_step, s // tm)
    side_in, side_out, side_shape = _side_cast_specs(w_side, grid)
    return pl.pallas_call(
        _qlat_kernel,
        grid=grid,
        in_specs=[
            pl.BlockSpec((tm, d), lambda j, i: (i, 0)),
            pl.BlockSpec((tn, d), lambda j, i: (j0 + j, 0)),
            side_in,
        ],
        out_specs=[pl.BlockSpec((heads_per_step, D_LATENT, tm), lambda j, i: (j, 0, i)), side_out],
        out_shape=[jax.ShapeDtypeStruct((N_HEADS, D_LATENT, s), BF16), side_shape],
        scratch_shapes=[pltpu.VMEM((tn, d), BF16)],
        compiler_params=_params("parallel", "arbitrary"),
        name="qlat_proj",
    )(x_bf, w_in_t, w_side)


def _idx_kernel(x_ref, w_ref, g_ref, xbf_ref, ckv_ref, kidx_ref, qit_ref, wit_ref, w_s):
    @pl.when(pl.program_id(0) == 0)
    def _():
        w_s[...] = w_ref[...].astype(w_s.dtype)

    x = x_ref[...].astype(xbf_ref.dtype)
    xbf_ref[...] = x
    nq = N_IDX_HEADS * D_IDX
    r_q, r_k, r_w = D_LATENT, D_LATENT + nq, D_LATENT + nq + D_IDX
    c = _dot_nt(x, w_s[0:r_q])
    ms = jnp.mean(c * c, axis=-1, keepdims=True)
    ckv_ref[...] = (c * lax.rsqrt(ms + RMS_EPS) * g_ref[...]).astype(ckv_ref.dtype)
    k_hi, k_lo = _split_hi_lo(_dot_nt(x, w_s[r_k:r_w]))
    kidx_ref[...] = jnp.concatenate([k_hi, k_hi, k_lo], axis=1)
    q_t = _dot_nt(w_s[r_q:r_k], x)
    q_hi, q_lo = _split_hi_lo(q_t.reshape(N_IDX_HEADS, D_IDX, q_t.shape[1]))
    qit_ref[...] = jnp.concatenate([q_hi, q_lo, q_hi], axis=1)
    wit_ref[...] = _dot_nt(w_s[r_w:r_w + N_IDX_HEADS], x) * IDX_SCALE


def _idx_proj(x, w_in_t, row0, g_kv, tm=512):
    s, d = x.shape
    nq = N_IDX_HEADS * D_IDX
    n = D_LATENT + nq + D_IDX + N_IDX_HEADS
    n_pad = -(-n // LANES) * LANES
    return pl.pallas_call(
        _idx_kernel,
        grid=(s // tm,),
        in_specs=[
            pl.BlockSpec((tm, d), lambda i: (i, 0)),
            pl.BlockSpec((pl.Element(n_pad), pl.Element(d)), lambda i: (row0, 0)),
            pl.BlockSpec((1, D_LATENT), lambda i: (0, 0)),
        ],
        out_specs=[
            pl.BlockSpec((tm, d), lambda i: (i, 0)),
            pl.BlockSpec((tm, D_LATENT), lambda i: (i, 0)),
            pl.BlockSpec((tm, 3 * D_IDX), lambda i: (i, 0)),
            pl.BlockSpec((N_IDX_HEADS, 3 * D_IDX, tm), lambda i: (0, 0, i)),
            pl.BlockSpec((N_IDX_HEADS, tm), lambda i: (0, i)),
        ],
        out_shape=[
            jax.ShapeDtypeStruct((s, d), BF16),
            jax.ShapeDtypeStruct((s, D_LATENT), BF16),
            jax.ShapeDtypeStruct((s, 3 * D_IDX), BF16),
            jax.ShapeDtypeStruct((N_IDX_HEADS, 3 * D_IDX, s), BF16),
            jax.ShapeDtypeStruct((N_IDX_HEADS, s), F32),
        ],
        scratch_shapes=[pltpu.VMEM((n_pad, d), BF16)],
        compiler_params=_params("arbitrary"),
        name="idx_proj",
    )(x, w_in_t, g_kv)


def _flip_key(v):
    return v ^ ((v >> 31) & 0x7FFFFFFF)


def _key_to_f32(key):
    return pltpu.bitcast(_flip_key(key), F32)


def _unrolled_loop(n, body, unroll):
    start = 0
    while unroll >= 1:
        def trip(i, carry, start=start, unroll=unroll):
            for r in range(unroll):
                body(start + unroll * i + r)
            return carry

        trips = (n - start) // unroll
        lax.fori_loop(0, trips, trip, 0)
        start = start + unroll * trips
        unroll //= 2


def _attn_kernel(qit_ref, wit_ref, kidx_ref, qt_ref, kv_ref, kvt_ref, wuvt_ref, yt_ref,
                 score_ref, ext_ref, qi_s, q_s, w_s, m_ref, l_ref, acc_ref, s_buf, p_buf, a_buf,
                 *, topk):
    b = pl.program_id(0)
    n_heads, d_lat, tq = qt_ref.shape
    n_kt_all, tk, _ = kidx_ref.shape
    n_groups, _, gw = q_s.shape
    hg = gw // tq
    n_kt = ((b + 1) * tq + tk - 1) // tk
    kpos = lax.broadcasted_iota(I32, (tk, tq), 0)
    qpos = b * tq + lax.broadcasted_iota(I32, (tk, tq), 1)

    for h in range(n_heads):
        g, c = divmod(h, hg)
        qi_s[g, :, c * tq:(c + 1) * tq] = qit_ref[h]
        q_s[g, :, c * tq:(c + 1) * tq] = qt_ref[h]
        w_s[g, :, c * tq:(c + 1) * tq] = wit_ref[h:h + 1, :]

    def score_tile(j):
        kidx = kidx_ref[j]
        score = jnp.zeros((tk, tq), F32)
        for g in range(n_groups):
            rel = jnp.maximum(_dot(kidx, qi_s[g]), 0.0) * w_s[g]
            for c in range(hg):
                score = score + rel[:, c * tq:(c + 1) * tq]
        score = jnp.where(kpos + j * tk <= qpos, score, -jnp.inf)
        score_ref[j] = score
        slabs = score.reshape(tk // fold, fold, tq)
        top, bot = ext_ref[0], ext_ref[1]
        for r in range(tk // fold):
            top = jnp.maximum(top, slabs[r])
            bot = jnp.minimum(bot, jnp.where(slabs[r] > -jnp.inf, slabs[r], jnp.inf))
        ext_ref[0], ext_ref[1] = top, bot

    fold = 8 * SUBLANES
    ext_ref[0] = jnp.full((fold, tq), -jnp.inf, F32)
    ext_ref[1] = jnp.full((fold, tq), jnp.inf, F32)
    _unrolled_loop(n_kt, score_tile, TILE_UNROLL)

    kf = float(topk)

    n_causal = (qpos[0:1, :] + 1).astype(F32)

    top, bot = ext_ref[0], ext_ref[1]
    rep = lambda v: jnp.broadcast_to(v, (SUBLANES, tq))
    short = n_causal < kf
    lo0 = rep(jnp.where(short, KEY_NEG_INF,
                        _flip_key(pltpu.bitcast(jnp.min(bot, axis=0, keepdims=True), I32))))
    hi0 = rep(_flip_key(pltpu.bitcast(jnp.max(top, axis=0, keepdims=True), I32)) + 1)
    c_lo0 = rep(jnp.where(short, 0.0, n_causal))

    def status(lo, hi, c_lo):
        surplus = c_lo > kf
        return jnp.max(jnp.where(surplus & (hi - 1 > lo), 2.0, jnp.where(surplus, 1.0, 0.0)))

    def search(tiles):
        def run():
            def count_ge(cand):
                cnt = jnp.zeros((fold, tq), F32)
                for j in range(tiles):
                    hit = jnp.where(score_ref[j] >= cand, 1.0, 0.0)
                    cnt = cnt + jnp.sum(hit.reshape(tk // fold, fold, tq), axis=0)
                return jnp.sum(cnt, axis=0, keepdims=True)

            def probe(lo, hi, c_lo, by_key):
                live = (c_lo > kf) & (hi - 1 > lo)
                key_mid = (lo >> 1) + (hi >> 1) + (lo & hi & 1)
                val_mid = 0.5 * (_key_to_f32(lo) + _key_to_f32(hi - 1))
                pk = jnp.where(by_key, key_mid, _flip_key(pltpu.bitcast(val_mid, I32)))
                pk = jnp.minimum(jnp.maximum(pk, lo + 1), hi - 1)
                pk = jnp.where(live, pk, lo)
                c = rep(count_ge(_key_to_f32(jnp.maximum(pk, KEY_NEG_INF))[0:1]))
                up = live & (c >= kf)
                down = live & (c < kf)
                return jnp.where(up, pk, lo), jnp.where(down, pk, hi), jnp.where(up, c, c_lo)

            def probes(state):
                lo, hi, c_lo, _, trip = state
                lo, hi, c_lo = probe(lo, hi, c_lo, False)
                lo, hi, c_lo = probe(lo, hi, c_lo, trip % 2 == 1)
                flag = lax.cond(trip >= UNCHECKED_TRIPS, lambda: status(lo, hi, c_lo),
                                lambda: jnp.float32(2.0))
                return lo, hi, c_lo, flag, trip + 1

            def keep_going(state):
                return jnp.logical_and(state[3] > 1.5, state[4] < 64)

            state = (lo0, hi0, c_lo0, status(lo0, hi0, c_lo0), jnp.int32(0))
            lo, _, _, flag, _ = lax.while_loop(keep_going, probes, state)
            return lo[0:1], flag
        return run

    thr_key, status_out = lax.switch(n_kt - 1, [search(k) for k in range(1, n_kt_all + 1)])
    thr = _key_to_f32(jnp.maximum(thr_key, KEY_NEG_INF))

    @pl.when(status_out > 0.5)
    def _():
        def count(pred):
            def body(j, cnt):
                hit = jnp.where(pred(kpos + j * tk, score_ref[j]), 1.0, 0.0)
                return cnt + jnp.sum(hit.reshape(tk // fold, fold, tq), axis=0)

            cnt = lax.fori_loop(0, n_kt, body, jnp.zeros((fold, tq), F32))
            return jnp.sum(cnt, axis=0, keepdims=True)

        need = kf - count(lambda pos, sc: sc > thr)
        n_pos_bits = (n_kt_all * tk - 1).bit_length()

        def pos_step(t, cut):
            cand = cut | jnp.left_shift(jnp.int32(1), n_pos_bits - 1 - t)
            before = count(lambda pos, sc: (sc == thr) & (pos < cand))
            return jnp.where(before < need, cand, cut)

        cut = lax.fori_loop(0, n_pos_bits, pos_step, jnp.zeros((1, tq), I32))

        def drop_tile(j, carry):
            sc = score_ref[j]
            score_ref[j] = jnp.where((sc == thr) & (kpos + j * tk > cut), -jnp.inf, sc)
            return carry

        lax.fori_loop(0, n_kt, drop_tile, 0)

    def mask_tile(j, carry):
        keep = (score_ref[j] >= thr) & (kpos + j * tk <= qpos)
        score_ref[j] = jnp.where(keep, 0.0, NEG_BIG)
        return carry

    lax.fori_loop(0, n_kt, mask_tile, 0)

    m_ref[...] = jnp.full(m_ref.shape, NEG_BIG, F32)
    l_ref[...] = jnp.zeros(l_ref.shape, F32)
    acc_ref[...] = jnp.zeros(acc_ref.shape, F32)
    p_buf[...] = jnp.zeros(p_buf.shape, p_buf.dtype)
    a_buf[...] = jnp.ones(a_buf.shape, F32)

    def qk(j, g):
        bias = jnp.concatenate([score_ref[j]] * hg, axis=1)
        s_buf[g % 2] = _dot(kv_ref[j], q_s[g]) + bias

    def softmax(g):
        s = s_buf[g % 2]
        m_old = m_ref[g]
        m_new = jnp.maximum(m_old, jnp.max(s, axis=0, keepdims=True))
        p = jnp.exp2(s - m_new)
        alpha = jnp.exp2(m_old - m_new)
        l_ref[g] = alpha * l_ref[g] + jnp.sum(p, axis=0, keepdims=True)
        a_buf[g % 2] = alpha
        p_buf[g % 2] = p.astype(p_buf.dtype)
        m_ref[g] = m_new

    def pv(j, g):
        acc_ref[g] = a_buf[g % 2] * acc_ref[g] + _dot(kvt_ref[j], p_buf[g % 2])

    last = n_groups - 1
    qk(0, 0)

    def attn_tile(j):
        for g in range(n_groups):
            if g < last:
                qk(j, g + 1)
            else:
                qk(jnp.minimum(j + 1, n_kt - 1), 0)
            if g > 0:
                pv(j, g - 1)
            else:
                pv(jnp.maximum(j - 1, 0), last)
            softmax(g)

    _unrolled_loop(n_kt, attn_tile, TILE_UNROLL)
    pv(n_kt - 1, last)

    for h in range(n_heads):
        g, c = divmod(h, hg)
        lanes = slice(c * tq, (c + 1) * tq)
        inv_l = 1.0 / l_ref[g, :, lanes]
        o_t = (acc_ref[g, :, lanes] * inv_l).astype(wuvt_ref.dtype)
        yt_ref[h * D_VHEAD:(h + 1) * D_VHEAD, :] = _dot(wuvt_ref[h], o_t)


def _dsa_attention(qi_t, wi_t, kidx, q_t, kv, kvt, w_uv_t, topk, tq=128, heads_per_group=2):
    n_heads, d_lat, s = q_t.shape
    n_kt, tk, dk = kidx.shape
    assert tk % tq == 0
    n_groups = n_heads // heads_per_group
    gw = heads_per_group * tq
    const3 = lambda b: (0, 0, 0)
    return pl.pallas_call(
        functools.partial(_attn_kernel, topk=topk),
        grid=(s // tq,),
        in_specs=[
            pl.BlockSpec((N_IDX_HEADS, dk, tq), lambda b: (0, 0, b)),
            pl.BlockSpec((N_IDX_HEADS, tq), lambda b: (0, b)),
            pl.BlockSpec((n_kt, tk, dk), const3),
            pl.BlockSpec((n_heads, d_lat, tq), lambda b: (0, 0, b)),
            pl.BlockSpec((n_kt, tk, d_lat), const3),
            pl.BlockSpec((n_kt, d_lat, tk), const3),
            pl.BlockSpec((n_heads, D_VHEAD, d_lat), const3),
        ],
        out_specs=pl.BlockSpec((n_heads * D_VHEAD, tq), lambda b: (0, b)),
        out_shape=jax.ShapeDtypeStruct((n_heads * D_VHEAD, s), F32),
        scratch_shapes=[
            pltpu.VMEM((n_kt, tk, tq), F32),
            pltpu.VMEM((2, 8 * SUBLANES, tq), F32),
            pltpu.VMEM((n_groups, dk, gw), BF16),
            pltpu.VMEM((n_groups, d_lat, gw), BF16),
            pltpu.VMEM((n_groups, 1, gw), F32),
            pltpu.VMEM((n_groups, 1, gw), F32),
            pltpu.VMEM((n_groups, 1, gw), F32),
            pltpu.VMEM((n_groups, d_lat, gw), F32),
            pltpu.VMEM((2, tk, gw), F32),
            pltpu.VMEM((2, tk, gw), BF16),
            pltpu.VMEM((2, 1, gw), F32),
        ],
        compiler_params=_params("parallel"),
        name="dsa_attention",
    )(qi_t, wi_t, kidx, q_t, kv, kvt, w_uv_t)


def _merge_kernel(x_ref, z_ref, wgc_ref, wga_ref, wco_ref, yat_ref, o_ref, w_s, wco_s):
    @pl.when(pl.program_id(1) == 0)
    def _():
        w_s[0] = wgc_ref[...].astype(w_s.dtype)
        w_s[1] = wga_ref[...].astype(w_s.dtype)
        wco_s[...] = wco_ref[...].astype(wco_s.dtype)

    x = x_ref[...]
    g_conv = _sigmoid(_dot_nt(x, w_s[0]))
    g_attn = _sigmoid(_dot_nt(x, w_s[1]))
    y_conv = _dot(z_ref[...], wco_s[...])
    o_ref[...] = (g_conv * y_conv + g_attn * yat_ref[...].T).astype(o_ref.dtype)


def _merge(x_bf, z, w_in_t, row0, w_co, y_attn_t, tm=512, tn=512):
    s, d = z.shape
    n = w_co.shape[1]
    gate_rows = lambda off: pl.BlockSpec(
        (pl.Element(tn), pl.Element(d)),
        lambda j, i: (pl.multiple_of(row0 + off + j * tn, SUBLANES), 0))
    return pl.pallas_call(
        _merge_kernel,
        grid=(n // tn, s // tm),
        in_specs=[
            pl.BlockSpec((tm, d), lambda j, i: (i, 0)),
            pl.BlockSpec((tm, d), lambda j, i: (i, 0)),
            gate_rows(0),
            gate_rows(n),
            pl.BlockSpec((d, tn), lambda j, i: (0, j)),
            pl.BlockSpec((tn, tm), lambda j, i: (j, i)),
        ],
        out_specs=pl.BlockSpec((tm, tn), lambda j, i: (i, j)),
        out_shape=jax.ShapeDtypeStruct((s, n), BF16),
        scratch_shapes=[pltpu.VMEM((2, tn, d), BF16), pltpu.VMEM((d, tn), BF16)],
        compiler_params=_params("parallel", "arbitrary"),
        name="merge",
    )(x_bf, z, w_in_t, w_in_t, w_co, y_attn_t)


def _cast_resident(w_ref, w_s):
    @pl.when(pl.program_id(0) == 0)
    def _():
        w_s[...] = w_ref[...].astype(w_s.dtype)


def _resident(shape):
    return pl.BlockSpec(shape, lambda i: (0,) * len(shape), pipeline_mode=pl.Buffered(1))


def _oproj_kernel(m_ref, w_ref, h_ref, g_ref, b_ref, o_ref, obf_ref, w_s, *, alpha):
    _cast_resident(w_ref, w_s)
    v = alpha * h_ref[...] + _dot(m_ref[...], w_s[...])
    out = _layer_norm(v, g_ref[...], b_ref[...])
    o_ref[...] = out
    obf_ref[...] = out.astype(obf_ref.dtype)


def _oproj(mixed, w_o, h, g, b, alpha, tm=512):
    s, d = h.shape
    row = pl.BlockSpec((tm, d), lambda i: (i, 0))
    vec = pl.BlockSpec((1, d), lambda i: (0, 0))
    return pl.pallas_call(
        functools.partial(_oproj_kernel, alpha=alpha),
        grid=(s // tm,),
        in_specs=[row, _resident((d, d)), row, vec, vec],
        out_specs=[row, row],
        out_shape=[jax.ShapeDtypeStruct((s, d), F32), jax.ShapeDtypeStruct((s, d), BF16)],
        scratch_shapes=[pltpu.VMEM((d, d), BF16)],
        compiler_params=_params("arbitrary"),
        name="oproj_ln",
    )(mixed, w_o, h, g, b)


def _ffn_kernel(hbf_ref, wu_ref, wd_ref, h_ref, g_ref, b_ref, o_ref, obf_ref, acc_ref, *, alpha):
    f = pl.program_id(1)

    @pl.when(f == 0)
    def _():
        acc_ref[...] = jnp.zeros(acc_ref.shape, F32)

    a = jnp.maximum(_dot(hbf_ref[...], wu_ref[...]), 0.0)
    acc_ref[...] += _dot((a * a).astype(wd_ref.dtype), wd_ref[...])

    @pl.when(f == pl.num_programs(1) - 1)
    def _():
        out = _layer_norm(alpha * h_ref[...] + acc_ref[...], g_ref[...], b_ref[...])
        o_ref[...] = out
        obf_ref[...] = out.astype(obf_ref.dtype)


def _ffn(h_bf, w_up, w_down, h, g, b, alpha, tm=512, tf=1024):
    s, d = h.shape
    dff = w_up.shape[1]
    row = pl.BlockSpec((tm, d), lambda i, f: (i, 0))
    vec = pl.BlockSpec((1, d), lambda i, f: (0, 0))
    return pl.pallas_call(
        functools.partial(_ffn_kernel, alpha=alpha),
        grid=(s // tm, dff // tf),
        in_specs=[row, pl.BlockSpec((d, tf), lambda i, f: (0, f)),
                  pl.BlockSpec((tf, d), lambda i, f: (f, 0)), row, vec, vec],
        out_specs=[row, row],
        out_shape=[jax.ShapeDtypeStruct((s, d), F32), jax.ShapeDtypeStruct((s, d), BF16)],
        scratch_shapes=[pltpu.VMEM((tm, d), F32)],
        compiler_params=_params("parallel", "arbitrary"),
        name="ffn_ln",
    )(h_bf, w_up, w_down, h, g, b)


def _ple_kernel(hbf_ref, wg_ref, p_ref, wp_ref, h_ref, g_ref, b_ref, o_ref, wg_s, *, alpha):
    _cast_resident(wg_ref, wg_s)
    gate = _sigmoid(_dot(hbf_ref[...], wg_s[...]))
    pe = gate * _dot(p_ref[...], wp_ref[...])
    o_ref[...] = _layer_norm(alpha * h_ref[...] + pe, g_ref[...], b_ref[...])


def _ple(h_bf, w_pg, p_bf, w_p, h, g, b, alpha, tm=512):
    s, d = h.shape
    dp = p_bf.shape[1]
    row = pl.BlockSpec((tm, d), lambda i: (i, 0))
    vec = pl.BlockSpec((1, d), lambda i: (0, 0))
    return pl.pallas_call(
        functools.partial(_ple_kernel, alpha=alpha),
        grid=(s // tm,),
        in_specs=[row, _resident((d, d)),
                  pl.BlockSpec((tm, dp), lambda i: (i, 0)),
                  pl.BlockSpec((dp, d), lambda i: (0, 0)), row, vec, vec],
        out_specs=row,
        out_shape=jax.ShapeDtypeStruct((s, d), F32),
        scratch_shapes=[pltpu.VMEM((d, d), BF16)],
        compiler_params=_params("arbitrary"),
        name="ple_ln",
    )(h_bf, w_pg, p_bf, w_p, h, g, b)


def _layer(h, p, w_in, conv_w, w_conv_out, g_kv, w_uv, w_o, ln1_g, ln1_b, w_up, w_down,
           ln2_g, ln2_b, w_ple_gate, w_ple, ln3_g, ln3_b, alpha, tk=512):
    s, d = h.shape
    d_conv = conv_w.shape[1]
    nq = N_IDX_HEADS * D_IDX
    c_qlat = 3 * d_conv
    c_ckv = c_qlat + N_HEADS * D_LATENT
    c_gates = c_ckv + D_LATENT + nq + D_IDX + N_IDX_HEADS
    topk = min(TOPK_MAX, s // 4)
    n_kt = s // tk

    w_in_t = jnp.transpose(w_in)

    h_bf, ckv, kidx, qi_t, wi_t = _idx_proj(h, w_in_t, c_ckv, g_kv.reshape(1, D_LATENT))
    z, w_down_bf = _conv_branch(h_bf, w_in_t, conv_w, d_conv, w_down)
    q_t, w_up_bf = _qlat_proj(h_bf, w_in_t, c_qlat, w_up)

    kidx = kidx.reshape(n_kt, tk, 3 * D_IDX)
    kv = ckv.reshape(n_kt, tk, D_LATENT)
    kvt = jnp.transpose(kv, (0, 2, 1))

    w_uv_t = jnp.transpose(w_uv, (0, 2, 1)).astype(BF16)
    y_attn_t = _dsa_attention(qi_t, wi_t, kidx, q_t, kv, kvt, w_uv_t, topk)

    mixed = _merge(h_bf, z, w_in_t, c_gates, w_conv_out, y_attn_t)
    h1, h1_bf = _oproj(mixed, w_o, h, ln1_g.reshape(1, d), ln1_b.reshape(1, d), alpha)
    h2, h2_bf = _ffn(h1_bf, w_up_bf, w_down_bf, h1,
                     ln2_g.reshape(1, d), ln2_b.reshape(1, d), alpha)
    return _ple(h2_bf, w_ple_gate, p.astype(BF16), w_ple.astype(BF16), h2,
                ln3_g.reshape(1, d), ln3_b.reshape(1, d), alpha)


def kernel(x, p, w_in, conv_w, w_conv_out, g_kv, w_uv, w_o, ln1_g, ln1_b, w_up, w_down,
           ln2_g, ln2_b, w_ple_gate, w_ple, ln3_g, ln3_b):
    depth = w_in.shape[0]
    alpha = (2.0 * depth) ** 0.25
    batch = x.shape[0]
    outs = []
    for bi in range(batch):
        h = x[bi]
        for i in range(depth):
            h = _layer(h, p[i, bi], w_in[i], conv_w[i], w_conv_out[i], g_kv[i], w_uv[i], w_o[i],
                       ln1_g[i], ln1_b[i], w_up[i], w_down[i], ln2_g[i], ln2_b[i],
                       w_ple_gate[i], w_ple[i], ln3_g[i], ln3_b[i], alpha)
        outs.append(h)
    return jnp.stack(outs, axis=0)
```

```python
import functools

import jax
import jax.numpy as jnp
from jax import lax
from jax.experimental import pallas as pl
from jax.experimental.pallas import tpu as pltpu

F32 = jnp.float32
BF16 = jnp.bfloat16
I32 = jnp.int32

N_HEADS = 16
D_LATENT = 256
D_VHEAD = 128
N_IDX_HEADS = 16
D_IDX = 64
TOPK_MAX = 256
CONV_WIDTH = 3
LN_EPS = 1e-5
RMS_EPS = 1e-6
ATTN_SCALE = D_LATENT ** -0.5
IDX_SCALE = (D_IDX ** -0.5) * (N_IDX_HEADS ** -0.5)

VMEM_LIMIT_BYTES = 56 * 1024 * 1024
SUBLANES = 8
LANES = 128

INT_MIN = -(2 ** 31)
KEY_NEG_INF = INT_MIN + 0x7FFFFF
NEG_BIG = -1e30
TILE_UNROLL = 4
UNCHECKED_TRIPS = 8
LOG2E = 1.4426950408889634


def _params(*sem):
    return pltpu.CompilerParams(dimension_semantics=sem, vmem_limit_bytes=VMEM_LIMIT_BYTES)


def _dot(a, b):
    return jnp.dot(a, b, preferred_element_type=F32)


def _dot_nt(a, b):
    return lax.dot_general(a, b, (((1,), (1,)), ((), ())), preferred_element_type=F32)


def _sigmoid(v):
    return 1.0 / (1.0 + jnp.exp(-v))


def _layer_norm(v, g, b):
    mu = jnp.mean(v, axis=-1, keepdims=True)
    c = v - mu
    var = jnp.mean(c * c, axis=-1, keepdims=True)
    return c * lax.rsqrt(var + LN_EPS) * g + b


def _split_hi_lo(v):
    hi = v.astype(BF16)
    lo = (v - hi.astype(F32)).astype(BF16)
    return hi, lo


def _side_cast_specs(w, grid):
    n_steps = grid[0] * grid[1]
    rows, cols = w.shape
    chunk = rows // n_steps
    assert chunk * n_steps == rows and chunk % (2 * SUBLANES) == 0
    spec = pl.BlockSpec((chunk, cols), lambda j, i: (j * grid[1] + i, 0))
    return spec, spec, jax.ShapeDtypeStruct(w.shape, BF16)


def _conv_kernel(x_ref, wb_ref, wc_ref, wu_ref, cw_ref, side_ref, z_ref, side_bf_ref, cu_ref, w_s):
    side_bf_ref[...] = side_ref[...].astype(side_bf_ref.dtype)
    i = pl.program_id(1)
    tm = x_ref.shape[0]
    halo = SUBLANES

    @pl.when(i == 0)
    def _():
        w_s[0] = wb_ref[...].astype(w_s.dtype)
        w_s[1] = wc_ref[...].astype(w_s.dtype)
        w_s[2] = wu_ref[...].astype(w_s.dtype)
        cu_ref[0:halo, :] = jnp.zeros((halo, cu_ref.shape[1]), F32)

    @pl.when(i > 0)
    def _():
        cu_ref[0:halo, :] = cu_ref[tm:tm + halo, :]

    x = x_ref[...]
    cu_ref[halo:halo + tm, :] = _dot_nt(x, w_s[1]) * _dot_nt(x, w_s[2])
    cw = cw_ref[...]
    v = (cw[0:1, :] * cu_ref[halo - 2:halo - 2 + tm, :]
         + cw[1:2, :] * cu_ref[halo - 1:halo - 1 + tm, :]
         + cw[2:3, :] * cu_ref[halo:halo + tm, :])
    z_ref[...] = (_dot_nt(x, w_s[0]) * v).astype(z_ref.dtype)


def _conv_branch(x_bf, w_in_t, conv_w, d_conv, w_side, tm=1024, tn=512):
    s, d = x_bf.shape
    nj = d_conv // tn
    grid = (nj, s // tm)
    side_in, side_out, side_shape = _side_cast_specs(w_side, grid)
    return pl.pallas_call(
        _conv_kernel,
        grid=grid,
        in_specs=[
            pl.BlockSpec((tm, d), lambda j, i: (i, 0)),
            pl.BlockSpec((tn, d), lambda j, i: (j, 0)),
            pl.BlockSpec((tn, d), lambda j, i: (nj + j, 0)),
            pl.BlockSpec((tn, d), lambda j, i: (2 * nj + j, 0)),
            pl.BlockSpec((CONV_WIDTH, tn), lambda j, i: (0, j)),
            side_in,
        ],
        out_specs=[pl.BlockSpec((tm, tn), lambda j, i: (i, j)), side_out],
        out_shape=[jax.ShapeDtypeStruct((s, d_conv), BF16), side_shape],
        scratch_shapes=[pltpu.VMEM((tm + SUBLANES, tn), F32),
                        pltpu.VMEM((3, tn, d), BF16)],
        compiler_params=_params("parallel", "arbitrary"),
        name="conv_branch",
    )(x_bf, w_in_t, w_in_t, w_in_t, conv_w, w_side)


def _qlat_kernel(x_ref, w_ref, side_ref, q_ref, side_bf_ref, w_s):
    side_bf_ref[...] = side_ref[...].astype(side_bf_ref.dtype)

    @pl.when(pl.program_id(1) == 0)
    def _():
        w_s[...] = w_ref[...].astype(w_s.dtype)

    acc = _dot_nt(w_s[...], x_ref[...]) * (ATTN_SCALE * LOG2E)
    q_ref[...] = acc.astype(q_ref.dtype).reshape(q_ref.shape)


def _qlat_proj(x_bf, w_in_t, row0, w_side, tm=1024, heads_per_step=4):
    s, d = x_bf.shape
    tn = heads_per_step * D_LATENT
    j0 = row0 // tn
    assert j0 * tn == row0
    grid = (N_HEADS // heads_per_step, s // tm)
    side_in, side_out, side_shape = _side_cast_specs(w_side, grid)
    return pl.pallas_call(
        _qlat_kernel,
        grid=grid,
        in_specs=[
            pl.BlockSpec((tm, d), lambda j, i: (i, 0)),
            pl.BlockSpec((tn, d), lambda j, i: (j0 + j, 0)),
            side_in,
        ],
        out_specs=[pl.BlockSpec((heads_per_step, D_LATENT, tm), lambda j, i: (j, 0, i)), side_out],
        out_shape=[jax.ShapeDtypeStruct((N_HEADS, D_LATENT, s), BF16), side_shape],
        scratch_shapes=[pltpu.VMEM((tn, d), BF16)],
        compiler_params=_params("parallel", "arbitrary"),
        name="qlat_proj",
    )(x_bf, w_in_t, w_side)


def _idx_kernel(x_ref, w_ref, g_ref, xbf_ref, ckv_ref, kidx_ref, qit_ref, wit_ref, w_s):
    @pl.when(pl.program_id(0) == 0)
    def _():
        w_s[...] = w_ref[...].astype(w_s.dtype)

    x = x_ref[...].astype(xbf_ref.dtype)
    xbf_ref[...] = x
    nq = N_IDX_HEADS * D_IDX
    r_q, r_k, r_w = D_LATENT, D_LATENT + nq, D_LATENT + nq + D_IDX
    c = _dot_nt(x, w_s[0:r_q])
    ms = jnp.mean(c * c, axis=-1, keepdims=True)
    ckv_ref[...] = (c * lax.rsqrt(ms + RMS_EPS) * g_ref[...]).astype(ckv_ref.dtype)
    k_hi, k_lo = _split_hi_lo(_dot_nt(x, w_s[r_k:r_w]))
    kidx_ref[...] = jnp.concatenate([k_hi, k_hi, k_lo], axis=1)
    q_t = _dot_nt(w_s[r_q:r_k], x)
    q_hi, q_lo = _split_hi_lo(q_t.reshape(N_IDX_HEADS, D_IDX, q_t.shape[1]))
    qit_ref[...] = jnp.concatenate([q_hi, q_lo, q_hi], axis=1)
    wit_ref[...] = _dot_nt(w_s[r_w:r_w + N_IDX_HEADS], x) * IDX_SCALE


def _idx_proj(x, w_in_t, row0, g_kv, tm=512):
    s, d = x.shape
    nq = N_IDX_HEADS * D_IDX
    n = D_LATENT + nq + D_IDX + N_IDX_HEADS
    n_pad = -(-n // LANES) * LANES
    return pl.pallas_call(
        _idx_kernel,
        grid=(s // tm,),
        in_specs=[
            pl.BlockSpec((tm, d), lambda i: (i, 0)),
            pl.BlockSpec((pl.Element(n_pad), pl.Element(d)), lambda i: (row0, 0)),
            pl.BlockSpec((1, D_LATENT), lambda i: (0, 0)),
        ],
        out_specs=[
            pl.BlockSpec((tm, d), lambda i: (i, 0)),
            pl.BlockSpec((tm, D_LATENT), lambda i: (i, 0)),
            pl.BlockSpec((tm, 3 * D_IDX), lambda i: (i, 0)),
            pl.BlockSpec((N_IDX_HEADS, 3 * D_IDX, tm), lambda i: (0, 0, i)),
            pl.BlockSpec((N_IDX_HEADS, tm), lambda i: (0, i)),
        ],
        out_shape=[
            jax.ShapeDtypeStruct((s, d), BF16),
            jax.ShapeDtypeStruct((s, D_LATENT), BF16),
            jax.ShapeDtypeStruct((s, 3 * D_IDX), BF16),
            jax.ShapeDtypeStruct((N_IDX_HEADS, 3 * D_IDX, s), BF16),
            jax.ShapeDtypeStruct((N_IDX_HEADS, s), F32),
        ],
        scratch_shapes=[pltpu.VMEM((n_pad, d), BF16)],
        compiler_params=_params("arbitrary"),
        name="idx_proj",
    )(x, w_in_t, g_kv)


def _flip_key(v):
    return v ^ ((v >> 31) & 0x7FFFFFFF)


def _key_to_f32(key):
    return pltpu.bitcast(_flip_key(key), F32)


def _unrolled_loop(n, body, unroll):
    start = 0
    while unroll >= 1:
        def trip(i, carry, start=start, unroll=unroll):
            for r in range(unroll):
                body(start + unroll * i + r)
            return carry

        trips = (n - start) // unroll
        lax.fori_loop(0, trips, trip, 0)
        start = start + unroll * trips
        unroll = 1 if unroll > 1 else 0


def _attn_kernel(qit_ref, wit_ref, kidx_ref, qt_ref, kv_ref, kvt_ref, wuvt_ref, yt_ref,
                 score_ref, ext_ref, qi_s, q_s, w_s, m_ref, l_ref, acc_ref, s_buf, p_buf, a_buf,
                 *, topk):
    b = pl.program_id(0)
    n_heads, d_lat, tq = qt_ref.shape
    n_kt_all, tk, _ = kidx_ref.shape
    n_groups, _, gw = q_s.shape
    hg = gw // tq
    n_kt = ((b + 1) * tq + tk - 1) // tk
    kpos = lax.broadcasted_iota(I32, (tk, tq), 0)
    qpos = b * tq + lax.broadcasted_iota(I32, (tk, tq), 1)

    for h in range(n_heads):
        g, c = divmod(h, hg)
        qi_s[g, :, c * tq:(c + 1) * tq] = qit_ref[h]
        q_s[g, :, c * tq:(c + 1) * tq] = qt_ref[h]
        w_s[g, :, c * tq:(c + 1) * tq] = wit_ref[h:h + 1, :]

    def score_tile(j):
        kidx = kidx_ref[j]
        score = jnp.zeros((tk, tq), F32)
        for g in range(n_groups):
            rel = jnp.maximum(_dot(kidx, qi_s[g]), 0.0) * w_s[g]
            for c in range(hg):
                score = score + rel[:, c * tq:(c + 1) * tq]
        score = jnp.where(kpos + j * tk <= qpos, score, -jnp.inf)
        score_ref[j] = score
        slabs = score.reshape(tk // fold, fold, tq)
        top, bot = ext_ref[0], ext_ref[1]
        for r in range(tk // fold):
            top = jnp.maximum(top, slabs[r])
            bot = jnp.minimum(bot, jnp.where(slabs[r] > -jnp.inf, slabs[r], jnp.inf))
        ext_ref[0], ext_ref[1] = top, bot

    fold = 8 * SUBLANES
    ext_ref[0] = jnp.full((fold, tq), -jnp.inf, F32)
    ext_ref[1] = jnp.full((fold, tq), jnp.inf, F32)
    _unrolled_loop(n_kt, score_tile, TILE_UNROLL)

    kf = float(topk)

    n_causal = (qpos[0:1, :] + 1).astype(F32)

    top, bot = ext_ref[0], ext_ref[1]
    rep = lambda v: jnp.broadcast_to(v, (SUBLANES, tq))
    short = n_causal < kf
    lo0 = rep(jnp.where(short, KEY_NEG_INF,
                        _flip_key(pltpu.bitcast(jnp.min(bot, axis=0, keepdims=True), I32))))
    hi0 = rep(_flip_key(pltpu.bitcast(jnp.max(top, axis=0, keepdims=True), I32)) + 1)
    c_lo0 = rep(jnp.where(short, 0.0, n_causal))

    def status(lo, hi, c_lo):
        surplus = c_lo > kf
        return jnp.max(jnp.where(surplus & (hi - 1 > lo), 2.0, jnp.where(surplus, 1.0, 0.0)))

    def search(tiles):
        def run():
            def count_ge(cand):
                cnt = jnp.zeros((fold, tq), F32)
                for j in range(tiles):
                    hit = jnp.where(score_ref[j] >= cand, 1.0, 0.0)
                    cnt = cnt + jnp.sum(hit.reshape(tk // fold, fold, tq), axis=0)
                return jnp.sum(cnt, axis=0, keepdims=True)

            def probe(lo, hi, c_lo, by_key):
                live = (c_lo > kf) & (hi - 1 > lo)
                key_mid = (lo >> 1) + (hi >> 1) + (lo & hi & 1)
                val_mid = 0.5 * (_key_to_f32(lo) + _key_to_f32(hi - 1))
                pk = jnp.where(by_key, key_mid, _flip_key(pltpu.bitcast(val_mid, I32)))
                pk = jnp.minimum(jnp.maximum(pk, lo + 1), hi - 1)
                pk = jnp.where(live, pk, lo)
                c = rep(count_ge(_key_to_f32(jnp.maximum(pk, KEY_NEG_INF))[0:1]))
                up = live & (c >= kf)
                down = live & (c < kf)
                return jnp.where(up, pk, lo), jnp.where(down, pk, hi), jnp.where(up, c, c_lo)

            def probes(state):
                lo, hi, c_lo, _, trip = state
                lo, hi, c_lo = probe(lo, hi, c_lo, False)
                lo, hi, c_lo = probe(lo, hi, c_lo, trip % 2 == 1)
                flag = lax.cond(trip >= UNCHECKED_TRIPS, lambda: status(lo, hi, c_lo),
                                lambda: jnp.float32(2.0))
                return lo, hi, c_lo, flag, trip + 1

            def keep_going(state):
                return jnp.logical_and(state[3] > 1.5, state[4] < 64)

            state = (lo0, hi0, c_lo0, status(lo0, hi0, c_lo0), jnp.int32(0))
            lo, _, _, flag, _ = lax.while_loop(keep_going, probes, state)
            return lo[0:1], flag
        return run

    thr_key, status_out = lax.switch(n_kt - 1, [search(k) for k in range(1, n_kt_all + 1)])
    thr = _key_to_f32(jnp.maximum(thr_key, KEY_NEG_INF))

    @pl.when(status_out > 0.5)
    def _():
        def count(pred):
            def body(j, cnt):
                hit = jnp.where(pred(kpos + j * tk, score_ref[j]), 1.0, 0.0)
                return cnt + jnp.sum(hit.reshape(tk // fold, fold, tq), axis=0)

            cnt = lax.fori_loop(0, n_kt, body, jnp.zeros((fold, tq), F32))
            return jnp.sum(cnt, axis=0, keepdims=True)

        need = kf - count(lambda pos, sc: sc > thr)
        n_pos_bits = (n_kt_all * tk - 1).bit_length()

        def pos_step(t, cut):
            cand = cut | jnp.left_shift(jnp.int32(1), n_pos_bits - 1 - t)
            before = count(lambda pos, sc: (sc == thr) & (pos < cand))
            return jnp.where(before < need, cand, cut)

        cut = lax.fori_loop(0, n_pos_bits, pos_step, jnp.zeros((1, tq), I32))

        def drop_tile(j, carry):
            sc = score_ref[j]
            score_ref[j] = jnp.where((sc == thr) & (kpos + j * tk > cut), -jnp.inf, sc)
            return carry

        lax.fori_loop(0, n_kt, drop_tile, 0)

    def mask_tile(j, carry):
        keep = (score_ref[j] >= thr) & (kpos + j * tk <= qpos)
        score_ref[j] = jnp.where(keep, 0.0, NEG_BIG)
        return carry

    lax.fori_loop(0, n_kt, mask_tile, 0)

    m_ref[...] = jnp.full(m_ref.shape, NEG_BIG, F32)
    l_ref[...] = jnp.zeros(l_ref.shape, F32)
    acc_ref[...] = jnp.zeros(acc_ref.shape, F32)
    p_buf[...] = jnp.zeros(p_buf.shape, p_buf.dtype)
    a_buf[...] = jnp.ones(a_buf.shape, F32)

    def qk(j, g):
        bias = jnp.concatenate([score_ref[j]] * hg, axis=1)
        s_buf[g % 2] = _dot(kv_ref[j], q_s[g]) + bias

    def softmax(g):
        s = s_buf[g % 2]
        m_old = m_ref[g]
        m_new = jnp.maximum(m_old, jnp.max(s, axis=0, keepdims=True))
        p = jnp.exp2(s - m_new)
        alpha = jnp.exp2(m_old - m_new)
        l_ref[g] = alpha * l_ref[g] + jnp.sum(p, axis=0, keepdims=True)
        a_buf[g % 2] = alpha
        p_buf[g % 2] = p.astype(p_buf.dtype)
        m_ref[g] = m_new

    def pv(j, g):
        acc_ref[g] = a_buf[g % 2] * acc_ref[g] + _dot(kvt_ref[j], p_buf[g % 2])

    last = n_groups - 1
    qk(0, 0)

    def attn_tile(j):
        for g in range(n_groups):
            if g < last:
                qk(j, g + 1)
            else:
                qk(jnp.minimum(j + 1, n_kt - 1), 0)
            if g > 0:
                pv(j, g - 1)
            else:
                pv(jnp.maximum(j - 1, 0), last)
            softmax(g)

    _unrolled_loop(n_kt, attn_tile, TILE_UNROLL)
    pv(n_kt - 1, last)

    for h in range(n_heads):
        g, c = divmod(h, hg)
        lanes = slice(c * tq, (c + 1) * tq)
        inv_l = 1.0 / l_ref[g, :, lanes]
        o_t = (acc_ref[g, :, lanes] * inv_l).astype(wuvt_ref.dtype)
        yt_ref[h * D_VHEAD:(h + 1) * D_VHEAD, :] = _dot(wuvt_ref[h], o_t)


def _dsa_attention(qi_t, wi_t, kidx, q_t, kv, kvt, w_uv_t, topk, tq=128, heads_per_group=2):
    n_heads, d_lat, s = q_t.shape
    n_kt, tk, dk = kidx.shape
    assert tk % tq == 0
    n_groups = n_heads // heads_per_group
    gw = heads_per_group * tq
    const3 = lambda b: (0, 0, 0)
    return pl.pallas_call(
        functools.partial(_attn_kernel, topk=topk),
        grid=(s // tq,),
        in_specs=[
            pl.BlockSpec((N_IDX_HEADS, dk, tq), lambda b: (0, 0, b)),
            pl.BlockSpec((N_IDX_HEADS, tq), lambda b: (0, b)),
            pl.BlockSpec((n_kt, tk, dk), const3),
            pl.BlockSpec((n_heads, d_lat, tq), lambda b: (0, 0, b)),
            pl.BlockSpec((n_kt, tk, d_lat), const3),
            pl.BlockSpec((n_kt, d_lat, tk), const3),
            pl.BlockSpec((n_heads, D_VHEAD, d_lat), const3),
        ],
        out_specs=pl.BlockSpec((n_heads * D_VHEAD, tq), lambda b: (0, b)),
        out_shape=jax.ShapeDtypeStruct((n_heads * D_VHEAD, s), F32),
        scratch_shapes=[
            pltpu.VMEM((n_kt, tk, tq), F32),
            pltpu.VMEM((2, 8 * SUBLANES, tq), F32),
            pltpu.VMEM((n_groups, dk, gw), BF16),
            pltpu.VMEM((n_groups, d_lat, gw), BF16),
            pltpu.VMEM((n_groups, 1, gw), F32),
            pltpu.VMEM((n_groups, 1, gw), F32),
            pltpu.VMEM((n_groups, 1, gw), F32),
            pltpu.VMEM((n_groups, d_lat, gw), F32),
            pltpu.VMEM((2, tk, gw), F32),
            pltpu.VMEM((2, tk, gw), BF16),
            pltpu.VMEM((2, 1, gw), F32),
        ],
        compiler_params=_params("parallel"),
        name="dsa_attention",
    )(qi_t, wi_t, kidx, q_t, kv, kvt, w_uv_t)


def _merge_kernel(x_ref, z_ref, wgc_ref, wga_ref, wco_ref, yat_ref, o_ref, w_s, wco_s):
    @pl.when(pl.program_id(1) == 0)
    def _():
        w_s[0] = wgc_ref[...].astype(w_s.dtype)
        w_s[1] = wga_ref[...].astype(w_s.dtype)
        wco_s[...] = wco_ref[...].astype(wco_s.dtype)

    x = x_ref[...]
    g_conv = _sigmoid(_dot_nt(x, w_s[0]))
    g_attn = _sigmoid(_dot_nt(x, w_s[1]))
    y_conv = _dot(z_ref[...], wco_s[...])
    o_ref[...] = (g_conv * y_conv + g_attn * yat_ref[...].T).astype(o_ref.dtype)


def _merge(x_bf, z, w_in_t, row0, w_co, y_attn_t, tm=512, tn=512):
    s, d = z.shape
    n = w_co.shape[1]
    gate_rows = lambda off: pl.BlockSpec(
        (pl.Element(tn), pl.Element(d)),
        lambda j, i: (pl.multiple_of(row0 + off + j * tn, SUBLANES), 0))
    return pl.pallas_call(
        _merge_kernel,
        grid=(n // tn, s // tm),
        in_specs=[
            pl.BlockSpec((tm, d), lambda j, i: (i, 0)),
            pl.BlockSpec((tm, d), lambda j, i: (i, 0)),
            gate_rows(0),
            gate_rows(n),
            pl.BlockSpec((d, tn), lambda j, i: (0, j)),
            pl.BlockSpec((tn, tm), lambda j, i: (j, i)),
        ],
        out_specs=pl.BlockSpec((tm, tn), lambda j, i: (i, j)),
        out_shape=jax.ShapeDtypeStruct((s, n), BF16),
        scratch_shapes=[pltpu.VMEM((2, tn, d), BF16), pltpu.VMEM((d, tn), BF16)],
        compiler_params=_params("parallel", "arbitrary"),
        name="merge",
    )(x_bf, z, w_in_t, w_in_t, w_co, y_attn_t)


def _cast_resident(w_ref, w_s):
    @pl.when(pl.program_id(0) == 0)
    def _():
        w_s[...] = w_ref[...].astype(w_s.dtype)


def _resident(shape):
    return pl.BlockSpec(shape, lambda i: (0,) * len(shape), pipeline_mode=pl.Buffered(1))


def _oproj_kernel(m_ref, w_ref, h_ref, g_ref, b_ref, o_ref, obf_ref, w_s, *, alpha):
    _cast_resident(w_ref, w_s)
    v = alpha * h_ref[...] + _dot(m_ref[...], w_s[...])
    out = _layer_norm(v, g_ref[...], b_ref[...])
    o_ref[...] = out
    obf_ref[...] = out.astype(obf_ref.dtype)


def _oproj(mixed, w_o, h, g, b, alpha, tm=512):
    s, d = h.shape
    row = pl.BlockSpec((tm, d), lambda i: (i, 0))
    vec = pl.BlockSpec((1, d), lambda i: (0, 0))
    return pl.pallas_call(
        functools.partial(_oproj_kernel, alpha=alpha),
        grid=(s // tm,),
        in_specs=[row, _resident((d, d)), row, vec, vec],
        out_specs=[row, row],
        out_shape=[jax.ShapeDtypeStruct((s, d), F32), jax.ShapeDtypeStruct((s, d), BF16)],
        scratch_shapes=[pltpu.VMEM((d, d), BF16)],
        compiler_params=_params("arbitrary"),
        name="oproj_ln",
    )(mixed, w_o, h, g, b)


def _ffn_kernel(hbf_ref, wu_ref, wd_ref, h_ref, g_ref, b_ref, o_ref, obf_ref, acc_ref, *, alpha):
    f = pl.program_id(1)

    @pl.when(f == 0)
    def _():
        acc_ref[...] = jnp.zeros(acc_ref.shape, F32)

    a = jnp.maximum(_dot(hbf_ref[...], wu_ref[...]), 0.0)
    acc_ref[...] += _dot((a * a).astype(wd_ref.dtype), wd_ref[...])

    @pl.when(f == pl.num_programs(1) - 1)
    def _():
        out = _layer_norm(alpha * h_ref[...] + acc_ref[...], g_ref[...], b_ref[...])
        o_ref[...] = out
        obf_ref[...] = out.astype(obf_ref.dtype)


def _ffn(h_bf, w_up, w_down, h, g, b, alpha, tm=512, tf=1024):
    s, d = h.shape
    dff = w_up.shape[1]
    row = pl.BlockSpec((tm, d), lambda i, f: (i, 0))
    vec = pl.BlockSpec((1, d), lambda i, f: (0, 0))
    return pl.pallas_call(
        functools.partial(_ffn_kernel, alpha=alpha),
        grid=(s // tm, dff // tf),
        in_specs=[row, pl.BlockSpec((d, tf), lambda i, f: (0, f)),
                  pl.BlockSpec((tf, d), lambda i, f: (f, 0)), row, vec, vec],
        out_specs=[row, row],
        out_shape=[jax.ShapeDtypeStruct((s, d), F32), jax.ShapeDtypeStruct((s, d), BF16)],
        scratch_shapes=[pltpu.VMEM((tm, d), F32)],
        compiler_params=_params("parallel", "arbitrary"),
        name="ffn_ln",
    )(h_bf, w_up, w_down, h, g, b)


def _ple_kernel(hbf_ref, wg_ref, p_ref, wp_ref, h_ref, g_ref, b_ref, o_ref, wg_s, *, alpha):
    _cast_resident(wg_ref, wg_s)
    gate = _sigmoid(_dot(hbf_ref[...], wg_s[...]))
    pe = gate * _dot(p_ref[...], wp_ref[...])
    o_ref[...] = _layer_norm(alpha * h_ref[...] + pe, g_ref[...], b_ref[...])


def _ple(h_bf, w_pg, p_bf, w_p, h, g, b, alpha, tm=512):
    s, d = h.shape
    dp = p_bf.shape[1]
    row = pl.BlockSpec((tm, d), lambda i: (i, 0))
    vec = pl.BlockSpec((1, d), lambda i: (0, 0))
    return pl.pallas_call(
        functools.partial(_ple_kernel, alpha=alpha),
        grid=(s // tm,),
        in_specs=[row, _resident((d, d)),
                  pl.BlockSpec((tm, dp), lambda i: (i, 0)),
                  pl.BlockSpec((dp, d), lambda i: (0, 0)), row, vec, vec],
        out_specs=row,
        out_shape=jax.ShapeDtypeStruct((s, d), F32),
        scratch_shapes=[pltpu.VMEM((d, d), BF16)],
        compiler_params=_params("arbitrary"),
        name="ple_ln",
    )(h_bf, w_pg, p_bf, w_p, h, g, b)


def _layer(h, p, w_in, conv_w, w_conv_out, g_kv, w_uv, w_o, ln1_g, ln1_b, w_up, w_down,
           ln2_g, ln2_b, w_ple_gate, w_ple, ln3_g, ln3_b, alpha, tk=512):
    s, d = h.shape
    d_conv = conv_w.shape[1]
    nq = N_IDX_HEADS * D_IDX
    c_qlat = 3 * d_conv
    c_ckv = c_qlat + N_HEADS * D_LATENT
    c_gates = c_ckv + D_LATENT + nq + D_IDX + N_IDX_HEADS
    topk = min(TOPK_MAX, s // 4)
    n_kt = s // tk

    w_in_t = jnp.transpose(w_in)

    h_bf, ckv, kidx, qi_t, wi_t = _idx_proj(h, w_in_t, c_ckv, g_kv.reshape(1, D_LATENT))
    z, w_down_bf = _conv_branch(h_bf, w_in_t, conv_w, d_conv, w_down)
    q_t, w_up_bf = _qlat_proj(h_bf, w_in_t, c_qlat, w_up)

    kidx = kidx.reshape(n_kt, tk, 3 * D_IDX)
    kv = ckv.reshape(n_kt, tk, D_LATENT)
    kvt = jnp.transpose(kv, (0, 2, 1))

    w_uv_t = jnp.transpose(w_uv, (0, 2, 1)).astype(BF16)
    y_attn_t = _dsa_attention(qi_t, wi_t, kidx, q_t, kv, kvt, w_uv_t, topk)

    mixed = _merge(h_bf, z, w_in_t, c_gates, w_conv_out, y_attn_t)
    h1, h1_bf = _oproj(mixed, w_o, h, ln1_g.reshape(1, d), ln1_b.reshape(1, d), alpha)
    h2, h2_bf = _ffn(h1_bf, w_up_bf, w_down_bf, h1,
                     ln2_g.reshape(1, d), ln2_b.reshape(1, d), alpha)
    return _ple(h2_bf, w_ple_gate, p.astype(BF16), w_ple.astype(BF16), h2,
                ln3_g.reshape(1, d), ln3_b.reshape(1, d), alpha)


def kernel(x, p, w_in, conv_w, w_conv_out, g_kv, w_uv, w_o, ln1_g, ln1_b, w_up, w_down,
           ln2_g, ln2_b, w_ple_gate, w_ple, ln3_g, ln3_b):
    depth = w_in.shape[0]
    alpha = (2.0 * depth) ** 0.25
    batch = x.shape[0]
    outs = []
    for bi in range(batch):
        h = x[bi]
        for i in range(depth):
            h = _layer(h, p[i, bi], w_in[i], conv_w[i], w_conv_out[i], g_kv[i], w_uv[i], w_o[i],
                       ln1_g[i], ln1_b[i], w_up[i], w_down[i], ln2_g[i], ln2_b[i],
                       w_ple_gate[i], w_ple[i], ln3_g[i], ln3_b[i], alpha)
        outs.append(h)
    return jnp.stack(outs, axis=0)
```

```python
import functools

import jax
import jax.numpy as jnp
from jax import lax
from jax.experimental import pallas as pl
from jax.experimental.pallas import tpu as pltpu

F32 = jnp.float32
BF16 = jnp.bfloat16
I32 = jnp.int32

N_HEADS = 16
D_LATENT = 256
D_VHEAD = 128
N_IDX_HEADS = 16
D_IDX = 64
TOPK_MAX = 256
CONV_WIDTH = 3
LN_EPS = 1e-5
RMS_EPS = 1e-6
ATTN_SCALE = D_LATENT ** -0.5
IDX_SCALE = (D_IDX ** -0.5) * (N_IDX_HEADS ** -0.5)

VMEM_LIMIT_BYTES = 56 * 1024 * 1024
SUBLANES = 8
LANES = 128

INT_MIN = -(2 ** 31)
KEY_NEG_INF = INT_MIN + 0x7FFFFF
NEG_BIG = -1e30
TILE_UNROLL = 4
UNCHECKED_TRIPS = 8
LOG2E = 1.4426950408889634


def _params(*sem):
    return pltpu.CompilerParams(dimension_semantics=sem, vmem_limit_bytes=VMEM_LIMIT_BYTES)


def _dot(a, b):
    return jnp.dot(a, b, preferred_element_type=F32)


def _dot_nt(a, b):
    return lax.dot_general(a, b, (((1,), (1,)), ((), ())), preferred_element_type=F32)


def _sigmoid(v):
    return 1.0 / (1.0 + jnp.exp(-v))


def _layer_norm(v, g, b):
    mu = jnp.mean(v, axis=-1, keepdims=True)
    c = v - mu
    var = jnp.mean(c * c, axis=-1, keepdims=True)
    return c * lax.rsqrt(var + LN_EPS) * g + b


def _split_hi_lo(v):
    hi = v.astype(BF16)
    lo = (v - hi.astype(F32)).astype(BF16)
    return hi, lo


def _side_cast_specs(w, grid):
    n_steps = grid[0] * grid[1]
    rows, cols = w.shape
    chunk = rows // n_steps
    assert chunk * n_steps == rows and chunk % (2 * SUBLANES) == 0
    spec = pl.BlockSpec((chunk, cols), lambda j, i: (j * grid[1] + i, 0))
    return spec, spec, jax.ShapeDtypeStruct(w.shape, BF16)


def _conv_kernel(x_ref, wb_ref, wc_ref, wu_ref, cw_ref, side_ref, z_ref, side_bf_ref, cu_ref, w_s):
    side_bf_ref[...] = side_ref[...].astype(side_bf_ref.dtype)
    i = pl.program_id(1)
    tm = x_ref.shape[0]
    halo = SUBLANES

    @pl.when(i == 0)
    def _():
        w_s[0] = wb_ref[...].astype(w_s.dtype)
        w_s[1] = wc_ref[...].astype(w_s.dtype)
        w_s[2] = wu_ref[...].astype(w_s.dtype)
        cu_ref[0:halo, :] = jnp.zeros((halo, cu_ref.shape[1]), F32)

    @pl.when(i > 0)
    def _():
        cu_ref[0:halo, :] = cu_ref[tm:tm + halo, :]

    x = x_ref[...]
    cu_ref[halo:halo + tm, :] = _dot_nt(x, w_s[1]) * _dot_nt(x, w_s[2])
    cw = cw_ref[...]
    v = (cw[0:1, :] * cu_ref[halo - 2:halo - 2 + tm, :]
         + cw[1:2, :] * cu_ref[halo - 1:halo - 1 + tm, :]
         + cw[2:3, :] * cu_ref[halo:halo + tm, :])
    z_ref[...] = (_dot_nt(x, w_s[0]) * v).astype(z_ref.dtype)


def _conv_branch(x_bf, w_in_t, conv_w, d_conv, w_side, tm=1024, tn=512):
    s, d = x_bf.shape
    nj = d_conv // tn
    grid = (nj, s // tm)
    side_in, side_out, side_shape = _side_cast_specs(w_side, grid)
    return pl.pallas_call(
        _conv_kernel,
        grid=grid,
        in_specs=[
            pl.BlockSpec((tm, d), lambda j, i: (i, 0)),
            pl.BlockSpec((tn, d), lambda j, i: (j, 0)),
            pl.BlockSpec((tn, d), lambda j, i: (nj + j, 0)),
            pl.BlockSpec((tn, d), lambda j, i: (2 * nj + j, 0)),
            pl.BlockSpec((CONV_WIDTH, tn), lambda j, i: (0, j)),
            side_in,
        ],
        out_specs=[pl.BlockSpec((tm, tn), lambda j, i: (i, j)), side_out],
        out_shape=[jax.ShapeDtypeStruct((s, d_conv), BF16), side_shape],
        scratch_shapes=[pltpu.VMEM((tm + SUBLANES, tn), F32),
                        pltpu.VMEM((3, tn, d), BF16)],
        compiler_params=_params("parallel", "arbitrary"),
        name="conv_branch",
    )(x_bf, w_in_t, w_in_t, w_in_t, conv_w, w_side)


def _qlat_kernel(x_ref, w_ref, side_ref, q_ref, side_bf_ref, w_s):
    side_bf_ref[...] = side_ref[...].astype(side_bf_ref.dtype)

    @pl.when(pl.program_id(1) == 0)
    def _():
        w_s[...] = w_ref[...].astype(w_s.dtype)

    acc = _dot_nt(w_s[...], x_ref[...]) * (ATTN_SCALE * LOG2E)
    q_ref[...] = acc.astype(q_ref.dtype).reshape(q_ref.shape)


def _qlat_proj(x_bf, w_in_t, row0, w_side, tm=1024, heads_per_step=4):
    s, d = x_bf.shape
    tn = heads_per_step * D_LATENT
    j0 = row0 // tn
    assert j0 * tn == row0
    grid = (N_HEADS // heads_per_step, s // tm)
    side_in, side_out, side_shape = _side_cast_specs(w_side, grid)
    return pl.pallas_call(
        _qlat_kernel,
        grid=grid,
        in_specs=[
            pl.BlockSpec((tm, d), lambda j, i: (i, 0)),
            pl.BlockSpec((tn, d), lambda j, i: (j0 + j, 0)),
            side_in,
        ],
        out_specs=[pl.BlockSpec((heads_per_step, D_LATENT, tm), lambda j, i: (j, 0, i)), side_out],
        out_shape=[jax.ShapeDtypeStruct((N_HEADS, D_LATENT, s), BF16), side_shape],
        scratch_shapes=[pltpu.VMEM((tn, d), BF16)],
        compiler_params=_params("parallel", "arbitrary"),
        name="qlat_proj",
    )(x_bf, w_in_t, w_side)


def _idx_kernel(x_ref, w_ref, g_ref, xbf_ref, ckv_ref, kidx_ref, qit_ref, wit_ref, w_s):
    @pl.when(pl.program_id(0) == 0)
    def _():
        w_s[...] = w_ref[...].astype(w_s.dtype)

    x = x_ref[...].astype(xbf_ref.dtype)
    xbf_ref[...] = x
    nq = N_IDX_HEADS * D_IDX
    r_q, r_k, r_w = D_LATENT, D_LATENT + nq, D_LATENT + nq + D_IDX
    c = _dot_nt(x, w_s[0:r_q])
    ms = jnp.mean(c * c, axis=-1, keepdims=True)
    ckv_ref[...] = (c * lax.rsqrt(ms + RMS_EPS) * g_ref[...]).astype(ckv_ref.dtype)
    k_hi, k_lo = _split_hi_lo(_dot_nt(x, w_s[r_k:r_w]))
    kidx_ref[...] = jnp.concatenate([k_hi, k_hi, k_lo], axis=1)
    q_t = _dot_nt(w_s[r_q:r_k], x)
    q_hi, q_lo = _split_hi_lo(q_t.reshape(N_IDX_HEADS, D_IDX, q_t.shape[1]))
    qit_ref[...] = jnp.concatenate([q_hi, q_lo, q_hi], axis=1)
    wit_ref[...] = _dot_nt(w_s[r_w:r_w + N_IDX_HEADS], x) * IDX_SCALE


def _idx_proj(x, w_in_t, row0, g_kv, tm=512):
    s, d = x.shape
    nq = N_IDX_HEADS * D_IDX
    n = D_LATENT + nq + D_IDX + N_IDX_HEADS
    n_pad = -(-n // LANES) * LANES
    return pl.pallas_call(
        _idx_kernel,
        grid=(s // tm,),
        in_specs=[
            pl.BlockSpec((tm, d), lambda i: (i, 0)),
            pl.BlockSpec((pl.Element(n_pad), pl.Element(d)), lambda i: (row0, 0)),
            pl.BlockSpec((1, D_LATENT), lambda i: (0, 0)),
        ],
        out_specs=[
            pl.BlockSpec((tm, d), lambda i: (i, 0)),
            pl.BlockSpec((tm, D_LATENT), lambda i: (i, 0)),
            pl.BlockSpec((tm, 3 * D_IDX), lambda i: (i, 0)),
            pl.BlockSpec((N_IDX_HEADS, 3 * D_IDX, tm), lambda i: (0, 0, i)),
            pl.BlockSpec((N_IDX_HEADS, tm), lambda i: (0, i)),
        ],
        out_shape=[
            jax.ShapeDtypeStruct((s, d), BF16),
            jax.ShapeDtypeStruct((s, D_LATENT), BF16),
            jax.ShapeDtypeStruct((s, 3 * D_IDX), BF16),
            jax.ShapeDtypeStruct((N_IDX_HEADS, 3 * D_IDX, s), BF16),
            jax.ShapeDtypeStruct((N_IDX_HEADS, s), F32),
        ],
        scratch_shapes=[pltpu.VMEM((n_pad, d), BF16)],
        compiler_params=_params("arbitrary"),
        name="idx_proj",
    )(x, w_in_t, g_kv)


def _flip_key(v):
    return v ^ ((v >> 31) & 0x7FFFFFFF)


def _key_to_f32(key):
    return pltpu.bitcast(_flip_key(key), F32)


def _unrolled_loop(n, body, unroll):
    start = 0
    while unroll >= 1:
        def trip(i, carry, start=start, unroll=unroll):
            for r in range(unroll):
                body(start + unroll * i + r)
            return carry

        trips = (n - start) // unroll
        lax.fori_loop(0, trips, trip, 0)
        start = start + unroll * trips
        unroll //= 2


def _attn_kernel(qit_ref, wit_ref, kidx_ref, qt_ref, kv_ref, kvt_ref, wuvt_ref, yt_ref,
                 score_ref, ext_ref, qi_s, q_s, w_s, m_ref, l_ref, acc_ref, s_buf, p_buf, a_buf,
                 *, topk):
    b = pl.program_id(0)
    n_heads, d_lat, tq = qt_ref.shape
    n_kt_all, tk, _ = kidx_ref.shape
    n_groups, _, gw = q_s.shape
    hg = gw // tq
    n_kt = ((b + 1) * tq + tk - 1) // tk
    kpos = lax.broadcasted_iota(I32, (tk, tq), 0)
    qpos = b * tq + lax.broadcasted_iota(I32, (tk, tq), 1)

    for h in range(n_heads):
        g, c = divmod(h, hg)
        qi_s[g, :, c * tq:(c + 1) * tq] = qit_ref[h]
        q_s[g, :, c * tq:(c + 1) * tq] = qt_ref[h]
        w_s[g, :, c * tq:(c + 1) * tq] = wit_ref[h:h + 1, :]

    def score_tile(j):
        kidx = kidx_ref[j]
        score = jnp.zeros((tk, tq), F32)
        for g in range(n_groups):
            rel = jnp.maximum(_dot(kidx, qi_s[g]), 0.0) * w_s[g]
            for c in range(hg):
                score = score + rel[:, c * tq:(c + 1) * tq]
        score = jnp.where(kpos + j * tk <= qpos, score, -jnp.inf)
        score_ref[j] = score
        slabs = score.reshape(tk // fold, fold, tq)
        top, bot = ext_ref[0], ext_ref[1]
        for r in range(tk // fold):
            top = jnp.maximum(top, slabs[r])
            bot = jnp.minimum(bot, jnp.where(slabs[r] > -jnp.inf, slabs[r], jnp.inf))
        ext_ref[0], ext_ref[1] = top, bot

    fold = 8 * SUBLANES
    ext_ref[0] = jnp.full((fold, tq), -jnp.inf, F32)
    ext_ref[1] = jnp.full((fold, tq), jnp.inf, F32)
    _unrolled_loop(n_kt, score_tile, TILE_UNROLL)

    kf = float(topk)

    n_causal = (qpos[0:1, :] + 1).astype(F32)

    top, bot = ext_ref[0], ext_ref[1]
    rep = lambda v: jnp.broadcast_to(v, (SUBLANES, tq))
    short = n_causal < kf
    lo0 = rep(jnp.where(short, KEY_NEG_INF,
                        _flip_key(pltpu.bitcast(jnp.min(bot, axis=0, keepdims=True), I32))))
    hi0 = rep(_flip_key(pltpu.bitcast(jnp.max(top, axis=0, keepdims=True), I32)) + 1)
    c_lo0 = rep(jnp.where(short, 0.0, n_causal))

    def status(lo, hi, c_lo):
        surplus = c_lo > kf
        return jnp.max(jnp.where(surplus & (hi - 1 > lo), 2.0, jnp.where(surplus, 1.0, 0.0)))

    def search(tiles):
        def run():
            def count_ge(cand):
                cnt = jnp.zeros((fold, tq), F32)
                for j in range(tiles):
                    hit = jnp.where(score_ref[j] >= cand, 1.0, 0.0)
                    cnt = cnt + jnp.sum(hit.reshape(tk // fold, fold, tq), axis=0)
                return jnp.sum(cnt, axis=0, keepdims=True)

            def probe(lo, hi, c_lo, by_key):
                live = (c_lo > kf) & (hi - 1 > lo)
                key_mid = (lo >> 1) + (hi >> 1) + (lo & hi & 1)
                val_mid = 0.5 * (_key_to_f32(lo) + _key_to_f32(hi - 1))
                pk = jnp.where(by_key, key_mid, _flip_key(pltpu.bitcast(val_mid, I32)))
                pk = jnp.minimum(jnp.maximum(pk, lo + 1), hi - 1)
                pk = jnp.where(live, pk, lo)
                c = rep(count_ge(_key_to_f32(jnp.maximum(pk, KEY_NEG_INF))[0:1]))
                up = live & (c >= kf)
                down = live & (c < kf)
                return jnp.where(up, pk, lo), jnp.where(down, pk, hi), jnp.where(up, c, c_lo)

            def probes(state):
                lo, hi, c_lo, _, trip = state
                lo, hi, c_lo = probe(lo, hi, c_lo, False)
                lo, hi, c_lo = probe(lo, hi, c_lo, trip % 2 == 1)
                flag = lax.cond(trip >= UNCHECKED_TRIPS, lambda: status(lo, hi, c_lo),
                                lambda: jnp.float32(2.0))
                return lo, hi, c_lo, flag, trip + 1

            def keep_going(state):
                return jnp.logical_and(state[3] > 1.5, state[4] < 64)

            state = (lo0, hi0, c_lo0, status(lo0, hi0, c_lo0), jnp.int32(0))
            lo, _, _, flag, _ = lax.while_loop(keep_going, probes, state)
            return lo[0:1], flag
        return run

    thr_key, status_out = lax.switch(n_kt - 1, [search(k) for k in range(1, n_kt_all + 1)])
    thr = _key_to_f32(jnp.maximum(thr_key, KEY_NEG_INF))

    @pl.when(status_out > 0.5)
    def _():
        def count(pred):
            def body(j, cnt):
                hit = jnp.where(pred(kpos + j * tk, score_ref[j]), 1.0, 0.0)
                return cnt + jnp.sum(hit.reshape(tk // fold, fold, tq), axis=0)

            cnt = lax.fori_loop(0, n_kt, body, jnp.zeros((fold, tq), F32))
            return jnp.sum(cnt, axis=0, keepdims=True)

        need = kf - count(lambda pos, sc: sc > thr)
        n_pos_bits = (n_kt_all * tk - 1).bit_length()

        def pos_step(t, cut):
            cand = cut | jnp.left_shift(jnp.int32(1), n_pos_bits - 1 - t)
            before = count(lambda pos, sc: (sc == thr) & (pos < cand))
            return jnp.where(before < need, cand, cut)

        cut = lax.fori_loop(0, n_pos_bits, pos_step, jnp.zeros((1, tq), I32))

        def drop_tile(j, carry):
            sc = score_ref[j]
            score_ref[j] = jnp.where((sc == thr) & (kpos + j * tk > cut), -jnp.inf, sc)
            return carry

        lax.fori_loop(0, n_kt, drop_tile, 0)

    def mask_tile(j, carry):
        keep = (score_ref[j] >= thr) & (kpos + j * tk <= qpos)
        score_ref[j] = jnp.where(keep, 0.0, NEG_BIG)
        return carry

    lax.fori_loop(0, n_kt, mask_tile, 0)

    m_ref[...] = jnp.full(m_ref.shape, NEG_BIG, F32)
    l_ref[...] = jnp.zeros(l_ref.shape, F32)
    acc_ref[...] = jnp.zeros(acc_ref.shape, F32)
    p_buf[...] = jnp.zeros(p_buf.shape, p_buf.dtype)
    a_buf[...] = jnp.ones(a_buf.shape, F32)

    def qk(j, g):
        bias = jnp.concatenate([score_ref[j]] * hg, axis=1)
        s_buf[g % 2] = _dot(kv_ref[j], q_s[g]) + bias

    def softmax(g):
        s = s_buf[g % 2]
        m_old = m_ref[g]
        m_new = jnp.maximum(m_old, jnp.max(s, axis=0, keepdims=True))
        p = jnp.exp2(s - m_new)
        alpha = jnp.exp2(m_old - m_new)
        l_ref[g] = alpha * l_ref[g] + jnp.sum(p, axis=0, keepdims=True)
        a_buf[g % 2] = alpha
        p_buf[g % 2] = p.astype(p_buf.dtype)
        m_ref[g] = m_new

    def pv(j, g):
        acc_ref[g] = a_buf[g % 2] * acc_ref[g] + _dot(kvt_ref[j], p_buf[g % 2])

    last = n_groups - 1
    qk(0, 0)

    def attn_tile(j):
        for g in range(n_groups):
            if g < last:
                qk(j, g + 1)
            else:
                qk(jnp.minimum(j + 1, n_kt - 1), 0)
            if g > 0:
                pv(j, g - 1)
            else:
                pv(jnp.maximum(j - 1, 0), last)
            softmax(g)

    _unrolled_loop(n_kt, attn_tile, TILE_UNROLL)
    pv(n_kt - 1, last)

    for h in range(n_heads):
        g, c = divmod(h, hg)
        lanes = slice(c * tq, (c + 1) * tq)
        inv_l = 1.0 / l_ref[g, :, lanes]
        o_t = (acc_ref[g, :, lanes] * inv_l).astype(wuvt_ref.dtype)
        yt_ref[h * D_VHEAD:(h + 1) * D_VHEAD, :] = _dot(wuvt_ref[h], o_t)


def _dsa_attention(qi_t, wi_t, kidx, q_t, kv, kvt, w_uv_t, topk, tq=128, heads_per_group=2):
    n_heads, d_lat, s = q_t.shape
    n_kt, tk, dk = kidx.shape
    assert tk % tq == 0
    n_groups = n_heads // heads_per_group
    gw = heads_per_group * tq
    const3 = lambda b: (0, 0, 0)
    return pl.pallas_call(
        functools.partial(_attn_kernel, topk=topk),
        grid=(s // tq,),
        in_specs=[
            pl.BlockSpec((N_IDX_HEADS, dk, tq), lambda b: (0, 0, b)),
            pl.BlockSpec((N_IDX_HEADS, tq), lambda b: (0, b)),
            pl.BlockSpec((n_kt, tk, dk), const3),
            pl.BlockSpec((n_heads, d_lat, tq), lambda b: (0, 0, b)),
            pl.BlockSpec((n_kt, tk, d_lat), const3),
            pl.BlockSpec((n_kt, d_lat, tk), const3),
            pl.BlockSpec((n_heads, D_VHEAD, d_lat), const3),
        ],
        out_specs=pl.BlockSpec((n_heads * D_VHEAD, tq), lambda b: (0, b)),
        out_shape=jax.ShapeDtypeStruct((n_heads * D_VHEAD, s), F32),
        scratch_shapes=[
            pltpu.VMEM((n_kt, tk, tq), F32),
            pltpu.VMEM((2, 8 * SUBLANES, tq), F32),
            pltpu.VMEM((n_groups, dk, gw), BF16),
            pltpu.VMEM((n_groups, d_lat, gw), BF16),
            pltpu.VMEM((n_groups, 1, gw), F32),
            pltpu.VMEM((n_groups, 1, gw), F32),
            pltpu.VMEM((n_groups, 1, gw), F32),
            pltpu.VMEM((n_groups, d_lat, gw), F32),
            pltpu.VMEM((2, tk, gw), F32),
            pltpu.VMEM((2, tk, gw), BF16),
            pltpu.VMEM((2, 1, gw), F32),
        ],
        compiler_params=_params("parallel"),
        name="dsa_attention",
    )(qi_t, wi_t, kidx, q_t, kv, kvt, w_uv_t)


def _merge_kernel(x_ref, z_ref, wgc_ref, wga_ref, wco_ref, yat_ref, side_ref, o_ref, side_bf_ref,
                  w_s, wco_s):
    side_bf_ref[...] = side_ref[...].astype(side_bf_ref.dtype)

    @pl.when(pl.program_id(1) == 0)
    def _():
        w_s[0] = wgc_ref[...].astype(w_s.dtype)
        w_s[1] = wga_ref[...].astype(w_s.dtype)
        wco_s[...] = wco_ref[...].astype(wco_s.dtype)

    x = x_ref[...]
    g_conv = _sigmoid(_dot_nt(x, w_s[0]))
    g_attn = _sigmoid(_dot_nt(x, w_s[1]))
    y_conv = _dot(z_ref[...], wco_s[...])
    o_ref[...] = (g_conv * y_conv + g_attn * yat_ref[...].T).astype(o_ref.dtype)


def _merge(x_bf, z, w_in_t, row0, w_co, y_attn_t, w_side, tm=512, tn=512):
    s, d = z.shape
    n = w_co.shape[1]
    grid = (n // tn, s // tm)
    side_in, side_out, side_shape = _side_cast_specs(w_side, grid)
    gate_rows = lambda off: pl.BlockSpec(
        (pl.Element(tn), pl.Element(d)),
        lambda j, i: (pl.multiple_of(row0 + off + j * tn, SUBLANES), 0))
    return pl.pallas_call(
        _merge_kernel,
        grid=grid,
        in_specs=[
            pl.BlockSpec((tm, d), lambda j, i: (i, 0)),
            pl.BlockSpec((tm, d), lambda j, i: (i, 0)),
            gate_rows(0),
            gate_rows(n),
            pl.BlockSpec((d, tn), lambda j, i: (0, j)),
            pl.BlockSpec((tn, tm), lambda j, i: (j, i)),
            side_in,
        ],
        out_specs=[pl.BlockSpec((tm, tn), lambda j, i: (i, j)), side_out],
        out_shape=[jax.ShapeDtypeStruct((s, n), BF16), side_shape],
        scratch_shapes=[pltpu.VMEM((2, tn, d), BF16), pltpu.VMEM((d, tn), BF16)],
        compiler_params=_params("parallel", "arbitrary"),
        name="merge",
    )(x_bf, z, w_in_t, w_in_t, w_co, y_attn_t, w_side)


def _oproj_kernel(m_ref, w_ref, h_ref, g_ref, b_ref, o_ref, obf_ref, *, alpha):
    v = alpha * h_ref[...] + _dot(m_ref[...], w_ref[...])
    out = _layer_norm(v, g_ref[...], b_ref[...])
    o_ref[...] = out
    obf_ref[...] = out.astype(obf_ref.dtype)


def _oproj(mixed, w_o, h, g, b, alpha, tm=512):
    s, d = h.shape
    row = pl.BlockSpec((tm, d), lambda i: (i, 0))
    vec = pl.BlockSpec((1, d), lambda i: (0, 0))
    return pl.pallas_call(
        functools.partial(_oproj_kernel, alpha=alpha),
        grid=(s // tm,),
        in_specs=[row, pl.BlockSpec((d, d), lambda i: (0, 0)), row, vec, vec],
        out_specs=[row, row],
        out_shape=[jax.ShapeDtypeStruct((s, d), F32), jax.ShapeDtypeStruct((s, d), BF16)],
        compiler_params=_params("parallel"),
        name="oproj_ln",
    )(mixed, w_o, h, g, b)


def _ffn_kernel(hbf_ref, wu_ref, wd_ref, h_ref, g_ref, b_ref, side_ref, o_ref, obf_ref, side_bf_ref,
                acc_ref, *, alpha):
    side_bf_ref[...] = side_ref[...].astype(side_bf_ref.dtype)
    f = pl.program_id(1)

    @pl.when(f == 0)
    def _():
        acc_ref[...] = jnp.zeros(acc_ref.shape, F32)

    a = jnp.maximum(_dot(hbf_ref[...], wu_ref[...]), 0.0)
    acc_ref[...] += _dot((a * a).astype(wd_ref.dtype), wd_ref[...])

    @pl.when(f == pl.num_programs(1) - 1)
    def _():
        out = _layer_norm(alpha * h_ref[...] + acc_ref[...], g_ref[...], b_ref[...])
        o_ref[...] = out
        obf_ref[...] = out.astype(obf_ref.dtype)


def _ffn(h_bf, w_up, w_down, h, g, b, alpha, w_side, tm=512, tf=1024):
    s, d = h.shape
    dff = w_up.shape[1]
    grid = (s // tm, dff // tf)
    side_in, side_out, side_shape = _side_cast_specs(w_side, grid)
    row = pl.BlockSpec((tm, d), lambda i, f: (i, 0))
    vec = pl.BlockSpec((1, d), lambda i, f: (0, 0))
    return pl.pallas_call(
        functools.partial(_ffn_kernel, alpha=alpha),
        grid=grid,
        in_specs=[row, pl.BlockSpec((d, tf), lambda i, f: (0, f)),
                  pl.BlockSpec((tf, d), lambda i, f: (f, 0)), row, vec, vec, side_in],
        out_specs=[row, row, side_out],
        out_shape=[jax.ShapeDtypeStruct((s, d), F32), jax.ShapeDtypeStruct((s, d), BF16),
                   side_shape],
        scratch_shapes=[pltpu.VMEM((tm, d), F32)],
        compiler_params=_params("parallel", "arbitrary"),
        name="ffn_ln",
    )(h_bf, w_up, w_down, h, g, b, w_side)


def _ple_kernel(hbf_ref, wg_ref, p_ref, wp_ref, h_ref, g_ref, b_ref, o_ref, *, alpha):
    gate = _sigmoid(_dot(hbf_ref[...], wg_ref[...]))
    pe = gate * _dot(p_ref[...], wp_ref[...])
    o_ref[...] = _layer_norm(alpha * h_ref[...] + pe, g_ref[...], b_ref[...])


def _ple(h_bf, w_pg, p_bf, w_p, h, g, b, alpha, tm=512):
    s, d = h.shape
    dp = p_bf.shape[1]
    row = pl.BlockSpec((tm, d), lambda i: (i, 0))
    vec = pl.BlockSpec((1, d), lambda i: (0, 0))
    return pl.pallas_call(
        functools.partial(_ple_kernel, alpha=alpha),
        grid=(s // tm,),
        in_specs=[row, pl.BlockSpec((d, d), lambda i: (0, 0)),
                  pl.BlockSpec((tm, dp), lambda i: (i, 0)),
                  pl.BlockSpec((dp, d), lambda i: (0, 0)), row, vec, vec],
        out_specs=row,
        out_shape=jax.ShapeDtypeStruct((s, d), F32),
        compiler_params=_params("parallel"),
        name="ple_ln",
    )(h_bf, w_pg, p_bf, w_p, h, g, b)


def _layer(h, p, w_in, conv_w, w_conv_out, g_kv, w_uv, w_o, ln1_g, ln1_b, w_up, w_down,
           ln2_g, ln2_b, w_ple_gate, w_ple, ln3_g, ln3_b, alpha, tk=512):
    s, d = h.shape
    d_conv = conv_w.shape[1]
    nq = N_IDX_HEADS * D_IDX
    c_qlat = 3 * d_conv
    c_ckv = c_qlat + N_HEADS * D_LATENT
    c_gates = c_ckv + D_LATENT + nq + D_IDX + N_IDX_HEADS
    topk = min(TOPK_MAX, s // 4)
    n_kt = s // tk

    w_in_t = jnp.transpose(w_in)

    h_bf, ckv, kidx, qi_t, wi_t = _idx_proj(h, w_in_t, c_ckv, g_kv.reshape(1, D_LATENT))
    z, w_down_bf = _conv_branch(h_bf, w_in_t, conv_w, d_conv, w_down)
    q_t, w_up_bf = _qlat_proj(h_bf, w_in_t, c_qlat, w_up)

    kidx = kidx.reshape(n_kt, tk, 3 * D_IDX)
    kv = ckv.reshape(n_kt, tk, D_LATENT)
    kvt = jnp.transpose(kv, (0, 2, 1))

    w_uv_t = jnp.transpose(w_uv, (0, 2, 1)).astype(BF16)
    y_attn_t = _dsa_attention(qi_t, wi_t, kidx, q_t, kv, kvt, w_uv_t, topk)

    mixed, w_o_bf = _merge(h_bf, z, w_in_t, c_gates, w_conv_out, y_attn_t, w_o)
    h1, h1_bf = _oproj(mixed, w_o_bf, h, ln1_g.reshape(1, d), ln1_b.reshape(1, d), alpha)
    h2, h2_bf, w_pg_bf = _ffn(h1_bf, w_up_bf, w_down_bf, h1,
                              ln2_g.reshape(1, d), ln2_b.reshape(1, d), alpha, w_ple_gate)
    return _ple(h2_bf, w_pg_bf, p.astype(BF16), w_ple.astype(BF16), h2,
                ln3_g.reshape(1, d), ln3_b.reshape(1, d), alpha)


def kernel(x, p, w_in, conv_w, w_conv_out, g_kv, w_uv, w_o, ln1_g, ln1_b, w_up, w_down,
           ln2_g, ln2_b, w_ple_gate, w_ple, ln3_g, ln3_b):
    depth = w_in.shape[0]
    alpha = (2.0 * depth) ** 0.25
    batch = x.shape[0]
    outs = []
    for bi in range(batch):
        h = x[bi]
        for i in range(depth):
            h = _layer(h, p[i, bi], w_in[i], conv_w[i], w_conv_out[i], g_kv[i], w_uv[i], w_o[i],
                       ln1_g[i], ln1_b[i], w_up[i], w_down[i], ln2_g[i], ln2_b[i],
                       w_ple_gate[i], w_ple[i], ln3_g[i], ln3_b[i], alpha)
        outs.append(h)
    return jnp.stack(outs, axis=0)
```

```python
import functools

import jax
import jax.numpy as jnp
from jax import lax
from jax.experimental import pallas as pl
from jax.experimental.pallas import tpu as pltpu

F32 = jnp.float32
BF16 = jnp.bfloat16
I32 = jnp.int32

N_HEADS = 16
D_LATENT = 256
D_VHEAD = 128
N_IDX_HEADS = 16
D_IDX = 64
TOPK_MAX = 256
CONV_WIDTH = 3
LN_EPS = 1e-5
RMS_EPS = 1e-6
ATTN_SCALE = D_LATENT ** -0.5
IDX_SCALE = (D_IDX ** -0.5) * (N_IDX_HEADS ** -0.5)

VMEM_LIMIT_BYTES = 56 * 1024 * 1024
SUBLANES = 8
LANES = 128

INT_MIN = -(2 ** 31)
KEY_NEG_INF = INT_MIN + 0x7FFFFF
NEG_BIG = -1e30
TILE_UNROLL = 4
UNCHECKED_TRIPS = 9
LOG2E = 1.4426950408889634


def _params(*sem):
    return pltpu.CompilerParams(dimension_semantics=sem, vmem_limit_bytes=VMEM_LIMIT_BYTES)


def _dot(a, b):
    return jnp.dot(a, b, preferred_element_type=F32)


def _dot_nt(a, b):
    return lax.dot_general(a, b, (((1,), (1,)), ((), ())), preferred_element_type=F32)


def _sigmoid(v):
    return 1.0 / (1.0 + jnp.exp(-v))


def _layer_norm(v, g, b):
    mu = jnp.mean(v, axis=-1, keepdims=True)
    c = v - mu
    var = jnp.mean(c * c, axis=-1, keepdims=True)
    return c * lax.rsqrt(var + LN_EPS) * g + b


def _split_hi_lo(v):
    hi = v.astype(BF16)
    lo = (v - hi.astype(F32)).astype(BF16)
    return hi, lo


def _side_cast_specs(w, grid):
    n_steps = grid[0] * grid[1]
    rows, cols = w.shape
    chunk = rows // n_steps
    assert chunk * n_steps == rows and chunk % (2 * SUBLANES) == 0
    spec = pl.BlockSpec((chunk, cols), lambda j, i: (j * grid[1] + i, 0))
    return spec, spec, jax.ShapeDtypeStruct(w.shape, BF16)


def _conv_kernel(x_ref, wb_ref, wc_ref, wu_ref, cw_ref, side_ref, z_ref, side_bf_ref, cu_ref, w_s):
    side_bf_ref[...] = side_ref[...].astype(side_bf_ref.dtype)
    i = pl.program_id(1)
    tm = x_ref.shape[0]
    halo = SUBLANES

    @pl.when(i == 0)
    def _():
        w_s[0] = wb_ref[...].astype(w_s.dtype)
        w_s[1] = wc_ref[...].astype(w_s.dtype)
        w_s[2] = wu_ref[...].astype(w_s.dtype)
        cu_ref[0:halo, :] = jnp.zeros((halo, cu_ref.shape[1]), F32)

    @pl.when(i > 0)
    def _():
        cu_ref[0:halo, :] = cu_ref[tm:tm + halo, :]

    x = x_ref[...]
    cu_ref[halo:halo + tm, :] = _dot_nt(x, w_s[1]) * _dot_nt(x, w_s[2])
    cw = cw_ref[...]
    v = (cw[0:1, :] * cu_ref[halo - 2:halo - 2 + tm, :]
         + cw[1:2, :] * cu_ref[halo - 1:halo - 1 + tm, :]
         + cw[2:3, :] * cu_ref[halo:halo + tm, :])
    z_ref[...] = (_dot_nt(x, w_s[0]) * v).astype(z_ref.dtype)


def _conv_branch(x_bf, w_in_t, conv_w, d_conv, w_side, tm=1024, tn=512):
    s, d = x_bf.shape
    nj = d_conv // tn
    grid = (nj, s // tm)
    side_in, side_out, side_shape = _side_cast_specs(w_side, grid)
    return pl.pallas_call(
        _conv_kernel,
        grid=grid,
        in_specs=[
            pl.BlockSpec((tm, d), lambda j, i: (i, 0)),
            pl.BlockSpec((tn, d), lambda j, i: (j, 0)),
            pl.BlockSpec((tn, d), lambda j, i: (nj + j, 0)),
            pl.BlockSpec((tn, d), lambda j, i: (2 * nj + j, 0)),
            pl.BlockSpec((CONV_WIDTH, tn), lambda j, i: (0, j)),
            side_in,
        ],
        out_specs=[pl.BlockSpec((tm, tn), lambda j, i: (i, j)), side_out],
        out_shape=[jax.ShapeDtypeStruct((s, d_conv), BF16), side_shape],
        scratch_shapes=[pltpu.VMEM((tm + SUBLANES, tn), F32),
                        pltpu.VMEM((3, tn, d), BF16)],
        compiler_params=_params("parallel", "arbitrary"),
        name="conv_branch",
    )(x_bf, w_in_t, w_in_t, w_in_t, conv_w, w_side)


def _qlat_kernel(x_ref, w_ref, side_ref, q_ref, side_bf_ref, w_s):
    side_bf_ref[...] = side_ref[...].astype(side_bf_ref.dtype)

    @pl.when(pl.program_id(1) == 0)
    def _():
        w_s[...] = w_ref[...].astype(w_s.dtype)

    acc = _dot_nt(w_s[...], x_ref[...]) * (ATTN_SCALE * LOG2E)
    q_ref[...] = acc.astype(q_ref.dtype).reshape(q_ref.shape)


def _qlat_proj(x_bf, w_in_t, row0, w_side, tm=1024, heads_per_step=4):
    s, d = x_bf.shape
    tn = heads_per_step * D_LATENT
    j0 = row0 // tn
    assert j0 * tn == row0
    grid = (N_HEADS // heads_per_step, s // tm)
    side_in, side_out, side_shape = _side_cast_specs(w_side, grid)
    return pl.pallas_call(
        _qlat_kernel,
        grid=grid,
        in_specs=[
            pl.BlockSpec((tm, d), lambda j, i: (i, 0)),
            pl.BlockSpec((tn, d), lambda j, i: (j0 + j, 0)),
            side_in,
        ],
        out_specs=[pl.BlockSpec((heads_per_step, D_LATENT, tm), lambda j, i: (j, 0, i)), side_out],
        out_shape=[jax.ShapeDtypeStruct((N_HEADS, D_LATENT, s), BF16), side_shape],
        scratch_shapes=[pltpu.VMEM((tn, d), BF16)],
        compiler_params=_params("parallel", "arbitrary"),
        name="qlat_proj",
    )(x_bf, w_in_t, w_side)


def _idx_kernel(x_ref, w_ref, g_ref, xbf_ref, ckv_ref, kidx_ref, qit_ref, wit_ref, w_s):
    @pl.when(pl.program_id(0) == 0)
    def _():
        w_s[...] = w_ref[...].astype(w_s.dtype)

    x = x_ref[...].astype(xbf_ref.dtype)
    xbf_ref[...] = x
    nq = N_IDX_HEADS * D_IDX
    r_q, r_k, r_w = D_LATENT, D_LATENT + nq, D_LATENT + nq + D_IDX
    c = _dot_nt(x, w_s[0:r_q])
    ms = jnp.mean(c * c, axis=-1, keepdims=True)
    ckv_ref[...] = (c * lax.rsqrt(ms + RMS_EPS) * g_ref[...]).astype(ckv_ref.dtype)
    k_hi, k_lo = _split_hi_lo(_dot_nt(x, w_s[r_k:r_w]))
    kidx_ref[...] = jnp.concatenate([k_hi, k_hi, k_lo], axis=1)
    q_t = _dot_nt(w_s[r_q:r_k], x)
    q_hi, q_lo = _split_hi_lo(q_t.reshape(N_IDX_HEADS, D_IDX, q_t.shape[1]))
    qit_ref[...] = jnp.concatenate([q_hi, q_lo, q_hi], axis=1)
    wit_ref[...] = _dot_nt(w_s[r_w:r_w + N_IDX_HEADS], x) * IDX_SCALE


def _idx_proj(x, w_in_t, row0, g_kv, tm=512):
    s, d = x.shape
    nq = N_IDX_HEADS * D_IDX
    n = D_LATENT + nq + D_IDX + N_IDX_HEADS
    n_pad = -(-n // LANES) * LANES
    return pl.pallas_call(
        _idx_kernel,
        grid=(s // tm,),
        in_specs=[
            pl.BlockSpec((tm, d), lambda i: (i, 0)),
            pl.BlockSpec((pl.Element(n_pad), pl.Element(d)), lambda i: (row0, 0)),
            pl.BlockSpec((1, D_LATENT), lambda i: (0, 0)),
        ],
        out_specs=[
            pl.BlockSpec((tm, d), lambda i: (i, 0)),
            pl.BlockSpec((tm, D_LATENT), lambda i: (i, 0)),
            pl.BlockSpec((tm, 3 * D_IDX), lambda i: (i, 0)),
            pl.BlockSpec((N_IDX_HEADS, 3 * D_IDX, tm), lambda i: (0, 0, i)),
            pl.BlockSpec((N_IDX_HEADS, tm), lambda i: (0, i)),
        ],
        out_shape=[
            jax.ShapeDtypeStruct((s, d), BF16),
            jax.ShapeDtypeStruct((s, D_LATENT), BF16),
            jax.ShapeDtypeStruct((s, 3 * D_IDX), BF16),
            jax.ShapeDtypeStruct((N_IDX_HEADS, 3 * D_IDX, s), BF16),
            jax.ShapeDtypeStruct((N_IDX_HEADS, s), F32),
        ],
        scratch_shapes=[pltpu.VMEM((n_pad, d), BF16)],
        compiler_params=_params("arbitrary"),
        name="idx_proj",
    )(x, w_in_t, g_kv)


def _flip_key(v):
    return v ^ ((v >> 31) & 0x7FFFFFFF)


def _key_to_f32(key):
    return pltpu.bitcast(_flip_key(key), F32)


def _unrolled_loop(n, body, unroll):
    start = 0
    while unroll >= 1:
        def trip(i, carry, start=start, unroll=unroll):
            for r in range(unroll):
                body(start + unroll * i + r)
            return carry

        trips = (n - start) // unroll
        lax.fori_loop(0, trips, trip, 0)
        start = start + unroll * trips
        unroll //= 2


def _attn_kernel(qit_ref, wit_ref, kidx_ref, qt_ref, kv_ref, kvt_ref, wuvt_ref, yt_ref,
                 score_ref, ext_ref, qi_s, q_s, w_s, m_ref, l_ref, acc_ref, s_buf, p_buf, a_buf,
                 *, topk):
    b = pl.program_id(0)
    n_heads, d_lat, tq = qt_ref.shape
    n_kt_all, tk, _ = kidx_ref.shape
    n_groups, _, gw = q_s.shape
    hg = gw // tq
    n_kt = ((b + 1) * tq + tk - 1) // tk
    kpos = lax.broadcasted_iota(I32, (tk, tq), 0)
    qpos = b * tq + lax.broadcasted_iota(I32, (tk, tq), 1)

    for h in range(n_heads):
        g, c = divmod(h, hg)
        qi_s[g, :, c * tq:(c + 1) * tq] = qit_ref[h]
        q_s[g, :, c * tq:(c + 1) * tq] = qt_ref[h]
        w_s[g, :, c * tq:(c + 1) * tq] = wit_ref[h:h + 1, :]

    def score_tile(j):
        kidx = kidx_ref[j]
        score = jnp.zeros((tk, tq), F32)
        for g in range(n_groups):
            rel = jnp.maximum(_dot(kidx, qi_s[g]), 0.0) * w_s[g]
            for c in range(hg):
                score = score + rel[:, c * tq:(c + 1) * tq]
        score = jnp.where(kpos + j * tk <= qpos, score, -jnp.inf)
        score_ref[j] = score
        slabs = score.reshape(tk // fold, fold, tq)
        top, bot = ext_ref[0], ext_ref[1]
        for r in range(tk // fold):
            top = jnp.maximum(top, slabs[r])
            bot = jnp.minimum(bot, jnp.where(slabs[r] > -jnp.inf, slabs[r], jnp.inf))
        ext_ref[0], ext_ref[1] = top, bot

    fold = 8 * SUBLANES
    ext_ref[0] = jnp.full((fold, tq), -jnp.inf, F32)
    ext_ref[1] = jnp.full((fold, tq), jnp.inf, F32)
    _unrolled_loop(n_kt, score_tile, TILE_UNROLL)

    kf = float(topk)

    n_causal = (qpos[0:1, :] + 1).astype(F32)

    top, bot = ext_ref[0], ext_ref[1]
    rep = lambda v: jnp.broadcast_to(v, (SUBLANES, tq))
    short = n_causal < kf
    lo0 = rep(jnp.where(short, KEY_NEG_INF,
                        _flip_key(pltpu.bitcast(jnp.min(bot, axis=0, keepdims=True), I32))))
    hi0 = rep(_flip_key(pltpu.bitcast(jnp.max(top, axis=0, keepdims=True), I32)) + 1)
    c_lo0 = rep(jnp.where(short, 0.0, n_causal))

    def status(lo, hi, c_lo):
        surplus = c_lo > kf
        return jnp.max(jnp.where(surplus & (hi - 1 > lo), 2.0, jnp.where(surplus, 1.0, 0.0)))

    def search(tiles):
        def run():
            def count_ge(cand):
                cnt = jnp.zeros((fold, tq), F32)
                for j in range(tiles):
                    hit = jnp.where(score_ref[j] >= cand, 1.0, 0.0)
                    cnt = cnt + jnp.sum(hit.reshape(tk // fold, fold, tq), axis=0)
                return jnp.sum(cnt, axis=0, keepdims=True)

            def probe(lo, hi, c_lo, by_key):
                live = (c_lo > kf) & (hi - 1 > lo)
                key_mid = (lo >> 1) + (hi >> 1) + (lo & hi & 1)
                val_mid = 0.5 * (_key_to_f32(lo) + _key_to_f32(hi - 1))
                pk = jnp.where(by_key, key_mid, _flip_key(pltpu.bitcast(val_mid, I32)))
                pk = jnp.minimum(jnp.maximum(pk, lo + 1), hi - 1)
                pk = jnp.where(live, pk, lo)
                c = rep(count_ge(_key_to_f32(jnp.maximum(pk, KEY_NEG_INF))[0:1]))
                up = live & (c >= kf)
                down = live & (c < kf)
                return jnp.where(up, pk, lo), jnp.where(down, pk, hi), jnp.where(up, c, c_lo)

            def probes(state):
                lo, hi, c_lo, _, trip = state
                lo, hi, c_lo = probe(lo, hi, c_lo, False)
                lo, hi, c_lo = probe(lo, hi, c_lo, trip % 2 == 1)
                flag = lax.cond(trip >= UNCHECKED_TRIPS, lambda: status(lo, hi, c_lo),
                                lambda: jnp.float32(2.0))
                return lo, hi, c_lo, flag, trip + 1

            def keep_going(state):
                return jnp.logical_and(state[3] > 1.5, state[4] < 64)

            state = (lo0, hi0, c_lo0, status(lo0, hi0, c_lo0), jnp.int32(0))
            lo, _, _, flag, _ = lax.while_loop(keep_going, probes, state)
            return lo[0:1], flag
        return run

    thr_key, status_out = lax.switch(n_kt - 1, [search(k) for k in range(1, n_kt_all + 1)])
    thr = _key_to_f32(jnp.maximum(thr_key, KEY_NEG_INF))

    @pl.when(status_out > 0.5)
    def _():
        def count(pred):
            def body(j, cnt):
                hit = jnp.where(pred(kpos + j * tk, score_ref[j]), 1.0, 0.0)
                return cnt + jnp.sum(hit.reshape(tk // fold, fold, tq), axis=0)

            cnt = lax.fori_loop(0, n_kt, body, jnp.zeros((fold, tq), F32))
            return jnp.sum(cnt, axis=0, keepdims=True)

        need = kf - count(lambda pos, sc: sc > thr)
        n_pos_bits = (n_kt_all * tk - 1).bit_length()

        def pos_step(t, cut):
            cand = cut | jnp.left_shift(jnp.int32(1), n_pos_bits - 1 - t)
            before = count(lambda pos, sc: (sc == thr) & (pos < cand))
            return jnp.where(before < need, cand, cut)

        cut = lax.fori_loop(0, n_pos_bits, pos_step, jnp.zeros((1, tq), I32))

        def drop_tile(j, carry):
            sc = score_ref[j]
            score_ref[j] = jnp.where((sc == thr) & (kpos + j * tk > cut), -jnp.inf, sc)
            return carry

        lax.fori_loop(0, n_kt, drop_tile, 0)

    def mask_tile(j, carry):
        keep = (score_ref[j] >= thr) & (kpos + j * tk <= qpos)
        score_ref[j] = jnp.where(keep, 0.0, NEG_BIG)
        return carry

    lax.fori_loop(0, n_kt, mask_tile, 0)

    m_ref[...] = jnp.full(m_ref.shape, NEG_BIG, F32)
    l_ref[...] = jnp.zeros(l_ref.shape, F32)
    acc_ref[...] = jnp.zeros(acc_ref.shape, F32)
    p_buf[...] = jnp.zeros(p_buf.shape, p_buf.dtype)
    a_buf[...] = jnp.ones(a_buf.shape, F32)

    def qk(j, g):
        bias = jnp.concatenate([score_ref[j]] * hg, axis=1)
        s_buf[g % 2] = _dot(kv_ref[j], q_s[g]) + bias

    def softmax(g):
        s = s_buf[g % 2]
        m_old = m_ref[g]
        m_new = jnp.maximum(m_old, jnp.max(s, axis=0, keepdims=True))
        p = jnp.exp2(s - m_new)
        alpha = jnp.exp2(m_old - m_new)
        l_ref[g] = alpha * l_ref[g] + jnp.sum(p, axis=0, keepdims=True)
        a_buf[g % 2] = alpha
        p_buf[g % 2] = p.astype(p_buf.dtype)
        m_ref[g] = m_new

    def pv(j, g):
        acc_ref[g] = a_buf[g % 2] * acc_ref[g] + _dot(kvt_ref[j], p_buf[g % 2])

    last = n_groups - 1
    qk(0, 0)

    def attn_tile(j):
        for g in range(n_groups):
            if g < last:
                qk(j, g + 1)
            else:
                qk(jnp.minimum(j + 1, n_kt - 1), 0)
            if g > 0:
                pv(j, g - 1)
            else:
                pv(jnp.maximum(j - 1, 0), last)
            softmax(g)

    _unrolled_loop(n_kt, attn_tile, TILE_UNROLL)
    pv(n_kt - 1, last)

    for h in range(n_heads):
        g, c = divmod(h, hg)
        lanes = slice(c * tq, (c + 1) * tq)
        inv_l = 1.0 / l_ref[g, :, lanes]
        o_t = (acc_ref[g, :, lanes] * inv_l).astype(wuvt_ref.dtype)
        yt_ref[h * D_VHEAD:(h + 1) * D_VHEAD, :] = _dot(wuvt_ref[h], o_t)


def _dsa_attention(qi_t, wi_t, kidx, q_t, kv, kvt, w_uv_t, topk, tq=128, heads_per_group=2):
    n_heads, d_lat, s = q_t.shape
    n_kt, tk, dk = kidx.shape
    assert tk % tq == 0
    n_groups = n_heads // heads_per_group
    gw = heads_per_group * tq
    const3 = lambda b: (0, 0, 0)
    return pl.pallas_call(
        functools.partial(_attn_kernel, topk=topk),
        grid=(s // tq,),
        in_specs=[
            pl.BlockSpec((N_IDX_HEADS, dk, tq), lambda b: (0, 0, b)),
            pl.BlockSpec((N_IDX_HEADS, tq), lambda b: (0, b)),
            pl.BlockSpec((n_kt, tk, dk), const3),
            pl.BlockSpec((n_heads, d_lat, tq), lambda b: (0, 0, b)),
            pl.BlockSpec((n_kt, tk, d_lat), const3),
            pl.BlockSpec((n_kt, d_lat, tk), const3),
            pl.BlockSpec((n_heads, D_VHEAD, d_lat), const3),
        ],
        out_specs=pl.BlockSpec((n_heads * D_VHEAD, tq), lambda b: (0, b)),
        out_shape=jax.ShapeDtypeStruct((n_heads * D_VHEAD, s), F32),
        scratch_shapes=[
            pltpu.VMEM((n_kt, tk, tq), F32),
            pltpu.VMEM((2, 8 * SUBLANES, tq), F32),
            pltpu.VMEM((n_groups, dk, gw), BF16),
            pltpu.VMEM((n_groups, d_lat, gw), BF16),
            pltpu.VMEM((n_groups, 1, gw), F32),
            pltpu.VMEM((n_groups, 1, gw), F32),
            pltpu.VMEM((n_groups, 1, gw), F32),
            pltpu.VMEM((n_groups, d_lat, gw), F32),
            pltpu.VMEM((2, tk, gw), F32),
            pltpu.VMEM((2, tk, gw), BF16),
            pltpu.VMEM((2, 1, gw), F32),
        ],
        compiler_params=_params("parallel"),
        name="dsa_attention",
    )(qi_t, wi_t, kidx, q_t, kv, kvt, w_uv_t)


def _merge_kernel(x_ref, z_ref, wgc_ref, wga_ref, wco_ref, yat_ref, o_ref, w_s, wco_s):
    @pl.when(pl.program_id(1) == 0)
    def _():
        w_s[0] = wgc_ref[...].astype(w_s.dtype)
        w_s[1] = wga_ref[...].astype(w_s.dtype)
        wco_s[...] = wco_ref[...].astype(wco_s.dtype)

    x = x_ref[...]
    g_conv = _sigmoid(_dot_nt(x, w_s[0]))
    g_attn = _sigmoid(_dot_nt(x, w_s[1]))
    y_conv = _dot(z_ref[...], wco_s[...])
    o_ref[...] = (g_conv * y_conv + g_attn * yat_ref[...].T).astype(o_ref.dtype)


def _merge(x_bf, z, w_in_t, row0, w_co, y_attn_t, tm=512, tn=512):
    s, d = z.shape
    n = w_co.shape[1]
    gate_rows = lambda off: pl.BlockSpec(
        (pl.Element(tn), pl.Element(d)),
        lambda j, i: (pl.multiple_of(row0 + off + j * tn, SUBLANES), 0))
    return pl.pallas_call(
        _merge_kernel,
        grid=(n // tn, s // tm),
        in_specs=[
            pl.BlockSpec((tm, d), lambda j, i: (i, 0)),
            pl.BlockSpec((tm, d), lambda j, i: (i, 0)),
            gate_rows(0),
            gate_rows(n),
            pl.BlockSpec((d, tn), lambda j, i: (0, j)),
            pl.BlockSpec((tn, tm), lambda j, i: (j, i)),
        ],
        out_specs=pl.BlockSpec((tm, tn), lambda j, i: (i, j)),
        out_shape=jax.ShapeDtypeStruct((s, n), BF16),
        scratch_shapes=[pltpu.VMEM((2, tn, d), BF16), pltpu.VMEM((d, tn), BF16)],
        compiler_params=_params("parallel", "arbitrary"),
        name="merge",
    )(x_bf, z, w_in_t, w_in_t, w_co, y_attn_t)


def _cast_resident(w_ref, w_s):
    @pl.when(pl.program_id(0) == 0)
    def _():
        w_s[...] = w_ref[...].astype(w_s.dtype)


def _resident(shape):
    return pl.BlockSpec(shape, lambda i: (0,) * len(shape), pipeline_mode=pl.Buffered(1))


def _oproj_kernel(m_ref, w_ref, h_ref, g_ref, b_ref, o_ref, obf_ref, w_s, *, alpha):
    _cast_resident(w_ref, w_s)
    v = alpha * h_ref[...] + _dot(m_ref[...], w_s[...])
    out = _layer_norm(v, g_ref[...], b_ref[...])
    o_ref[...] = out
    obf_ref[...] = out.astype(obf_ref.dtype)


def _oproj(mixed, w_o, h, g, b, alpha, tm=512):
    s, d = h.shape
    row = pl.BlockSpec((tm, d), lambda i: (i, 0))
    vec = pl.BlockSpec((1, d), lambda i: (0, 0))
    return pl.pallas_call(
        functools.partial(_oproj_kernel, alpha=alpha),
        grid=(s // tm,),
        in_specs=[row, _resident((d, d)), row, vec, vec],
        out_specs=[row, row],
        out_shape=[jax.ShapeDtypeStruct((s, d), F32), jax.ShapeDtypeStruct((s, d), BF16)],
        scratch_shapes=[pltpu.VMEM((d, d), BF16)],
        compiler_params=_params("arbitrary"),
        name="oproj_ln",
    )(mixed, w_o, h, g, b)


def _ffn_kernel(hbf_ref, wu_ref, wd_ref, h_ref, g_ref, b_ref, o_ref, obf_ref, acc_ref, *, alpha):
    f = pl.program_id(1)

    @pl.when(f == 0)
    def _():
        acc_ref[...] = jnp.zeros(acc_ref.shape, F32)

    a = jnp.maximum(_dot(hbf_ref[...], wu_ref[...]), 0.0)
    acc_ref[...] += _dot((a * a).astype(wd_ref.dtype), wd_ref[...])

    @pl.when(f == pl.num_programs(1) - 1)
    def _():
        out = _layer_norm(alpha * h_ref[...] + acc_ref[...], g_ref[...], b_ref[...])
        o_ref[...] = out
        obf_ref[...] = out.astype(obf_ref.dtype)


def _ffn(h_bf, w_up, w_down, h, g, b, alpha, tm=512, tf=1024):
    s, d = h.shape
    dff = w_up.shape[1]
    row = pl.BlockSpec((tm, d), lambda i, f: (i, 0))
    vec = pl.BlockSpec((1, d), lambda i, f: (0, 0))
    return pl.pallas_call(
        functools.partial(_ffn_kernel, alpha=alpha),
        grid=(s // tm, dff // tf),
        in_specs=[row, pl.BlockSpec((d, tf), lambda i, f: (0, f)),
                  pl.BlockSpec((tf, d), lambda i, f: (f, 0)), row, vec, vec],
        out_specs=[row, row],
        out_shape=[jax.ShapeDtypeStruct((s, d), F32), jax.ShapeDtypeStruct((s, d), BF16)],
        scratch_shapes=[pltpu.VMEM((tm, d), F32)],
        compiler_params=_params("parallel", "arbitrary"),
        name="ffn_ln",
    )(h_bf, w_up, w_down, h, g, b)


def _ple_kernel(hbf_ref, wg_ref, p_ref, wp_ref, h_ref, g_ref, b_ref, o_ref, wg_s, *, alpha):
    _cast_resident(wg_ref, wg_s)
    gate = _sigmoid(_dot(hbf_ref[...], wg_s[...]))
    pe = gate * _dot(p_ref[...], wp_ref[...])
    o_ref[...] = _layer_norm(alpha * h_ref[...] + pe, g_ref[...], b_ref[...])


def _ple(h_bf, w_pg, p_bf, w_p, h, g, b, alpha, tm=512):
    s, d = h.shape
    dp = p_bf.shape[1]
    row = pl.BlockSpec((tm, d), lambda i: (i, 0))
    vec = pl.BlockSpec((1, d), lambda i: (0, 0))
    return pl.pallas_call(
        functools.partial(_ple_kernel, alpha=alpha),
        grid=(s // tm,),
        in_specs=[row, _resident((d, d)),
                  pl.BlockSpec((tm, dp), lambda i: (i, 0)),
                  pl.BlockSpec((dp, d), lambda i: (0, 0)), row, vec, vec],
        out_specs=row,
        out_shape=jax.ShapeDtypeStruct((s, d), F32),
        scratch_shapes=[pltpu.VMEM((d, d), BF16)],
        compiler_params=_params("arbitrary"),
        name="ple_ln",
    )(h_bf, w_pg, p_bf, w_p, h, g, b)


def _layer(h, p, w_in, conv_w, w_conv_out, g_kv, w_uv, w_o, ln1_g, ln1_b, w_up, w_down,
           ln2_g, ln2_b, w_ple_gate, w_ple, ln3_g, ln3_b, alpha, tk=512):
    s, d = h.shape
    d_conv = conv_w.shape[1]
    nq = N_IDX_HEADS * D_IDX
    c_qlat = 3 * d_conv
    c_ckv = c_qlat + N_HEADS * D_LATENT
    c_gates = c_ckv + D_LATENT + nq + D_IDX + N_IDX_HEADS
    topk = min(TOPK_MAX, s // 4)
    n_kt = s // tk

    w_in_t = jnp.transpose(w_in)

    h_bf, ckv, kidx, qi_t, wi_t = _idx_proj(h, w_in_t, c_ckv, g_kv.reshape(1, D_LATENT))
    z, w_down_bf = _conv_branch(h_bf, w_in_t, conv_w, d_conv, w_down)
    q_t, w_up_bf = _qlat_proj(h_bf, w_in_t, c_qlat, w_up)

    kidx = kidx.reshape(n_kt, tk, 3 * D_IDX)
    kv = ckv.reshape(n_kt, tk, D_LATENT)
    kvt = jnp.transpose(kv, (0, 2, 1))

    w_uv_t = jnp.transpose(w_uv, (0, 2, 1)).astype(BF16)
    y_attn_t = _dsa_attention(qi_t, wi_t, kidx, q_t, kv, kvt, w_uv_t, topk)

    mixed = _merge(h_bf, z, w_in_t, c_gates, w_conv_out, y_attn_t)
    h1, h1_bf = _oproj(mixed, w_o, h, ln1_g.reshape(1, d), ln1_b.reshape(1, d), alpha)
    h2, h2_bf = _ffn(h1_bf, w_up_bf, w_down_bf, h1,
                     ln2_g.reshape(1, d), ln2_b.reshape(1, d), alpha)
    return _ple(h2_bf, w_ple_gate, p.astype(BF16), w_ple.astype(BF16), h2,
                ln3_g.reshape(1, d), ln3_b.reshape(1, d), alpha)


def kernel(x, p, w_in, conv_w, w_conv_out, g_kv, w_uv, w_o, ln1_g, ln1_b, w_up, w_down,
           ln2_g, ln2_b, w_ple_gate, w_ple, ln3_g, ln3_b):
    depth = w_in.shape[0]
    alpha = (2.0 * depth) ** 0.25
    batch = x.shape[0]
    outs = []
    for bi in range(batch):
        h = x[bi]
        for i in range(depth):
            h = _layer(h, p[i, bi], w_in[i], conv_w[i], w_conv_out[i], g_kv[i], w_uv[i], w_o[i],
                       ln1_g[i], ln1_b[i], w_up[i], w_down[i], ln2_g[i], ln2_b[i],
                       w_ple_gate[i], w_ple[i], ln3_g[i], ln3_b[i], alpha)
        outs.append(h)
    return jnp.stack(outs, axis=0)
```

```python
import functools

import jax
import jax.numpy as jnp
from jax import lax
from jax.experimental import pallas as pl
from jax.experimental.pallas import tpu as pltpu

F32 = jnp.float32
BF16 = jnp.bfloat16
I32 = jnp.int32

N_HEADS = 16
D_LATENT = 256
D_VHEAD = 128
N_IDX_HEADS = 16
D_IDX = 64
TOPK_MAX = 256
CONV_WIDTH = 3
LN_EPS = 1e-5
RMS_EPS = 1e-6
ATTN_SCALE = D_LATENT ** -0.5
IDX_SCALE = (D_IDX ** -0.5) * (N_IDX_HEADS ** -0.5)

VMEM_LIMIT_BYTES = 56 * 1024 * 1024
FFN_PLE_VMEM_LIMIT_BYTES = 61 * 1024 * 1024
SUBLANES = 8
LANES = 128

INT_MIN = -(2 ** 31)
KEY_NEG_INF = INT_MIN + 0x7FFFFF
NEG_BIG = -1e30
TILE_UNROLL = 4
UNCHECKED_TRIPS = 8
LOG2E = 1.4426950408889634


def _params(*sem):
    return pltpu.CompilerParams(dimension_semantics=sem, vmem_limit_bytes=VMEM_LIMIT_BYTES)


def _dot(a, b):
    return jnp.dot(a, b, preferred_element_type=F32)


def _dot_nt(a, b):
    return lax.dot_general(a, b, (((1,), (1,)), ((), ())), preferred_element_type=F32)


def _sigmoid(v):
    return 1.0 / (1.0 + jnp.exp(-v))


def _layer_norm(v, g, b):
    mu = jnp.mean(v, axis=-1, keepdims=True)
    c = v - mu
    var = jnp.mean(c * c, axis=-1, keepdims=True)
    return c * lax.rsqrt(var + LN_EPS) * g + b


def _split_hi_lo(v):
    hi = v.astype(BF16)
    lo = (v - hi.astype(F32)).astype(BF16)
    return hi, lo


def _side_cast_specs(w, grid):
    n_steps = grid[0] * grid[1]
    rows, cols = w.shape
    chunk = rows // n_steps
    assert chunk * n_steps == rows and chunk % (2 * SUBLANES) == 0
    spec = pl.BlockSpec((chunk, cols), lambda j, i: (j * grid[1] + i, 0))
    return spec, spec, jax.ShapeDtypeStruct(w.shape, BF16)


def _conv_kernel(x_ref, wb_ref, wc_ref, wu_ref, cw_ref, side_ref, z_ref, side_bf_ref, cu_ref, w_s):
    side_bf_ref[...] = side_ref[...].astype(side_bf_ref.dtype)
    i = pl.program_id(1)
    tm = x_ref.shape[0]
    halo = SUBLANES

    @pl.when(i == 0)
    def _():
        w_s[0] = wb_ref[...].astype(w_s.dtype)
        w_s[1] = wc_ref[...].astype(w_s.dtype)
        w_s[2] = wu_ref[...].astype(w_s.dtype)
        cu_ref[0:halo, :] = jnp.zeros((halo, cu_ref.shape[1]), F32)

    @pl.when(i > 0)
    def _():
        cu_ref[0:halo, :] = cu_ref[tm:tm + halo, :]

    x = x_ref[...]
    cu_ref[halo:halo + tm, :] = _dot_nt(x, w_s[1]) * _dot_nt(x, w_s[2])
    cw = cw_ref[...]
    v = (cw[0:1, :] * cu_ref[halo - 2:halo - 2 + tm, :]
         + cw[1:2, :] * cu_ref[halo - 1:halo - 1 + tm, :]
         + cw[2:3, :] * cu_ref[halo:halo + tm, :])
    z_ref[...] = (_dot_nt(x, w_s[0]) * v).astype(z_ref.dtype)


def _conv_branch(x_bf, w_in_t, conv_w, d_conv, w_side, tm=1024, tn=512):
    s, d = x_bf.shape
    nj = d_conv // tn
    grid = (nj, s // tm)
    side_in, side_out, side_shape = _side_cast_specs(w_side, grid)
    return pl.pallas_call(
        _conv_kernel,
        grid=grid,
        in_specs=[
            pl.BlockSpec((tm, d), lambda j, i: (i, 0)),
            pl.BlockSpec((tn, d), lambda j, i: (j, 0)),
            pl.BlockSpec((tn, d), lambda j, i: (nj + j, 0)),
            pl.BlockSpec((tn, d), lambda j, i: (2 * nj + j, 0)),
            pl.BlockSpec((CONV_WIDTH, tn), lambda j, i: (0, j)),
            side_in,
        ],
        out_specs=[pl.BlockSpec((tm, tn), lambda j, i: (i, j)), side_out],
        out_shape=[jax.ShapeDtypeStruct((s, d_conv), BF16), side_shape],
        scratch_shapes=[pltpu.VMEM((tm + SUBLANES, tn), F32),
                        pltpu.VMEM((3, tn, d), BF16)],
        compiler_params=_params("parallel", "arbitrary"),
        name="conv_branch",
    )(x_bf, w_in_t, w_in_t, w_in_t, conv_w, w_side)


def _qlat_kernel(x_ref, w_ref, side_ref, q_ref, side_bf_ref, w_s):
    side_bf_ref[...] = side_ref[...].astype(side_bf_ref.dtype)

    @pl.when(pl.program_id(1) == 0)
    def _():
        w_s[...] = w_ref[...].astype(w_s.dtype)

    acc = _dot_nt(w_s[...], x_ref[...]) * (ATTN_SCALE * LOG2E)
    q_ref[...] = acc.astype(q_ref.dtype).reshape(q_ref.shape)


def _qlat_proj(x_bf, w_in_t, row0, w_side, tm=1024, heads_per_step=4):
    s, d = x_bf.shape
    tn = heads_per_step * D_LATENT
    j0 = row0 // tn
    assert j0 * tn == row0
    grid = (N_HEADS // heads_per_step, s // tm)
    side_in, side_out, side_shape = _side_cast_specs(w_side, grid)
    return pl.pallas_call(
        _qlat_kernel,
        grid=grid,
        in_specs=[
            pl.BlockSpec((tm, d), lambda j, i: (i, 0)),
            pl.BlockSpec((tn, d), lambda j, i: (j0 + j, 0)),
            side_in,
        ],
        out_specs=[pl.BlockSpec((heads_per_step, D_LATENT, tm), lambda j, i: (j, 0, i)), side_out],
        out_shape=[jax.ShapeDtypeStruct((N_HEADS, D_LATENT, s), BF16), side_shape],
        scratch_shapes=[pltpu.VMEM((tn, d), BF16)],
        compiler_params=_params("parallel", "arbitrary"),
        name="qlat_proj",
    )(x_bf, w_in_t, w_side)


def _idx_kernel(x_ref, w_ref, g_ref, xbf_ref, ckv_ref, kidx_ref, qit_ref, wit_ref, w_s):
    @pl.when(pl.program_id(0) == 0)
    def _():
        w_s[...] = w_ref[...].astype(w_s.dtype)

    x = x_ref[...].astype(xbf_ref.dtype)
    xbf_ref[...] = x
    nq = N_IDX_HEADS * D_IDX
    r_q, r_k, r_w = D_LATENT, D_LATENT + nq, D_LATENT + nq + D_IDX
    c = _dot_nt(x, w_s[0:r_q])
    ms = jnp.mean(c * c, axis=-1, keepdims=True)
    ckv_ref[...] = (c * lax.rsqrt(ms + RMS_EPS) * g_ref[...]).astype(ckv_ref.dtype)
    k_hi, k_lo = _split_hi_lo(_dot_nt(x, w_s[r_k:r_w]))
    kidx_ref[...] = jnp.concatenate([k_hi, k_hi, k_lo], axis=1)
    q_t = _dot_nt(w_s[r_q:r_k], x)
    q_hi, q_lo = _split_hi_lo(q_t.reshape(N_IDX_HEADS, D_IDX, q_t.shape[1]))
    qit_ref[...] = jnp.concatenate([q_hi, q_lo, q_hi], axis=1)
    wit_ref[...] = _dot_nt(w_s[r_w:r_w + N_IDX_HEADS], x) * IDX_SCALE


def _idx_proj(x, w_in_t, row0, g_kv, tm=512):
    s, d = x.shape
    nq = N_IDX_HEADS * D_IDX
    n = D_LATENT + nq + D_IDX + N_IDX_HEADS
    n_pad = -(-n // LANES) * LANES
    return pl.pallas_call(
        _idx_kernel,
        grid=(s // tm,),
        in_specs=[
            pl.BlockSpec((tm, d), lambda i: (i, 0)),
            pl.BlockSpec((pl.Element(n_pad), pl.Element(d)), lambda i: (row0, 0)),
            pl.BlockSpec((1, D_LATENT), lambda i: (0, 0)),
        ],
        out_specs=[
            pl.BlockSpec((tm, d), lambda i: (i, 0)),
            pl.BlockSpec((tm, D_LATENT), lambda i: (i, 0)),
            pl.BlockSpec((tm, 3 * D_IDX), lambda i: (i, 0)),
            pl.BlockSpec((N_IDX_HEADS, 3 * D_IDX, tm), lambda i: (0, 0, i)),
            pl.BlockSpec((N_IDX_HEADS, tm), lambda i: (0, i)),
        ],
        out_shape=[
            jax.ShapeDtypeStruct((s, d), BF16),
            jax.ShapeDtypeStruct((s, D_LATENT), BF16),
            jax.ShapeDtypeStruct((s, 3 * D_IDX), BF16),
            jax.ShapeDtypeStruct((N_IDX_HEADS, 3 * D_IDX, s), BF16),
            jax.ShapeDtypeStruct((N_IDX_HEADS, s), F32),
        ],
        scratch_shapes=[pltpu.VMEM((n_pad, d), BF16)],
        compiler_params=_params("arbitrary"),
        name="idx_proj",
    )(x, w_in_t, g_kv)


def _flip_key(v):
    return v ^ ((v >> 31) & 0x7FFFFFFF)


def _key_to_f32(key):
    return pltpu.bitcast(_flip_key(key), F32)


def _unrolled_loop(n, body, unroll):
    start = 0
    while unroll >= 1:
        def trip(i, carry, start=start, unroll=unroll):
            for r in range(unroll):
                body(start + unroll * i + r)
            return carry

        trips = (n - start) // unroll
        lax.fori_loop(0, trips, trip, 0)
        start = start + unroll * trips
        unroll //= 2


def _attn_kernel(qit_ref, wit_ref, kidx_ref, qt_ref, kv_ref, kvt_ref, wuvt_ref, yt_ref,
                 score_ref, ext_ref, qi_s, q_s, w_s, m_ref, l_ref, acc_ref, s_buf, p_buf, a_buf,
                 *, topk):
    b = pl.program_id(0)
    n_heads, d_lat, tq = qt_ref.shape
    n_kt_all, tk, _ = kidx_ref.shape
    n_groups, _, gw = q_s.shape
    hg = gw // tq
    n_kt = ((b + 1) * tq + tk - 1) // tk
    kpos = lax.broadcasted_iota(I32, (tk, tq), 0)
    qpos = b * tq + lax.broadcasted_iota(I32, (tk, tq), 1)

    for h in range(n_heads):
        g, c = divmod(h, hg)
        qi_s[g, :, c * tq:(c + 1) * tq] = qit_ref[h]
        q_s[g, :, c * tq:(c + 1) * tq] = qt_ref[h]
        w_s[g, :, c * tq:(c + 1) * tq] = wit_ref[h:h + 1, :]

    def score_tile(j):
        kidx = kidx_ref[j]
        score = jnp.zeros((tk, tq), F32)
        for g in range(n_groups):
            rel = jnp.maximum(_dot(kidx, qi_s[g]), 0.0) * w_s[g]
            for c in range(hg):
                score = score + rel[:, c * tq:(c + 1) * tq]
        score = jnp.where(kpos + j * tk <= qpos, score, -jnp.inf)
        score_ref[j] = score
        slabs = score.reshape(tk // fold, fold, tq)
        top, bot = ext_ref[0], ext_ref[1]
        for r in range(tk // fold):
            top = jnp.maximum(top, slabs[r])
            bot = jnp.minimum(bot, jnp.where(slabs[r] > -jnp.inf, slabs[r], jnp.inf))
        ext_ref[0], ext_ref[1] = top, bot

    fold = 8 * SUBLANES
    ext_ref[0] = jnp.full((fold, tq), -jnp.inf, F32)
    ext_ref[1] = jnp.full((fold, tq), jnp.inf, F32)
    _unrolled_loop(n_kt, score_tile, TILE_UNROLL)

    kf = float(topk)

    n_causal = (qpos[0:1, :] + 1).astype(F32)

    top, bot = ext_ref[0], ext_ref[1]
    rep = lambda v: jnp.broadcast_to(v, (SUBLANES, tq))
    short = n_causal < kf
    lo0 = rep(jnp.where(short, KEY_NEG_INF,
                        _flip_key(pltpu.bitcast(jnp.min(bot, axis=0, keepdims=True), I32))))
    hi0 = rep(_flip_key(pltpu.bitcast(jnp.max(top, axis=0, keepdims=True), I32)) + 1)
    c_lo0 = rep(jnp.where(short, 0.0, n_causal))

    def status(lo, hi, c_lo):
        surplus = c_lo > kf
        return jnp.max(jnp.where(surplus & (hi - 1 > lo), 2.0, jnp.where(surplus, 1.0, 0.0)))

    def search(tiles):
        def run():
            def count_ge(cand):
                cnt = jnp.zeros((fold, tq), F32)
                for j in range(tiles):
                    hit = jnp.where(score_ref[j] >= cand, 1.0, 0.0)
                    cnt = cnt + jnp.sum(hit.reshape(tk // fold, fold, tq), axis=0)
                return jnp.sum(cnt, axis=0, keepdims=True)

            def probe(lo, hi, c_lo, by_key):
                live = (c_lo > kf) & (hi - 1 > lo)
                key_mid = (lo >> 1) + (hi >> 1) + (lo & hi & 1)
                val_mid = 0.5 * (_key_to_f32(lo) + _key_to_f32(hi - 1))
                pk = jnp.where(by_key, key_mid, _flip_key(pltpu.bitcast(val_mid, I32)))
                pk = jnp.minimum(jnp.maximum(pk, lo + 1), hi - 1)
                pk = jnp.where(live, pk, lo)
                c = rep(count_ge(_key_to_f32(jnp.maximum(pk, KEY_NEG_INF))[0:1]))
                up = live & (c >= kf)
                down = live & (c < kf)
                return jnp.where(up, pk, lo), jnp.where(down, pk, hi), jnp.where(up, c, c_lo)

            def probes(state):
                lo, hi, c_lo, _, trip = state
                lo, hi, c_lo = probe(lo, hi, c_lo, False)
                lo, hi, c_lo = probe(lo, hi, c_lo, trip % 2 == 1)
                flag = lax.cond(trip >= UNCHECKED_TRIPS, lambda: status(lo, hi, c_lo),
                                lambda: jnp.float32(2.0))
                return lo, hi, c_lo, flag, trip + 1

            def keep_going(state):
                return jnp.logical_and(state[3] > 1.5, state[4] < 64)

            state = (lo0, hi0, c_lo0, status(lo0, hi0, c_lo0), jnp.int32(0))
            lo, _, _, flag, _ = lax.while_loop(keep_going, probes, state)
            return lo[0:1], flag
        return run

    thr_key, status_out = lax.switch(n_kt - 1, [search(k) for k in range(1, n_kt_all + 1)])
    thr = _key_to_f32(jnp.maximum(thr_key, KEY_NEG_INF))

    @pl.when(status_out > 0.5)
    def _():
        def count(pred):
            def body(j, cnt):
                hit = jnp.where(pred(kpos + j * tk, score_ref[j]), 1.0, 0.0)
                return cnt + jnp.sum(hit.reshape(tk // fold, fold, tq), axis=0)

            cnt = lax.fori_loop(0, n_kt, body, jnp.zeros((fold, tq), F32))
            return jnp.sum(cnt, axis=0, keepdims=True)

        need = kf - count(lambda pos, sc: sc > thr)
        n_pos_bits = (n_kt_all * tk - 1).bit_length()

        def pos_step(t, cut):
            cand = cut | jnp.left_shift(jnp.int32(1), n_pos_bits - 1 - t)
            before = count(lambda pos, sc: (sc == thr) & (pos < cand))
            return jnp.where(before < need, cand, cut)

        cut = lax.fori_loop(0, n_pos_bits, pos_step, jnp.zeros((1, tq), I32))

        def drop_tile(j, carry):
            sc = score_ref[j]
            score_ref[j] = jnp.where((sc == thr) & (kpos + j * tk > cut), -jnp.inf, sc)
            return carry

        lax.fori_loop(0, n_kt, drop_tile, 0)

    def mask_tile(j, carry):
        keep = (score_ref[j] >= thr) & (kpos + j * tk <= qpos)
        score_ref[j] = jnp.where(keep, 0.0, NEG_BIG)
        return carry

    lax.fori_loop(0, n_kt, mask_tile, 0)

    m_ref[...] = jnp.full(m_ref.shape, NEG_BIG, F32)
    l_ref[...] = jnp.zeros(l_ref.shape, F32)
    acc_ref[...] = jnp.zeros(acc_ref.shape, F32)
    p_buf[...] = jnp.zeros(p_buf.shape, p_buf.dtype)
    a_buf[...] = jnp.ones(a_buf.shape, F32)

    def qk(j, g):
        bias = jnp.concatenate([score_ref[j]] * hg, axis=1)
        s_buf[g % 2] = _dot(kv_ref[j], q_s[g]) + bias

    def softmax(g):
        s = s_buf[g % 2]
        m_old = m_ref[g]
        m_new = jnp.maximum(m_old, jnp.max(s, axis=0, keepdims=True))
        p = jnp.exp2(s - m_new)
        alpha = jnp.exp2(m_old - m_new)
        l_ref[g] = alpha * l_ref[g] + jnp.sum(p, axis=0, keepdims=True)
        a_buf[g % 2] = alpha
        p_buf[g % 2] = p.astype(p_buf.dtype)
        m_ref[g] = m_new

    def pv(j, g):
        acc_ref[g] = a_buf[g % 2] * acc_ref[g] + _dot(kvt_ref[j], p_buf[g % 2])

    last = n_groups - 1
    qk(0, 0)

    def attn_tile(j):
        for g in range(n_groups):
            if g < last:
                qk(j, g + 1)
            else:
                qk(jnp.minimum(j + 1, n_kt - 1), 0)
            if g > 0:
                pv(j, g - 1)
            else:
                pv(jnp.maximum(j - 1, 0), last)
            softmax(g)

    _unrolled_loop(n_kt, attn_tile, TILE_UNROLL)
    pv(n_kt - 1, last)

    for h in range(n_heads):
        g, c = divmod(h, hg)
        lanes = slice(c * tq, (c + 1) * tq)
        inv_l = 1.0 / l_ref[g, :, lanes]
        o_t = (acc_ref[g, :, lanes] * inv_l).astype(wuvt_ref.dtype)
        yt_ref[h * D_VHEAD:(h + 1) * D_VHEAD, :] = _dot(wuvt_ref[h], o_t)


def _dsa_attention(qi_t, wi_t, kidx, q_t, kv, kvt, w_uv_t, topk, tq=128, heads_per_group=2):
    n_heads, d_lat, s = q_t.shape
    n_kt, tk, dk = kidx.shape
    assert tk % tq == 0
    n_groups = n_heads // heads_per_group
    gw = heads_per_group * tq
    const3 = lambda b: (0, 0, 0)
    return pl.pallas_call(
        functools.partial(_attn_kernel, topk=topk),
        grid=(s // tq,),
        in_specs=[
            pl.BlockSpec((N_IDX_HEADS, dk, tq), lambda b: (0, 0, b)),
            pl.BlockSpec((N_IDX_HEADS, tq), lambda b: (0, b)),
            pl.BlockSpec((n_kt, tk, dk), const3),
            pl.BlockSpec((n_heads, d_lat, tq), lambda b: (0, 0, b)),
            pl.BlockSpec((n_kt, tk, d_lat), const3),
            pl.BlockSpec((n_kt, d_lat, tk), const3),
            pl.BlockSpec((n_heads, D_VHEAD, d_lat), const3),
        ],
        out_specs=pl.BlockSpec((n_heads * D_VHEAD, tq), lambda b: (0, b)),
        out_shape=jax.ShapeDtypeStruct((n_heads * D_VHEAD, s), F32),
        scratch_shapes=[
            pltpu.VMEM((n_kt, tk, tq), F32),
            pltpu.VMEM((2, 8 * SUBLANES, tq), F32),
            pltpu.VMEM((n_groups, dk, gw), BF16),
            pltpu.VMEM((n_groups, d_lat, gw), BF16),
            pltpu.VMEM((n_groups, 1, gw), F32),
            pltpu.VMEM((n_groups, 1, gw), F32),
            pltpu.VMEM((n_groups, 1, gw), F32),
            pltpu.VMEM((n_groups, d_lat, gw), F32),
            pltpu.VMEM((2, tk, gw), F32),
            pltpu.VMEM((2, tk, gw), BF16),
            pltpu.VMEM((2, 1, gw), F32),
        ],
        compiler_params=_params("parallel"),
        name="dsa_attention",
    )(qi_t, wi_t, kidx, q_t, kv, kvt, w_uv_t)


def _merge_kernel(x_ref, z_ref, wgc_ref, wga_ref, wco_ref, yat_ref, o_ref, w_s, wco_s):
    @pl.when(pl.program_id(1) == 0)
    def _():
        w_s[0] = wgc_ref[...].astype(w_s.dtype)
        w_s[1] = wga_ref[...].astype(w_s.dtype)
        wco_s[...] = wco_ref[...].astype(wco_s.dtype)

    x = x_ref[...]
    g_conv = _sigmoid(_dot_nt(x, w_s[0]))
    g_attn = _sigmoid(_dot_nt(x, w_s[1]))
    y_conv = _dot(z_ref[...], wco_s[...])
    o_ref[...] = (g_conv * y_conv + g_attn * yat_ref[...].T).astype(o_ref.dtype)


def _merge(x_bf, z, w_in_t, row0, w_co, y_attn_t, tm=512, tn=512):
    s, d = z.shape
    n = w_co.shape[1]
    gate_rows = lambda off: pl.BlockSpec(
        (pl.Element(tn), pl.Element(d)),
        lambda j, i: (pl.multiple_of(row0 + off + j * tn, SUBLANES), 0))
    return pl.pallas_call(
        _merge_kernel,
        grid=(n // tn, s // tm),
        in_specs=[
            pl.BlockSpec((tm, d), lambda j, i: (i, 0)),
            pl.BlockSpec((tm, d), lambda j, i: (i, 0)),
            gate_rows(0),
            gate_rows(n),
            pl.BlockSpec((d, tn), lambda j, i: (0, j)),
            pl.BlockSpec((tn, tm), lambda j, i: (j, i)),
        ],
        out_specs=pl.BlockSpec((tm, tn), lambda j, i: (i, j)),
        out_shape=jax.ShapeDtypeStruct((s, n), BF16),
        scratch_shapes=[pltpu.VMEM((2, tn, d), BF16), pltpu.VMEM((d, tn), BF16)],
        compiler_params=_params("parallel", "arbitrary"),
        name="merge",
    )(x_bf, z, w_in_t, w_in_t, w_co, y_attn_t)


def _cast_resident(w_ref, w_s):
    @pl.when(pl.program_id(0) == 0)
    def _():
        w_s[...] = w_ref[...].astype(w_s.dtype)


def _resident(shape):
    return pl.BlockSpec(shape, lambda i: (0,) * len(shape), pipeline_mode=pl.Buffered(1))


def _oproj_kernel(m_ref, w_ref, h_ref, g_ref, b_ref, o_ref, obf_ref, w_s, *, alpha):
    _cast_resident(w_ref, w_s)
    v = alpha * h_ref[...] + _dot(m_ref[...], w_s[...])
    out = _layer_norm(v, g_ref[...], b_ref[...])
    o_ref[...] = out
    obf_ref[...] = out.astype(obf_ref.dtype)


def _oproj(mixed, w_o, h, g, b, alpha, tm=512):
    s, d = h.shape
    row = pl.BlockSpec((tm, d), lambda i: (i, 0))
    vec = pl.BlockSpec((1, d), lambda i: (0, 0))
    return pl.pallas_call(
        functools.partial(_oproj_kernel, alpha=alpha),
        grid=(s // tm,),
        in_specs=[row, _resident((d, d)), row, vec, vec],
        out_specs=[row, row],
        out_shape=[jax.ShapeDtypeStruct((s, d), F32), jax.ShapeDtypeStruct((s, d), BF16)],
        scratch_shapes=[pltpu.VMEM((d, d), BF16)],
        compiler_params=_params("arbitrary"),
        name="oproj_ln",
    )(mixed, w_o, h, g, b)


def _ffn_kernel(hbf_ref, wu_ref, wd_ref, h_ref, g2_ref, b2_ref, wg_ref, p_ref, wp_ref, g3_ref, b3_ref,
                o_ref, acc_ref, *, alpha):
    f = pl.program_id(1)

    @pl.when(f == 0)
    def _():
        acc_ref[...] = jnp.zeros(acc_ref.shape, F32)

    a = jnp.maximum(_dot(hbf_ref[...], wu_ref[...]), 0.0)
    acc_ref[...] += _dot((a * a).astype(wd_ref.dtype), wd_ref[...])

    @pl.when(f == pl.num_programs(1) - 1)
    def _():
        h2 = _layer_norm(alpha * h_ref[...] + acc_ref[...], g2_ref[...], b2_ref[...])
        gate = _sigmoid(_dot(h2.astype(wg_ref.dtype), wg_ref[...]))
        pe = gate * _dot(p_ref[...], wp_ref[...])
        o_ref[...] = _layer_norm(alpha * h2 + pe, g3_ref[...], b3_ref[...])


def _ffn_ple(h_bf, w_up, w_down, h, g2, b2, w_pg, p_bf, w_p, g3, b3, alpha, tm=512, tf=1024):
    s, d = h.shape
    dff = w_up.shape[1]
    dp = p_bf.shape[1]
    row = pl.BlockSpec((tm, d), lambda i, f: (i, 0))
    vec = pl.BlockSpec((1, d), lambda i, f: (0, 0))
    once = lambda shape: pl.BlockSpec(shape, lambda i, f: (0, 0), pipeline_mode=pl.Buffered(1))
    return pl.pallas_call(
        functools.partial(_ffn_kernel, alpha=alpha),
        grid=(s // tm, dff // tf),
        in_specs=[row, pl.BlockSpec((d, tf), lambda i, f: (0, f)),
                  pl.BlockSpec((tf, d), lambda i, f: (f, 0)), row, vec, vec,
                  once((d, d)), pl.BlockSpec((tm, dp), lambda i, f: (i, 0)), once((dp, d)),
                  vec, vec],
        out_specs=row,
        out_shape=jax.ShapeDtypeStruct((s, d), F32),
        scratch_shapes=[pltpu.VMEM((tm, d), F32)],
        compiler_params=pltpu.CompilerParams(dimension_semantics=("parallel", "arbitrary"),
                                             vmem_limit_bytes=FFN_PLE_VMEM_LIMIT_BYTES),
        name="ffn_ple_ln",
    )(h_bf, w_up, w_down, h, g2, b2, w_pg, p_bf, w_p, g3, b3)


def _layer(h, p, w_in, conv_w, w_conv_out, g_kv, w_uv, w_o, ln1_g, ln1_b, w_up, w_down,
           ln2_g, ln2_b, w_ple_gate, w_ple, ln3_g, ln3_b, alpha, tk=512):
    s, d = h.shape
    d_conv = conv_w.shape[1]
    nq = N_IDX_HEADS * D_IDX
    c_qlat = 3 * d_conv
    c_ckv = c_qlat + N_HEADS * D_LATENT
    c_gates = c_ckv + D_LATENT + nq + D_IDX + N_IDX_HEADS
    topk = min(TOPK_MAX, s // 4)
    n_kt = s // tk

    w_in_t = jnp.transpose(w_in)

    h_bf, ckv, kidx, qi_t, wi_t = _idx_proj(h, w_in_t, c_ckv, g_kv.reshape(1, D_LATENT))
    z, w_down_bf = _conv_branch(h_bf, w_in_t, conv_w, d_conv, w_down)
    q_t, w_up_bf = _qlat_proj(h_bf, w_in_t, c_qlat, w_up)

    kidx = kidx.reshape(n_kt, tk, 3 * D_IDX)
    kv = ckv.reshape(n_kt, tk, D_LATENT)
    kvt = jnp.transpose(kv, (0, 2, 1))

    w_uv_t = jnp.transpose(w_uv, (0, 2, 1)).astype(BF16)
    y_attn_t = _dsa_attention(qi_t, wi_t, kidx, q_t, kv, kvt, w_uv_t, topk)

    mixed = _merge(h_bf, z, w_in_t, c_gates, w_conv_out, y_attn_t)
    h1, h1_bf = _oproj(mixed, w_o, h, ln1_g.reshape(1, d), ln1_b.reshape(1, d), alpha)
    return _ffn_ple(h1_bf, w_up_bf, w_down_bf, h1, ln2_g.reshape(1, d), ln2_b.reshape(1, d),
                    w_ple_gate.astype(BF16), p.astype(BF16), w_ple.astype(BF16),
                    ln3_g.reshape(1, d), ln3_b.reshape(1, d), alpha)


def kernel(x, p, w_in, conv_w, w_conv_out, g_kv, w_uv, w_o, ln1_g, ln1_b, w_up, w_down,
           ln2_g, ln2_b, w_ple_gate, w_ple, ln3_g, ln3_b):
    depth = w_in.shape[0]
    alpha = (2.0 * depth) ** 0.25
    batch = x.shape[0]
    outs = []
    for bi in range(batch):
        h = x[bi]
        for i in range(depth):
            h = _layer(h, p[i, bi], w_in[i], conv_w[i], w_conv_out[i], g_kv[i], w_uv[i], w_o[i],
                       ln1_g[i], ln1_b[i], w_up[i], w_down[i], ln2_g[i], ln2_b[i],
                       w_ple_gate[i], w_ple[i], ln3_g[i], ln3_b[i], alpha)
        outs.append(h)
    return jnp.stack(outs, axis=0)
```
